```python
import math
import jax, jax.numpy as jnp
from jax import lax
import numpy as np

D_MODEL = 1024
BATCH = 8
SEQ = 4096
DEPTH = 1

HEAD_DIM = 64
BLK = 128
RMS_EPS = 1e-6
NSA_HEADS = 8
NSA_KV_GROUPS = 2
NSA_CMP_LEN = 32
NSA_CMP_STRIDE = 16
NSA_CMP_HIDDEN = 256
NSA_SLC_LEN = 64
NSA_N_SEL = 16
NSA_WINDOW = 512
DIL_CONFIGS = ((128, 1), (512, 4), (2048, 16))
DIL_HEADS_PER_GROUP = 4
DIL_HEADS = DIL_HEADS_PER_GROUP * len(DIL_CONFIGS)
D_FF = ((8 * D_MODEL // 3 + 127) // 128) * 128
Q_NSA = NSA_HEADS * HEAD_DIM
KV_NSA = NSA_KV_GROUPS * HEAD_DIM
GATE_NSA = NSA_HEADS * 3
QKV_DIL = DIL_HEADS * HEAD_DIM
IN_DIM = Q_NSA + 6 * KV_NSA + GATE_NSA + 3 * QKV_DIL + 2 * D_MODEL
DIL_OUT = DIL_HEADS_PER_GROUP * HEAD_DIM

kernel_name = "hybrid_nsa_dilated_macaron_block"


def rms_norm(x, g):
    xf = x.astype(jnp.float32)
    y = xf * lax.rsqrt(jnp.mean(xf * xf, axis=-1, keepdims=True) + RMS_EPS)
    return (y * g.astype(jnp.float32)).astype(x.dtype)


def swiglu(x, w_gate, w_up, w_down):
    return (jax.nn.silu(x @ w_gate) * (x @ w_up)) @ w_down


def alibi_slopes(n):
    return jnp.asarray(2.0 ** (-8.0 * np.arange(1, n + 1) / n), dtype=jnp.float32)


def masked_softmax(s, mask):
    s = jnp.where(mask, s, -jnp.inf)
    m = jnp.max(s, axis=-1, keepdims=True)
    m = jnp.where(jnp.isfinite(m), m, 0.0)
    p = jnp.exp(s - m)
    den = jnp.maximum(jnp.sum(p, axis=-1, keepdims=True), 1e-30)
    return p / den, (m + jnp.log(den))[..., 0]


def banded_attention(q, k, v, n_prev, max_back, step, slopes):
    N, L, H, hd = q.shape
    G = k.shape[2]
    rep = H // G
    nb = L // BLK
    W = (n_prev + 1) * BLK
    qb = q.reshape(N, nb, BLK, G, rep, hd)

    def windows(t):
        tp = jnp.pad(t, ((0, 0), (n_prev * BLK, 0), (0, 0), (0, 0))).reshape(N, nb + n_prev, BLK, G, hd)
        return jnp.concatenate([tp[:, j:j + nb] for j in range(n_prev + 1)], axis=2)

    kw, vw = windows(k), windows(v)
    s = jnp.einsum('nbqgrd,nbkgd->nbgrqk', qb, kw, preferred_element_type=jnp.float32) * hd ** -0.5
    diff = (jnp.arange(BLK)[:, None] + n_prev * BLK) - jnp.arange(W)[None, :]
    key_abs = jnp.arange(nb)[:, None] * BLK - n_prev * BLK + jnp.arange(W)[None, :]
    s = s - slopes.reshape(G, rep)[:, :, None, None] * (diff * step).astype(jnp.float32)
    mask = ((diff >= 0) & (diff <= max_back))[None, None, None, None] & (key_abs >= 0)[None, :, None, None, None, :]
    p, lse = masked_softmax(s, mask)
    o = jnp.einsum('nbgrqk,nbkgd->nbqgrd', p.astype(v.dtype), vw).reshape(N, L, H, hd)
    lse = lse.transpose(0, 1, 4, 2, 3).reshape(N, L, H)
    return o, lse


def nsa_attention(q, k_cmp, v_cmp, k_slc, v_slc, k_win, v_win, gates,
                  pe_k, w_ck1, w_ck2, pe_v, w_cv1, w_cv2):
    B, S, H, hd = q.shape
    G = NSA_KV_GROUPS
    rep = H // G
    scale = hd ** -0.5
    slopes = alibi_slopes(H)
    sl_gr = slopes.reshape(G, rep)
    pos = jnp.arange(S)
    qg = q.reshape(B, S, G, rep, hd)

    n_cmp = (S - NSA_CMP_LEN) // NSA_CMP_STRIDE + 1
    blk_idx = jnp.arange(n_cmp)[:, None] * NSA_CMP_STRIDE + jnp.arange(NSA_CMP_LEN)[None, :]

    def compress(t, pe, w1, w2):
        tb = t[:, blk_idx] + pe[None, None, :, None, :]
        tb = tb.transpose(0, 1, 3, 2, 4).reshape(B, n_cmp, G, NSA_CMP_LEN * hd)
        return jax.nn.gelu(tb @ w1) @ w2

    kc = compress(k_cmp, pe_k, w_ck1, w_ck2)
    vc = compress(v_cmp, pe_v, w_cv1, w_cv2)
    cmp_end = jnp.arange(n_cmp) * NSA_CMP_STRIDE + NSA_CMP_LEN - 1
    dist_c = pos[:, None] - cmp_end[None, :]
    s_c = jnp.einsum('bsgrd,bcgd->bgrsc', qg, kc, preferred_element_type=jnp.float32) * scale
    s_c = s_c - sl_gr[:, :, None, None] * dist_c.astype(jnp.float32)
    p_cmp, _ = masked_softmax(s_c, dist_c >= 0)
    o_cmp = jnp.einsum('bgrsc,bcgd->bsgrd', p_cmp.astype(vc.dtype), vc).reshape(B, S, H, hd)

    n_slc = S // NSA_SLC_LEN
    c_start = jnp.arange(n_cmp) * NSA_CMP_STRIDE
    s_start = jnp.arange(n_slc) * NSA_SLC_LEN
    overlap = jnp.clip(jnp.minimum(c_start[:, None] + NSA_CMP_LEN, s_start[None, :] + NSA_SLC_LEN)
                       - jnp.maximum(c_start[:, None], s_start[None, :]), 0, None).astype(jnp.float32) / NSA_CMP_LEN
    imp = jnp.einsum('bgrsc,cj->bsgj', p_cmp, overlap)
    jb = jnp.arange(n_slc)[None, :]
    own = (pos // NSA_SLC_LEN)[:, None]
    forced = (jb == 0) | (jb == own) | (jb == own - 1)
    valid = jb * NSA_SLC_LEN <= pos[:, None]
    imp = jnp.where(forced[None, :, None, :], jnp.inf, jnp.where(valid[None, :, None, :], imp, -jnp.inf))
    n_sel = min(NSA_N_SEL, n_slc)
    _, sel = lax.top_k(imp, n_sel)

    kb = k_slc.reshape(B, n_slc, NSA_SLC_LEN, G, hd).transpose(0, 3, 1, 2, 4)
    vb = v_slc.reshape(B, n_slc, NSA_SLC_LEN, G, hd).transpose(0, 3, 1, 2, 4)
    nqb = S // BLK
    bi = jnp.arange(B)[:, None, None, None]
    gi = jnp.arange(G)[None, :, None, None]
    n_keys = n_sel * NSA_SLC_LEN

    def slc_block(args):
        qblk, selblk, qpos = args
        idx = selblk.transpose(0, 2, 1, 3)
        kg = kb[bi, gi, idx].reshape(B, G, BLK, n_keys, hd)
        vg = vb[bi, gi, idx].reshape(B, G, BLK, n_keys, hd)
        kpos = (idx[..., None] * NSA_SLC_LEN + jnp.arange(NSA_SLC_LEN)).reshape(B, G, BLK, n_keys)
        dist = qpos[None, None, :, None] - kpos
        s = jnp.einsum('bqgrd,bgqkd->bgrqk', qblk, kg, preferred_element_type=jnp.float32) * scale
        s = s - sl_gr[None, :, :, None, None] * dist[:, :, None].astype(jnp.float32)
        p, _ = masked_softmax(s, (dist >= 0)[:, :, None])
        return jnp.einsum('bgrqk,bgqkd->bqgrd', p.astype(vg.dtype), vg)

    xs = (qg.reshape(B, nqb, BLK, G, rep, hd).swapaxes(0, 1),
          sel.reshape(B, nqb, BLK, G, n_sel).swapaxes(0, 1),
          pos.reshape(nqb, BLK))
    o_slc = lax.map(slc_block, xs).swapaxes(0, 1).reshape(B, S, H, hd)

    o_win, _ = banded_attention(q, k_win, v_win, NSA_WINDOW // BLK, NSA_WINDOW - 1, 1, slopes)

    g = jax.nn.sigmoid(gates)
    return g[..., 0:1] * o_cmp + g[..., 1:2] * o_slc + g[..., 2:3] * o_win


def dilated_attention(q, k, v):
    B, S, _, hd = q.shape
    hpg = DIL_HEADS_PER_GROUP
    slopes = alibi_slopes(DIL_HEADS).reshape(len(DIL_CONFIGS), hpg)
    outs, lses = [], []
    for gidx, (window, dil) in enumerate(DIL_CONFIGS):
        Lp = -(-S // (dil * BLK)) * dil * BLK
        Ls = Lp // dil

        def strided(t):
            t = jnp.pad(t[:, :, gidx * hpg:(gidx + 1) * hpg], ((0, 0), (0, Lp - S), (0, 0), (0, 0)))
            return t.reshape(B, Ls, dil, hpg, hd).transpose(0, 2, 1, 3, 4).reshape(B * dil, Ls, hpg, hd)

        o, lse = banded_attention(strided(q), strided(k), strided(v), 1, window // dil, dil, slopes[gidx])
        outs.append(o.reshape(B, dil, Ls, hpg, hd).transpose(0, 2, 1, 3, 4).reshape(B, Lp, hpg, hd)[:, :S])
        lses.append(lse.reshape(B, dil, Ls, hpg).transpose(0, 2, 1, 3).reshape(B, Lp, hpg)[:, :S])
    w = jax.nn.softmax(jnp.stack(lses), axis=0)
    o = jnp.sum(w[..., None] * jnp.stack(outs).astype(jnp.float32), axis=0)
    return o.astype(q.dtype)


def token_mixer(h, w_in, pe_k, w_ck1, w_ck2, pe_v, w_cv1, w_cv2, w_nsa_o, w_dil_o, w_mix_out):
    B, S, _ = h.shape
    proj = h @ w_in
    sizes = [Q_NSA, 6 * KV_NSA, GATE_NSA, 3 * QKV_DIL, 2 * D_MODEL]
    q_n, kv_n, g_n, qkv_d, merge = jnp.split(proj, np.cumsum(sizes)[:-1].tolist(), axis=-1)
    q_n = q_n.reshape(B, S, NSA_HEADS, HEAD_DIM)
    kv_n = kv_n.reshape(B, S, 6, NSA_KV_GROUPS, HEAD_DIM)
    g_n = g_n.reshape(B, S, NSA_HEADS, 3)
    o_nsa = nsa_attention(q_n, kv_n[:, :, 0], kv_n[:, :, 1], kv_n[:, :, 2], kv_n[:, :, 3],
                          kv_n[:, :, 4], kv_n[:, :, 5], g_n, pe_k, w_ck1, w_ck2, pe_v, w_cv1, w_cv2)
    qkv_d = qkv_d.reshape(B, S, 3, DIL_HEADS, HEAD_DIM)
    o_dil = dilated_attention(qkv_d[:, :, 0], qkv_d[:, :, 1], qkv_d[:, :, 2])
    y_nsa = o_nsa.reshape(B, S, Q_NSA) @ w_nsa_o
    y_dil = o_dil.reshape(B, S, DIL_OUT) @ w_dil_o
    g_a, g_b = jnp.split(merge, 2, axis=-1)
    y = jax.nn.sigmoid(g_a) * y_nsa + jax.nn.sigmoid(g_b) * y_dil
    return y @ w_mix_out


def setup_inputs(seed: int = 0) -> dict:
    key = jax.random.key(seed)
    ks = iter(jax.random.split(key, 32))

    def w(shape, fan_in):
        return jax.random.normal(next(ks), (DEPTH,) + shape, jnp.float32) * fan_in ** -0.5

    def gain():
        return 1.0 + 0.01 * jax.random.normal(next(ks), (DEPTH, D_MODEL), jnp.float32)

    inp = {}
    inp["x"] = jax.random.normal(next(ks), (BATCH, SEQ, D_MODEL), jnp.float32)
    inp["ffn1_pre"] = gain()
    inp["ffn1_post"] = gain()
    inp["ffn1_w_gate"] = w((D_MODEL, D_FF), D_MODEL)
    inp["ffn1_w_up"] = w((D_MODEL, D_FF), D_MODEL)
    inp["ffn1_w_down"] = w((D_FF, D_MODEL), D_FF)
    inp["mix_pre"] = gain()
    inp["mix_post"] = gain()
    inp["w_in"] = w((D_MODEL, IN_DIM), D_MODEL)
    inp["nsa_pe_k"] = 0.1 * jax.random.normal(next(ks), (DEPTH, NSA_CMP_LEN, HEAD_DIM), jnp.float32)
    inp["nsa_w_ck1"] = w((NSA_CMP_LEN * HEAD_DIM, NSA_CMP_HIDDEN), NSA_CMP_LEN * HEAD_DIM)
    inp["nsa_w_ck2"] = w((NSA_CMP_HIDDEN, HEAD_DIM), NSA_CMP_HIDDEN)
    inp["nsa_pe_v"] = 0.1 * jax.random.normal(next(ks), (DEPTH, NSA_CMP_LEN, HEAD_DIM), jnp.float32)
    inp["nsa_w_cv1"] = w((NSA_CMP_LEN * HEAD_DIM, NSA_CMP_HIDDEN), NSA_CMP_LEN * HEAD_DIM)
    inp["nsa_w_cv2"] = w((NSA_CMP_HIDDEN, HEAD_DIM), NSA_CMP_HIDDEN)
    inp["w_nsa_o"] = w((Q_NSA, D_MODEL), Q_NSA)
    inp["w_dil_o"] = w((DIL_OUT, D_MODEL), DIL_OUT)
    inp["w_mix_out"] = w((D_MODEL, D_MODEL), D_MODEL)
    inp["ffn2_pre"] = gain()
    inp["ffn2_post"] = gain()
    inp["ffn2_w_gate"] = w((D_MODEL, D_FF), D_MODEL)
    inp["ffn2_w_up"] = w((D_MODEL, D_FF), D_MODEL)
    inp["ffn2_w_down"] = w((D_FF, D_MODEL), D_FF)
    return inp


def reference(x, ffn1_pre, ffn1_post, ffn1_w_gate, ffn1_w_up, ffn1_w_down,
              mix_pre, mix_post, w_in, nsa_pe_k, nsa_w_ck1, nsa_w_ck2,
              nsa_pe_v, nsa_w_cv1, nsa_w_cv2, w_nsa_o, w_dil_o, w_mix_out,
              ffn2_pre, ffn2_post, ffn2_w_gate, ffn2_w_up, ffn2_w_down):
    for l in range(DEPTH):
        h = rms_norm(x, ffn1_pre[l])
        x = x + 0.5 * rms_norm(swiglu(h, ffn1_w_gate[l], ffn1_w_up[l], ffn1_w_down[l]), ffn1_post[l])
        h = rms_norm(x, mix_pre[l])
        y = token_mixer(h, w_in[l], nsa_pe_k[l], nsa_w_ck1[l], nsa_w_ck2[l], nsa_pe_v[l],
                        nsa_w_cv1[l], nsa_w_cv2[l], w_nsa_o[l], w_dil_o[l], w_mix_out[l])
        x = x + rms_norm(y, mix_post[l])
        h = rms_norm(x, ffn2_pre[l])
        x = x + 0.5 * rms_norm(swiglu(h, ffn2_w_gate[l], ffn2_w_up[l], ffn2_w_down[l]), ffn2_post[l])
    return x
```

```python
import functools
import math

import numpy as np
import jax
import jax.numpy as jnp
from jax import lax
from jax.experimental import pallas as pl
from jax.experimental.pallas import tpu as pltpu

F32 = jnp.float32
BF16 = jnp.bfloat16

D_MODEL = 1024
HEAD_DIM = 64
RMS_EPS = 1e-6
NSA_HEADS = 8
NSA_GROUPS = 2
NSA_REP = NSA_HEADS // NSA_GROUPS
CMP_LEN = 32
CMP_STRIDE = 16
CMP_HIDDEN = 256
SLC_LEN = 64
N_SEL = 16
NSA_WINDOW = 512
DIL_CONFIGS = ((128, 1), (512, 4), (2048, 16))
DIL_HPG = 4
DIL_HEADS = DIL_HPG * len(DIL_CONFIGS)
D_FF = 2816
Q_NSA = NSA_HEADS * HEAD_DIM
KV_NSA = NSA_GROUPS * HEAD_DIM
GATE_NSA = NSA_HEADS * 3
QKV_DIL = DIL_HEADS * HEAD_DIM
DIL_OUT = DIL_HPG * HEAD_DIM
LANES = 128
NEG = -1e30
SCALE = HEAD_DIM ** -0.5
VMEM_LIMIT = 56 * 1024 * 1024


def _slopes(n):
    return [float(np.float32(2.0 ** (-8.0 * (i + 1) / n))) for i in range(n)]


NSA_SLOPES = _slopes(NSA_HEADS)
DIL_SLOPES = _slopes(DIL_HEADS)


def _params(sem):
    return pltpu.CompilerParams(dimension_semantics=sem, vmem_limit_bytes=VMEM_LIMIT)


def _rms(x, g):
    ms = jnp.mean(x * x, axis=-1, keepdims=True)
    return x * lax.rsqrt(ms + RMS_EPS) * g


def _sigmoid(x):
    return 1.0 / (1.0 + jnp.exp(-x))


def _dot(a, b):
    return jnp.dot(a, b, preferred_element_type=F32)


def _dot_nt(a, b):
    return lax.dot_general(a, b, (((1,), (1,)), ((), ())), preferred_element_type=F32)


def _split_dot(a, b_bf16):
    hi = a.astype(BF16)
    r1 = a - hi.astype(F32)
    mid = r1.astype(BF16)
    lo = (r1 - mid.astype(F32)).astype(BF16)
    return _dot(hi, b_bf16) + _dot(mid, b_bf16) + _dot(lo, b_bf16)


def _ffn_kernel(x_ref, pre_ref, post_ref, wg_ref, wu_ref, wd_ref, o_ref, h_scr, acc_scr):
    j = pl.program_id(1)

    @pl.when(j == 0)
    def _():
        h_scr[...] = _rms(x_ref[...], pre_ref[...]).astype(BF16)
        acc_scr[...] = jnp.zeros_like(acc_scr)

    h = h_scr[...]
    g = _dot(h, wg_ref[...])
    u = _dot(h, wu_ref[...])
    a = (g * _sigmoid(g) * u).astype(BF16)
    acc_scr[...] += _dot(a, wd_ref[...])

    @pl.when(j == pl.num_programs(1) - 1)
    def _():
        o_ref[...] = x_ref[...] + 0.5 * _rms(acc_scr[...], post_ref[...])


def _ffn(x2d, pre, post, wg, wu, wd, *, tm=512, tf=1408):
    T = x2d.shape[0]
    return pl.pallas_call(
        _ffn_kernel,
        grid=(T // tm, D_FF // tf),
        in_specs=[
            pl.BlockSpec((tm, D_MODEL), lambda i, j: (i, 0)),
            pl.BlockSpec((1, D_MODEL), lambda i, j: (0, 0)),
            pl.BlockSpec((1, D_MODEL), lambda i, j: (0, 0)),
            pl.BlockSpec((D_MODEL, tf), lambda i, j: (0, j)),
            pl.BlockSpec((D_MODEL, tf), lambda i, j: (0, j)),
            pl.BlockSpec((tf, D_MODEL), lambda i, j: (j, 0)),
        ],
        out_specs=pl.BlockSpec((tm, D_MODEL), lambda i, j: (i, 0)),
        out_shape=jax.ShapeDtypeStruct((T, D_MODEL), F32),
        scratch_shapes=[pltpu.VMEM((tm, D_MODEL), BF16), pltpu.VMEM((tm, D_MODEL), F32)],
        compiler_params=_params(("parallel", "arbitrary")),
        name="ffn",
    )(x2d, pre, post, wg, wu, wd)


def _proj_kernel(x_ref, pre_ref, w_ref, wgate_ref, p_ref, g_ref):
    h = _rms(x_ref[...], pre_ref[...]).astype(BF16)
    p_ref[...] = _dot(h, w_ref[...]).astype(BF16)
    g_ref[...] = _dot(h, wgate_ref[...])


def _proj(x2d, pre, w_main, w_gate, *, tm=256):
    T = x2d.shape[0]
    n = w_main.shape[1]
    return pl.pallas_call(
        _proj_kernel,
        grid=(T // tm,),
        in_specs=[
            pl.BlockSpec((tm, D_MODEL), lambda i: (i, 0)),
            pl.BlockSpec((1, D_MODEL), lambda i: (0, 0)),
            pl.BlockSpec((D_MODEL, n), lambda i: (0, 0)),
            pl.BlockSpec((D_MODEL, LANES), lambda i: (0, 0)),
        ],
        out_specs=[
            pl.BlockSpec((tm, n), lambda i: (i, 0)),
            pl.BlockSpec((tm, LANES), lambda i: (i, 0)),
        ],
        out_shape=[jax.ShapeDtypeStruct((T, n), BF16), jax.ShapeDtypeStruct((T, LANES), F32)],
        compiler_params=_params(("parallel",)),
        name="proj",
    )(x2d, pre, w_main, w_gate)


def _gelu_tanh(x):
    c = math.sqrt(2.0 / math.pi)
    return x * (0.5 * (1.0 + jnp.tanh(c * (x + 0.044715 * (x * x * x)))))


def _cmp_mlp_kernel(x_ref, pe_ref, w1cat_ref, w1_ref, w2_ref, o_ref):
    n_chunks = x_ref.shape[0]
    ab = _dot(x_ref[...], w1cat_ref[...])
    first = ab[:, :CMP_HIDDEN]
    second = pltpu.roll(ab[:, CMP_HIDDEN:], n_chunks - 1, 0)
    pe8 = jnp.broadcast_to(pe_ref[...], (8, pe_ref.shape[1]))
    bias = _dot(pe8, w1_ref[...])[0:1]
    hid = _gelu_tanh(first + second + bias).astype(BF16)
    out = _dot(hid, w2_ref[...])
    row = lax.broadcasted_iota(jnp.int32, out.shape, 0)
    o_ref[...] = jnp.where(row < n_chunks - 1, out, 0.0).astype(BF16)


def _cmp_mlp(xc, pe, w1cat, w1, w2):
    _, BG, n_chunks, width = xc.shape
    return pl.pallas_call(
        _cmp_mlp_kernel,
        grid=(2, BG),
        in_specs=[
            pl.BlockSpec((None, None, n_chunks, width), lambda t, n: (t, n, 0, 0)),
            pl.BlockSpec((None, 1, 2 * width), lambda t, n: (t, 0, 0)),
            pl.BlockSpec((None, width, 2 * CMP_HIDDEN), lambda t, n: (t, 0, 0)),
            pl.BlockSpec((None, 2 * width, CMP_HIDDEN), lambda t, n: (t, 0, 0)),
            pl.BlockSpec((None, CMP_HIDDEN, HEAD_DIM), lambda t, n: (t, 0, 0)),
        ],
        out_specs=pl.BlockSpec((None, None, n_chunks, HEAD_DIM), lambda t, n: (t, n, 0, 0)),
        out_shape=jax.ShapeDtypeStruct((2, BG, n_chunks, HEAD_DIM), BF16),
        compiler_params=_params(("parallel", "parallel")),
        name="cmp_mlp",
    )(xc, pe, w1cat, w1, w2)


def _group_slope_col(g, rows, tq):
    head = lax.broadcasted_iota(jnp.int32, (rows, 1), 0) // tq
    col = jnp.zeros((rows, 1), F32)
    for r in range(NSA_REP):
        sl = jnp.where(g == 0, NSA_SLOPES[r], NSA_SLOPES[NSA_REP + r])
        col = jnp.where(head == r, sl, col)
    return col


def _stacked_qpos(i, rows, tq):
    row = lax.broadcasted_iota(jnp.int32, (rows, 1), 0)
    return i * tq + (row % tq)


def _normalize(acc):
    den = pltpu.roll(acc, HEAD_DIM, 1)
    return (acc / den)[:, :HEAD_DIM], den[:, :HEAD_DIM]


def _cmp_attn_kernel(q_ref, kc_ref, vc_ref, ov_ref, o_ref, sel_ref, *, tq):
    g = pl.program_id(1)
    i = pl.program_id(2)
    rows = NSA_REP * tq
    n_ck = kc_ref.shape[0]
    q = q_ref[...].reshape(rows, HEAD_DIM) * SCALE
    s = _dot_nt(q, kc_ref[...])
    qpos = _stacked_qpos(i, rows, tq)
    c = lax.broadcasted_iota(jnp.int32, (1, n_ck), 1)
    dist = qpos - (c * CMP_STRIDE + (CMP_LEN - 1))
    ok = (dist >= 0) & (c < n_ck - 1)
    s = s - _group_slope_col(g, rows, tq) * dist.astype(F32)
    s = jnp.where(ok, s, NEG)
    m = jnp.max(s, axis=-1, keepdims=True)
    p = jnp.where(ok, jnp.exp(s - m), 0.0)
    den = jnp.maximum(jnp.sum(p, axis=-1, keepdims=True), 1e-30)
    p = p / den
    o = _dot(p.astype(BF16), vc_ref[...])
    o_ref[...] = o.reshape(NSA_REP, tq, HEAD_DIM)

    psum = p[0:tq]
    for r in range(1, NSA_REP):
        psum = psum + p[r * tq:(r + 1) * tq]
    imp = _split_dot(psum, ov_ref[...])

    n_slc = (n_ck * CMP_STRIDE) // SLC_LEN
    pos = i * tq + lax.broadcasted_iota(jnp.int32, (tq, 1), 0)
    own = pos // SLC_LEN
    jb = lax.broadcasted_iota(jnp.int32, (1, LANES), 1)
    jbf = jb.astype(F32)
    forced = (jb == 0) | (jb == own) | (jb == own - 1)
    valid = (jb * SLC_LEN <= pos) & (jb < n_slc)
    vals = jnp.where(valid & jnp.logical_not(forced), imp, -jnp.inf)
    sel = jnp.where(forced, 1.0, 0.0)
    for _ in range(N_SEL - 3):
        mx = jnp.max(vals, axis=-1, keepdims=True)
        cand = jnp.where(vals == mx, jbf, 1e9)
        idx = jnp.min(cand, axis=-1, keepdims=True)
        pick = (jbf == idx) & (mx > -jnp.inf)
        sel = jnp.where(pick, 1.0, sel)
        vals = jnp.where(pick, -jnp.inf, vals)
    sel_ref[...] = sel.astype(BF16)


def _cmp_attn(q, kc, vc, ov, *, tq=256):
    B, H, S, _ = q.shape
    BG, n_ck, _ = kc.shape
    G = NSA_GROUPS
    return pl.pallas_call(
        functools.partial(_cmp_attn_kernel, tq=tq),
        grid=(B, G, S // tq),
        in_specs=[
            pl.BlockSpec((None, NSA_REP, tq, HEAD_DIM), lambda b, g, i: (b, g, i, 0)),
            pl.BlockSpec((None, n_ck, HEAD_DIM), lambda b, g, i: (b * G + g, 0, 0)),
            pl.BlockSpec((None, n_ck, HEAD_DIM), lambda b, g, i: (b * G + g, 0, 0)),
            pl.BlockSpec((n_ck, LANES), lambda b, g, i: (0, 0)),
        ],
        out_specs=[
            pl.BlockSpec((None, NSA_REP, tq, HEAD_DIM), lambda b, g, i: (b, g, i, 0)),
            pl.BlockSpec((None, tq, LANES), lambda b, g, i: (b * G + g, i, 0)),
        ],
        out_shape=[
            jax.ShapeDtypeStruct((B, H, S, HEAD_DIM), F32),
            jax.ShapeDtypeStruct((BG, S, LANES), BF16),
        ],
        compiler_params=_params(("parallel", "parallel", "parallel")),
        name="cmp_attn",
    )(q, kc, vc, ov)


def _slc_attn_kernel(q_ref, k_ref, v_ref, sel_ref, o_ref, m_scr, acc_scr, *, tq, tk):
    g = pl.program_id(1)
    i = pl.program_id(2)
    rows = NSA_REP * tq
    q = q_ref[...].reshape(rows, HEAD_DIM) * SCALE
    sel = sel_ref[...]
    qpos = _stacked_qpos(i, rows, tq)
    slope = _group_slope_col(g, rows, tq)
    m_scr[...] = jnp.full_like(m_scr, NEG)
    acc_scr[...] = jnp.zeros_like(acc_scr)
    blocks_per_chunk = tk // SLC_LEN

    def body(c, carry):
        ks = pl.multiple_of(c * tk, tk)
        k = k_ref[pl.ds(ks, tk), :]
        v = v_ref[pl.ds(ks, tk), :]
        s = _dot_nt(q, k)
        jrow = lax.broadcasted_iota(jnp.int32, (LANES, tk), 0)
        tcol = lax.broadcasted_iota(jnp.int32, (LANES, tk), 1)
        expand = jnp.where(jrow == c * blocks_per_chunk + tcol // SLC_LEN, 1.0, 0.0).astype(BF16)
        chosen = _dot(sel, expand)
        chosen = jnp.concatenate([chosen] * NSA_REP, axis=0)
        kpos = ks + lax.broadcasted_iota(jnp.int32, (1, tk), 1)
        dist = qpos - kpos
        ok = (chosen > 0.5) & (dist >= 0)
        s = jnp.where(ok, s - slope * dist.astype(F32), NEG)
        m_prev = m_scr[...]
        m_new = jnp.maximum(m_prev, jnp.max(s, axis=-1, keepdims=True))
        alpha = jnp.exp(m_prev - m_new)
        p = jnp.exp(s - m_new)
        acc_scr[...] = alpha * acc_scr[...] + _dot(p.astype(BF16), v)
        m_scr[...] = m_new
        return carry

    n_chunks = (i * tq + tq - 1) // tk + 1
    lax.fori_loop(0, n_chunks, body, 0)
    o, _ = _normalize(acc_scr[...])
    o_ref[...] = o.reshape(NSA_REP, tq, HEAD_DIM)


def _slc_attn(q, k, v1, sel, *, tq=128, tk=512):
    B, H, S, _ = q.shape
    G = NSA_GROUPS
    rows = NSA_REP * tq
    return pl.pallas_call(
        functools.partial(_slc_attn_kernel, tq=tq, tk=tk),
        grid=(B, G, S // tq),
        in_specs=[
            pl.BlockSpec((None, NSA_REP, tq, HEAD_DIM), lambda b, g, i: (b, g, i, 0)),
            pl.BlockSpec((None, S, HEAD_DIM), lambda b, g, i: (b * G + g, 0, 0)),
            pl.BlockSpec((None, S, LANES), lambda b, g, i: (b * G + g, 0, 0)),
            pl.BlockSpec((None, tq, LANES), lambda b, g, i: (b * G + g, i, 0)),
        ],
        out_specs=pl.BlockSpec((None, NSA_REP, tq, HEAD_DIM), lambda b, g, i: (b, g, i, 0)),
        out_shape=jax.ShapeDtypeStruct((B, H, S, HEAD_DIM), F32),
        scratch_shapes=[pltpu.VMEM((rows, 1), F32), pltpu.VMEM((rows, LANES), F32)],
        compiler_params=_params(("parallel", "parallel", "arbitrary")),
        name="slc_attn",
    )(q, k, v1, sel)


def _win_attn_kernel(q_ref, k_ref, v_ref, o_ref, *, tq):
    g = pl.program_id(1)
    i = pl.program_id(2)
    rows = NSA_REP * tq
    n_prev = NSA_WINDOW // tq
    width = (n_prev + 1) * tq
    q = q_ref[...].reshape(rows, HEAD_DIM) * SCALE
    ks = pl.multiple_of(jnp.maximum(i - n_prev, 0) * tq, tq)
    k = k_ref[pl.ds(ks, width), :]
    v = v_ref[pl.ds(ks, width), :]
    s = _dot_nt(q, k)
    kpos = ks + lax.broadcasted_iota(jnp.int32, (1, width), 1)
    dist = _stacked_qpos(i, rows, tq) - kpos
    ok = (dist >= 0) & (dist <= NSA_WINDOW - 1)
    s = jnp.where(ok, s - _group_slope_col(g, rows, tq) * dist.astype(F32), NEG)
    m = jnp.max(s, axis=-1, keepdims=True)
    p = jnp.exp(s - m)
    o, _ = _normalize(_dot(p.astype(BF16), v))
    o_ref[...] = o.reshape(NSA_REP, tq, HEAD_DIM)


def _win_attn(q, k, v1, *, tq=128):
    B, H, S, _ = q.shape
    G = NSA_GROUPS
    return pl.pallas_call(
        functools.partial(_win_attn_kernel, tq=tq),
        grid=(B, G, S // tq),
        in_specs=[
            pl.BlockSpec((None, NSA_REP, tq, HEAD_DIM), lambda b, g, i: (b, g, i, 0)),
            pl.BlockSpec((None, S, HEAD_DIM), lambda b, g, i: (b * G + g, 0, 0)),
            pl.BlockSpec((None, S, LANES), lambda b, g, i: (b * G + g, 0, 0)),
        ],
        out_specs=pl.BlockSpec((None, NSA_REP, tq, HEAD_DIM), lambda b, g, i: (b, g, i, 0)),
        out_shape=jax.ShapeDtypeStruct((B, H, S, HEAD_DIM), F32),
        compiler_params=_params(("parallel", "parallel", "arbitrary")),
        name="win_attn",
    )(q, k, v1)


def _dil_attn_kernel(q_ref, k_ref, v_ref, o_ref, lse_ref, *, tq, step, max_back, slopes):
    i = pl.program_id(1)
    width = 2 * tq
    ks = pl.multiple_of(jnp.maximum(i - 1, 0) * tq, tq)
    kpos = ks + lax.broadcasted_iota(jnp.int32, (1, width), 1)
    qpos = i * tq + lax.broadcasted_iota(jnp.int32, (tq, 1), 0)
    dist = qpos - kpos
    ok = (dist >= 0) & (dist <= max_back)
    bias = (dist * step).astype(F32)
    for h in range(DIL_HPG):
        q = q_ref[h] * SCALE
        k = k_ref[h, pl.ds(ks, width), :]
        v = v_ref[h, pl.ds(ks, width), :]
        s = jnp.where(ok, _dot_nt(q, k) - slopes[h] * bias, NEG)
        m = jnp.max(s, axis=-1, keepdims=True)
        p = jnp.exp(s - m)
        o, den = _normalize(_dot(p.astype(BF16), v))
        o_ref[h] = o
        lse_ref[h] = m + jnp.log(den)


def _dil_attn(q, k, v1, *, step, max_back, slopes, tq=128):
    N, _, L, _ = q.shape
    kern = functools.partial(_dil_attn_kernel, tq=tq, step=step, max_back=max_back, slopes=slopes)
    return pl.pallas_call(
        kern,
        grid=(N, L // tq),
        in_specs=[
            pl.BlockSpec((None, DIL_HPG, tq, HEAD_DIM), lambda n, i: (n, 0, i, 0)),
            pl.BlockSpec((None, DIL_HPG, L, HEAD_DIM), lambda n, i: (n, 0, 0, 0)),
            pl.BlockSpec((None, DIL_HPG, L, LANES), lambda n, i: (n, 0, 0, 0)),
        ],
        out_specs=[
            pl.BlockSpec((None, DIL_HPG, tq, HEAD_DIM), lambda n, i: (n, 0, i, 0)),
            pl.BlockSpec((None, DIL_HPG, tq, HEAD_DIM), lambda n, i: (n, 0, i, 0)),
        ],
        out_shape=[
            jax.ShapeDtypeStruct((N, DIL_HPG, L, HEAD_DIM), F32),
            jax.ShapeDtypeStruct((N, DIL_HPG, L, HEAD_DIM), F32),
        ],
        compiler_params=_params(("parallel", "arbitrary")),
        name="dil_attn",
    )(q, k, v1)


def _final_kernel(x_ref, pre_ref, post_ref, ocmp_ref, oslc_ref, owin_ref, gate_ref, gexp_ref,
                  od0_ref, od1_ref, od2_ref, l0_ref, l1_ref, l2_ref,
                  wmerge_ref, wnsa_ref, wdil_ref, wout_ref, o_ref):
    x = x_ref[...]
    h = _rms(x, pre_ref[...]).astype(BF16)
    merge = _dot(h, wmerge_ref[...])
    g_a = _sigmoid(merge[:, :D_MODEL])
    g_b = _sigmoid(merge[:, D_MODEL:])

    sg = _sigmoid(gate_ref[...])
    o_nsa = (_split_dot(sg, gexp_ref[0]) * ocmp_ref[...]
             + _split_dot(sg, gexp_ref[1]) * oslc_ref[...]
             + _split_dot(sg, gexp_ref[2]) * owin_ref[...])
    y_nsa = _dot(o_nsa.astype(BF16), wnsa_ref[...])

    l0, l1, l2 = l0_ref[...], l1_ref[...], l2_ref[...]
    mx = jnp.maximum(jnp.maximum(l0, l1), l2)
    e0, e1, e2 = jnp.exp(l0 - mx), jnp.exp(l1 - mx), jnp.exp(l2 - mx)
    den = e0 + e1 + e2
    o_dil = (e0 / den) * od0_ref[...] + (e1 / den) * od1_ref[...] + (e2 / den) * od2_ref[...]
    y_dil = _dot(o_dil.astype(BF16), wdil_ref[...])

    y = g_a * y_nsa + g_b * y_dil
    z = _dot(y.astype(BF16), wout_ref[...])
    o_ref[...] = x + _rms(z, post_ref[...])


def _final(x2d, pre, post, ocmp, oslc, owin, gates, gexp, od, ld, wmerge, wnsa, wdil, wout, *, tm=256):
    T = x2d.shape[0]
    tok = lambda w: pl.BlockSpec((tm, w), lambda i: (i, 0))
    full = lambda a: pl.BlockSpec(a.shape, lambda i: (0,) * a.ndim)
    return pl.pallas_call(
        _final_kernel,
        grid=(T // tm,),
        in_specs=[tok(D_MODEL), full(pre), full(post), tok(Q_NSA), tok(Q_NSA), tok(Q_NSA), tok(LANES),
                  full(gexp), tok(DIL_OUT), tok(DIL_OUT), tok(DIL_OUT), tok(DIL_OUT), tok(DIL_OUT),
                  tok(DIL_OUT), full(wmerge), full(wnsa), full(wdil), full(wout)],
        out_specs=tok(D_MODEL),
        out_shape=jax.ShapeDtypeStruct((T, D_MODEL), F32),
        compiler_params=_params(("parallel",)),
        name="final",
    )(x2d, pre, post, ocmp, oslc, owin, gates, gexp, od[0], od[1], od[2], ld[0], ld[1], ld[2],
      wmerge, wnsa, wdil, wout)


def _overlap_matrix(n_ck, n_slc):
    c0 = np.arange(n_ck)[:, None] * CMP_STRIDE
    s0 = np.arange(n_slc)[None, :] * SLC_LEN
    ov = np.clip(np.minimum(c0 + CMP_LEN, s0 + SLC_LEN) - np.maximum(c0, s0), 0, None) / CMP_LEN
    out = np.zeros((n_ck, LANES), np.float32)
    out[:, :n_slc] = ov
    out[n_ck - 1] = 0.0
    return out


def _gate_expand():
    e = np.zeros((3, LANES, Q_NSA), np.float32)
    for h in range(NSA_HEADS):
        for br in range(3):
            e[br, h * 3 + br, h * HEAD_DIM:(h + 1) * HEAD_DIM] = 1.0
    return e


def _mixer(x2d, B, S, mix_pre, mix_post, w_in, pe_k, w_ck1, w_ck2, pe_v, w_cv1, w_cv2,
           w_nsa_o, w_dil_o, w_mix_out):
    G = NSA_GROUPS
    o_q, o_kv, o_g, o_d, o_m = np.cumsum([0, Q_NSA, 6 * KV_NSA, GATE_NSA, 3 * QKV_DIL]).tolist()
    w_main = jnp.concatenate([w_in[:, o_q:o_g], w_in[:, o_d:o_m]], axis=1).astype(BF16)
    w_gate = jnp.pad(w_in[:, o_g:o_d], ((0, 0), (0, LANES - GATE_NSA))).astype(BF16)
    w_merge = w_in[:, o_m:].astype(BF16)

    P, gates = _proj(x2d, mix_pre[None, :], w_main, w_gate)

    q_n = P[:, :Q_NSA].reshape(B, S, NSA_HEADS, HEAD_DIM).transpose(0, 2, 1, 3)
    kv_n = P[:, Q_NSA:Q_NSA + 6 * KV_NSA].reshape(B, S, 6, G, HEAD_DIM).transpose(2, 0, 3, 1, 4)
    kv_n = kv_n.reshape(6, B * G, S, HEAD_DIM)
    ones = jnp.ones((B * G, S, LANES - HEAD_DIM), BF16)
    with_ones = lambda v: jnp.concatenate([v, ones], axis=-1)

    n_ck = S // CMP_STRIDE
    xc = kv_n[0:2].reshape(2, B * G, n_ck, CMP_STRIDE * HEAD_DIM)
    half = CMP_STRIDE * HEAD_DIM
    w1 = jnp.stack([w_ck1, w_cv1])
    w1cat = jnp.concatenate([w1[:, :half], w1[:, half:]], axis=2).astype(BF16)
    pe = jnp.stack([pe_k.reshape(1, -1), pe_v.reshape(1, -1)]).astype(BF16)
    w2 = jnp.stack([w_ck2, w_cv2]).astype(BF16)
    kvc = _cmp_mlp(xc, pe, w1cat, w1.astype(BF16), w2)

    ov = jnp.asarray(_overlap_matrix(n_ck, S // SLC_LEN), BF16)
    o_cmp, sel = _cmp_attn(q_n, kvc[0], kvc[1], ov)
    o_slc = _slc_attn(q_n, kv_n[2], with_ones(kv_n[3]), sel)
    o_win = _win_attn(q_n, kv_n[4], with_ones(kv_n[5]))
    to_tok = lambda o: o.transpose(0, 2, 1, 3).reshape(B * S, Q_NSA)

    qkv_d = P[:, Q_NSA + 6 * KV_NSA:].reshape(B, S, 3, len(DIL_CONFIGS), DIL_HPG, HEAD_DIM)
    od, ld = [], []
    for gi, (window, dil) in enumerate(DIL_CONFIGS):
        assert S % (dil * 128) == 0
        Ls = S // dil
        t = qkv_d[:, :, :, gi].reshape(B, Ls, dil, 3, DIL_HPG, HEAD_DIM).transpose(3, 0, 2, 4, 1, 5)
        t = t.reshape(3, B * dil, DIL_HPG, Ls, HEAD_DIM)
        v1 = jnp.concatenate([t[2], jnp.ones(t[2].shape, BF16)], axis=-1)
        o, lse = _dil_attn(t[0], t[1], v1, step=dil, max_back=window // dil,
                           slopes=DIL_SLOPES[gi * DIL_HPG:(gi + 1) * DIL_HPG])
        unstride = lambda a: a.reshape(B, dil, DIL_HPG, Ls, HEAD_DIM).transpose(0, 3, 1, 2, 4).reshape(B * S, DIL_OUT)
        od.append(unstride(o))
        ld.append(unstride(lse))

    gexp = jnp.asarray(_gate_expand(), BF16)
    return _final(x2d, mix_pre[None, :], mix_post[None, :], to_tok(o_cmp), to_tok(o_slc), to_tok(o_win),
                  gates, gexp, od, ld, w_merge, w_nsa_o.astype(BF16), w_dil_o.astype(BF16),
                  w_mix_out.astype(BF16))


def kernel(x, ffn1_pre, ffn1_post, ffn1_w_gate, ffn1_w_up, ffn1_w_down, mix_pre, mix_post, w_in, nsa_pe_k, nsa_w_ck1, nsa_w_ck2, nsa_pe_v, nsa_w_cv1, nsa_w_cv2, w_nsa_o, w_dil_o, w_mix_out, ffn2_pre, ffn2_post, ffn2_w_gate, ffn2_w_up, ffn2_w_down):
    B, S, _ = x.shape
    x2d = x.reshape(B * S, D_MODEL)
    for l in range(ffn1_pre.shape[0]):
        x2d = _ffn(x2d, ffn1_pre[l][None, :], ffn1_post[l][None, :], ffn1_w_gate[l].astype(BF16),
                   ffn1_w_up[l].astype(BF16), ffn1_w_down[l].astype(BF16))
        x2d = _mixer(x2d, B, S, mix_pre[l], mix_post[l], w_in[l], nsa_pe_k[l], nsa_w_ck1[l], nsa_w_ck2[l],
                     nsa_pe_v[l], nsa_w_cv1[l], nsa_w_cv2[l], w_nsa_o[l], w_dil_o[l], w_mix_out[l])
        x2d = _ffn(x2d, ffn2_pre[l][None, :], ffn2_post[l][None, :], ffn2_w_gate[l].astype(BF16),
                   ffn2_w_up[l].astype(BF16), ffn2_w_down[l].astype(BF16))
    return x2d.reshape(B, S, D_MODEL)
```

```python
import functools
import math

import numpy as np
import jax
import jax.numpy as jnp
from jax import lax
from jax.experimental import pallas as pl
from jax.experimental.pallas import tpu as pltpu

F32 = jnp.float32
BF16 = jnp.bfloat16

D_MODEL = 1024
HEAD_DIM = 64
RMS_EPS = 1e-6
NSA_HEADS = 8
NSA_GROUPS = 2
NSA_REP = NSA_HEADS // NSA_GROUPS
CMP_LEN = 32
CMP_STRIDE = 16
CMP_HIDDEN = 256
SLC_LEN = 64
N_SEL = 16
NSA_WINDOW = 512
DIL_CONFIGS = ((128, 1), (512, 4), (2048, 16))
DIL_HPG = 4
DIL_HEADS = DIL_HPG * len(DIL_CONFIGS)
D_FF = 2816
Q_NSA = NSA_HEADS * HEAD_DIM
KV_NSA = NSA_GROUPS * HEAD_DIM
GATE_NSA = NSA_HEADS * 3
QKV_DIL = DIL_HEADS * HEAD_DIM
DIL_OUT = DIL_HPG * HEAD_DIM
LANES = 128
NEG = -1e30
SCALE = HEAD_DIM ** -0.5
VMEM_LIMIT = 56 * 1024 * 1024
NSA_TQ = 128
VT_ROWS = 80


def _slopes(n):
    return [float(np.float32(2.0 ** (-8.0 * (i + 1) / n))) for i in range(n)]


NSA_SLOPES = _slopes(NSA_HEADS)
DIL_SLOPES = _slopes(DIL_HEADS)


def _params(sem):
    return pltpu.CompilerParams(dimension_semantics=sem, vmem_limit_bytes=VMEM_LIMIT)


def _rms(x, g):
    ms = jnp.mean(x * x, axis=-1, keepdims=True)
    return x * lax.rsqrt(ms + RMS_EPS) * g


def _sigmoid(x):
    return 1.0 / (1.0 + jnp.exp(-x))


def _dot(a, b):
    return jnp.dot(a, b, preferred_element_type=F32)


def _dot_nt(a, b):
    return lax.dot_general(a, b, (((1,), (1,)), ((), ())), preferred_element_type=F32)


def _split3(a):
    hi = a.astype(BF16)
    r1 = a - hi.astype(F32)
    mid = r1.astype(BF16)
    lo = (r1 - mid.astype(F32)).astype(BF16)
    return hi, mid, lo


def _ffn_kernel(x_ref, pre_ref, post_ref, wg_ref, wu_ref, wd_ref, o_ref, h_scr, acc_scr):
    j = pl.program_id(1)

    @pl.when(j == 0)
    def _():
        h_scr[...] = _rms(x_ref[...], pre_ref[...]).astype(BF16)
        acc_scr[...] = jnp.zeros_like(acc_scr)

    h = h_scr[...]
    g = _dot(h, wg_ref[...])
    u = _dot(h, wu_ref[...])
    a = (g * _sigmoid(g) * u).astype(BF16)
    acc_scr[...] += _dot(a, wd_ref[...])

    @pl.when(j == pl.num_programs(1) - 1)
    def _():
        o_ref[...] = x_ref[...] + 0.5 * _rms(acc_scr[...], post_ref[...])


def _ffn(x2d, pre, post, wg, wu, wd, *, tm=512, tf=1408):
    T = x2d.shape[0]
    return pl.pallas_call(
        _ffn_kernel,
        grid=(T // tm, D_FF // tf),
        in_specs=[
            pl.BlockSpec((tm, D_MODEL), lambda i, j: (i, 0)),
            pl.BlockSpec((1, D_MODEL), lambda i, j: (0, 0)),
            pl.BlockSpec((1, D_MODEL), lambda i, j: (0, 0)),
            pl.BlockSpec((D_MODEL, tf), lambda i, j: (0, j)),
            pl.BlockSpec((D_MODEL, tf), lambda i, j: (0, j)),
            pl.BlockSpec((tf, D_MODEL), lambda i, j: (j, 0)),
        ],
        out_specs=pl.BlockSpec((tm, D_MODEL), lambda i, j: (i, 0)),
        out_shape=jax.ShapeDtypeStruct((T, D_MODEL), F32),
        scratch_shapes=[pltpu.VMEM((tm, D_MODEL), BF16), pltpu.VMEM((tm, D_MODEL), F32)],
        compiler_params=_params(("parallel", "arbitrary")),
        name="ffn",
    )(x2d, pre, post, wg, wu, wd)


def _proj_kernel(x_ref, pre_ref, wq_ref, wkv_ref, wd_ref, wgate_ref, q_ref, kv_ref, d_ref, g_ref):
    h = _rms(x_ref[...], pre_ref[...]).astype(BF16)
    q_ref[...] = _dot(h, wq_ref[...]).astype(BF16)
    kv_ref[...] = _dot(h, wkv_ref[...]).astype(BF16)
    d_ref[...] = _dot(h, wd_ref[...]).astype(BF16)
    g_ref[...] = _dot(h, wgate_ref[...])


def _proj(x2d, pre, wq, wkv, wd, wgate, *, tm=256):
    T = x2d.shape[0]
    tok = lambda w: pl.BlockSpec((tm, w), lambda i: (i, 0))
    full = lambda a: pl.BlockSpec(a.shape, lambda i: (0,) * a.ndim)
    widths = [wq.shape[1], wkv.shape[1], wd.shape[1]]
    return pl.pallas_call(
        _proj_kernel,
        grid=(T // tm,),
        in_specs=[tok(D_MODEL), full(pre), full(wq), full(wkv), full(wd), full(wgate)],
        out_specs=[tok(w) for w in widths] + [tok(LANES)],
        out_shape=[jax.ShapeDtypeStruct((T, w), BF16) for w in widths]
        + [jax.ShapeDtypeStruct((T, LANES), F32)],
        compiler_params=_params(("parallel",)),
        name="proj",
    )(x2d, pre, wq, wkv, wd, wgate)


def _gelu_tanh(x):
    c = math.sqrt(2.0 / math.pi)
    return x * (0.5 * (1.0 + jnp.tanh(c * (x + 0.044715 * (x * x * x)))))


def _cmp_mlp_kernel(x_ref, pe_ref, w1cat_ref, w1_ref, w2_ref, o_ref):
    n_chunks = x_ref.shape[0]
    ab = _dot(x_ref[...], w1cat_ref[...])
    first = ab[:, :CMP_HIDDEN]
    second = pltpu.roll(ab[:, CMP_HIDDEN:], n_chunks - 1, 0)
    pe8 = jnp.broadcast_to(pe_ref[...], (8, pe_ref.shape[1]))
    bias = _dot(pe8, w1_ref[...])[0:1]
    hid = _gelu_tanh(first + second + bias).astype(BF16)
    out = _dot(hid, w2_ref[...])
    row = lax.broadcasted_iota(jnp.int32, out.shape, 0)
    o_ref[...] = jnp.where(row < n_chunks - 1, out, 0.0).astype(BF16)


def _cmp_mlp(xc, pe, w1cat, w1, w2):
    _, BG, n_chunks, width = xc.shape
    return pl.pallas_call(
        _cmp_mlp_kernel,
        grid=(2, BG),
        in_specs=[
            pl.BlockSpec((None, None, n_chunks, width), lambda t, n: (t, n, 0, 0)),
            pl.BlockSpec((None, 1, 2 * width), lambda t, n: (t, 0, 0)),
            pl.BlockSpec((None, width, 2 * CMP_HIDDEN), lambda t, n: (t, 0, 0)),
            pl.BlockSpec((None, 2 * width, CMP_HIDDEN), lambda t, n: (t, 0, 0)),
            pl.BlockSpec((None, CMP_HIDDEN, HEAD_DIM), lambda t, n: (t, 0, 0)),
        ],
        out_specs=pl.BlockSpec((None, None, n_chunks, HEAD_DIM), lambda t, n: (t, n, 0, 0)),
        out_shape=jax.ShapeDtypeStruct((2, BG, n_chunks, HEAD_DIM), BF16),
        compiler_params=_params(("parallel", "parallel")),
        name="cmp_mlp",
    )(xc, pe, w1cat, w1, w2)


def _online_update(s_t, v_t, m, acc):
    m_new = jnp.maximum(m, jnp.max(s_t, axis=0, keepdims=True))
    alpha = jnp.exp(m - m_new)
    p = jnp.exp(s_t - m_new)
    return m_new, alpha * acc + _dot(v_t, p.astype(BF16))


def _nsa_kernel(q_ref, gate_ref, kc_ref, vct_ref, ks_ref, vst_ref, kw_ref, vwt_ref,
                band_ref, cmpmask_ref, ovt_ref, o_ref, selbias_scr, *, tq, tk):
    g = pl.program_id(1)
    i = pl.program_id(2)
    rows = NSA_REP * tq
    n_ck = kc_ref.shape[0]
    n_slc = ovt_ref.shape[0]
    q0 = i * tq

    lane = lax.broadcasted_iota(jnp.int32, (1, LANES), 1)
    slabs = []
    for r in range(NSA_REP):
        sl = jnp.where(g == 0, NSA_SLOPES[r], NSA_SLOPES[NSA_REP + r])
        meta = jnp.where(lane == HEAD_DIM, SLC_LEN * sl,
                         jnp.where(lane == HEAD_DIM + 1, sl,
                                   jnp.where(lane == HEAD_DIM + 2, -sl * q0.astype(F32), 0.0)))
        slabs.append(q_ref[:, r * LANES:(r + 1) * LANES] + meta.astype(BF16))
    q_aug = jnp.concatenate(slabs, axis=0)

    s_t = _dot_nt(kc_ref[...], q_aug)
    cstart = pl.multiple_of((n_ck - 8) - (tq // CMP_STRIDE) * i, 8)
    s_t = s_t + cmpmask_ref[pl.ds(cstart, n_ck), :]
    m = jnp.max(s_t, axis=0, keepdims=True)
    any_key = m > 0.5 * NEG
    p = jnp.exp(s_t - m)
    den = jnp.sum(p, axis=0, keepdims=True)
    p = p * jnp.where(any_key, 1.0 / den, 0.0)
    o_cmp = _dot(vct_ref[0:HEAD_DIM, :], p.astype(BF16))

    psum = p[:, 0:tq]
    for r in range(1, NSA_REP):
        psum = psum + p[:, r * tq:(r + 1) * tq]
    hi, mid, lo = _split3(psum)
    ovt = ovt_ref[...]
    imp = _dot(ovt, hi) + _dot(ovt, mid) + _dot(ovt, lo)
    pos = q0 + lax.broadcasted_iota(jnp.int32, (1, tq), 1)
    own = pos // SLC_LEN
    jb = lax.broadcasted_iota(jnp.int32, (n_slc, 1), 0)
    jbf = jb.astype(F32)
    forced = (jb == 0) | (jb == own) | (jb == own - 1)
    valid = jb * SLC_LEN <= pos
    vals = jnp.where(valid & jnp.logical_not(forced), imp, -jnp.inf)
    sel = jnp.where(forced, 1.0, 0.0)
    for _ in range(N_SEL - 3):
        mx = jnp.max(vals, axis=0, keepdims=True)
        cand = jnp.where(vals == mx, jbf, 1e9)
        idx = jnp.min(cand, axis=0, keepdims=True)
        pick = (jbf == idx) & (mx > -jnp.inf)
        sel = jnp.where(pick, 1.0, sel)
        vals = jnp.where(pick, -jnp.inf, vals)
    sbias = jnp.where((sel > 0.5) & (jb < q0 // SLC_LEN), 0.0, NEG)
    selbias_scr[...] = jnp.concatenate([sbias] * NSA_REP, axis=1)

    blocks_per_chunk = tk // SLC_LEN

    def body(c, carry):
        m_run, acc = carry
        ks = pl.multiple_of(c * tk, tk)
        s_c = _dot_nt(ks_ref[pl.ds(ks, tk), :], q_aug)
        parts = []
        for jj in range(blocks_per_chunk):
            brow = selbias_scr[pl.ds(c * blocks_per_chunk + jj, 1), :]
            parts.append(s_c[jj * SLC_LEN:(jj + 1) * SLC_LEN] + brow)
        s_c = jnp.concatenate(parts, axis=0)
        return _online_update(s_c, vst_ref[:, pl.ds(ks, tk)], m_run, acc)

    m0 = jnp.full((1, rows), NEG, F32)
    acc0 = jnp.zeros((VT_ROWS, rows), F32)
    m_run, acc = lax.fori_loop(0, (q0 + tk - 1) // tk, body, (m0, acc0))
    qs = pl.multiple_of(q0, tq)
    s_d = _dot_nt(ks_ref[pl.ds(qs, tq), :], q_aug) + band_ref[NSA_WINDOW:NSA_WINDOW + tq, :]
    m_run, acc = _online_update(s_d, vst_ref[:, pl.ds(qs, tq)], m_run, acc)
    o_slc = acc[0:HEAD_DIM] * (1.0 / acc[HEAD_DIM:HEAD_DIM + 1])

    width = NSA_WINDOW + tq
    back = jnp.minimum(q0, NSA_WINDOW)
    ws = pl.multiple_of(q0 - back, tq)
    s_w = _dot_nt(kw_ref[pl.ds(ws, width), :], q_aug)
    s_w = s_w + band_ref[pl.ds(pl.multiple_of(NSA_WINDOW - back, tq), width), :]
    m_w = jnp.max(s_w, axis=0, keepdims=True)
    acc_w = _dot(vwt_ref[:, pl.ds(ws, width)], jnp.exp(s_w - m_w).astype(BF16))
    o_win = acc_w[0:HEAD_DIM] * (1.0 / acc_w[HEAD_DIM:HEAD_DIM + 1])

    gate_t = _sigmoid(gate_ref[...].T)
    outs = []
    for r in range(NSA_REP):
        cols = slice(r * tq, (r + 1) * tq)

        def gate_row(branch):
            lo_row = 3 * r + branch
            hi_row = 3 * (NSA_REP + r) + branch
            return jnp.where(g == 0, gate_t[lo_row:lo_row + 1], gate_t[hi_row:hi_row + 1])

        outs.append(gate_row(0) * o_cmp[:, cols] + gate_row(1) * o_slc[:, cols]
                    + gate_row(2) * o_win[:, cols])
    o_ref[...] = jnp.concatenate(outs, axis=0).T.astype(BF16)


def _nsa(q_pad, gates, kc, vct, ks, vst, kw, vwt, band, cmpmask, ovt, *, B, S, tq=NSA_TQ, tk=256):
    G = NSA_GROUPS
    n_ck = kc.shape[1]
    n_slc = ovt.shape[0]
    rows = NSA_REP * tq
    nq = S // tq
    per_bg = lambda shape: pl.BlockSpec((None,) + shape, lambda b, g, i: (b * G + g, 0, 0))
    full = lambda a: pl.BlockSpec(a.shape, lambda b, g, i: (0,) * a.ndim)
    return pl.pallas_call(
        functools.partial(_nsa_kernel, tq=tq, tk=tk),
        grid=(B, G, nq),
        in_specs=[
            pl.BlockSpec((tq, NSA_REP * LANES), lambda b, g, i: (b * nq + i, g)),
            pl.BlockSpec((tq, LANES), lambda b, g, i: (b * nq + i, 0)),
            per_bg((n_ck, LANES)), per_bg((VT_ROWS, n_ck)),
            per_bg((S, LANES)), per_bg((VT_ROWS, S)),
            per_bg((S, LANES)), per_bg((VT_ROWS, S)),
            full(band), full(cmpmask), full(ovt),
        ],
        out_specs=pl.BlockSpec((tq, NSA_REP * HEAD_DIM), lambda b, g, i: (b * nq + i, g)),
        out_shape=jax.ShapeDtypeStruct((B * S, Q_NSA), BF16),
        scratch_shapes=[pltpu.VMEM((n_slc, rows), F32)],
        compiler_params=_params(("parallel", "parallel", "arbitrary")),
        name="nsa",
    )(q_pad, gates, kc, vct, ks, vst, kw, vwt, band, cmpmask, ovt)


def _normalize(acc):
    den = pltpu.roll(acc, HEAD_DIM, 1)
    return (acc / den)[:, :HEAD_DIM], den[:, :HEAD_DIM]


def _dil_attn_kernel(q_ref, k_ref, v_ref, o_ref, lse_ref, *, tq, step, max_back, slopes):
    i = pl.program_id(1)
    width = 2 * tq
    ks = pl.multiple_of(jnp.maximum(i - 1, 0) * tq, tq)
    kpos = ks + lax.broadcasted_iota(jnp.int32, (1, width), 1)
    qpos = i * tq + lax.broadcasted_iota(jnp.int32, (tq, 1), 0)
    dist = qpos - kpos
    ok = (dist >= 0) & (dist <= max_back)
    bias = (dist * step).astype(F32)
    for h in range(DIL_HPG):
        q = q_ref[h] * SCALE
        k = k_ref[h, pl.ds(ks, width), :]
        v = v_ref[h, pl.ds(ks, width), :]
        s = jnp.where(ok, _dot_nt(q, k) - slopes[h] * bias, NEG)
        m = jnp.max(s, axis=-1, keepdims=True)
        p = jnp.exp(s - m)
        o, den = _normalize(_dot(p.astype(BF16), v))
        o_ref[h] = o
        lse_ref[h] = m + jnp.log(den)


def _dil_attn(q, k, v1, *, step, max_back, slopes, tq=128):
    N, _, L, _ = q.shape
    kern = functools.partial(_dil_attn_kernel, tq=tq, step=step, max_back=max_back, slopes=slopes)
    return pl.pallas_call(
        kern,
        grid=(N, L // tq),
        in_specs=[
            pl.BlockSpec((None, DIL_HPG, tq, HEAD_DIM), lambda n, i: (n, 0, i, 0)),
            pl.BlockSpec((None, DIL_HPG, L, HEAD_DIM), lambda n, i: (n, 0, 0, 0)),
            pl.BlockSpec((None, DIL_HPG, L, LANES), lambda n, i: (n, 0, 0, 0)),
        ],
        out_specs=[
            pl.BlockSpec((None, DIL_HPG, tq, HEAD_DIM), lambda n, i: (n, 0, i, 0)),
            pl.BlockSpec((None, DIL_HPG, tq, HEAD_DIM), lambda n, i: (n, 0, i, 0)),
        ],
        out_shape=[
            jax.ShapeDtypeStruct((N, DIL_HPG, L, HEAD_DIM), F32),
            jax.ShapeDtypeStruct((N, DIL_HPG, L, HEAD_DIM), F32),
        ],
        compiler_params=_params(("parallel", "arbitrary")),
        name="dil_attn",
    )(q, k, v1)


def _final_kernel(x_ref, pre_ref, post_ref, onsa_ref, od0_ref, od1_ref, od2_ref, l0_ref, l1_ref, l2_ref,
                  wmerge_ref, wnsa_ref, wdil_ref, wout_ref, o_ref):
    x = x_ref[...]
    h = _rms(x, pre_ref[...]).astype(BF16)
    merge = _dot(h, wmerge_ref[...])
    g_a = _sigmoid(merge[:, :D_MODEL])
    g_b = _sigmoid(merge[:, D_MODEL:])
    y_nsa = _dot(onsa_ref[...], wnsa_ref[...])

    l0, l1, l2 = l0_ref[...], l1_ref[...], l2_ref[...]
    mx = jnp.maximum(jnp.maximum(l0, l1), l2)
    e0, e1, e2 = jnp.exp(l0 - mx), jnp.exp(l1 - mx), jnp.exp(l2 - mx)
    den = e0 + e1 + e2
    o_dil = (e0 / den) * od0_ref[...] + (e1 / den) * od1_ref[...] + (e2 / den) * od2_ref[...]
    y_dil = _dot(o_dil.astype(BF16), wdil_ref[...])

    y = g_a * y_nsa + g_b * y_dil
    z = _dot(y.astype(BF16), wout_ref[...])
    o_ref[...] = x + _rms(z, post_ref[...])


def _final(x2d, pre, post, onsa, od, ld, wmerge, wnsa, wdil, wout, *, tm=256):
    T = x2d.shape[0]
    tok = lambda w: pl.BlockSpec((tm, w), lambda i: (i, 0))
    full = lambda a: pl.BlockSpec(a.shape, lambda i: (0,) * a.ndim)
    return pl.pallas_call(
        _final_kernel,
        grid=(T // tm,),
        in_specs=[tok(D_MODEL), full(pre), full(post), tok(Q_NSA),
                  tok(DIL_OUT), tok(DIL_OUT), tok(DIL_OUT), tok(DIL_OUT), tok(DIL_OUT), tok(DIL_OUT),
                  full(wmerge), full(wnsa), full(wdil), full(wout)],
        out_specs=tok(D_MODEL),
        out_shape=jax.ShapeDtypeStruct((T, D_MODEL), F32),
        compiler_params=_params(("parallel",)),
        name="final",
    )(x2d, pre, post, onsa, od[0], od[1], od[2], ld[0], ld[1], ld[2], wmerge, wnsa, wdil, wout)


def _overlap_t(n_ck, n_slc):
    c0 = np.arange(n_ck)[None, :] * CMP_STRIDE
    s0 = np.arange(n_slc)[:, None] * SLC_LEN
    ov = np.clip(np.minimum(c0 + CMP_LEN, s0 + SLC_LEN) - np.maximum(c0, s0), 0, None) / CMP_LEN
    ov[:, n_ck - 1] = 0.0
    return ov.astype(np.float32)


def _pos_columns(pos):
    cols = np.zeros((len(pos), LANES - HEAD_DIM), np.float32)
    cols[:, 0] = pos // SLC_LEN
    cols[:, 1] = pos % SLC_LEN
    cols[:, 2] = 1.0
    return cols


def _band_table(tq):
    u = np.arange(2 * NSA_WINDOW + tq)[:, None]
    r = (np.arange(NSA_REP * tq) % tq)[None, :]
    dist = NSA_WINDOW + r - u
    return np.where((dist >= 0) & (dist <= NSA_WINDOW - 1), 0.0, NEG).astype(np.float32)


def _cmp_mask_table(n_ck, tq):
    u = np.arange(2 * n_ck - 8)[:, None]
    r = (np.arange(NSA_REP * tq) % tq)[None, :]
    end = (u - (n_ck - 8)) * CMP_STRIDE + CMP_LEN - 1
    return np.where(end <= r, 0.0, NEG).astype(np.float32)


def _mixer(x2d, B, S, mix_pre, mix_post, w_in, pe_k, w_ck1, w_ck2, pe_v, w_cv1, w_cv2,
           w_nsa_o, w_dil_o, w_mix_out):
    G = NSA_GROUPS
    BG = B * G
    assert S % (16 * 128) == 0 and NSA_TQ // CMP_STRIDE == 8
    o_q, o_kv, o_g, o_d, o_m = np.cumsum([0, Q_NSA, 6 * KV_NSA, GATE_NSA, 3 * QKV_DIL]).tolist()
    wq = (w_in[:, o_q:o_kv] * SCALE).reshape(D_MODEL, NSA_HEADS, HEAD_DIM)
    wq = jnp.pad(wq, ((0, 0), (0, 0), (0, LANES - HEAD_DIM))).reshape(D_MODEL, NSA_HEADS * LANES)
    w_gate = jnp.pad(w_in[:, o_g:o_d], ((0, 0), (0, LANES - GATE_NSA)))
    q_pad, kv, qkvd, gates = _proj(x2d, mix_pre[None, :], wq.astype(BF16), w_in[:, o_kv:o_g].astype(BF16),
                                   w_in[:, o_d:o_m].astype(BF16), w_gate.astype(BF16))

    kv_n = kv.reshape(B, S, 6, G, HEAD_DIM).transpose(2, 0, 3, 1, 4).reshape(6, BG, S, HEAD_DIM)
    n_ck = S // CMP_STRIDE
    half = CMP_STRIDE * HEAD_DIM
    xc = kv_n[0:2].reshape(2, BG, n_ck, half)
    w1 = jnp.stack([w_ck1, w_cv1])
    w1cat = jnp.concatenate([w1[:, :half], w1[:, half:]], axis=2).astype(BF16)
    pe = jnp.stack([pe_k.reshape(1, -1), pe_v.reshape(1, -1)]).astype(BF16)
    w2 = jnp.stack([w_ck2, w_cv2]).astype(BF16)
    kvc = _cmp_mlp(xc, pe, w1cat, w1.astype(BF16), w2)

    def key_aug(k, pos):
        cols = jnp.asarray(_pos_columns(pos), BF16)
        return jnp.concatenate([k, jnp.broadcast_to(cols, (BG,) + cols.shape)], axis=-1)

    def val_t(v):
        L = v.shape[1]
        tail = np.zeros((VT_ROWS - HEAD_DIM, L), np.float32)
        tail[0] = 1.0
        tail = jnp.broadcast_to(jnp.asarray(tail, BF16), (BG,) + tail.shape)
        return jnp.concatenate([v.transpose(0, 2, 1), tail], axis=1)

    tok_pos = np.arange(S)
    cmp_pos = np.arange(n_ck) * CMP_STRIDE + CMP_LEN - 1
    band = jnp.asarray(_band_table(NSA_TQ))
    cmpmask = jnp.asarray(_cmp_mask_table(n_ck, NSA_TQ))
    ovt = jnp.asarray(_overlap_t(n_ck, S // SLC_LEN), BF16)
    o_nsa = _nsa(q_pad, gates, key_aug(kvc[0], cmp_pos), val_t(kvc[1]),
                 key_aug(kv_n[2], tok_pos), val_t(kv_n[3]), key_aug(kv_n[4], tok_pos), val_t(kv_n[5]),
                 band, cmpmask, ovt, B=B, S=S)

    qkv_d = qkvd.reshape(B, S, 3, len(DIL_CONFIGS), DIL_HPG, HEAD_DIM)
    od, ld = [], []
    for gi, (window, dil) in enumerate(DIL_CONFIGS):
        Ls = S // dil
        t = qkv_d[:, :, :, gi].reshape(B, Ls, dil, 3, DIL_HPG, HEAD_DIM).transpose(3, 0, 2, 4, 1, 5)
        t = t.reshape(3, B * dil, DIL_HPG, Ls, HEAD_DIM)
        v1 = jnp.concatenate([t[2], jnp.ones(t[2].shape, BF16)], axis=-1)
        o, lse = _dil_attn(t[0], t[1], v1, step=dil, max_back=window // dil,
                           slopes=DIL_SLOPES[gi * DIL_HPG:(gi + 1) * DIL_HPG])
        unstride = lambda a: a.reshape(B, dil, DIL_HPG, Ls, HEAD_DIM).transpose(0, 3, 1, 2, 4).reshape(B * S, DIL_OUT)
        od.append(unstride(o))
        ld.append(unstride(lse))

    return _final(x2d, mix_pre[None, :], mix_post[None, :], o_nsa, od, ld, w_in[:, o_m:].astype(BF16),
                  w_nsa_o.astype(BF16), w_dil_o.astype(BF16), w_mix_out.astype(BF16))


def kernel(x, ffn1_pre, ffn1_post, ffn1_w_gate, ffn1_w_up, ffn1_w_down, mix_pre, mix_post, w_in, nsa_pe_k, nsa_w_ck1, nsa_w_ck2, nsa_pe_v, nsa_w_cv1, nsa_w_cv2, w_nsa_o, w_dil_o, w_mix_out, ffn2_pre, ffn2_post, ffn2_w_gate, ffn2_w_up, ffn2_w_down):
    B, S, _ = x.shape
    x2d = x.reshape(B * S, D_MODEL)
    for l in range(ffn1_pre.shape[0]):
        x2d = _ffn(x2d, ffn1_pre[l][None, :], ffn1_post[l][None, :], ffn1_w_gate[l].astype(BF16),
                   ffn1_w_up[l].astype(BF16), ffn1_w_down[l].astype(BF16))
        x2d = _mixer(x2d, B, S, mix_pre[l], mix_post[l], w_in[l], nsa_pe_k[l], nsa_w_ck1[l], nsa_w_ck2[l],
                     nsa_pe_v[l], nsa_w_cv1[l], nsa_w_cv2[l], w_nsa_o[l], w_dil_o[l], w_mix_out[l])
        x2d = _ffn(x2d, ffn2_pre[l][None, :], ffn2_post[l][None, :], ffn2_w_gate[l].astype(BF16),
                   ffn2_w_up[l].astype(BF16), ffn2_w_down[l].astype(BF16))
    return x2d.reshape(B, S, D_MODEL)
```

```python
import functools
import math

import numpy as np
import jax
import jax.numpy as jnp
from jax import lax
from jax.experimental import pallas as pl
from jax.experimental.pallas import tpu as pltpu

F32 = jnp.float32
BF16 = jnp.bfloat16

D_MODEL = 1024
HEAD_DIM = 64
RMS_EPS = 1e-6
NSA_HEADS = 8
NSA_GROUPS = 2
NSA_REP = NSA_HEADS // NSA_GROUPS
CMP_LEN = 32
CMP_STRIDE = 16
CMP_HIDDEN = 256
SLC_LEN = 64
N_SEL = 16
NSA_WINDOW = 512
DIL_CONFIGS = ((128, 1), (512, 4), (2048, 16))
DIL_HPG = 4
DIL_HEADS = DIL_HPG * len(DIL_CONFIGS)
D_FF = 2816
Q_NSA = NSA_HEADS * HEAD_DIM
KV_NSA = NSA_GROUPS * HEAD_DIM
GATE_NSA = NSA_HEADS * 3
QKV_DIL = DIL_HEADS * HEAD_DIM
DIL_OUT = DIL_HPG * HEAD_DIM
DIL_QKV = 3 * DIL_OUT
DIL_TQ = 256
LANES = 128
NEG = -1e30
SCALE = HEAD_DIM ** -0.5
VMEM_LIMIT = 56 * 1024 * 1024
NSA_TQ = 128
VT_ROWS = 80


def _slopes(n):
    return [float(np.float32(2.0 ** (-8.0 * (i + 1) / n))) for i in range(n)]


NSA_SLOPES = _slopes(NSA_HEADS)
DIL_SLOPES = _slopes(DIL_HEADS)


def _params(sem):
    return pltpu.CompilerParams(dimension_semantics=sem, vmem_limit_bytes=VMEM_LIMIT)


def _rms(x, g):
    ms = jnp.mean(x * x, axis=-1, keepdims=True)
    return x * lax.rsqrt(ms + RMS_EPS) * g


def _sigmoid(x):
    return 1.0 / (1.0 + jnp.exp(-x))


def _dot(a, b):
    return jnp.dot(a, b, preferred_element_type=F32)


def _dot_nt(a, b):
    return lax.dot_general(a, b, (((1,), (1,)), ((), ())), preferred_element_type=F32)


def _split3(a):
    hi = a.astype(BF16)
    r1 = a - hi.astype(F32)
    mid = r1.astype(BF16)
    lo = (r1 - mid.astype(F32)).astype(BF16)
    return hi, mid, lo


def _ffn_kernel(x_ref, pre_ref, post_ref, wg_ref, wu_ref, wd_ref, o_ref, h_scr, acc_scr):
    j = pl.program_id(1)

    @pl.when(j == 0)
    def _():
        h_scr[...] = _rms(x_ref[...], pre_ref[...]).astype(BF16)
        acc_scr[...] = jnp.zeros_like(acc_scr)

    h = h_scr[...]
    g = _dot(h, wg_ref[...])
    u = _dot(h, wu_ref[...])
    a = (g * _sigmoid(g) * u).astype(BF16)
    acc_scr[...] += _dot(a, wd_ref[...])

    @pl.when(j == pl.num_programs(1) - 1)
    def _():
        o_ref[...] = x_ref[...] + 0.5 * _rms(acc_scr[...], post_ref[...])


def _ffn(x2d, pre, post, wg, wu, wd, *, tm=512, tf=1408):
    T = x2d.shape[0]
    return pl.pallas_call(
        _ffn_kernel,
        grid=(T // tm, D_FF // tf),
        in_specs=[
            pl.BlockSpec((tm, D_MODEL), lambda i, j: (i, 0)),
            pl.BlockSpec((1, D_MODEL), lambda i, j: (0, 0)),
            pl.BlockSpec((1, D_MODEL), lambda i, j: (0, 0)),
            pl.BlockSpec((D_MODEL, tf), lambda i, j: (0, j)),
            pl.BlockSpec((D_MODEL, tf), lambda i, j: (0, j)),
            pl.BlockSpec((tf, D_MODEL), lambda i, j: (j, 0)),
        ],
        out_specs=pl.BlockSpec((tm, D_MODEL), lambda i, j: (i, 0)),
        out_shape=jax.ShapeDtypeStruct((T, D_MODEL), F32),
        scratch_shapes=[pltpu.VMEM((tm, D_MODEL), BF16), pltpu.VMEM((tm, D_MODEL), F32)],
        compiler_params=_params(("parallel", "arbitrary")),
        name="ffn",
    )(x2d, pre, post, wg, wu, wd)


def _proj_kernel(x_ref, pre_ref, wq_ref, wkv_ref, wd_ref, wgate_ref, q_ref, kv_ref, g_ref, *rest):
    d_refs, scr = rest[:-1], rest[-1]
    tm = x_ref.shape[0]
    h = _rms(x_ref[...], pre_ref[...]).astype(BF16)
    q_ref[...] = _dot(h, wq_ref[...]).astype(BF16)
    kv_ref[...] = _dot(h, wkv_ref[...]).astype(BF16)
    g_ref[...] = _dot(h, wgate_ref[...])
    n_blk = DIL_QKV // LANES
    for gi, (_, dil) in enumerate(DIL_CONFIGS):
        res = _dot(h, wd_ref[:, gi * DIL_QKV:(gi + 1) * DIL_QKV])
        if dil == 1:
            d_refs[gi][0] = res.astype(BF16)
            continue
        for c in range(n_blk):
            scr[c] = res[:, c * LANES:(c + 1) * LANES]
        for r in range(dil):
            for c in range(n_blk):
                rows = scr[c, pl.ds(r, tm // dil, stride=dil), :]
                d_refs[gi][r, :, c * LANES:(c + 1) * LANES] = rows.astype(BF16)


def _proj(x2d, pre, wq, wkv, wd, wgate, *, B, S, tm=256):
    T = x2d.shape[0]
    nt = S // tm
    tok = lambda w: pl.BlockSpec((tm, w), lambda i: (i, 0))
    full = lambda a: pl.BlockSpec(a.shape, lambda i: (0,) * a.ndim)
    d_specs = [pl.BlockSpec((None, dil, tm // dil, DIL_QKV), lambda i: (i // nt, 0, i % nt, 0))
               for _, dil in DIL_CONFIGS]
    d_shapes = [jax.ShapeDtypeStruct((B, dil, S // dil, DIL_QKV), BF16) for _, dil in DIL_CONFIGS]
    return pl.pallas_call(
        _proj_kernel,
        grid=(T // tm,),
        in_specs=[tok(D_MODEL), full(pre), full(wq), full(wkv), full(wd), full(wgate)],
        out_specs=[tok(wq.shape[1]), tok(wkv.shape[1]), tok(LANES)] + d_specs,
        out_shape=[jax.ShapeDtypeStruct((T, wq.shape[1]), BF16), jax.ShapeDtypeStruct((T, wkv.shape[1]), BF16),
                   jax.ShapeDtypeStruct((T, LANES), F32)] + d_shapes,
        scratch_shapes=[pltpu.VMEM((DIL_QKV // LANES, tm, LANES), F32)],
        compiler_params=_params(("parallel",)),
        name="proj",
    )(x2d, pre, wq, wkv, wd, wgate)


def _gelu_tanh(x):
    c = math.sqrt(2.0 / math.pi)
    return x * (0.5 * (1.0 + jnp.tanh(c * (x + 0.044715 * (x * x * x)))))


def _cmp_mlp_kernel(x_ref, pe_ref, w1cat_ref, w1_ref, w2_ref, o_ref):
    n_chunks = x_ref.shape[0]
    ab = _dot(x_ref[...], w1cat_ref[...])
    first = ab[:, :CMP_HIDDEN]
    second = pltpu.roll(ab[:, CMP_HIDDEN:], n_chunks - 1, 0)
    pe8 = jnp.broadcast_to(pe_ref[...], (8, pe_ref.shape[1]))
    bias = _dot(pe8, w1_ref[...])[0:1]
    hid = _gelu_tanh(first + second + bias).astype(BF16)
    out = _dot(hid, w2_ref[...])
    row = lax.broadcasted_iota(jnp.int32, out.shape, 0)
    o_ref[...] = jnp.where(row < n_chunks - 1, out, 0.0).astype(BF16)


def _cmp_mlp(xc, pe, w1cat, w1, w2):
    _, BG, n_chunks, width = xc.shape
    return pl.pallas_call(
        _cmp_mlp_kernel,
        grid=(2, BG),
        in_specs=[
            pl.BlockSpec((None, None, n_chunks, width), lambda t, n: (t, n, 0, 0)),
            pl.BlockSpec((None, 1, 2 * width), lambda t, n: (t, 0, 0)),
            pl.BlockSpec((None, width, 2 * CMP_HIDDEN), lambda t, n: (t, 0, 0)),
            pl.BlockSpec((None, 2 * width, CMP_HIDDEN), lambda t, n: (t, 0, 0)),
            pl.BlockSpec((None, CMP_HIDDEN, HEAD_DIM), lambda t, n: (t, 0, 0)),
        ],
        out_specs=pl.BlockSpec((None, None, n_chunks, HEAD_DIM), lambda t, n: (t, n, 0, 0)),
        out_shape=jax.ShapeDtypeStruct((2, BG, n_chunks, HEAD_DIM), BF16),
        compiler_params=_params(("parallel", "parallel")),
        name="cmp_mlp",
    )(xc, pe, w1cat, w1, w2)


def _online_update(s_t, v_t, m, acc):
    m_new = jnp.maximum(m, jnp.max(s_t, axis=0, keepdims=True))
    alpha = jnp.exp(m - m_new)
    p = jnp.exp(s_t - m_new)
    return m_new, alpha * acc + _dot(v_t, p.astype(BF16))


def _nsa_kernel(q_ref, gate_ref, kc_ref, vct_ref, ks_ref, vst_ref, kw_ref, vwt_ref,
                band_ref, cmpmask_ref, ovt_ref, o_ref, selbias_scr, *, tq, tk):
    g = pl.program_id(1)
    i = pl.program_id(2)
    rows = NSA_REP * tq
    n_ck = kc_ref.shape[0]
    n_slc = ovt_ref.shape[0]
    q0 = i * tq

    lane = lax.broadcasted_iota(jnp.int32, (1, LANES), 1)
    slabs = []
    for r in range(NSA_REP):
        sl = jnp.where(g == 0, NSA_SLOPES[r], NSA_SLOPES[NSA_REP + r])
        meta = jnp.where(lane == HEAD_DIM, SLC_LEN * sl,
                         jnp.where(lane == HEAD_DIM + 1, sl,
                                   jnp.where(lane == HEAD_DIM + 2, -sl * q0.astype(F32), 0.0)))
        slabs.append(q_ref[:, r * LANES:(r + 1) * LANES] + meta.astype(BF16))
    q_aug = jnp.concatenate(slabs, axis=0)

    s_t = _dot_nt(kc_ref[...], q_aug)
    cstart = pl.multiple_of((n_ck - 8) - (tq // CMP_STRIDE) * i, 8)
    s_t = s_t + cmpmask_ref[pl.ds(cstart, n_ck), :]
    m = jnp.max(s_t, axis=0, keepdims=True)
    any_key = m > 0.5 * NEG
    p = jnp.exp(s_t - m)
    den = jnp.sum(p, axis=0, keepdims=True)
    p = p * jnp.where(any_key, 1.0 / den, 0.0)
    o_cmp = _dot(vct_ref[0:HEAD_DIM, :], p.astype(BF16))

    psum = p[:, 0:tq]
    for r in range(1, NSA_REP):
        psum = psum + p[:, r * tq:(r + 1) * tq]
    hi, mid, lo = _split3(psum)
    ovt = ovt_ref[...]
    imp = _dot(ovt, hi) + _dot(ovt, mid) + _dot(ovt, lo)
    pos = q0 + lax.broadcasted_iota(jnp.int32, (1, tq), 1)
    own = pos // SLC_LEN
    jb = lax.broadcasted_iota(jnp.int32, (n_slc, 1), 0)
    jbf = jb.astype(F32)
    forced = (jb == 0) | (jb == own) | (jb == own - 1)
    valid = jb * SLC_LEN <= pos
    vals = jnp.where(valid & jnp.logical_not(forced), imp, -jnp.inf)
    sel = jnp.where(forced, 1.0, 0.0)
    for _ in range(N_SEL - 3):
        mx = jnp.max(vals, axis=0, keepdims=True)
        cand = jnp.where(vals == mx, jbf, 1e9)
        idx = jnp.min(cand, axis=0, keepdims=True)
        pick = (jbf == idx) & (mx > -jnp.inf)
        sel = jnp.where(pick, 1.0, sel)
        vals = jnp.where(pick, -jnp.inf, vals)
    sbias = jnp.where((sel > 0.5) & (jb < q0 // SLC_LEN), 0.0, NEG)
    selbias_scr[...] = jnp.concatenate([sbias] * NSA_REP, axis=1)

    blocks_per_chunk = tk // SLC_LEN

    def body(c, carry):
        m_run, acc = carry
        ks = pl.multiple_of(c * tk, tk)
        s_c = _dot_nt(ks_ref[pl.ds(ks, tk), :], q_aug)
        parts = []
        for jj in range(blocks_per_chunk):
            brow = selbias_scr[pl.ds(c * blocks_per_chunk + jj, 1), :]
            parts.append(s_c[jj * SLC_LEN:(jj + 1) * SLC_LEN] + brow)
        s_c = jnp.concatenate(parts, axis=0)
        return _online_update(s_c, vst_ref[:, pl.ds(ks, tk)], m_run, acc)

    m0 = jnp.full((1, rows), NEG, F32)
    acc0 = jnp.zeros((VT_ROWS, rows), F32)
    m_run, acc = lax.fori_loop(0, (q0 + tk - 1) // tk, body, (m0, acc0))
    qs = pl.multiple_of(q0, tq)
    s_d = _dot_nt(ks_ref[pl.ds(qs, tq), :], q_aug) + band_ref[NSA_WINDOW:NSA_WINDOW + tq, :]
    m_run, acc = _online_update(s_d, vst_ref[:, pl.ds(qs, tq)], m_run, acc)
    o_slc = acc[0:HEAD_DIM] * (1.0 / acc[HEAD_DIM:HEAD_DIM + 1])

    width = NSA_WINDOW + tq
    back = jnp.minimum(q0, NSA_WINDOW)
    ws = pl.multiple_of(q0 - back, tq)
    s_w = _dot_nt(kw_ref[pl.ds(ws, width), :], q_aug)
    s_w = s_w + band_ref[pl.ds(pl.multiple_of(NSA_WINDOW - back, tq), width), :]
    m_w = jnp.max(s_w, axis=0, keepdims=True)
    acc_w = _dot(vwt_ref[:, pl.ds(ws, width)], jnp.exp(s_w - m_w).astype(BF16))
    o_win = acc_w[0:HEAD_DIM] * (1.0 / acc_w[HEAD_DIM:HEAD_DIM + 1])

    gate_t = _sigmoid(gate_ref[...].T)
    outs = []
    for r in range(NSA_REP):
        cols = slice(r * tq, (r + 1) * tq)

        def gate_row(branch):
            lo_row = 3 * r + branch
            hi_row = 3 * (NSA_REP + r) + branch
            return jnp.where(g == 0, gate_t[lo_row:lo_row + 1], gate_t[hi_row:hi_row + 1])

        outs.append(gate_row(0) * o_cmp[:, cols] + gate_row(1) * o_slc[:, cols]
                    + gate_row(2) * o_win[:, cols])
    o_ref[...] = jnp.concatenate(outs, axis=0).T.astype(BF16)


def _nsa(q_pad, gates, kc, vct, ks, vst, kw, vwt, band, cmpmask, ovt, *, B, S, tq=NSA_TQ, tk=256):
    G = NSA_GROUPS
    n_ck = kc.shape[1]
    n_slc = ovt.shape[0]
    rows = NSA_REP * tq
    nq = S // tq
    per_bg = lambda shape: pl.BlockSpec((None,) + shape, lambda b, g, i: (b * G + g, 0, 0))
    full = lambda a: pl.BlockSpec(a.shape, lambda b, g, i: (0,) * a.ndim)
    return pl.pallas_call(
        functools.partial(_nsa_kernel, tq=tq, tk=tk),
        grid=(B, G, nq),
        in_specs=[
            pl.BlockSpec((tq, NSA_REP * LANES), lambda b, g, i: (b * nq + i, g)),
            pl.BlockSpec((tq, LANES), lambda b, g, i: (b * nq + i, 0)),
            per_bg((n_ck, LANES)), per_bg((VT_ROWS, n_ck)),
            per_bg((S, LANES)), per_bg((VT_ROWS, S)),
            per_bg((S, LANES)), per_bg((VT_ROWS, S)),
            full(band), full(cmpmask), full(ovt),
        ],
        out_specs=pl.BlockSpec((tq, NSA_REP * HEAD_DIM), lambda b, g, i: (b * nq + i, g)),
        out_shape=jax.ShapeDtypeStruct((B * S, Q_NSA), BF16),
        scratch_shapes=[pltpu.VMEM((n_slc, rows), F32)],
        compiler_params=_params(("parallel", "parallel", "arbitrary")),
        name="nsa",
    )(q_pad, gates, kc, vct, ks, vst, kw, vwt, band, cmpmask, ovt)


def _dil_attn_kernel(q_ref, k_ref, v_ref, dist_ref, o_ref, lse_ref, vt_scr, *, tq, width, max_back, slopes):
    i = pl.program_id(1)
    L = k_ref.shape[0]
    t_chunk = min(L, 512)

    @pl.when(i == 0)
    def _():
        for pair in range(DIL_OUT // LANES):
            for c in range(L // t_chunk):
                v = v_ref[c * t_chunk:(c + 1) * t_chunk, pair * LANES:(pair + 1) * LANES].astype(F32)
                vt_scr[pair * LANES:(pair + 1) * LANES, c * t_chunk:(c + 1) * t_chunk] = v.T.astype(BF16)

    back = jnp.minimum(i * tq, max_back)
    ks = pl.multiple_of(i * tq - back, max_back)
    dist = dist_ref[pl.ds(pl.multiple_of(max_back - back, max_back), width), :]
    lane = lax.broadcasted_iota(jnp.int32, (1, LANES), 1)
    outs, lses = [], []
    for h in range(DIL_HPG):
        pair, half = divmod(h, 2)
        in_head = (lane >= half * HEAD_DIM) & (lane < (half + 1) * HEAD_DIM)
        q = q_ref[:, pair * LANES:(pair + 1) * LANES]
        q = jnp.where(in_head, q, jnp.zeros_like(q))
        k = k_ref[pl.ds(ks, width), pair * LANES:(pair + 1) * LANES]
        s_t = _dot_nt(k, q) - slopes[h] * dist
        m = jnp.max(s_t, axis=0, keepdims=True)
        p = jnp.exp(s_t - m)
        den = jnp.sum(p, axis=0, keepdims=True)
        o_t = _dot(vt_scr[h * HEAD_DIM:(h + 1) * HEAD_DIM, pl.ds(ks, width)], p.astype(BF16))
        outs.append(o_t * (1.0 / den))
        lses.append(jnp.broadcast_to(m + jnp.log(den), (HEAD_DIM, tq)))
    o_ref[...] = jnp.concatenate(outs, axis=0).T
    lse_ref[...] = jnp.concatenate(lses, axis=0).T


def _dil_attn(qkv, dist_tbl, *, max_back, slopes, tq):
    N, L, _ = qkv.shape
    width = dist_tbl.shape[0] - max_back
    kern = functools.partial(_dil_attn_kernel, tq=tq, width=width, max_back=max_back, slopes=slopes)
    return pl.pallas_call(
        kern,
        grid=(N, L // tq),
        in_specs=[
            pl.BlockSpec((None, tq, DIL_OUT), lambda n, i: (n, i, 0)),
            pl.BlockSpec((None, L, DIL_OUT), lambda n, i: (n, 0, 1)),
            pl.BlockSpec((None, L, DIL_OUT), lambda n, i: (n, 0, 2)),
            pl.BlockSpec(dist_tbl.shape, lambda n, i: (0, 0)),
        ],
        out_specs=[
            pl.BlockSpec((None, tq, DIL_OUT), lambda n, i: (n, i, 0)),
            pl.BlockSpec((None, tq, DIL_OUT), lambda n, i: (n, i, 0)),
        ],
        out_shape=[
            jax.ShapeDtypeStruct((N, L, DIL_OUT), F32),
            jax.ShapeDtypeStruct((N, L, DIL_OUT), F32),
        ],
        scratch_shapes=[pltpu.VMEM((DIL_OUT, L), BF16)],
        compiler_params=_params(("parallel", "arbitrary")),
        name="dil_attn",
    )(qkv, qkv, qkv, dist_tbl)


def _final_kernel(x_ref, pre_ref, post_ref, onsa_ref, od0_ref, od1_ref, od2_ref, l0_ref, l1_ref, l2_ref,
                  wmerge_ref, wnsa_ref, wdil_ref, wout_ref, o_ref, scr):
    tm = x_ref.shape[0]
    x = x_ref[...]
    h = _rms(x, pre_ref[...]).astype(BF16)
    merge = _dot(h, wmerge_ref[...])
    g_a = _sigmoid(merge[:, :D_MODEL])
    g_b = _sigmoid(merge[:, D_MODEL:])
    y_nsa = _dot(onsa_ref[...], wnsa_ref[...])

    n_blk = DIL_OUT // LANES
    slot = 0
    token_major = []
    for (_, dil), o_in, l_in in zip(DIL_CONFIGS, (od0_ref, od1_ref, od2_ref), (l0_ref, l1_ref, l2_ref)):
        pair = []
        for src in (o_in, l_in):
            if dil == 1:
                pair.append(src[0])
                continue
            for r in range(dil):
                for c in range(n_blk):
                    scr[slot + c, pl.ds(r, tm // dil, stride=dil), :] = src[r, :, c * LANES:(c + 1) * LANES]
            pair.append(jnp.concatenate([scr[slot + c] for c in range(n_blk)], axis=1))
            slot += n_blk
        token_major.append(pair)
    (o0, l0), (o1, l1), (o2, l2) = token_major
    mx = jnp.maximum(jnp.maximum(l0, l1), l2)
    e0, e1, e2 = jnp.exp(l0 - mx), jnp.exp(l1 - mx), jnp.exp(l2 - mx)
    den = e0 + e1 + e2
    o_dil = (e0 / den) * o0 + (e1 / den) * o1 + (e2 / den) * o2
    y_dil = _dot(o_dil.astype(BF16), wdil_ref[...])

    y = g_a * y_nsa + g_b * y_dil
    z = _dot(y.astype(BF16), wout_ref[...])
    o_ref[...] = x + _rms(z, post_ref[...])


def _final(x2d, pre, post, onsa, od, ld, wmerge, wnsa, wdil, wout, *, S, tm=256):
    T = x2d.shape[0]
    nt = S // tm
    tok = lambda w: pl.BlockSpec((tm, w), lambda i: (i, 0))
    full = lambda a: pl.BlockSpec(a.shape, lambda i: (0,) * a.ndim)
    strided = [pl.BlockSpec((None, dil, tm // dil, DIL_OUT), lambda i: (i // nt, 0, i % nt, 0))
               for _, dil in DIL_CONFIGS]
    n_strided = sum(1 for _, dil in DIL_CONFIGS if dil > 1)
    return pl.pallas_call(
        _final_kernel,
        grid=(T // tm,),
        in_specs=[tok(D_MODEL), full(pre), full(post), tok(Q_NSA)] + strided + strided
        + [full(wmerge), full(wnsa), full(wdil), full(wout)],
        out_specs=tok(D_MODEL),
        out_shape=jax.ShapeDtypeStruct((T, D_MODEL), F32),
        scratch_shapes=[pltpu.VMEM((2 * n_strided * (DIL_OUT // LANES), tm, LANES), F32)],
        compiler_params=_params(("parallel",)),
        name="final",
    )(x2d, pre, post, onsa, od[0], od[1], od[2], ld[0], ld[1], ld[2], wmerge, wnsa, wdil, wout)


def _overlap_t(n_ck, n_slc):
    c0 = np.arange(n_ck)[None, :] * CMP_STRIDE
    s0 = np.arange(n_slc)[:, None] * SLC_LEN
    ov = np.clip(np.minimum(c0 + CMP_LEN, s0 + SLC_LEN) - np.maximum(c0, s0), 0, None) / CMP_LEN
    ov[:, n_ck - 1] = 0.0
    return ov.astype(np.float32)


def _pos_columns(pos):
    cols = np.zeros((len(pos), LANES - HEAD_DIM), np.float32)
    cols[:, 0] = pos // SLC_LEN
    cols[:, 1] = pos % SLC_LEN
    cols[:, 2] = 1.0
    return cols


def _band_table(tq):
    u = np.arange(2 * NSA_WINDOW + tq)[:, None]
    r = (np.arange(NSA_REP * tq) % tq)[None, :]
    dist = NSA_WINDOW + r - u
    return np.where((dist >= 0) & (dist <= NSA_WINDOW - 1), 0.0, NEG).astype(np.float32)


def _dil_dist_table(tq, width, max_back, dil):
    u = np.arange(width + max_back)[:, None]
    r = np.arange(tq)[None, :]
    dist = max_back + r - u
    return np.where((dist >= 0) & (dist <= max_back), dist * dil, -NEG).astype(np.float32)


def _cmp_mask_table(n_ck, tq):
    u = np.arange(2 * n_ck - 8)[:, None]
    r = (np.arange(NSA_REP * tq) % tq)[None, :]
    end = (u - (n_ck - 8)) * CMP_STRIDE + CMP_LEN - 1
    return np.where(end <= r, 0.0, NEG).astype(np.float32)


def _mixer(x2d, B, S, mix_pre, mix_post, w_in, pe_k, w_ck1, w_ck2, pe_v, w_cv1, w_cv2,
           w_nsa_o, w_dil_o, w_mix_out):
    G = NSA_GROUPS
    BG = B * G
    assert S % (16 * 128) == 0 and NSA_TQ // CMP_STRIDE == 8
    o_q, o_kv, o_g, o_d, o_m = np.cumsum([0, Q_NSA, 6 * KV_NSA, GATE_NSA, 3 * QKV_DIL]).tolist()
    wq = (w_in[:, o_q:o_kv] * SCALE).reshape(D_MODEL, NSA_HEADS, HEAD_DIM)
    wq = jnp.pad(wq, ((0, 0), (0, 0), (0, LANES - HEAD_DIM))).reshape(D_MODEL, NSA_HEADS * LANES)
    w_gate = jnp.pad(w_in[:, o_g:o_d], ((0, 0), (0, LANES - GATE_NSA)))
    wd = w_in[:, o_d:o_m].reshape(D_MODEL, 3, len(DIL_CONFIGS), DIL_OUT)
    wd = (wd * jnp.asarray([SCALE, 1.0, 1.0], F32)[None, :, None, None]).transpose(0, 2, 1, 3)
    wd = wd.reshape(D_MODEL, 3 * QKV_DIL)
    q_pad, kv, gates, *qkv_d = _proj(x2d, mix_pre[None, :], wq.astype(BF16), w_in[:, o_kv:o_g].astype(BF16),
                                     wd.astype(BF16), w_gate.astype(BF16), B=B, S=S)

    kv_n = kv.reshape(B, S, 6, G, HEAD_DIM).transpose(2, 0, 3, 1, 4).reshape(6, BG, S, HEAD_DIM)
    n_ck = S // CMP_STRIDE
    half = CMP_STRIDE * HEAD_DIM
    xc = kv_n[0:2].reshape(2, BG, n_ck, half)
    w1 = jnp.stack([w_ck1, w_cv1])
    w1cat = jnp.concatenate([w1[:, :half], w1[:, half:]], axis=2).astype(BF16)
    pe = jnp.stack([pe_k.reshape(1, -1), pe_v.reshape(1, -1)]).astype(BF16)
    w2 = jnp.stack([w_ck2, w_cv2]).astype(BF16)
    kvc = _cmp_mlp(xc, pe, w1cat, w1.astype(BF16), w2)

    def key_aug(k, pos):
        cols = jnp.asarray(_pos_columns(pos), BF16)
        return jnp.concatenate([k, jnp.broadcast_to(cols, (BG,) + cols.shape)], axis=-1)

    def val_t(v):
        L = v.shape[1]
        tail = np.zeros((VT_ROWS - HEAD_DIM, L), np.float32)
        tail[0] = 1.0
        tail = jnp.broadcast_to(jnp.asarray(tail, BF16), (BG,) + tail.shape)
        return jnp.concatenate([v.transpose(0, 2, 1), tail], axis=1)

    tok_pos = np.arange(S)
    cmp_pos = np.arange(n_ck) * CMP_STRIDE + CMP_LEN - 1
    band = jnp.asarray(_band_table(NSA_TQ))
    cmpmask = jnp.asarray(_cmp_mask_table(n_ck, NSA_TQ))
    ovt = jnp.asarray(_overlap_t(n_ck, S // SLC_LEN), BF16)
    o_nsa = _nsa(q_pad, gates, key_aug(kvc[0], cmp_pos), val_t(kvc[1]),
                 key_aug(kv_n[2], tok_pos), val_t(kv_n[3]), key_aug(kv_n[4], tok_pos), val_t(kv_n[5]),
                 band, cmpmask, ovt, B=B, S=S)

    od, ld = [], []
    for gi, (window, dil) in enumerate(DIL_CONFIGS):
        Ls = S // dil
        max_back = window // dil
        tq = min(DIL_TQ, Ls)
        width = tq + max_back if Ls > tq else tq
        o, lse = _dil_attn(qkv_d[gi].reshape(B * dil, Ls, DIL_QKV),
                           jnp.asarray(_dil_dist_table(tq, width, max_back, dil)),
                           max_back=max_back, slopes=DIL_SLOPES[gi * DIL_HPG:(gi + 1) * DIL_HPG], tq=tq)
        od.append(o.reshape(B, dil, Ls, DIL_OUT))
        ld.append(lse.reshape(B, dil, Ls, DIL_OUT))

    return _final(x2d, mix_pre[None, :], mix_post[None, :], o_nsa, od, ld, w_in[:, o_m:].astype(BF16),
                  w_nsa_o.astype(BF16), w_dil_o.astype(BF16), w_mix_out.astype(BF16), S=S)


def kernel(x, ffn1_pre, ffn1_post, ffn1_w_gate, ffn1_w_up, ffn1_w_down, mix_pre, mix_post, w_in, nsa_pe_k, nsa_w_ck1, nsa_w_ck2, nsa_pe_v, nsa_w_cv1, nsa_w_cv2, w_nsa_o, w_dil_o, w_mix_out, ffn2_pre, ffn2_post, ffn2_w_gate, ffn2_w_up, ffn2_w_down):
    B, S, _ = x.shape
    x2d = x.reshape(B * S, D_MODEL)
    for l in range(ffn1_pre.shape[0]):
        x2d = _ffn(x2d, ffn1_pre[l][None, :], ffn1_post[l][None, :], ffn1_w_gate[l].astype(BF16),
                   ffn1_w_up[l].astype(BF16), ffn1_w_down[l].astype(BF16))
        x2d = _mixer(x2d, B, S, mix_pre[l], mix_post[l], w_in[l], nsa_pe_k[l], nsa_w_ck1[l], nsa_w_ck2[l],
                     nsa_pe_v[l], nsa_w_cv1[l], nsa_w_cv2[l], w_nsa_o[l], w_dil_o[l], w_mix_out[l])
        x2d = _ffn(x2d, ffn2_pre[l][None, :], ffn2_post[l][None, :], ffn2_w_gate[l].astype(BF16),
                   ffn2_w_up[l].astype(BF16), ffn2_w_down[l].astype(BF16))
    return x2d.reshape(B, S, D_MODEL)
```

```python
import functools
import math

import numpy as np
import jax
import jax.numpy as jnp
from jax import lax
from jax.experimental import pallas as pl
from jax.experimental.pallas import tpu as pltpu

F32 = jnp.float32
BF16 = jnp.bfloat16

D_MODEL = 1024
HEAD_DIM = 64
RMS_EPS = 1e-6
NSA_HEADS = 8
NSA_GROUPS = 2
NSA_REP = NSA_HEADS // NSA_GROUPS
CMP_LEN = 32
CMP_STRIDE = 16
CMP_HIDDEN = 256
SLC_LEN = 64
N_SEL = 16
NSA_WINDOW = 512
DIL_CONFIGS = ((128, 1), (512, 4), (2048, 16))
DIL_HPG = 4
DIL_HEADS = DIL_HPG * len(DIL_CONFIGS)
D_FF = 2816
Q_NSA = NSA_HEADS * HEAD_DIM
KV_NSA = NSA_GROUPS * HEAD_DIM
GATE_NSA = NSA_HEADS * 3
QKV_DIL = DIL_HEADS * HEAD_DIM
DIL_OUT = DIL_HPG * HEAD_DIM
DIL_QKV = 3 * DIL_OUT
DIL_TQ = 256
LANES = 128
NEG = -1e30
SCALE = HEAD_DIM ** -0.5
VMEM_LIMIT = 56 * 1024 * 1024
NSA_TQ = 128
VT_ROWS = 80


def _slopes(n):
    return [float(np.float32(2.0 ** (-8.0 * (i + 1) / n))) for i in range(n)]


NSA_SLOPES = _slopes(NSA_HEADS)
DIL_SLOPES = _slopes(DIL_HEADS)


def _params(sem):
    return pltpu.CompilerParams(dimension_semantics=sem, vmem_limit_bytes=VMEM_LIMIT)


def _rms(x, g):
    ms = jnp.mean(x * x, axis=-1, keepdims=True)
    return x * lax.rsqrt(ms + RMS_EPS) * g


def _sigmoid(x):
    return 1.0 / (1.0 + jnp.exp(-x))


def _dot(a, b):
    return jnp.dot(a, b, preferred_element_type=F32)


def _dot_nt(a, b):
    return lax.dot_general(a, b, (((1,), (1,)), ((), ())), preferred_element_type=F32)


def _split3(a):
    hi = a.astype(BF16)
    r1 = a - hi.astype(F32)
    mid = r1.astype(BF16)
    lo = (r1 - mid.astype(F32)).astype(BF16)
    return hi, mid, lo


def _ffn_kernel(x_ref, pre_ref, post_ref, wg_ref, wu_ref, wd_ref, o_ref, h_scr, acc_scr):
    j = pl.program_id(1)

    @pl.when(j == 0)
    def _():
        h_scr[...] = _rms(x_ref[...], pre_ref[...]).astype(BF16)
        acc_scr[...] = jnp.zeros_like(acc_scr)

    h = h_scr[...]
    g = _dot(h, wg_ref[...])
    u = _dot(h, wu_ref[...])
    a = (g * _sigmoid(g) * u).astype(BF16)
    acc_scr[...] += _dot(a, wd_ref[...])

    @pl.when(j == pl.num_programs(1) - 1)
    def _():
        o_ref[...] = x_ref[...] + 0.5 * _rms(acc_scr[...], post_ref[...])


def _ffn(x2d, pre, post, wg, wu, wd, *, tm=512, tf=1408):
    T = x2d.shape[0]
    return pl.pallas_call(
        _ffn_kernel,
        grid=(T // tm, D_FF // tf),
        in_specs=[
            pl.BlockSpec((tm, D_MODEL), lambda i, j: (i, 0)),
            pl.BlockSpec((1, D_MODEL), lambda i, j: (0, 0)),
            pl.BlockSpec((1, D_MODEL), lambda i, j: (0, 0)),
            pl.BlockSpec((D_MODEL, tf), lambda i, j: (0, j)),
            pl.BlockSpec((D_MODEL, tf), lambda i, j: (0, j)),
            pl.BlockSpec((tf, D_MODEL), lambda i, j: (j, 0)),
        ],
        out_specs=pl.BlockSpec((tm, D_MODEL), lambda i, j: (i, 0)),
        out_shape=jax.ShapeDtypeStruct((T, D_MODEL), F32),
        scratch_shapes=[pltpu.VMEM((tm, D_MODEL), BF16), pltpu.VMEM((tm, D_MODEL), F32)],
        compiler_params=_params(("parallel", "arbitrary")),
        name="ffn",
    )(x2d, pre, post, wg, wu, wd)


def _proj_kernel(x_ref, pre_ref, wq_ref, wkv_ref, wd_ref, wgate_ref, q_ref, kv_ref, g_ref, *rest):
    d_refs, scr = rest[:-1], rest[-1]
    tm = x_ref.shape[0]
    h = _rms(x_ref[...], pre_ref[...]).astype(BF16)
    q_ref[...] = _dot(h, wq_ref[...]).astype(BF16)
    kv_ref[...] = _dot(h, wkv_ref[...]).astype(BF16)
    g_ref[...] = _dot(h, wgate_ref[...])
    n_blk = DIL_QKV // LANES
    for gi, (_, dil) in enumerate(DIL_CONFIGS):
        res = _dot(h, wd_ref[:, gi * DIL_QKV:(gi + 1) * DIL_QKV])
        if dil == 1:
            d_refs[gi][0] = res.astype(BF16)
            continue
        for c in range(n_blk):
            scr[c] = res[:, c * LANES:(c + 1) * LANES]
        for r in range(dil):
            for c in range(n_blk):
                rows = scr[c, pl.ds(r, tm // dil, stride=dil), :]
                d_refs[gi][r, :, c * LANES:(c + 1) * LANES] = rows.astype(BF16)


def _proj(x2d, pre, wq, wkv, wd, wgate, *, B, S, tm=256):
    T = x2d.shape[0]
    nt = S // tm
    tok = lambda w: pl.BlockSpec((tm, w), lambda i: (i, 0))
    full = lambda a: pl.BlockSpec(a.shape, lambda i: (0,) * a.ndim)
    d_specs = [pl.BlockSpec((None, dil, tm // dil, DIL_QKV), lambda i: (i // nt, 0, i % nt, 0))
               for _, dil in DIL_CONFIGS]
    d_shapes = [jax.ShapeDtypeStruct((B, dil, S // dil, DIL_QKV), BF16) for _, dil in DIL_CONFIGS]
    return pl.pallas_call(
        _proj_kernel,
        grid=(T // tm,),
        in_specs=[tok(D_MODEL), full(pre), full(wq), full(wkv), full(wd), full(wgate)],
        out_specs=[tok(wq.shape[1]), tok(wkv.shape[1]), tok(LANES)] + d_specs,
        out_shape=[jax.ShapeDtypeStruct((T, wq.shape[1]), BF16), jax.ShapeDtypeStruct((T, wkv.shape[1]), BF16),
                   jax.ShapeDtypeStruct((T, LANES), F32)] + d_shapes,
        scratch_shapes=[pltpu.VMEM((DIL_QKV // LANES, tm, LANES), F32)],
        compiler_params=_params(("parallel",)),
        name="proj",
    )(x2d, pre, wq, wkv, wd, wgate)


def _gelu_tanh(x):
    c = math.sqrt(2.0 / math.pi)
    return x * (0.5 * (1.0 + jnp.tanh(c * (x + 0.044715 * (x * x * x)))))


def _cmp_mlp_kernel(x_ref, pe_ref, w1cat_ref, w1_ref, w2_ref, o_ref):
    n_chunks = x_ref.shape[0]
    ab = _dot(x_ref[...], w1cat_ref[...])
    first = ab[:, :CMP_HIDDEN]
    second = pltpu.roll(ab[:, CMP_HIDDEN:], n_chunks - 1, 0)
    pe8 = jnp.broadcast_to(pe_ref[...], (8, pe_ref.shape[1]))
    bias = _dot(pe8, w1_ref[...])[0:1]
    hid = _gelu_tanh(first + second + bias).astype(BF16)
    out = _dot(hid, w2_ref[...])
    row = lax.broadcasted_iota(jnp.int32, out.shape, 0)
    o_ref[...] = jnp.where(row < n_chunks - 1, out, 0.0).astype(BF16)


def _cmp_mlp(xc, pe, w1cat, w1, w2):
    _, BG, n_chunks, width = xc.shape
    return pl.pallas_call(
        _cmp_mlp_kernel,
        grid=(2, BG),
        in_specs=[
            pl.BlockSpec((None, None, n_chunks, width), lambda t, n: (t, n, 0, 0)),
            pl.BlockSpec((None, 1, 2 * width), lambda t, n: (t, 0, 0)),
            pl.BlockSpec((None, width, 2 * CMP_HIDDEN), lambda t, n: (t, 0, 0)),
            pl.BlockSpec((None, 2 * width, CMP_HIDDEN), lambda t, n: (t, 0, 0)),
            pl.BlockSpec((None, CMP_HIDDEN, HEAD_DIM), lambda t, n: (t, 0, 0)),
        ],
        out_specs=pl.BlockSpec((None, None, n_chunks, HEAD_DIM), lambda t, n: (t, n, 0, 0)),
        out_shape=jax.ShapeDtypeStruct((2, BG, n_chunks, HEAD_DIM), BF16),
        compiler_params=_params(("parallel", "parallel")),
        name="cmp_mlp",
    )(xc, pe, w1cat, w1, w2)


def _online_update(s_t, v_t, m, acc):
    m_new = jnp.maximum(m, jnp.max(s_t, axis=0, keepdims=True))
    alpha = jnp.exp(m - m_new)
    p = jnp.exp(s_t - m_new)
    return m_new, alpha * acc + _dot(v_t, p.astype(BF16))


def _nsa_kernel(q_ref, gate_ref, kc_ref, vct_ref, ks_ref, vst_ref, kw_ref, vwt_ref,
                band_ref, cmpmask_ref, ovt_ref, o_ref, selbias_scr, m_scr, acc_scr, need_smem, *, tq, tk):
    g = pl.program_id(1)
    i = pl.program_id(2)
    rows = NSA_REP * tq
    n_ck = kc_ref.shape[0]
    n_slc = ovt_ref.shape[0]
    q0 = i * tq

    lane = lax.broadcasted_iota(jnp.int32, (1, LANES), 1)
    slabs = []
    for r in range(NSA_REP):
        sl = jnp.where(g == 0, NSA_SLOPES[r], NSA_SLOPES[NSA_REP + r])
        meta = jnp.where(lane == HEAD_DIM, SLC_LEN * sl,
                         jnp.where(lane == HEAD_DIM + 1, sl,
                                   jnp.where(lane == HEAD_DIM + 2, -sl * q0.astype(F32), 0.0)))
        slabs.append(q_ref[:, r * LANES:(r + 1) * LANES] + meta.astype(BF16))
    q_aug = jnp.concatenate(slabs, axis=0)

    s_t = _dot_nt(kc_ref[...], q_aug)
    cstart = pl.multiple_of((n_ck - 8) - (tq // CMP_STRIDE) * i, 8)
    s_t = s_t + cmpmask_ref[pl.ds(cstart, n_ck), :]
    m = jnp.max(s_t, axis=0, keepdims=True)
    any_key = m > 0.5 * NEG
    p = jnp.exp(s_t - m)
    den = jnp.sum(p, axis=0, keepdims=True)
    p = p * jnp.where(any_key, 1.0 / den, 0.0)
    o_cmp = _dot(vct_ref[0:HEAD_DIM, :], p.astype(BF16))

    psum = p[:, 0:tq]
    for r in range(1, NSA_REP):
        psum = psum + p[:, r * tq:(r + 1) * tq]
    hi, mid, lo = _split3(psum)
    ovt = ovt_ref[...]
    imp = _dot(ovt, hi) + _dot(ovt, mid) + _dot(ovt, lo)
    pos = q0 + lax.broadcasted_iota(jnp.int32, (1, tq), 1)
    own = pos // SLC_LEN
    jb = lax.broadcasted_iota(jnp.int32, (n_slc, 1), 0)
    jbf = jb.astype(F32)
    forced = (jb == 0) | (jb == own) | (jb == own - 1)
    valid = jb * SLC_LEN <= pos
    vals = jnp.where(valid & jnp.logical_not(forced), imp, -jnp.inf)
    sel = jnp.where(forced, 1.0, 0.0)
    for _ in range(N_SEL - 3):
        mx = jnp.max(vals, axis=0, keepdims=True)
        cand = jnp.where(vals == mx, jbf, 1e9)
        idx = jnp.min(cand, axis=0, keepdims=True)
        pick = (jbf == idx) & (mx > -jnp.inf)
        sel = jnp.where(pick, 1.0, sel)
        vals = jnp.where(pick, -jnp.inf, vals)
    sbias = jnp.where((sel > 0.5) & (jb < q0 // SLC_LEN), 0.0, NEG)
    selbias_scr[...] = jnp.concatenate([sbias] * NSA_REP, axis=1)
    blocks_per_chunk = tk // SLC_LEN
    for c in range(n_slc // blocks_per_chunk):
        chunk_max = jnp.max(sbias[c * blocks_per_chunk:(c + 1) * blocks_per_chunk])
        need_smem[c] = (chunk_max > 0.5 * NEG).astype(jnp.int32)

    width = NSA_WINDOW + tq
    back = jnp.minimum(q0, NSA_WINDOW)
    ws = pl.multiple_of(q0 - back, tq)
    s_w = _dot_nt(kw_ref[pl.ds(ws, width), :], q_aug)
    s_w = s_w + band_ref[pl.ds(pl.multiple_of(NSA_WINDOW - back, tq), width), :]
    m_w = jnp.max(s_w, axis=0, keepdims=True)
    acc_w = _dot(vwt_ref[:, pl.ds(ws, width)], jnp.exp(s_w - m_w).astype(BF16))
    o_win = acc_w[0:HEAD_DIM] * (1.0 / acc_w[HEAD_DIM:HEAD_DIM + 1])

    qs = pl.multiple_of(q0, tq)
    s_d = _dot_nt(ks_ref[pl.ds(qs, tq), :], q_aug) + band_ref[NSA_WINDOW:NSA_WINDOW + tq, :]
    m_d = jnp.max(s_d, axis=0, keepdims=True)
    m_scr[...] = m_d
    acc_scr[...] = _dot(vst_ref[:, pl.ds(qs, tq)], jnp.exp(s_d - m_d).astype(BF16))

    def body(c, carry):
        @pl.when(need_smem[c] != 0)
        def _():
            ks = pl.multiple_of(c * tk, tk)
            s_c = _dot_nt(ks_ref[pl.ds(ks, tk), :], q_aug)
            parts = []
            for jj in range(blocks_per_chunk):
                brow = selbias_scr[pl.ds(c * blocks_per_chunk + jj, 1), :]
                parts.append(s_c[jj * SLC_LEN:(jj + 1) * SLC_LEN] + brow)
            s_c = jnp.concatenate(parts, axis=0)
            m_new, acc_new = _online_update(s_c, vst_ref[:, pl.ds(ks, tk)], m_scr[...], acc_scr[...])
            m_scr[...] = m_new
            acc_scr[...] = acc_new
        return carry

    lax.fori_loop(0, (q0 + tk - 1) // tk, body, 0)
    acc = acc_scr[...]
    o_slc = acc[0:HEAD_DIM] * (1.0 / acc[HEAD_DIM:HEAD_DIM + 1])

    gate_t = _sigmoid(gate_ref[...].T)
    outs = []
    for r in range(NSA_REP):
        cols = slice(r * tq, (r + 1) * tq)

        def gate_row(branch):
            lo_row = 3 * r + branch
            hi_row = 3 * (NSA_REP + r) + branch
            return jnp.where(g == 0, gate_t[lo_row:lo_row + 1], gate_t[hi_row:hi_row + 1])

        outs.append(gate_row(0) * o_cmp[:, cols] + gate_row(1) * o_slc[:, cols]
                    + gate_row(2) * o_win[:, cols])
    o_ref[...] = jnp.concatenate(outs, axis=0).T.astype(BF16)


def _nsa(q_pad, gates, kc, vct, ks, vst, kw, vwt, band, cmpmask, ovt, *, B, S, tq=NSA_TQ, tk=512):
    G = NSA_GROUPS
    n_ck = kc.shape[1]
    n_slc = ovt.shape[0]
    rows = NSA_REP * tq
    nq = S // tq
    per_bg = lambda shape: pl.BlockSpec((None,) + shape, lambda b, g, i: (b * G + g, 0, 0))
    full = lambda a: pl.BlockSpec(a.shape, lambda b, g, i: (0,) * a.ndim)
    return pl.pallas_call(
        functools.partial(_nsa_kernel, tq=tq, tk=tk),
        grid=(B, G, nq),
        in_specs=[
            pl.BlockSpec((tq, NSA_REP * LANES), lambda b, g, i: (b * nq + i, g)),
            pl.BlockSpec((tq, LANES), lambda b, g, i: (b * nq + i, 0)),
            per_bg((n_ck, LANES)), per_bg((VT_ROWS, n_ck)),
            per_bg((S, LANES)), per_bg((VT_ROWS, S)),
            per_bg((S, LANES)), per_bg((VT_ROWS, S)),
            full(band), full(cmpmask), full(ovt),
        ],
        out_specs=pl.BlockSpec((tq, NSA_REP * HEAD_DIM), lambda b, g, i: (b * nq + i, g)),
        out_shape=jax.ShapeDtypeStruct((B * S, Q_NSA), BF16),
        scratch_shapes=[pltpu.VMEM((n_slc, rows), F32), pltpu.VMEM((1, rows), F32),
                        pltpu.VMEM((VT_ROWS, rows), F32), pltpu.SMEM((n_slc * SLC_LEN // tk,), jnp.int32)],
        compiler_params=_params(("parallel", "parallel", "arbitrary")),
        name="nsa",
    )(q_pad, gates, kc, vct, ks, vst, kw, vwt, band, cmpmask, ovt)


def _dil_attn_kernel(q_ref, k_ref, v_ref, dist_ref, o_ref, lse_ref, vt_scr, *, tq, width, max_back, slopes):
    i = pl.program_id(1)
    L = k_ref.shape[0]
    t_chunk = min(L, 512)

    @pl.when(i == 0)
    def _():
        for pair in range(DIL_OUT // LANES):
            for c in range(L // t_chunk):
                v = v_ref[c * t_chunk:(c + 1) * t_chunk, pair * LANES:(pair + 1) * LANES].astype(F32)
                vt_scr[pair * LANES:(pair + 1) * LANES, c * t_chunk:(c + 1) * t_chunk] = v.T.astype(BF16)

    back = jnp.minimum(i * tq, max_back)
    ks = pl.multiple_of(i * tq - back, max_back)
    dist = dist_ref[pl.ds(pl.multiple_of(max_back - back, max_back), width), :]
    lane = lax.broadcasted_iota(jnp.int32, (1, LANES), 1)
    outs, lses = [], []
    for h in range(DIL_HPG):
        pair, half = divmod(h, 2)
        in_head = (lane >= half * HEAD_DIM) & (lane < (half + 1) * HEAD_DIM)
        q = q_ref[:, pair * LANES:(pair + 1) * LANES]
        q = jnp.where(in_head, q, jnp.zeros_like(q))
        k = k_ref[pl.ds(ks, width), pair * LANES:(pair + 1) * LANES]
        s_t = _dot_nt(k, q) - slopes[h] * dist
        m = jnp.max(s_t, axis=0, keepdims=True)
        p = jnp.exp(s_t - m)
        den = jnp.sum(p, axis=0, keepdims=True)
        o_t = _dot(vt_scr[h * HEAD_DIM:(h + 1) * HEAD_DIM, pl.ds(ks, width)], p.astype(BF16))
        outs.append(o_t * (1.0 / den))
        lses.append(jnp.broadcast_to(m + jnp.log(den), (HEAD_DIM, tq)))
    o_ref[...] = jnp.concatenate(outs, axis=0).T
    lse_ref[...] = jnp.concatenate(lses, axis=0).T


def _dil_attn(qkv, dist_tbl, *, max_back, slopes, tq):
    N, L, _ = qkv.shape
    width = dist_tbl.shape[0] - max_back
    kern = functools.partial(_dil_attn_kernel, tq=tq, width=width, max_back=max_back, slopes=slopes)
    return pl.pallas_call(
        kern,
        grid=(N, L // tq),
        in_specs=[
            pl.BlockSpec((None, tq, DIL_OUT), lambda n, i: (n, i, 0)),
            pl.BlockSpec((None, L, DIL_OUT), lambda n, i: (n, 0, 1)),
            pl.BlockSpec((None, L, DIL_OUT), lambda n, i: (n, 0, 2)),
            pl.BlockSpec(dist_tbl.shape, lambda n, i: (0, 0)),
        ],
        out_specs=[
            pl.BlockSpec((None, tq, DIL_OUT), lambda n, i: (n, i, 0)),
            pl.BlockSpec((None, tq, DIL_OUT), lambda n, i: (n, i, 0)),
        ],
        out_shape=[
            jax.ShapeDtypeStruct((N, L, DIL_OUT), F32),
            jax.ShapeDtypeStruct((N, L, DIL_OUT), F32),
        ],
        scratch_shapes=[pltpu.VMEM((DIL_OUT, L), BF16)],
        compiler_params=_params(("parallel", "arbitrary")),
        name="dil_attn",
    )(qkv, qkv, qkv, dist_tbl)


def _final_kernel(x_ref, pre_ref, post_ref, onsa_ref, od0_ref, od1_ref, od2_ref, l0_ref, l1_ref, l2_ref,
                  wmerge_ref, wnsa_ref, wdil_ref, wout_ref, o_ref, scr):
    tm = x_ref.shape[0]
    x = x_ref[...]
    h = _rms(x, pre_ref[...]).astype(BF16)
    merge = _dot(h, wmerge_ref[...])
    g_a = _sigmoid(merge[:, :D_MODEL])
    g_b = _sigmoid(merge[:, D_MODEL:])
    y_nsa = _dot(onsa_ref[...], wnsa_ref[...])

    n_blk = DIL_OUT // LANES
    slot = 0
    token_major = []
    for (_, dil), o_in, l_in in zip(DIL_CONFIGS, (od0_ref, od1_ref, od2_ref), (l0_ref, l1_ref, l2_ref)):
        pair = []
        for src in (o_in, l_in):
            if dil == 1:
                pair.append(src[0])
                continue
            for r in range(dil):
                for c in range(n_blk):
                    scr[slot + c, pl.ds(r, tm // dil, stride=dil), :] = src[r, :, c * LANES:(c + 1) * LANES]
            pair.append(jnp.concatenate([scr[slot + c] for c in range(n_blk)], axis=1))
            slot += n_blk
        token_major.append(pair)
    (o0, l0), (o1, l1), (o2, l2) = token_major
    mx = jnp.maximum(jnp.maximum(l0, l1), l2)
    e0, e1, e2 = jnp.exp(l0 - mx), jnp.exp(l1 - mx), jnp.exp(l2 - mx)
    den = e0 + e1 + e2
    o_dil = (e0 / den) * o0 + (e1 / den) * o1 + (e2 / den) * o2
    y_dil = _dot(o_dil.astype(BF16), wdil_ref[...])

    y = g_a * y_nsa + g_b * y_dil
    z = _dot(y.astype(BF16), wout_ref[...])
    o_ref[...] = x + _rms(z, post_ref[...])


def _final(x2d, pre, post, onsa, od, ld, wmerge, wnsa, wdil, wout, *, S, tm=256):
    T = x2d.shape[0]
    nt = S // tm
    tok = lambda w: pl.BlockSpec((tm, w), lambda i: (i, 0))
    full = lambda a: pl.BlockSpec(a.shape, lambda i: (0,) * a.ndim)
    strided = [pl.BlockSpec((None, dil, tm // dil, DIL_OUT), lambda i: (i // nt, 0, i % nt, 0))
               for _, dil in DIL_CONFIGS]
    n_strided = sum(1 for _, dil in DIL_CONFIGS if dil > 1)
    return pl.pallas_call(
        _final_kernel,
        grid=(T // tm,),
        in_specs=[tok(D_MODEL), full(pre), full(post), tok(Q_NSA)] + strided + strided
        + [full(wmerge), full(wnsa), full(wdil), full(wout)],
        out_specs=tok(D_MODEL),
        out_shape=jax.ShapeDtypeStruct((T, D_MODEL), F32),
        scratch_shapes=[pltpu.VMEM((2 * n_strided * (DIL_OUT // LANES), tm, LANES), F32)],
        compiler_params=_params(("parallel",)),
        name="final",
    )(x2d, pre, post, onsa, od[0], od[1], od[2], ld[0], ld[1], ld[2], wmerge, wnsa, wdil, wout)


def _overlap_t(n_ck, n_slc):
    c0 = np.arange(n_ck)[None, :] * CMP_STRIDE
    s0 = np.arange(n_slc)[:, None] * SLC_LEN
    ov = np.clip(np.minimum(c0 + CMP_LEN, s0 + SLC_LEN) - np.maximum(c0, s0), 0, None) / CMP_LEN
    ov[:, n_ck - 1] = 0.0
    return ov.astype(np.float32)


def _pos_columns(pos):
    cols = np.zeros((len(pos), LANES - HEAD_DIM), np.float32)
    cols[:, 0] = pos // SLC_LEN
    cols[:, 1] = pos % SLC_LEN
    cols[:, 2] = 1.0
    return cols


def _band_table(tq):
    u = np.arange(2 * NSA_WINDOW + tq)[:, None]
    r = (np.arange(NSA_REP * tq) % tq)[None, :]
    dist = NSA_WINDOW + r - u
    return np.where((dist >= 0) & (dist <= NSA_WINDOW - 1), 0.0, NEG).astype(np.float32)


def _dil_dist_table(tq, width, max_back, dil):
    u = np.arange(width + max_back)[:, None]
    r = np.arange(tq)[None, :]
    dist = max_back + r - u
    return np.where((dist >= 0) & (dist <= max_back), dist * dil, -NEG).astype(np.float32)


def _cmp_mask_table(n_ck, tq):
    u = np.arange(2 * n_ck - 8)[:, None]
    r = (np.arange(NSA_REP * tq) % tq)[None, :]
    end = (u - (n_ck - 8)) * CMP_STRIDE + CMP_LEN - 1
    return np.where(end <= r, 0.0, NEG).astype(np.float32)


def _mixer(x2d, B, S, mix_pre, mix_post, w_in, pe_k, w_ck1, w_ck2, pe_v, w_cv1, w_cv2,
           w_nsa_o, w_dil_o, w_mix_out):
    G = NSA_GROUPS
    BG = B * G
    assert S % (16 * 128) == 0 and NSA_TQ // CMP_STRIDE == 8
    o_q, o_kv, o_g, o_d, o_m = np.cumsum([0, Q_NSA, 6 * KV_NSA, GATE_NSA, 3 * QKV_DIL]).tolist()
    wq = (w_in[:, o_q:o_kv] * SCALE).reshape(D_MODEL, NSA_HEADS, HEAD_DIM)
    wq = jnp.pad(wq, ((0, 0), (0, 0), (0, LANES - HEAD_DIM))).reshape(D_MODEL, NSA_HEADS * LANES)
    w_gate = jnp.pad(w_in[:, o_g:o_d], ((0, 0), (0, LANES - GATE_NSA)))
    wd = w_in[:, o_d:o_m].reshape(D_MODEL, 3, len(DIL_CONFIGS), DIL_OUT)
    wd = (wd * jnp.asarray([SCALE, 1.0, 1.0], F32)[None, :, None, None]).transpose(0, 2, 1, 3)
    wd = wd.reshape(D_MODEL, 3 * QKV_DIL)
    q_pad, kv, gates, *qkv_d = _proj(x2d, mix_pre[None, :], wq.astype(BF16), w_in[:, o_kv:o_g].astype(BF16),
                                     wd.astype(BF16), w_gate.astype(BF16), B=B, S=S)

    kv_n = kv.reshape(B, S, 6, G, HEAD_DIM).transpose(2, 0, 3, 1, 4).reshape(6, BG, S, HEAD_DIM)
    n_ck = S // CMP_STRIDE
    half = CMP_STRIDE * HEAD_DIM
    xc = kv_n[0:2].reshape(2, BG, n_ck, half)
    w1 = jnp.stack([w_ck1, w_cv1])
    w1cat = jnp.concatenate([w1[:, :half], w1[:, half:]], axis=2).astype(BF16)
    pe = jnp.stack([pe_k.reshape(1, -1), pe_v.reshape(1, -1)]).astype(BF16)
    w2 = jnp.stack([w_ck2, w_cv2]).astype(BF16)
    kvc = _cmp_mlp(xc, pe, w1cat, w1.astype(BF16), w2)

    def key_aug(k, pos):
        cols = jnp.asarray(_pos_columns(pos), BF16)
        return jnp.concatenate([k, jnp.broadcast_to(cols, (BG,) + cols.shape)], axis=-1)

    def val_t(v):
        L = v.shape[1]
        tail = np.zeros((VT_ROWS - HEAD_DIM, L), np.float32)
        tail[0] = 1.0
        tail = jnp.broadcast_to(jnp.asarray(tail, BF16), (BG,) + tail.shape)
        return jnp.concatenate([v.transpose(0, 2, 1), tail], axis=1)

    tok_pos = np.arange(S)
    cmp_pos = np.arange(n_ck) * CMP_STRIDE + CMP_LEN - 1
    band = jnp.asarray(_band_table(NSA_TQ))
    cmpmask = jnp.asarray(_cmp_mask_table(n_ck, NSA_TQ))
    ovt = jnp.asarray(_overlap_t(n_ck, S // SLC_LEN), BF16)
    o_nsa = _nsa(q_pad, gates, key_aug(kvc[0], cmp_pos), val_t(kvc[1]),
                 key_aug(kv_n[2], tok_pos), val_t(kv_n[3]), key_aug(kv_n[4], tok_pos), val_t(kv_n[5]),
                 band, cmpmask, ovt, B=B, S=S)

    od, ld = [], []
    for gi, (window, dil) in enumerate(DIL_CONFIGS):
        Ls = S // dil
        max_back = window // dil
        tq = min(DIL_TQ, Ls)
        width = tq + max_back if Ls > tq else tq
        o, lse = _dil_attn(qkv_d[gi].reshape(B * dil, Ls, DIL_QKV),
                           jnp.asarray(_dil_dist_table(tq, width, max_back, dil)),
                           max_back=max_back, slopes=DIL_SLOPES[gi * DIL_HPG:(gi + 1) * DIL_HPG], tq=tq)
        od.append(o.reshape(B, dil, Ls, DIL_OUT))
        ld.append(lse.reshape(B, dil, Ls, DIL_OUT))

    return _final(x2d, mix_pre[None, :], mix_post[None, :], o_nsa, od, ld, w_in[:, o_m:].astype(BF16),
                  w_nsa_o.astype(BF16), w_dil_o.astype(BF16), w_mix_out.astype(BF16), S=S)


def kernel(x, ffn1_pre, ffn1_post, ffn1_w_gate, ffn1_w_up, ffn1_w_down, mix_pre, mix_post, w_in, nsa_pe_k, nsa_w_ck1, nsa_w_ck2, nsa_pe_v, nsa_w_cv1, nsa_w_cv2, w_nsa_o, w_dil_o, w_mix_out, ffn2_pre, ffn2_post, ffn2_w_gate, ffn2_w_up, ffn2_w_down):
    B, S, _ = x.shape
    x2d = x.reshape(B * S, D_MODEL)
    for l in range(ffn1_pre.shape[0]):
        x2d = _ffn(x2d, ffn1_pre[l][None, :], ffn1_post[l][None, :], ffn1_w_gate[l].astype(BF16),
                   ffn1_w_up[l].astype(BF16), ffn1_w_down[l].astype(BF16))
        x2d = _mixer(x2d, B, S, mix_pre[l], mix_post[l], w_in[l], nsa_pe_k[l], nsa_w_ck1[l], nsa_w_ck2[l],
                     nsa_pe_v[l], nsa_w_cv1[l], nsa_w_cv2[l], w_nsa_o[l], w_dil_o[l], w_mix_out[l])
        x2d = _ffn(x2d, ffn2_pre[l][None, :], ffn2_post[l][None, :], ffn2_w_gate[l].astype(BF16),
                   ffn2_w_up[l].astype(BF16), ffn2_w_down[l].astype(BF16))
    return x2d.reshape(B, S, D_MODEL)
```

```python
import functools
import math

import numpy as np
import jax
import jax.numpy as jnp
from jax import lax
from jax.experimental import pallas as pl
from jax.experimental.pallas import tpu as pltpu

F32 = jnp.float32
BF16 = jnp.bfloat16

D_MODEL = 1024
HEAD_DIM = 64
RMS_EPS = 1e-6
NSA_HEADS = 8
NSA_GROUPS = 2
NSA_REP = NSA_HEADS // NSA_GROUPS
CMP_LEN = 32
CMP_STRIDE = 16
CMP_HIDDEN = 256
SLC_LEN = 64
N_SEL = 16
NSA_WINDOW = 512
DIL_CONFIGS = ((128, 1), (512, 4), (2048, 16))
DIL_HPG = 4
DIL_HEADS = DIL_HPG * len(DIL_CONFIGS)
D_FF = 2816
Q_NSA = NSA_HEADS * HEAD_DIM
KV_NSA = NSA_GROUPS * HEAD_DIM
GATE_NSA = NSA_HEADS * 3
QKV_DIL = DIL_HEADS * HEAD_DIM
DIL_OUT = DIL_HPG * HEAD_DIM
DIL_QKV = 3 * DIL_OUT
DIL_TQ = 256
LANES = 128
NEG = -1e30
SCALE = HEAD_DIM ** -0.5
VMEM_LIMIT = 56 * 1024 * 1024
NSA_TQ = 128
VT_ROWS = 80


def _slopes(n):
    return [float(np.float32(2.0 ** (-8.0 * (i + 1) / n))) for i in range(n)]


NSA_SLOPES = _slopes(NSA_HEADS)
DIL_SLOPES = _slopes(DIL_HEADS)


def _params(sem):
    return pltpu.CompilerParams(dimension_semantics=sem, vmem_limit_bytes=VMEM_LIMIT)


def _rms(x, g):
    ms = jnp.mean(x * x, axis=-1, keepdims=True)
    return x * lax.rsqrt(ms + RMS_EPS) * g


def _sigmoid(x):
    return 1.0 / (1.0 + jnp.exp(-x))


def _dot(a, b):
    return jnp.dot(a, b, preferred_element_type=F32)


def _dot_nt(a, b):
    return lax.dot_general(a, b, (((1,), (1,)), ((), ())), preferred_element_type=F32)


def _split3(a):
    hi = a.astype(BF16)
    r1 = a - hi.astype(F32)
    mid = r1.astype(BF16)
    lo = (r1 - mid.astype(F32)).astype(BF16)
    return hi, mid, lo


def _ffn_kernel(x_ref, pre_ref, post_ref, wg_ref, wu_ref, wd_ref, o_ref, h_scr, acc_scr):
    j = pl.program_id(1)

    @pl.when(j == 0)
    def _():
        h_scr[...] = _rms(x_ref[...], pre_ref[...]).astype(BF16)
        acc_scr[...] = jnp.zeros_like(acc_scr)

    h = h_scr[...]
    g = _dot(h, wg_ref[...])
    u = _dot(h, wu_ref[...])
    a = (g * _sigmoid(g) * u).astype(BF16)
    acc_scr[...] += _dot(a, wd_ref[...])

    @pl.when(j == pl.num_programs(1) - 1)
    def _():
        o_ref[...] = x_ref[...] + 0.5 * _rms(acc_scr[...], post_ref[...])


def _ffn(x2d, pre, post, wg, wu, wd, *, tm=512, tf=1408):
    T = x2d.shape[0]
    return pl.pallas_call(
        _ffn_kernel,
        grid=(T // tm, D_FF // tf),
        in_specs=[
            pl.BlockSpec((tm, D_MODEL), lambda i, j: (i, 0)),
            pl.BlockSpec((1, D_MODEL), lambda i, j: (0, 0)),
            pl.BlockSpec((1, D_MODEL), lambda i, j: (0, 0)),
            pl.BlockSpec((D_MODEL, tf), lambda i, j: (0, j)),
            pl.BlockSpec((D_MODEL, tf), lambda i, j: (0, j)),
            pl.BlockSpec((tf, D_MODEL), lambda i, j: (j, 0)),
        ],
        out_specs=pl.BlockSpec((tm, D_MODEL), lambda i, j: (i, 0)),
        out_shape=jax.ShapeDtypeStruct((T, D_MODEL), F32),
        scratch_shapes=[pltpu.VMEM((tm, D_MODEL), BF16), pltpu.VMEM((tm, D_MODEL), F32)],
        compiler_params=_params(("parallel", "arbitrary")),
        name="ffn",
    )(x2d, pre, post, wg, wu, wd)


def _proj_kernel(x_ref, pre_ref, wq_ref, wkv_ref, wd_ref, wgate_ref, q_ref, kv_ref, g_ref, *rest):
    d_refs, scr = rest[:-1], rest[-1]
    tm = x_ref.shape[0]
    h = _rms(x_ref[...], pre_ref[...]).astype(BF16)
    q_ref[...] = _dot(h, wq_ref[...]).astype(BF16)
    kv_ref[...] = _dot(h, wkv_ref[...]).astype(BF16)
    g_ref[...] = _dot(h, wgate_ref[...])
    n_blk = DIL_QKV // LANES
    for gi, (_, dil) in enumerate(DIL_CONFIGS):
        res = _dot(h, wd_ref[:, gi * DIL_QKV:(gi + 1) * DIL_QKV])
        if dil == 1:
            d_refs[gi][0] = res.astype(BF16)
            continue
        for c in range(n_blk):
            scr[c] = res[:, c * LANES:(c + 1) * LANES]
        for r in range(dil):
            for c in range(n_blk):
                rows = scr[c, pl.ds(r, tm // dil, stride=dil), :]
                d_refs[gi][r, :, c * LANES:(c + 1) * LANES] = rows.astype(BF16)


def _proj(x2d, pre, wq, wkv, wd, wgate, *, B, S, tm=256):
    T = x2d.shape[0]
    nt = S // tm
    tok = lambda w: pl.BlockSpec((tm, w), lambda i: (i, 0))
    full = lambda a: pl.BlockSpec(a.shape, lambda i: (0,) * a.ndim)
    d_specs = [pl.BlockSpec((None, dil, tm // dil, DIL_QKV), lambda i: (i // nt, 0, i % nt, 0))
               for _, dil in DIL_CONFIGS]
    d_shapes = [jax.ShapeDtypeStruct((B, dil, S // dil, DIL_QKV), BF16) for _, dil in DIL_CONFIGS]
    return pl.pallas_call(
        _proj_kernel,
        grid=(T // tm,),
        in_specs=[tok(D_MODEL), full(pre), full(wq), full(wkv), full(wd), full(wgate)],
        out_specs=[tok(wq.shape[1]), tok(wkv.shape[1]), tok(LANES)] + d_specs,
        out_shape=[jax.ShapeDtypeStruct((T, wq.shape[1]), BF16), jax.ShapeDtypeStruct((T, wkv.shape[1]), BF16),
                   jax.ShapeDtypeStruct((T, LANES), F32)] + d_shapes,
        scratch_shapes=[pltpu.VMEM((DIL_QKV // LANES, tm, LANES), F32)],
        compiler_params=_params(("parallel",)),
        name="proj",
    )(x2d, pre, wq, wkv, wd, wgate)


def _gelu_tanh(x):
    c = math.sqrt(2.0 / math.pi)
    return x * (0.5 * (1.0 + jnp.tanh(c * (x + 0.044715 * (x * x * x)))))


def _cmp_mlp_kernel(xk_ref, xv_ref, pe_ref, w1cat_ref, w1_ref, w2_ref, kcols_ref, vtail_ref, kc_ref, vct_ref):
    n_chunks = xk_ref.shape[0]

    def mlp(x_ref, t):
        ab = _dot(x_ref[...], w1cat_ref[t])
        first = ab[:, :CMP_HIDDEN]
        second = pltpu.roll(ab[:, CMP_HIDDEN:], n_chunks - 1, 0)
        pe8 = jnp.broadcast_to(pe_ref[t], (8, pe_ref.shape[2]))
        bias = _dot(pe8, w1_ref[t])[0:1]
        hid = _gelu_tanh(first + second + bias).astype(BF16)
        out = _dot(hid, w2_ref[t])
        row = lax.broadcasted_iota(jnp.int32, out.shape, 0)
        return jnp.where(row < n_chunks - 1, out, 0.0)

    kc_ref[...] = mlp(xk_ref, 0).astype(BF16) + kcols_ref[...]
    vct_ref[...] = mlp(xv_ref, 1).T[0:VT_ROWS].astype(BF16) + vtail_ref[...]


def _cmp_mlp(xk, xv, pe, w1cat, w1, w2, kcols, vtail):
    BG, n_chunks, width = xk.shape
    per_n = lambda shape: pl.BlockSpec((None,) + shape, lambda n: (n, 0, 0))
    full = lambda a: pl.BlockSpec(a.shape, lambda n: (0,) * a.ndim)
    return pl.pallas_call(
        _cmp_mlp_kernel,
        grid=(BG,),
        in_specs=[per_n((n_chunks, width)), per_n((n_chunks, width)), full(pe), full(w1cat), full(w1),
                  full(w2), full(kcols), full(vtail)],
        out_specs=[per_n((n_chunks, LANES)), per_n((VT_ROWS, n_chunks))],
        out_shape=[jax.ShapeDtypeStruct((BG, n_chunks, LANES), BF16),
                   jax.ShapeDtypeStruct((BG, VT_ROWS, n_chunks), BF16)],
        compiler_params=_params(("parallel",)),
        name="cmp_mlp",
    )(xk, xv, pe, w1cat, w1, w2, kcols, vtail)


def _online_update(s_t, v_t, m, acc):
    m_new = jnp.maximum(m, jnp.max(s_t, axis=0, keepdims=True))
    alpha = jnp.exp(m - m_new)
    p = jnp.exp(s_t - m_new)
    return m_new, alpha * acc + _dot(v_t, p.astype(BF16))


def _nsa_kernel(q_ref, gate_ref, kc_ref, vct_ref, kslc_ref, vslc_ref, kwin_ref, vwin_ref, kcols_ref,
                vtail_ref, band_ref, cmpmask_ref, ovt_ref, o_ref,
                selbias_scr, m_scr, acc_scr, ks_ref, vst_ref, kw_ref, vwt_ref, need_smem, list_smem,
                *, tq, tk):
    g = pl.program_id(1)
    i = pl.program_id(2)
    rows = NSA_REP * tq
    n_ck = kc_ref.shape[0]
    n_slc = ovt_ref.shape[0]
    S = kslc_ref.shape[0]
    q0 = i * tq

    @pl.when(i == 0)
    def _():
        ks_ref[...] = kslc_ref[...] + kcols_ref[...]
        kw_ref[...] = kwin_ref[...] + kcols_ref[...]
        t_chunk = min(S, 512)
        for src, dst in ((vslc_ref, vst_ref), (vwin_ref, vwt_ref)):
            dst[HEAD_DIM:VT_ROWS, :] = vtail_ref[HEAD_DIM:VT_ROWS, :]
            for c in range(S // t_chunk):
                v = src[c * t_chunk:(c + 1) * t_chunk, :].astype(F32)
                dst[0:HEAD_DIM, c * t_chunk:(c + 1) * t_chunk] = v.T[0:HEAD_DIM].astype(BF16)

    lane = lax.broadcasted_iota(jnp.int32, (1, LANES), 1)
    slabs = []
    for r in range(NSA_REP):
        sl = jnp.where(g == 0, NSA_SLOPES[r], NSA_SLOPES[NSA_REP + r])
        meta = jnp.where(lane == HEAD_DIM, SLC_LEN * sl,
                         jnp.where(lane == HEAD_DIM + 1, sl,
                                   jnp.where(lane == HEAD_DIM + 2, -sl * q0.astype(F32), 0.0)))
        slabs.append(q_ref[:, r * LANES:(r + 1) * LANES] + meta.astype(BF16))
    q_aug = jnp.concatenate(slabs, axis=0)

    s_t = _dot_nt(kc_ref[...], q_aug)
    cstart = pl.multiple_of((n_ck - 8) - (tq // CMP_STRIDE) * i, 8)
    s_t = s_t + cmpmask_ref[pl.ds(cstart, n_ck), :]
    m = jnp.max(s_t, axis=0, keepdims=True)
    any_key = m > 0.5 * NEG
    p = jnp.exp(s_t - m)
    den = jnp.sum(p, axis=0, keepdims=True)
    p = p * jnp.where(any_key, 1.0 / den, 0.0)
    o_cmp = _dot(vct_ref[0:HEAD_DIM, :], p.astype(BF16))

    psum = p[:, 0:tq]
    for r in range(1, NSA_REP):
        psum = psum + p[:, r * tq:(r + 1) * tq]
    hi, mid, lo = _split3(psum)
    ovt = ovt_ref[...]
    imp = _dot(ovt, hi) + _dot(ovt, mid) + _dot(ovt, lo)
    pos = q0 + lax.broadcasted_iota(jnp.int32, (1, tq), 1)
    own = pos // SLC_LEN
    jb = lax.broadcasted_iota(jnp.int32, (n_slc, 1), 0)
    jbf = jb.astype(F32)
    forced = (jb == 0) | (jb == own) | (jb == own - 1)
    valid = jb * SLC_LEN <= pos
    vals = jnp.where(valid & jnp.logical_not(forced), imp, -jnp.inf)
    sel = jnp.where(forced, 1.0, 0.0)
    for _ in range(N_SEL - 3):
        mx = jnp.max(vals, axis=0, keepdims=True)
        cand = jnp.where(vals == mx, jbf, 1e9)
        idx = jnp.min(cand, axis=0, keepdims=True)
        pick = (jbf == idx) & (mx > -jnp.inf)
        sel = jnp.where(pick, 1.0, sel)
        vals = jnp.where(pick, -jnp.inf, vals)
    sbias = jnp.where((sel > 0.5) & (jb < q0 // SLC_LEN), 0.0, NEG)
    selbias_scr[...] = jnp.concatenate([sbias] * NSA_REP, axis=1)
    blocks_per_chunk = tk // SLC_LEN
    for c in range(n_slc // blocks_per_chunk):
        chunk_max = jnp.max(sbias[c * blocks_per_chunk:(c + 1) * blocks_per_chunk])
        need_smem[c] = (chunk_max > 0.5 * NEG).astype(jnp.int32)
    n_need = jnp.int32(0)
    for c in range(n_slc // blocks_per_chunk):
        list_smem[n_need] = jnp.int32(c)
        n_need = n_need + need_smem[c]

    width = NSA_WINDOW + tq
    back = jnp.minimum(q0, NSA_WINDOW)
    ws = pl.multiple_of(q0 - back, tq)
    s_w = _dot_nt(kw_ref[pl.ds(ws, width), :], q_aug)
    s_w = s_w + band_ref[pl.ds(pl.multiple_of(NSA_WINDOW - back, tq), width), :]
    m_w = jnp.max(s_w, axis=0, keepdims=True)
    acc_w = _dot(vwt_ref[:, pl.ds(ws, width)], jnp.exp(s_w - m_w).astype(BF16))
    o_win = acc_w[0:HEAD_DIM] * (1.0 / acc_w[HEAD_DIM:HEAD_DIM + 1])

    qs = pl.multiple_of(q0, tq)
    s_d = _dot_nt(ks_ref[pl.ds(qs, tq), :], q_aug) + band_ref[NSA_WINDOW:NSA_WINDOW + tq, :]
    m_d = jnp.max(s_d, axis=0, keepdims=True)
    m_scr[...] = m_d
    acc_scr[...] = _dot(vst_ref[:, pl.ds(qs, tq)], jnp.exp(s_d - m_d).astype(BF16))

    def chunk_update(c, m_run, acc):
        ks = pl.multiple_of(c * tk, tk)
        s_c = _dot_nt(ks_ref[pl.ds(ks, tk), :], q_aug)
        parts = []
        for jj in range(blocks_per_chunk):
            brow = selbias_scr[pl.ds(c * blocks_per_chunk + jj, 1), :]
            parts.append(s_c[jj * SLC_LEN:(jj + 1) * SLC_LEN] + brow)
        return _online_update(jnp.concatenate(parts, axis=0), vst_ref[:, pl.ds(ks, tk)], m_run, acc)

    def pair_body(t, carry):
        m_run, acc = chunk_update(list_smem[2 * t], m_scr[...], acc_scr[...])
        m_run, acc = chunk_update(list_smem[2 * t + 1], m_run, acc)
        m_scr[...] = m_run
        acc_scr[...] = acc
        return carry

    lax.fori_loop(0, n_need // 2, pair_body, 0)

    @pl.when(n_need % 2 == 1)
    def _():
        m_run, acc = chunk_update(list_smem[n_need - 1], m_scr[...], acc_scr[...])
        m_scr[...] = m_run
        acc_scr[...] = acc

    acc = acc_scr[...]
    o_slc = acc[0:HEAD_DIM] * (1.0 / acc[HEAD_DIM:HEAD_DIM + 1])

    gate_t = _sigmoid(gate_ref[...].T)
    outs = []
    for r in range(NSA_REP):
        cols = slice(r * tq, (r + 1) * tq)

        def gate_row(branch):
            lo_row = 3 * r + branch
            hi_row = 3 * (NSA_REP + r) + branch
            return jnp.where(g == 0, gate_t[lo_row:lo_row + 1], gate_t[hi_row:hi_row + 1])

        outs.append(gate_row(0) * o_cmp[:, cols] + gate_row(1) * o_slc[:, cols]
                    + gate_row(2) * o_win[:, cols])
    o_ref[...] = jnp.concatenate(outs, axis=0).T.astype(BF16)


def _nsa(q_pad, gates, kc, vct, kv_pad, kcols, vtail, band, cmpmask, ovt, *, B, S, tq=NSA_TQ, tk=512):
    G = NSA_GROUPS
    n_ck = kc.shape[1]
    n_slc = ovt.shape[0]
    rows = NSA_REP * tq
    nq = S // tq
    n_chunks = n_slc * SLC_LEN // tk
    per_bg = lambda shape: pl.BlockSpec((None,) + shape, lambda b, g, i: (b * G + g, 0, 0))
    full = lambda a: pl.BlockSpec(a.shape, lambda b, g, i: (0,) * a.ndim)
    kv_kind = lambda kind: pl.BlockSpec((S, LANES), lambda b, g, i: (b, kind * G + g))
    return pl.pallas_call(
        functools.partial(_nsa_kernel, tq=tq, tk=tk),
        grid=(B, G, nq),
        in_specs=[
            pl.BlockSpec((tq, NSA_REP * LANES), lambda b, g, i: (b * nq + i, g)),
            pl.BlockSpec((tq, LANES), lambda b, g, i: (b * nq + i, 0)),
            per_bg((n_ck, LANES)), per_bg((VT_ROWS, n_ck)),
            kv_kind(2), kv_kind(3), kv_kind(4), kv_kind(5),
            full(kcols), full(vtail), full(band), full(cmpmask), full(ovt),
        ],
        out_specs=pl.BlockSpec((tq, NSA_REP * HEAD_DIM), lambda b, g, i: (b * nq + i, g)),
        out_shape=jax.ShapeDtypeStruct((B * S, Q_NSA), BF16),
        scratch_shapes=[pltpu.VMEM((n_slc, rows), F32), pltpu.VMEM((1, rows), F32),
                        pltpu.VMEM((VT_ROWS, rows), F32),
                        pltpu.VMEM((S, LANES), BF16), pltpu.VMEM((VT_ROWS, S), BF16),
                        pltpu.VMEM((S, LANES), BF16), pltpu.VMEM((VT_ROWS, S), BF16),
                        pltpu.SMEM((n_chunks,), jnp.int32), pltpu.SMEM((n_chunks + 1,), jnp.int32)],
        compiler_params=_params(("arbitrary", "arbitrary", "arbitrary")),
        name="nsa",
    )(q_pad, gates, kc, vct, kv_pad, kv_pad, kv_pad, kv_pad, kcols, vtail, band, cmpmask, ovt)


def _dil_attn_kernel(q_ref, k_ref, v_ref, dist_ref, o_ref, lse_ref, vt_scr, *, tq, width, max_back, slopes):
    i = pl.program_id(1)
    L = k_ref.shape[0]
    t_chunk = min(L, 512)

    @pl.when(i == 0)
    def _():
        for pair in range(DIL_OUT // LANES):
            for c in range(L // t_chunk):
                v = v_ref[c * t_chunk:(c + 1) * t_chunk, pair * LANES:(pair + 1) * LANES].astype(F32)
                vt_scr[pair * LANES:(pair + 1) * LANES, c * t_chunk:(c + 1) * t_chunk] = v.T.astype(BF16)

    back = jnp.minimum(i * tq, max_back)
    ks = pl.multiple_of(i * tq - back, max_back)
    dist = dist_ref[pl.ds(pl.multiple_of(max_back - back, max_back), width), :]
    lane = lax.broadcasted_iota(jnp.int32, (1, LANES), 1)
    outs, lses = [], []
    for h in range(DIL_HPG):
        pair, half = divmod(h, 2)
        in_head = (lane >= half * HEAD_DIM) & (lane < (half + 1) * HEAD_DIM)
        q = q_ref[:, pair * LANES:(pair + 1) * LANES]
        q = jnp.where(in_head, q, jnp.zeros_like(q))
        k = k_ref[pl.ds(ks, width), pair * LANES:(pair + 1) * LANES]
        s_t = _dot_nt(k, q) - slopes[h] * dist
        m = jnp.max(s_t, axis=0, keepdims=True)
        p = jnp.exp(s_t - m)
        den = jnp.sum(p, axis=0, keepdims=True)
        o_t = _dot(vt_scr[h * HEAD_DIM:(h + 1) * HEAD_DIM, pl.ds(ks, width)], p.astype(BF16))
        outs.append(o_t * (1.0 / den))
        lses.append(jnp.broadcast_to(m + jnp.log(den), (HEAD_DIM, tq)))
    o_ref[...] = jnp.concatenate(outs, axis=0).T
    lse_ref[...] = jnp.concatenate(lses, axis=0).T


def _dil_attn(qkv, dist_tbl, *, max_back, slopes, tq):
    N, L, _ = qkv.shape
    width = dist_tbl.shape[0] - max_back
    kern = functools.partial(_dil_attn_kernel, tq=tq, width=width, max_back=max_back, slopes=slopes)
    return pl.pallas_call(
        kern,
        grid=(N, L // tq),
        in_specs=[
            pl.BlockSpec((None, tq, DIL_OUT), lambda n, i: (n, i, 0)),
            pl.BlockSpec((None, L, DIL_OUT), lambda n, i: (n, 0, 1)),
            pl.BlockSpec((None, L, DIL_OUT), lambda n, i: (n, 0, 2)),
            pl.BlockSpec(dist_tbl.shape, lambda n, i: (0, 0)),
        ],
        out_specs=[
            pl.BlockSpec((None, tq, DIL_OUT), lambda n, i: (n, i, 0)),
            pl.BlockSpec((None, tq, DIL_OUT), lambda n, i: (n, i, 0)),
        ],
        out_shape=[
            jax.ShapeDtypeStruct((N, L, DIL_OUT), F32),
            jax.ShapeDtypeStruct((N, L, DIL_OUT), F32),
        ],
        scratch_shapes=[pltpu.VMEM((DIL_OUT, L), BF16)],
        compiler_params=_params(("parallel", "arbitrary")),
        name="dil_attn",
    )(qkv, qkv, qkv, dist_tbl)


def _final_kernel(x_ref, pre_ref, post_ref, onsa_ref, od0_ref, od1_ref, od2_ref, l0_ref, l1_ref, l2_ref,
                  wmerge_ref, wnsa_ref, wdil_ref, wout_ref, o_ref, scr):
    tm = x_ref.shape[0]
    x = x_ref[...]
    h = _rms(x, pre_ref[...]).astype(BF16)
    merge = _dot(h, wmerge_ref[...])
    g_a = _sigmoid(merge[:, :D_MODEL])
    g_b = _sigmoid(merge[:, D_MODEL:])
    y_nsa = _dot(onsa_ref[...], wnsa_ref[...])

    n_blk = DIL_OUT // LANES
    slot = 0
    token_major = []
    for (_, dil), o_in, l_in in zip(DIL_CONFIGS, (od0_ref, od1_ref, od2_ref), (l0_ref, l1_ref, l2_ref)):
        pair = []
        for src in (o_in, l_in):
            if dil == 1:
                pair.append(src[0])
                continue
            for r in range(dil):
                for c in range(n_blk):
                    scr[slot + c, pl.ds(r, tm // dil, stride=dil), :] = src[r, :, c * LANES:(c + 1) * LANES]
            pair.append(jnp.concatenate([scr[slot + c] for c in range(n_blk)], axis=1))
            slot += n_blk
        token_major.append(pair)
    (o0, l0), (o1, l1), (o2, l2) = token_major
    mx = jnp.maximum(jnp.maximum(l0, l1), l2)
    e0, e1, e2 = jnp.exp(l0 - mx), jnp.exp(l1 - mx), jnp.exp(l2 - mx)
    den = e0 + e1 + e2
    o_dil = (e0 / den) * o0 + (e1 / den) * o1 + (e2 / den) * o2
    y_dil = _dot(o_dil.astype(BF16), wdil_ref[...])

    y = g_a * y_nsa + g_b * y_dil
    z = _dot(y.astype(BF16), wout_ref[...])
    o_ref[...] = x + _rms(z, post_ref[...])


def _final(x2d, pre, post, onsa, od, ld, wmerge, wnsa, wdil, wout, *, S, tm=256):
    T = x2d.shape[0]
    nt = S // tm
    tok = lambda w: pl.BlockSpec((tm, w), lambda i: (i, 0))
    full = lambda a: pl.BlockSpec(a.shape, lambda i: (0,) * a.ndim)
    strided = [pl.BlockSpec((None, dil, tm // dil, DIL_OUT), lambda i: (i // nt, 0, i % nt, 0))
               for _, dil in DIL_CONFIGS]
    n_strided = sum(1 for _, dil in DIL_CONFIGS if dil > 1)
    return pl.pallas_call(
        _final_kernel,
        grid=(T // tm,),
        in_specs=[tok(D_MODEL), full(pre), full(post), tok(Q_NSA)] + strided + strided
        + [full(wmerge), full(wnsa), full(wdil), full(wout)],
        out_specs=tok(D_MODEL),
        out_shape=jax.ShapeDtypeStruct((T, D_MODEL), F32),
        scratch_shapes=[pltpu.VMEM((2 * n_strided * (DIL_OUT // LANES), tm, LANES), F32)],
        compiler_params=_params(("parallel",)),
        name="final",
    )(x2d, pre, post, onsa, od[0], od[1], od[2], ld[0], ld[1], ld[2], wmerge, wnsa, wdil, wout)


def _overlap_t(n_ck, n_slc):
    c0 = np.arange(n_ck)[None, :] * CMP_STRIDE
    s0 = np.arange(n_slc)[:, None] * SLC_LEN
    ov = np.clip(np.minimum(c0 + CMP_LEN, s0 + SLC_LEN) - np.maximum(c0, s0), 0, None) / CMP_LEN
    ov[:, n_ck - 1] = 0.0
    return ov.astype(np.float32)


def _pos_columns(pos):
    cols = np.zeros((len(pos), LANES), np.float32)
    cols[:, HEAD_DIM] = pos // SLC_LEN
    cols[:, HEAD_DIM + 1] = pos % SLC_LEN
    cols[:, HEAD_DIM + 2] = 1.0
    return cols


def _ones_row_tail(n):
    tail = np.zeros((VT_ROWS, n), np.float32)
    tail[HEAD_DIM] = 1.0
    return tail


def _band_table(tq):
    u = np.arange(2 * NSA_WINDOW + tq)[:, None]
    r = (np.arange(NSA_REP * tq) % tq)[None, :]
    dist = NSA_WINDOW + r - u
    return np.where((dist >= 0) & (dist <= NSA_WINDOW - 1), 0.0, NEG).astype(np.float32)


def _dil_dist_table(tq, width, max_back, dil):
    u = np.arange(width + max_back)[:, None]
    r = np.arange(tq)[None, :]
    dist = max_back + r - u
    return np.where((dist >= 0) & (dist <= max_back), dist * dil, -NEG).astype(np.float32)


def _cmp_mask_table(n_ck, tq):
    u = np.arange(2 * n_ck - 8)[:, None]
    r = (np.arange(NSA_REP * tq) % tq)[None, :]
    end = (u - (n_ck - 8)) * CMP_STRIDE + CMP_LEN - 1
    return np.where(end <= r, 0.0, NEG).astype(np.float32)


def _mixer(x2d, B, S, mix_pre, mix_post, w_in, pe_k, w_ck1, w_ck2, pe_v, w_cv1, w_cv2,
           w_nsa_o, w_dil_o, w_mix_out):
    G = NSA_GROUPS
    BG = B * G
    assert S % (16 * 128) == 0 and NSA_TQ // CMP_STRIDE == 8
    o_q, o_kv, o_g, o_d, o_m = np.cumsum([0, Q_NSA, 6 * KV_NSA, GATE_NSA, 3 * QKV_DIL]).tolist()
    wq = (w_in[:, o_q:o_kv] * SCALE).reshape(D_MODEL, NSA_HEADS, HEAD_DIM)
    wq = jnp.pad(wq, ((0, 0), (0, 0), (0, LANES - HEAD_DIM))).reshape(D_MODEL, NSA_HEADS * LANES)
    w_gate = jnp.pad(w_in[:, o_g:o_d], ((0, 0), (0, LANES - GATE_NSA)))
    wd = w_in[:, o_d:o_m].reshape(D_MODEL, 3, len(DIL_CONFIGS), DIL_OUT)
    wd = (wd * jnp.asarray([SCALE, 1.0, 1.0], F32)[None, :, None, None]).transpose(0, 2, 1, 3)
    wd = wd.reshape(D_MODEL, 3 * QKV_DIL)
    wkv = w_in[:, o_kv:o_g].reshape(D_MODEL, 6 * G, HEAD_DIM)
    wkv = jnp.pad(wkv, ((0, 0), (0, 0), (0, LANES - HEAD_DIM))).reshape(D_MODEL, 6 * G * LANES)
    q_pad, kv_pad, gates, *qkv_d = _proj(x2d, mix_pre[None, :], wq.astype(BF16), wkv.astype(BF16),
                                         wd.astype(BF16), w_gate.astype(BF16), B=B, S=S)

    n_ck = S // CMP_STRIDE
    half = CMP_STRIDE * HEAD_DIM
    kv6 = kv_pad.reshape(B, S, 6, G, LANES)
    chunks = lambda kind: kv6[:, :, kind, :, :HEAD_DIM].transpose(0, 2, 1, 3).reshape(BG, n_ck, half)
    w1 = jnp.stack([w_ck1, w_cv1])
    w1cat = jnp.concatenate([w1[:, :half], w1[:, half:]], axis=2).astype(BF16)
    pe = jnp.stack([pe_k.reshape(1, -1), pe_v.reshape(1, -1)]).astype(BF16)
    w2 = jnp.pad(jnp.stack([w_ck2, w_cv2]), ((0, 0), (0, 0), (0, LANES - HEAD_DIM))).astype(BF16)
    cmp_pos = np.arange(n_ck) * CMP_STRIDE + CMP_LEN - 1
    kc, vct = _cmp_mlp(chunks(0), chunks(1), pe, w1cat, w1.astype(BF16), w2,
                       jnp.asarray(_pos_columns(cmp_pos), BF16), jnp.asarray(_ones_row_tail(n_ck), BF16))

    band = jnp.asarray(_band_table(NSA_TQ))
    cmpmask = jnp.asarray(_cmp_mask_table(n_ck, NSA_TQ))
    ovt = jnp.asarray(_overlap_t(n_ck, S // SLC_LEN), BF16)
    o_nsa = _nsa(q_pad, gates, kc, vct, kv_pad, jnp.asarray(_pos_columns(np.arange(S)), BF16),
                 jnp.asarray(_ones_row_tail(S), BF16), band, cmpmask, ovt, B=B, S=S)

    od, ld = [], []
    for gi, (window, dil) in enumerate(DIL_CONFIGS):
        Ls = S // dil
        max_back = window // dil
        tq = min(DIL_TQ, Ls)
        width = tq + max_back if Ls > tq else tq
        o, lse = _dil_attn(qkv_d[gi].reshape(B * dil, Ls, DIL_QKV),
                           jnp.asarray(_dil_dist_table(tq, width, max_back, dil)),
                           max_back=max_back, slopes=DIL_SLOPES[gi * DIL_HPG:(gi + 1) * DIL_HPG], tq=tq)
        od.append(o.reshape(B, dil, Ls, DIL_OUT))
        ld.append(lse.reshape(B, dil, Ls, DIL_OUT))

    return _final(x2d, mix_pre[None, :], mix_post[None, :], o_nsa, od, ld, w_in[:, o_m:].astype(BF16),
                  w_nsa_o.astype(BF16), w_dil_o.astype(BF16), w_mix_out.astype(BF16), S=S)


def kernel(x, ffn1_pre, ffn1_post, ffn1_w_gate, ffn1_w_up, ffn1_w_down, mix_pre, mix_post, w_in, nsa_pe_k, nsa_w_ck1, nsa_w_ck2, nsa_pe_v, nsa_w_cv1, nsa_w_cv2, w_nsa_o, w_dil_o, w_mix_out, ffn2_pre, ffn2_post, ffn2_w_gate, ffn2_w_up, ffn2_w_down):
    B, S, _ = x.shape
    x2d = x.reshape(B * S, D_MODEL)
    for l in range(ffn1_pre.shape[0]):
        x2d = _ffn(x2d, ffn1_pre[l][None, :], ffn1_post[l][None, :], ffn1_w_gate[l].astype(BF16),
                   ffn1_w_up[l].astype(BF16), ffn1_w_down[l].astype(BF16))
        x2d = _mixer(x2d, B, S, mix_pre[l], mix_post[l], w_in[l], nsa_pe_k[l], nsa_w_ck1[l], nsa_w_ck2[l],
                     nsa_pe_v[l], nsa_w_cv1[l], nsa_w_cv2[l], w_nsa_o[l], w_dil_o[l], w_mix_out[l])
        x2d = _ffn(x2d, ffn2_pre[l][None, :], ffn2_post[l][None, :], ffn2_w_gate[l].astype(BF16),
                   ffn2_w_up[l].astype(BF16), ffn2_w_down[l].astype(BF16))
    return x2d.reshape(B, S, D_MODEL)
```

```python
import functools
import math

import numpy as np
import jax
import jax.numpy as jnp
from jax import lax
from jax.experimental import pallas as pl
from jax.experimental.pallas import tpu as pltpu

F32 = jnp.float32
BF16 = jnp.bfloat16

D_MODEL = 1024
HEAD_DIM = 64
RMS_EPS = 1e-6
NSA_HEADS = 8
NSA_GROUPS = 2
NSA_REP = NSA_HEADS // NSA_GROUPS
CMP_LEN = 32
CMP_STRIDE = 16
CMP_HIDDEN = 256
SLC_LEN = 64
N_SEL = 16
NSA_WINDOW = 512
DIL_CONFIGS = ((128, 1), (512, 4), (2048, 16))
DIL_HPG = 4
DIL_HEADS = DIL_HPG * len(DIL_CONFIGS)
D_FF = 2816
Q_NSA = NSA_HEADS * HEAD_DIM
KV_NSA = NSA_GROUPS * HEAD_DIM
GATE_NSA = NSA_HEADS * 3
QKV_DIL = DIL_HEADS * HEAD_DIM
DIL_OUT = DIL_HPG * HEAD_DIM
DIL_QKV = 3 * DIL_OUT
DIL_TQ = 256
LANES = 128
NEG = -1e30
SCALE = HEAD_DIM ** -0.5
LOG2E = math.log2(math.e)
VMEM_LIMIT = 56 * 1024 * 1024
NSA_TQ = 128
VT_ROWS = 80


def _slopes(n):
    return [float(np.float32(2.0 ** (-8.0 * (i + 1) / n))) for i in range(n)]


NSA_SLOPES = _slopes(NSA_HEADS)
DIL_SLOPES = _slopes(DIL_HEADS)


def _params(sem):
    return pltpu.CompilerParams(dimension_semantics=sem, vmem_limit_bytes=VMEM_LIMIT)


def _rms(x, g):
    ms = jnp.mean(x * x, axis=-1, keepdims=True)
    return x * lax.rsqrt(ms + RMS_EPS) * g


def _sigmoid(x):
    return 1.0 / (1.0 + jnp.exp(-x))


def _dot(a, b):
    return jnp.dot(a, b, preferred_element_type=F32)


def _dot_nt(a, b):
    return lax.dot_general(a, b, (((1,), (1,)), ((), ())), preferred_element_type=F32)


def _split3(a):
    hi = a.astype(BF16)
    r1 = a - hi.astype(F32)
    mid = r1.astype(BF16)
    lo = (r1 - mid.astype(F32)).astype(BF16)
    return hi, mid, lo


def _ffn_kernel(x_ref, pre_ref, post_ref, wg_ref, wu_ref, wd_ref, o_ref, h_scr, acc_scr):
    j = pl.program_id(1)

    @pl.when(j == 0)
    def _():
        h_scr[...] = _rms(x_ref[...], pre_ref[...]).astype(BF16)
        acc_scr[...] = jnp.zeros_like(acc_scr)

    h = h_scr[...]
    g = _dot(h, wg_ref[...])
    u = _dot(h, wu_ref[...])
    a = (g * _sigmoid(g) * u).astype(BF16)
    acc_scr[...] += _dot(a, wd_ref[...])

    @pl.when(j == pl.num_programs(1) - 1)
    def _():
        o_ref[...] = x_ref[...] + 0.5 * _rms(acc_scr[...], post_ref[...])


def _ffn(x2d, pre, post, wg, wu, wd, *, tm=512, tf=1408):
    T = x2d.shape[0]
    return pl.pallas_call(
        _ffn_kernel,
        grid=(T // tm, D_FF // tf),
        in_specs=[
            pl.BlockSpec((tm, D_MODEL), lambda i, j: (i, 0)),
            pl.BlockSpec((1, D_MODEL), lambda i, j: (0, 0)),
            pl.BlockSpec((1, D_MODEL), lambda i, j: (0, 0)),
            pl.BlockSpec((D_MODEL, tf), lambda i, j: (0, j)),
            pl.BlockSpec((D_MODEL, tf), lambda i, j: (0, j)),
            pl.BlockSpec((tf, D_MODEL), lambda i, j: (j, 0)),
        ],
        out_specs=pl.BlockSpec((tm, D_MODEL), lambda i, j: (i, 0)),
        out_shape=jax.ShapeDtypeStruct((T, D_MODEL), F32),
        scratch_shapes=[pltpu.VMEM((tm, D_MODEL), BF16), pltpu.VMEM((tm, D_MODEL), F32)],
        compiler_params=_params(("parallel", "arbitrary")),
        name="ffn",
    )(x2d, pre, post, wg, wu, wd)


def _proj_kernel(x_ref, pre_ref, wq_ref, wkv_ref, wd_ref, wgate_ref, q_ref, kv_ref, g_ref, *rest):
    d_refs, scr = rest[:-1], rest[-1]
    tm = x_ref.shape[0]
    h = _rms(x_ref[...], pre_ref[...]).astype(BF16)
    q_ref[...] = _dot(h, wq_ref[...]).astype(BF16)
    kv_ref[...] = _dot(h, wkv_ref[...]).astype(BF16)
    g_ref[...] = _dot(h, wgate_ref[...])
    n_blk = DIL_QKV // LANES
    for gi, (_, dil) in enumerate(DIL_CONFIGS):
        res = _dot(h, wd_ref[:, gi * DIL_QKV:(gi + 1) * DIL_QKV])
        if dil == 1:
            d_refs[gi][0] = res.astype(BF16)
            continue
        for c in range(n_blk):
            scr[c] = res[:, c * LANES:(c + 1) * LANES]
        for r in range(dil):
            for c in range(n_blk):
                rows = scr[c, pl.ds(r, tm // dil, stride=dil), :]
                d_refs[gi][r, :, c * LANES:(c + 1) * LANES] = rows.astype(BF16)


def _proj(x2d, pre, wq, wkv, wd, wgate, *, B, S, tm=256):
    T = x2d.shape[0]
    nt = S // tm
    tok = lambda w: pl.BlockSpec((tm, w), lambda i: (i, 0))
    full = lambda a: pl.BlockSpec(a.shape, lambda i: (0,) * a.ndim)
    d_specs = [pl.BlockSpec((None, dil, tm // dil, DIL_QKV), lambda i: (i // nt, 0, i % nt, 0))
               for _, dil in DIL_CONFIGS]
    d_shapes = [jax.ShapeDtypeStruct((B, dil, S // dil, DIL_QKV), BF16) for _, dil in DIL_CONFIGS]
    return pl.pallas_call(
        _proj_kernel,
        grid=(T // tm,),
        in_specs=[tok(D_MODEL), full(pre), full(wq), full(wkv), full(wd), full(wgate)],
        out_specs=[tok(wq.shape[1]), tok(wkv.shape[1]), tok(LANES)] + d_specs,
        out_shape=[jax.ShapeDtypeStruct((T, wq.shape[1]), BF16), jax.ShapeDtypeStruct((T, wkv.shape[1]), BF16),
                   jax.ShapeDtypeStruct((T, LANES), F32)] + d_shapes,
        scratch_shapes=[pltpu.VMEM((DIL_QKV // LANES, tm, LANES), F32)],
        compiler_params=_params(("parallel",)),
        name="proj",
    )(x2d, pre, wq, wkv, wd, wgate)


def _gelu_tanh(x):
    c = math.sqrt(2.0 / math.pi)
    return x * (0.5 * (1.0 + jnp.tanh(c * (x + 0.044715 * (x * x * x)))))


def _cmp_mlp_kernel(xk_ref, xv_ref, pe_ref, w1cat_ref, w1_ref, w2_ref, kcols_ref, vtail_ref, kc_ref, vct_ref):
    n_chunks = xk_ref.shape[0]

    def mlp(x_ref, t):
        ab = _dot(x_ref[...], w1cat_ref[t])
        first = ab[:, :CMP_HIDDEN]
        second = pltpu.roll(ab[:, CMP_HIDDEN:], n_chunks - 1, 0)
        pe8 = jnp.broadcast_to(pe_ref[t], (8, pe_ref.shape[2]))
        bias = _dot(pe8, w1_ref[t])[0:1]
        hid = _gelu_tanh(first + second + bias).astype(BF16)
        out = _dot(hid, w2_ref[t])
        row = lax.broadcasted_iota(jnp.int32, out.shape, 0)
        return jnp.where(row < n_chunks - 1, out, 0.0)

    kc_ref[...] = mlp(xk_ref, 0).astype(BF16) + kcols_ref[...]
    vct_ref[...] = mlp(xv_ref, 1).T[0:VT_ROWS].astype(BF16) + vtail_ref[...]


def _cmp_mlp(xk, xv, pe, w1cat, w1, w2, kcols, vtail):
    BG, n_chunks, width = xk.shape
    per_n = lambda shape: pl.BlockSpec((None,) + shape, lambda n: (n, 0, 0))
    full = lambda a: pl.BlockSpec(a.shape, lambda n: (0,) * a.ndim)
    return pl.pallas_call(
        _cmp_mlp_kernel,
        grid=(BG,),
        in_specs=[per_n((n_chunks, width)), per_n((n_chunks, width)), full(pe), full(w1cat), full(w1),
                  full(w2), full(kcols), full(vtail)],
        out_specs=[per_n((n_chunks, LANES)), per_n((VT_ROWS, n_chunks))],
        out_shape=[jax.ShapeDtypeStruct((BG, n_chunks, LANES), BF16),
                   jax.ShapeDtypeStruct((BG, VT_ROWS, n_chunks), BF16)],
        compiler_params=_params(("parallel",)),
        name="cmp_mlp",
    )(xk, xv, pe, w1cat, w1, w2, kcols, vtail)


def _exp2_bf16(x):
    return jnp.exp2(x.astype(BF16))


def _online_update(s_t, v_t, m, acc):
    m_new = jnp.maximum(m, jnp.max(s_t, axis=0, keepdims=True))
    alpha = jnp.exp2(m - m_new)
    return m_new, alpha * acc + _dot(v_t, _exp2_bf16(s_t - m_new))


def _nsa_kernel(q_ref, gate_ref, kc_ref, vct_ref, kslc_ref, vslc_ref, kwin_ref, vwin_ref, kcols_ref,
                vtail_ref, band_ref, cmpmask_ref, ovt_ref, o_ref,
                selbias_scr, m_scr, acc_scr, ks_ref, vst_ref, kw_ref, vwt_ref, need_smem, list_smem,
                *, tq, tk):
    g = pl.program_id(1)
    i = pl.program_id(2)
    rows = NSA_REP * tq
    n_ck = kc_ref.shape[0]
    n_slc = ovt_ref.shape[0]
    S = kslc_ref.shape[0]
    q0 = i * tq

    @pl.when(i == 0)
    def _():
        ks_ref[...] = kslc_ref[...] + kcols_ref[...]
        kw_ref[...] = kwin_ref[...] + kcols_ref[...]
        t_chunk = min(S, 512)
        for src, dst in ((vslc_ref, vst_ref), (vwin_ref, vwt_ref)):
            dst[HEAD_DIM:VT_ROWS, :] = vtail_ref[HEAD_DIM:VT_ROWS, :]
            for c in range(S // t_chunk):
                v = src[c * t_chunk:(c + 1) * t_chunk, :].astype(F32)
                dst[0:HEAD_DIM, c * t_chunk:(c + 1) * t_chunk] = v.T[0:HEAD_DIM].astype(BF16)

    lane = lax.broadcasted_iota(jnp.int32, (1, LANES), 1)
    slabs = []
    for r in range(NSA_REP):
        sl = jnp.where(g == 0, NSA_SLOPES[r], NSA_SLOPES[NSA_REP + r]) * LOG2E
        meta = jnp.zeros((1, LANES), F32)
        for t, coef in enumerate((SLC_LEN * sl, sl, -sl * q0.astype(F32))):
            terms = _split3(jnp.full((1, LANES), coef, F32))
            for u in range(3):
                meta = jnp.where(lane == HEAD_DIM + 3 * t + u, terms[u].astype(F32), meta)
        slabs.append(q_ref[:, r * LANES:(r + 1) * LANES] + meta.astype(BF16))
    q_aug = jnp.concatenate(slabs, axis=0)

    s_t = _dot_nt(kc_ref[...], q_aug)
    cstart = pl.multiple_of((n_ck - 8) - (tq // CMP_STRIDE) * i, 8)
    s_t = s_t + cmpmask_ref[pl.ds(cstart, n_ck), :]
    m = jnp.max(s_t, axis=0, keepdims=True)
    any_key = m > 0.5 * NEG
    p = jnp.exp2(s_t - m)
    den = jnp.sum(p, axis=0, keepdims=True)
    p = p * jnp.where(any_key, 1.0 / den, 0.0)
    o_cmp = _dot(vct_ref[0:HEAD_DIM, :], p.astype(BF16))

    psum = p[:, 0:tq]
    for r in range(1, NSA_REP):
        psum = psum + p[:, r * tq:(r + 1) * tq]
    hi, mid, _ = _split3(psum)
    ovt = ovt_ref[...]
    imp = _dot(ovt, hi) + _dot(ovt, mid)
    pos = q0 + lax.broadcasted_iota(jnp.int32, (1, tq), 1)
    own = pos // SLC_LEN
    jb = lax.broadcasted_iota(jnp.int32, (n_slc, 1), 0)
    jbf = jb.astype(F32)
    forced = (jb == 0) | (jb == own) | (jb == own - 1)
    valid = jb * SLC_LEN <= pos
    vals = jnp.where(valid & jnp.logical_not(forced), imp, -jnp.inf)
    sel = jnp.where(forced, 1.0, 0.0)
    for _ in range(N_SEL - 3):
        mx = jnp.max(vals, axis=0, keepdims=True)
        cand = jnp.where(vals == mx, jbf, 1e9)
        idx = jnp.min(cand, axis=0, keepdims=True)
        pick = (jbf == idx) & (mx > -jnp.inf)
        sel = jnp.where(pick, 1.0, sel)
        vals = jnp.where(pick, -jnp.inf, vals)
    sbias = jnp.where((sel > 0.5) & (jb < q0 // SLC_LEN), 0.0, NEG)
    selbias_scr[...] = jnp.concatenate([sbias] * NSA_REP, axis=1)
    blocks_per_chunk = tk // SLC_LEN
    for c in range(n_slc // blocks_per_chunk):
        chunk_max = jnp.max(sbias[c * blocks_per_chunk:(c + 1) * blocks_per_chunk])
        need_smem[c] = (chunk_max > 0.5 * NEG).astype(jnp.int32)
    n_need = jnp.int32(0)
    for c in range(n_slc // blocks_per_chunk):
        list_smem[n_need] = jnp.int32(c)
        n_need = n_need + need_smem[c]

    width = NSA_WINDOW + tq
    back = jnp.minimum(q0, NSA_WINDOW)
    ws = pl.multiple_of(q0 - back, tq)
    s_w = _dot_nt(kw_ref[pl.ds(ws, width), :], q_aug)
    s_w = s_w + band_ref[pl.ds(pl.multiple_of(NSA_WINDOW - back, tq), width), :]
    m_w = jnp.max(s_w, axis=0, keepdims=True)
    acc_w = _dot(vwt_ref[:, pl.ds(ws, width)], _exp2_bf16(s_w - m_w))
    o_win = acc_w[0:HEAD_DIM] * (1.0 / acc_w[HEAD_DIM:HEAD_DIM + 1])

    qs = pl.multiple_of(q0, tq)
    s_d = _dot_nt(ks_ref[pl.ds(qs, tq), :], q_aug) + band_ref[NSA_WINDOW:NSA_WINDOW + tq, :]
    m_d = jnp.max(s_d, axis=0, keepdims=True)
    m_scr[...] = m_d
    acc_scr[...] = _dot(vst_ref[:, pl.ds(qs, tq)], _exp2_bf16(s_d - m_d))

    def chunk_update(c, m_run, acc):
        ks = pl.multiple_of(c * tk, tk)
        s_c = _dot_nt(ks_ref[pl.ds(ks, tk), :], q_aug)
        parts = []
        for jj in range(blocks_per_chunk):
            brow = selbias_scr[pl.ds(c * blocks_per_chunk + jj, 1), :]
            parts.append(s_c[jj * SLC_LEN:(jj + 1) * SLC_LEN] + brow)
        return _online_update(jnp.concatenate(parts, axis=0), vst_ref[:, pl.ds(ks, tk)], m_run, acc)

    def pair_body(t, carry):
        m_run, acc = chunk_update(list_smem[2 * t], m_scr[...], acc_scr[...])
        m_run, acc = chunk_update(list_smem[2 * t + 1], m_run, acc)
        m_scr[...] = m_run
        acc_scr[...] = acc
        return carry

    lax.fori_loop(0, n_need // 2, pair_body, 0)

    @pl.when(n_need % 2 == 1)
    def _():
        m_run, acc = chunk_update(list_smem[n_need - 1], m_scr[...], acc_scr[...])
        m_scr[...] = m_run
        acc_scr[...] = acc

    acc = acc_scr[...]
    o_slc = acc[0:HEAD_DIM] * (1.0 / acc[HEAD_DIM:HEAD_DIM + 1])

    gate_t = _sigmoid(gate_ref[...].T)
    outs = []
    for r in range(NSA_REP):
        cols = slice(r * tq, (r + 1) * tq)

        def gate_row(branch):
            lo_row = 3 * r + branch
            hi_row = 3 * (NSA_REP + r) + branch
            return jnp.where(g == 0, gate_t[lo_row:lo_row + 1], gate_t[hi_row:hi_row + 1])

        outs.append(gate_row(0) * o_cmp[:, cols] + gate_row(1) * o_slc[:, cols]
                    + gate_row(2) * o_win[:, cols])
    o_ref[...] = jnp.concatenate(outs, axis=0).T.astype(BF16)


def _nsa(q_pad, gates, kc, vct, kv_pad, kcols, vtail, band, cmpmask, ovt, *, B, S, tq=NSA_TQ, tk=512):
    G = NSA_GROUPS
    n_ck = kc.shape[1]
    n_slc = ovt.shape[0]
    rows = NSA_REP * tq
    nq = S // tq
    n_chunks = n_slc * SLC_LEN // tk
    per_bg = lambda shape: pl.BlockSpec((None,) + shape, lambda b, g, i: (b * G + g, 0, 0))
    full = lambda a: pl.BlockSpec(a.shape, lambda b, g, i: (0,) * a.ndim)
    kv_kind = lambda kind: pl.BlockSpec((S, LANES), lambda b, g, i: (b, kind * G + g))
    return pl.pallas_call(
        functools.partial(_nsa_kernel, tq=tq, tk=tk),
        grid=(B, G, nq),
        in_specs=[
            pl.BlockSpec((tq, NSA_REP * LANES), lambda b, g, i: (b * nq + i, g)),
            pl.BlockSpec((tq, LANES), lambda b, g, i: (b * nq + i, 0)),
            per_bg((n_ck, LANES)), per_bg((VT_ROWS, n_ck)),
            kv_kind(2), kv_kind(3), kv_kind(4), kv_kind(5),
            full(kcols), full(vtail), full(band), full(cmpmask), full(ovt),
        ],
        out_specs=pl.BlockSpec((tq, NSA_REP * HEAD_DIM), lambda b, g, i: (b * nq + i, g)),
        out_shape=jax.ShapeDtypeStruct((B * S, Q_NSA), BF16),
        scratch_shapes=[pltpu.VMEM((n_slc, rows), F32), pltpu.VMEM((1, rows), F32),
                        pltpu.VMEM((VT_ROWS, rows), F32),
                        pltpu.VMEM((S, LANES), BF16), pltpu.VMEM((VT_ROWS, S), BF16),
                        pltpu.VMEM((S, LANES), BF16), pltpu.VMEM((VT_ROWS, S), BF16),
                        pltpu.SMEM((n_chunks,), jnp.int32), pltpu.SMEM((n_chunks + 1,), jnp.int32)],
        compiler_params=_params(("arbitrary", "arbitrary", "arbitrary")),
        name="nsa",
    )(q_pad, gates, kc, vct, kv_pad, kv_pad, kv_pad, kv_pad, kcols, vtail, band, cmpmask, ovt)


def _dil_attn_kernel(q_ref, k_ref, v_ref, dist_ref, o_ref, lse_ref, vt_scr, *, tq, width, max_back, slopes):
    i = pl.program_id(1)
    L = k_ref.shape[0]
    t_chunk = min(L, 512)

    @pl.when(i == 0)
    def _():
        for pair in range(DIL_OUT // LANES):
            for c in range(L // t_chunk):
                v = v_ref[c * t_chunk:(c + 1) * t_chunk, pair * LANES:(pair + 1) * LANES].astype(F32)
                vt_scr[pair * LANES:(pair + 1) * LANES, c * t_chunk:(c + 1) * t_chunk] = v.T.astype(BF16)

    back = jnp.minimum(i * tq, max_back)
    ks = pl.multiple_of(i * tq - back, max_back)
    dist = dist_ref[pl.ds(pl.multiple_of(max_back - back, max_back), width), :]
    lane = lax.broadcasted_iota(jnp.int32, (1, LANES), 1)
    outs, lses = [], []
    for h in range(DIL_HPG):
        pair, half = divmod(h, 2)
        in_head = (lane >= half * HEAD_DIM) & (lane < (half + 1) * HEAD_DIM)
        q = q_ref[:, pair * LANES:(pair + 1) * LANES]
        q = jnp.where(in_head, q, jnp.zeros_like(q))
        k = k_ref[pl.ds(ks, width), pair * LANES:(pair + 1) * LANES]
        s_t = _dot_nt(k, q) - (slopes[h] * LOG2E) * dist
        m = jnp.max(s_t, axis=0, keepdims=True)
        p = jnp.exp2(s_t - m)
        den = jnp.sum(p, axis=0, keepdims=True)
        o_t =_dot(vt_scr[h * HEAD_DIM:(h + 1) * HEAD_DIM, pl.ds(ks, width)], p.astype(BF16))
        outs.append(o_t * (1.0 / den))
        lses.append(jnp.broadcast_to((m + jnp.log2(den)) * (1.0 / LOG2E), (HEAD_DIM, tq)))
    o_ref[...] = jnp.concatenate(outs, axis=0).T
    lse_ref[...] = jnp.concatenate(lses, axis=0).T


def _dil_attn(qkv, dist_tbl, *, max_back, slopes, tq):
    N, L, _ = qkv.shape
    width = dist_tbl.shape[0] - max_back
    kern = functools.partial(_dil_attn_kernel, tq=tq, width=width, max_back=max_back, slopes=slopes)
    return pl.pallas_call(
        kern,
        grid=(N, L // tq),
        in_specs=[
            pl.BlockSpec((None, tq, DIL_OUT), lambda n, i: (n, i, 0)),
            pl.BlockSpec((None, L, DIL_OUT), lambda n, i: (n, 0, 1)),
            pl.BlockSpec((None, L, DIL_OUT), lambda n, i: (n, 0, 2)),
            pl.BlockSpec(dist_tbl.shape, lambda n, i: (0, 0)),
        ],
        out_specs=[
            pl.BlockSpec((None, tq, DIL_OUT), lambda n, i: (n, i, 0)),
            pl.BlockSpec((None, tq, DIL_OUT), lambda n, i: (n, i, 0)),
        ],
        out_shape=[
            jax.ShapeDtypeStruct((N, L, DIL_OUT), F32),
            jax.ShapeDtypeStruct((N, L, DIL_OUT), F32),
        ],
        scratch_shapes=[pltpu.VMEM((DIL_OUT, L), BF16)],
        compiler_params=_params(("parallel", "arbitrary")),
        name="dil_attn",
    )(qkv, qkv, qkv, dist_tbl)


def _final_kernel(x_ref, pre_ref, post_ref, onsa_ref, od0_ref, od1_ref, od2_ref, l0_ref, l1_ref, l2_ref,
                  wmerge_ref, wnsa_ref, wdil_ref, wout_ref, o_ref, scr):
    tm = x_ref.shape[0]
    x = x_ref[...]
    h = _rms(x, pre_ref[...]).astype(BF16)
    merge = _dot(h, wmerge_ref[...])
    g_a = _sigmoid(merge[:, :D_MODEL])
    g_b = _sigmoid(merge[:, D_MODEL:])
    y_nsa = _dot(onsa_ref[...], wnsa_ref[...])

    n_blk = DIL_OUT // LANES
    slot = 0
    token_major = []
    for (_, dil), o_in, l_in in zip(DIL_CONFIGS, (od0_ref, od1_ref, od2_ref), (l0_ref, l1_ref, l2_ref)):
        pair = []
        for src in (o_in, l_in):
            if dil == 1:
                pair.append(src[0])
                continue
            for r in range(dil):
                for c in range(n_blk):
                    scr[slot + c, pl.ds(r, tm // dil, stride=dil), :] = src[r, :, c * LANES:(c + 1) * LANES]
            pair.append(jnp.concatenate([scr[slot + c] for c in range(n_blk)], axis=1))
            slot += n_blk
        token_major.append(pair)
    (o0, l0), (o1, l1), (o2, l2) = token_major
    mx = jnp.maximum(jnp.maximum(l0, l1), l2)
    e0, e1, e2 = jnp.exp(l0 - mx), jnp.exp(l1 - mx), jnp.exp(l2 - mx)
    den = e0 + e1 + e2
    o_dil = (e0 / den) * o0 + (e1 / den) * o1 + (e2 / den) * o2
    y_dil = _dot(o_dil.astype(BF16), wdil_ref[...])

    y = g_a * y_nsa + g_b * y_dil
    z = _dot(y.astype(BF16), wout_ref[...])
    o_ref[...] = x + _rms(z, post_ref[...])


def _final(x2d, pre, post, onsa, od, ld, wmerge, wnsa, wdil, wout, *, S, tm=256):
    T = x2d.shape[0]
    nt = S // tm
    tok = lambda w: pl.BlockSpec((tm, w), lambda i: (i, 0))
    full = lambda a: pl.BlockSpec(a.shape, lambda i: (0,) * a.ndim)
    strided = [pl.BlockSpec((None, dil, tm // dil, DIL_OUT), lambda i: (i // nt, 0, i % nt, 0))
               for _, dil in DIL_CONFIGS]
    n_strided = sum(1 for _, dil in DIL_CONFIGS if dil > 1)
    return pl.pallas_call(
        _final_kernel,
        grid=(T // tm,),
        in_specs=[tok(D_MODEL), full(pre), full(post), tok(Q_NSA)] + strided + strided
        + [full(wmerge), full(wnsa), full(wdil), full(wout)],
        out_specs=tok(D_MODEL),
        out_shape=jax.ShapeDtypeStruct((T, D_MODEL), F32),
        scratch_shapes=[pltpu.VMEM((2 * n_strided * (DIL_OUT // LANES), tm, LANES), F32)],
        compiler_params=_params(("parallel",)),
        name="final",
    )(x2d, pre, post, onsa, od[0], od[1], od[2], ld[0], ld[1], ld[2], wmerge, wnsa, wdil, wout)


def _overlap_t(n_ck, n_slc):
    c0 = np.arange(n_ck)[None, :] * CMP_STRIDE
    s0 = np.arange(n_slc)[:, None] * SLC_LEN
    ov = np.clip(np.minimum(c0 + CMP_LEN, s0 + SLC_LEN) - np.maximum(c0, s0), 0, None) / CMP_LEN
    ov[:, n_ck - 1] = 0.0
    return ov.astype(np.float32)


def _pos_columns(pos):
    cols = np.zeros((len(pos), LANES), np.float32)
    for t, vals in enumerate((pos // SLC_LEN, pos % SLC_LEN, np.ones(len(pos)))):
        for u in range(3):
            cols[:, HEAD_DIM + 3 * t + u] = vals
    return cols


def _ones_row_tail(n):
    tail = np.zeros((VT_ROWS, n), np.float32)
    tail[HEAD_DIM] = 1.0
    return tail


def _band_table(tq):
    u = np.arange(2 * NSA_WINDOW + tq)[:, None]
    r = (np.arange(NSA_REP * tq) % tq)[None, :]
    dist = NSA_WINDOW + r - u
    return np.where((dist >= 0) & (dist <= NSA_WINDOW - 1), 0.0, NEG).astype(np.float32)


def _dil_dist_table(tq, width, max_back, dil):
    u = np.arange(width + max_back)[:, None]
    r = np.arange(tq)[None, :]
    dist = max_back + r - u
    return np.where((dist >= 0) & (dist <= max_back), dist * dil, -NEG).astype(np.float32)


def _cmp_mask_table(n_ck, tq):
    u = np.arange(2 * n_ck - 8)[:, None]
    r = (np.arange(NSA_REP * tq) % tq)[None, :]
    end = (u - (n_ck - 8)) * CMP_STRIDE + CMP_LEN - 1
    return np.where(end <= r, 0.0, NEG).astype(np.float32)


def _mixer(x2d, B, S, mix_pre, mix_post, w_in, pe_k, w_ck1, w_ck2, pe_v, w_cv1, w_cv2,
           w_nsa_o, w_dil_o, w_mix_out):
    G = NSA_GROUPS
    BG = B * G
    assert S % (16 * 128) == 0 and NSA_TQ // CMP_STRIDE == 8
    o_q, o_kv, o_g, o_d, o_m = np.cumsum([0, Q_NSA, 6 * KV_NSA, GATE_NSA, 3 * QKV_DIL]).tolist()
    wq = (w_in[:, o_q:o_kv] * (SCALE * LOG2E)).reshape(D_MODEL, NSA_HEADS, HEAD_DIM)
    wq = jnp.pad(wq, ((0, 0), (0, 0), (0, LANES - HEAD_DIM))).reshape(D_MODEL, NSA_HEADS * LANES)
    w_gate = jnp.pad(w_in[:, o_g:o_d], ((0, 0), (0, LANES - GATE_NSA)))
    wd = w_in[:, o_d:o_m].reshape(D_MODEL, 3, len(DIL_CONFIGS), DIL_OUT)
    wd = (wd * jnp.asarray([SCALE * LOG2E, 1.0, 1.0], F32)[None, :, None, None]).transpose(0, 2, 1, 3)
    wd = wd.reshape(D_MODEL, 3 * QKV_DIL)
    wkv = w_in[:, o_kv:o_g].reshape(D_MODEL, 6 * G, HEAD_DIM)
    wkv = jnp.pad(wkv, ((0, 0), (0, 0), (0, LANES - HEAD_DIM))).reshape(D_MODEL, 6 * G * LANES)
    q_pad, kv_pad, gates, *qkv_d = _proj(x2d, mix_pre[None, :], wq.astype(BF16), wkv.astype(BF16),
                                         wd.astype(BF16), w_gate.astype(BF16), B=B, S=S)

    n_ck = S // CMP_STRIDE
    half = CMP_STRIDE * HEAD_DIM
    def chunks(kind):
        per_g = [kv_pad[:, (kind * G + g) * LANES:(kind * G + g) * LANES + HEAD_DIM].reshape(B, 1, n_ck, half)
                 for g in range(G)]
        return jnp.concatenate(per_g, axis=1).reshape(BG, n_ck, half)
    w1 = jnp.stack([w_ck1, w_cv1])
    w1cat = jnp.concatenate([w1[:, :half], w1[:, half:]], axis=2).astype(BF16)
    pe = jnp.stack([pe_k.reshape(1, -1), pe_v.reshape(1, -1)]).astype(BF16)
    w2 = jnp.pad(jnp.stack([w_ck2, w_cv2]), ((0, 0), (0, 0), (0, LANES - HEAD_DIM))).astype(BF16)
    cmp_pos = np.arange(n_ck) * CMP_STRIDE + CMP_LEN - 1
    kc, vct = _cmp_mlp(chunks(0), chunks(1), pe, w1cat, w1.astype(BF16), w2,
                       jnp.asarray(_pos_columns(cmp_pos), BF16), jnp.asarray(_ones_row_tail(n_ck), BF16))

    band = jnp.asarray(_band_table(NSA_TQ))
    cmpmask = jnp.asarray(_cmp_mask_table(n_ck, NSA_TQ))
    ovt = jnp.asarray(_overlap_t(n_ck, S // SLC_LEN), BF16)
    o_nsa = _nsa(q_pad, gates, kc, vct, kv_pad, jnp.asarray(_pos_columns(np.arange(S)), BF16),
                 jnp.asarray(_ones_row_tail(S), BF16), band, cmpmask, ovt, B=B, S=S)

    od, ld = [], []
    for gi, (window, dil) in enumerate(DIL_CONFIGS):
        Ls = S // dil
        max_back = window // dil
        tq = min(DIL_TQ, Ls)
        width = tq + max_back if Ls > tq else tq
        o, lse = _dil_attn(qkv_d[gi].reshape(B * dil, Ls, DIL_QKV),
                           jnp.asarray(_dil_dist_table(tq, width, max_back, dil)),
                           max_back=max_back, slopes=DIL_SLOPES[gi * DIL_HPG:(gi + 1) * DIL_HPG], tq=tq)
        od.append(o.reshape(B, dil, Ls, DIL_OUT))
        ld.append(lse.reshape(B, dil, Ls, DIL_OUT))

    return _final(x2d, mix_pre[None, :], mix_post[None, :], o_nsa, od, ld, w_in[:, o_m:].astype(BF16),
                  w_nsa_o.astype(BF16), w_dil_o.astype(BF16), w_mix_out.astype(BF16), S=S)


def kernel(x, ffn1_pre, ffn1_post, ffn1_w_gate, ffn1_w_up, ffn1_w_down, mix_pre, mix_post, w_in, nsa_pe_k, nsa_w_ck1, nsa_w_ck2, nsa_pe_v, nsa_w_cv1, nsa_w_cv2, w_nsa_o, w_dil_o, w_mix_out, ffn2_pre, ffn2_post, ffn2_w_gate, ffn2_w_up, ffn2_w_down):
    B, S, _ = x.shape
    x2d = x.reshape(B * S, D_MODEL)
    for l in range(ffn1_pre.shape[0]):
        x2d = _ffn(x2d, ffn1_pre[l][None, :], ffn1_post[l][None, :], ffn1_w_gate[l].astype(BF16),
                   ffn1_w_up[l].astype(BF16), ffn1_w_down[l].astype(BF16))
        x2d = _mixer(x2d, B, S, mix_pre[l], mix_post[l], w_in[l], nsa_pe_k[l], nsa_w_ck1[l], nsa_w_ck2[l],
                     nsa_pe_v[l], nsa_w_cv1[l], nsa_w_cv2[l], w_nsa_o[l], w_dil_o[l], w_mix_out[l])
        x2d = _ffn(x2d, ffn2_pre[l][None, :], ffn2_post[l][None, :], ffn2_w_gate[l].astype(BF16),
                   ffn2_w_up[l].astype(BF16), ffn2_w_down[l].astype(BF16))
    return x2d.reshape(B, S, D_MODEL)
```

```python
import functools
import math

import numpy as np
import jax
import jax.numpy as jnp
from jax import lax
from jax.experimental import pallas as pl
from jax.experimental.pallas import tpu as pltpu

F32 = jnp.float32
BF16 = jnp.bfloat16

D_MODEL = 1024
HEAD_DIM = 64
RMS_EPS = 1e-6
NSA_HEADS = 8
NSA_GROUPS = 2
NSA_REP = NSA_HEADS // NSA_GROUPS
CMP_LEN = 32
CMP_STRIDE = 16
CMP_HIDDEN = 256
SLC_LEN = 64
N_SEL = 16
NSA_WINDOW = 512
DIL_CONFIGS = ((128, 1), (512, 4), (2048, 16))
DIL_HPG = 4
DIL_HEADS = DIL_HPG * len(DIL_CONFIGS)
D_FF = 2816
Q_NSA = NSA_HEADS * HEAD_DIM
KV_NSA = NSA_GROUPS * HEAD_DIM
GATE_NSA = NSA_HEADS * 3
QKV_DIL = DIL_HEADS * HEAD_DIM
DIL_OUT = DIL_HPG * HEAD_DIM
DIL_QKV = 3 * DIL_OUT
DIL_TQ = 256
LANES = 128
NEG = -1e30
SCALE = HEAD_DIM ** -0.5
LOG2E = math.log2(math.e)
VMEM_LIMIT = 56 * 1024 * 1024
NSA_TQ = 128
VT_ROWS = 80


def _slopes(n):
    return [float(np.float32(2.0 ** (-8.0 * (i + 1) / n))) for i in range(n)]


NSA_SLOPES = _slopes(NSA_HEADS)
DIL_SLOPES = _slopes(DIL_HEADS)


def _params(sem):
    return pltpu.CompilerParams(dimension_semantics=sem, vmem_limit_bytes=VMEM_LIMIT)


def _rms(x, g):
    ms = jnp.mean(x * x, axis=-1, keepdims=True)
    return x * lax.rsqrt(ms + RMS_EPS) * g


def _sigmoid(x):
    return 1.0 / (1.0 + jnp.exp(-x))


def _dot(a, b):
    return jnp.dot(a, b, preferred_element_type=F32)


def _dot_nt(a, b):
    return lax.dot_general(a, b, (((1,), (1,)), ((), ())), preferred_element_type=F32)


def _split3(a):
    hi = a.astype(BF16)
    r1 = a - hi.astype(F32)
    mid = r1.astype(BF16)
    lo = (r1 - mid.astype(F32)).astype(BF16)
    return hi, mid, lo


def _ffn_kernel(x_ref, pre_ref, post_ref, wg_ref, wu_ref, wd_ref, o_ref):
    x = x_ref[...]
    h = _rms(x, pre_ref[...]).astype(BF16)
    g = _dot(h, wg_ref[...])
    u = _dot(h, wu_ref[...])
    a = (g * _sigmoid(g) * u).astype(BF16)
    o_ref[...] = x + 0.5 * _rms(_dot(a, wd_ref[...]), post_ref[...])


def _resident(a):
    return pl.BlockSpec(a.shape, lambda *_: (0,) * a.ndim, pipeline_mode=pl.Buffered(1))


def _ffn(x2d, pre, post, wg, wu, wd, *, tm=512):
    T = x2d.shape[0]
    return pl.pallas_call(
        _ffn_kernel,
        grid=(T // tm,),
        in_specs=[pl.BlockSpec((tm, D_MODEL), lambda i: (i, 0)), _resident(pre), _resident(post),
                  _resident(wg), _resident(wu), _resident(wd)],
        out_specs=pl.BlockSpec((tm, D_MODEL), lambda i: (i, 0)),
        out_shape=jax.ShapeDtypeStruct((T, D_MODEL), F32),
        compiler_params=_params(("parallel",)),
        name="ffn",
    )(x2d, pre, post, wg, wu, wd)


def _proj_kernel(x_ref, pre_ref, wq_ref, wkv_ref, wd_ref, wgate_ref, q_ref, kv_ref, g_ref, *rest):
    d_refs, scr = rest[:-1], rest[-1]
    tm = x_ref.shape[0]
    h = _rms(x_ref[...], pre_ref[...]).astype(BF16)
    q_ref[...] = _dot(h, wq_ref[...]).astype(BF16)
    kv_ref[...] = _dot(h, wkv_ref[...]).astype(BF16)
    g_ref[...] = _dot(h, wgate_ref[...])
    n_blk = DIL_QKV // LANES
    for gi, (_, dil) in enumerate(DIL_CONFIGS):
        res = _dot(h, wd_ref[:, gi * DIL_QKV:(gi + 1) * DIL_QKV])
        if dil == 1:
            d_refs[gi][0] = res.astype(BF16)
            continue
        for c in range(n_blk):
            scr[c] = res[:, c * LANES:(c + 1) * LANES]
        for r in range(dil):
            for c in range(n_blk):
                rows = scr[c, pl.ds(r, tm // dil, stride=dil), :]
                d_refs[gi][r, :, c * LANES:(c + 1) * LANES] = rows.astype(BF16)


def _proj(x2d, pre, wq, wkv, wd, wgate, *, B, S, tm=256):
    T = x2d.shape[0]
    nt = S // tm
    tok = lambda w: pl.BlockSpec((tm, w), lambda i: (i, 0))
    full = lambda a: pl.BlockSpec(a.shape, lambda i: (0,) * a.ndim)
    d_specs = [pl.BlockSpec((None, dil, tm // dil, DIL_QKV), lambda i: (i // nt, 0, i % nt, 0))
               for _, dil in DIL_CONFIGS]
    d_shapes = [jax.ShapeDtypeStruct((B, dil, S // dil, DIL_QKV), BF16) for _, dil in DIL_CONFIGS]
    return pl.pallas_call(
        _proj_kernel,
        grid=(T // tm,),
        in_specs=[tok(D_MODEL), full(pre), full(wq), full(wkv), full(wd), full(wgate)],
        out_specs=[tok(wq.shape[1]), tok(wkv.shape[1]), tok(LANES)] + d_specs,
        out_shape=[jax.ShapeDtypeStruct((T, wq.shape[1]), BF16), jax.ShapeDtypeStruct((T, wkv.shape[1]), BF16),
                   jax.ShapeDtypeStruct((T, LANES), F32)] + d_shapes,
        scratch_shapes=[pltpu.VMEM((DIL_QKV // LANES, tm, LANES), F32)],
        compiler_params=_params(("parallel",)),
        name="proj",
    )(x2d, pre, wq, wkv, wd, wgate)


def _gelu_tanh(x):
    c = math.sqrt(2.0 / math.pi)
    return x * (0.5 * (1.0 + jnp.tanh(c * (x + 0.044715 * (x * x * x)))))


def _cmp_mlp_kernel(xk_ref, xv_ref, pe_ref, w1cat_ref, w1_ref, w2_ref, kcols_ref, vtail_ref, kc_ref, vct_ref):
    n_chunks = xk_ref.shape[0]

    def mlp(x_ref, t):
        ab = _dot(x_ref[...], w1cat_ref[t])
        first = ab[:, :CMP_HIDDEN]
        second = pltpu.roll(ab[:, CMP_HIDDEN:], n_chunks - 1, 0)
        pe8 = jnp.broadcast_to(pe_ref[t], (8, pe_ref.shape[2]))
        bias = _dot(pe8, w1_ref[t])[0:1]
        hid = _gelu_tanh(first + second + bias).astype(BF16)
        out = _dot(hid, w2_ref[t])
        row = lax.broadcasted_iota(jnp.int32, out.shape, 0)
        return jnp.where(row < n_chunks - 1, out, 0.0)

    kc_ref[...] = mlp(xk_ref, 0).astype(BF16) + kcols_ref[...]
    vct_ref[...] = mlp(xv_ref, 1).T[0:VT_ROWS].astype(BF16) + vtail_ref[...]


def _cmp_mlp(xk, xv, pe, w1cat, w1, w2, kcols, vtail):
    BG, n_chunks, width = xk.shape
    per_n = lambda shape: pl.BlockSpec((None,) + shape, lambda n: (n, 0, 0))
    full = lambda a: pl.BlockSpec(a.shape, lambda n: (0,) * a.ndim)
    return pl.pallas_call(
        _cmp_mlp_kernel,
        grid=(BG,),
        in_specs=[per_n((n_chunks, width)), per_n((n_chunks, width)), full(pe), full(w1cat), full(w1),
                  full(w2), full(kcols), full(vtail)],
        out_specs=[per_n((n_chunks, LANES)), per_n((VT_ROWS, n_chunks))],
        out_shape=[jax.ShapeDtypeStruct((BG, n_chunks, LANES), BF16),
                   jax.ShapeDtypeStruct((BG, VT_ROWS, n_chunks), BF16)],
        compiler_params=_params(("parallel",)),
        name="cmp_mlp",
    )(xk, xv, pe, w1cat, w1, w2, kcols, vtail)


def _exp2_bf16(x):
    return jnp.exp2(x.astype(BF16))


def _online_update(s_t, v_t, m, acc):
    m_new = jnp.maximum(m, jnp.max(s_t, axis=0, keepdims=True))
    alpha = jnp.exp2(m - m_new)
    return m_new, alpha * acc + _dot(v_t, _exp2_bf16(s_t - m_new))


def _nsa_kernel(q_ref, gate_ref, kc_ref, vct_ref, kslc_ref, vslc_ref, kwin_ref, vwin_ref, kcols_ref,
                vtail_ref, band_ref, cmpmask_ref, ovt_ref, o_ref,
                selbias_scr, m_scr, acc_scr, ks_ref, vst_ref, kw_ref, vwt_ref, need_smem, list_smem,
                *, tq, tk):
    g = pl.program_id(1)
    i = pl.program_id(2)
    rows = NSA_REP * tq
    n_ck = kc_ref.shape[0]
    n_slc = ovt_ref.shape[0]
    S = kslc_ref.shape[0]
    q0 = i * tq

    @pl.when(i == 0)
    def _():
        ks_ref[...] = kslc_ref[...] + kcols_ref[...]
        kw_ref[...] = kwin_ref[...] + kcols_ref[...]
        t_chunk = min(S, 512)
        for src, dst in ((vslc_ref, vst_ref), (vwin_ref, vwt_ref)):
            dst[HEAD_DIM:VT_ROWS, :] = vtail_ref[HEAD_DIM:VT_ROWS, :]
            for c in range(S // t_chunk):
                v = src[c * t_chunk:(c + 1) * t_chunk, :].astype(F32)
                dst[0:HEAD_DIM, c * t_chunk:(c + 1) * t_chunk] = v.T[0:HEAD_DIM].astype(BF16)

    lane = lax.broadcasted_iota(jnp.int32, (1, LANES), 1)
    slabs = []
    for r in range(NSA_REP):
        sl = jnp.where(g == 0, NSA_SLOPES[r], NSA_SLOPES[NSA_REP + r]) * LOG2E
        meta = jnp.zeros((1, LANES), F32)
        for t, coef in enumerate((SLC_LEN * sl, sl, -sl * q0.astype(F32))):
            terms = _split3(jnp.full((1, LANES), coef, F32))
            for u in range(3):
                meta = jnp.where(lane == HEAD_DIM + 3 * t + u, terms[u].astype(F32), meta)
        slabs.append(q_ref[:, r * LANES:(r + 1) * LANES] + meta.astype(BF16))
    q_aug = jnp.concatenate(slabs, axis=0)

    width = NSA_WINDOW + tq
    back = jnp.minimum(q0, NSA_WINDOW)
    ws = pl.multiple_of(q0 - back, tq)
    qs = pl.multiple_of(q0, tq)
    s_all = _dot_nt(jnp.concatenate([kc_ref[...], kw_ref[pl.ds(ws, width), :], ks_ref[pl.ds(qs, tq), :]],
                                    axis=0), q_aug)

    s_t = s_all[0:n_ck]
    cstart = pl.multiple_of((n_ck - 8) - (tq // CMP_STRIDE) * i, 8)
    s_t = s_t + cmpmask_ref[pl.ds(cstart, n_ck), :]
    m = jnp.max(s_t, axis=0, keepdims=True)
    any_key = m > 0.5 * NEG
    p = jnp.exp2(s_t - m)
    den = jnp.sum(p, axis=0, keepdims=True)
    p = p * jnp.where(any_key, 1.0 / den, 0.0)
    o_cmp = _dot(vct_ref[0:HEAD_DIM, :], p.astype(BF16))

    psum = p[:, 0:tq]
    for r in range(1, NSA_REP):
        psum = psum + p[:, r * tq:(r + 1) * tq]
    hi, mid, _ = _split3(psum)
    ovt = ovt_ref[...]
    imp = _dot(ovt, hi) + _dot(ovt, mid)
    pos = q0 + lax.broadcasted_iota(jnp.int32, (1, tq), 1)
    own = pos // SLC_LEN
    jb = lax.broadcasted_iota(jnp.int32, (n_slc, 1), 0)
    jbf = jb.astype(F32)
    forced = (jb == 0) | (jb == own) | (jb == own - 1)
    valid = jb * SLC_LEN <= pos
    vals = jnp.where(valid & jnp.logical_not(forced), imp, -jnp.inf)
    sel = jnp.where(forced, 1.0, 0.0)
    for _ in range(N_SEL - 3):
        mx = jnp.max(vals, axis=0, keepdims=True)
        cand = jnp.where(vals == mx, jbf, 1e9)
        idx = jnp.min(cand, axis=0, keepdims=True)
        pick = (jbf == idx) & (mx > -jnp.inf)
        sel = jnp.where(pick, 1.0, sel)
        vals = jnp.where(pick, -jnp.inf, vals)
    sbias = jnp.where((sel > 0.5) & (jb < q0 // SLC_LEN), 0.0, NEG)
    selbias_scr[...] = jnp.concatenate([sbias] * NSA_REP, axis=1)
    blocks_per_chunk = tk // SLC_LEN
    for c in range(n_slc // blocks_per_chunk):
        chunk_max = jnp.max(sbias[c * blocks_per_chunk:(c + 1) * blocks_per_chunk])
        need_smem[c] = (chunk_max > 0.5 * NEG).astype(jnp.int32)
    n_need = jnp.int32(0)
    for c in range(n_slc // blocks_per_chunk):
        list_smem[n_need] = jnp.int32(c)
        n_need = n_need + need_smem[c]

    s_w = s_all[n_ck:n_ck + width] + band_ref[pl.ds(pl.multiple_of(NSA_WINDOW - back, tq), width), :]
    m_w = jnp.max(s_w, axis=0, keepdims=True)
    acc_w = _dot(vwt_ref[:, pl.ds(ws, width)], _exp2_bf16(s_w - m_w))
    o_win = acc_w[0:HEAD_DIM] * (1.0 / acc_w[HEAD_DIM:HEAD_DIM + 1])

    s_d = s_all[n_ck + width:n_ck + width + tq] + band_ref[NSA_WINDOW:NSA_WINDOW + tq, :]
    m_d = jnp.max(s_d, axis=0, keepdims=True)
    m_scr[...] = m_d
    acc_scr[...] = _dot(vst_ref[:, pl.ds(qs, tq)], _exp2_bf16(s_d - m_d))

    def masked(s_c, c):
        parts = []
        for jj in range(blocks_per_chunk):
            brow = selbias_scr[pl.ds(c * blocks_per_chunk + jj, 1), :]
            parts.append(s_c[jj * SLC_LEN:(jj + 1) * SLC_LEN] + brow)
        return jnp.concatenate(parts, axis=0)

    def chunk_update(c, m_run, acc):
        ks = pl.multiple_of(c * tk, tk)
        s_c = _dot_nt(ks_ref[pl.ds(ks, tk), :], q_aug)
        return _online_update(masked(s_c, c), vst_ref[:, pl.ds(ks, tk)], m_run, acc)

    def pair_body(t, carry):
        ca, cb = list_smem[2 * t], list_smem[2 * t + 1]
        ka = pl.multiple_of(ca * tk, tk)
        kb = pl.multiple_of(cb * tk, tk)
        keys = jnp.concatenate([ks_ref[pl.ds(ka, tk), :], ks_ref[pl.ds(kb, tk), :]], axis=0)
        s_ab = _dot_nt(keys, q_aug)
        m_run, acc = _online_update(masked(s_ab[0:tk], ca), vst_ref[:, pl.ds(ka, tk)], m_scr[...], acc_scr[...])
        m_run, acc = _online_update(masked(s_ab[tk:2 * tk], cb), vst_ref[:, pl.ds(kb, tk)], m_run, acc)
        m_scr[...] = m_run
        acc_scr[...] = acc
        return carry

    lax.fori_loop(0, n_need // 2, pair_body, 0)

    @pl.when(n_need % 2 == 1)
    def _():
        m_run, acc = chunk_update(list_smem[n_need - 1], m_scr[...], acc_scr[...])
        m_scr[...] = m_run
        acc_scr[...] = acc

    acc = acc_scr[...]
    o_slc = acc[0:HEAD_DIM] * (1.0 / acc[HEAD_DIM:HEAD_DIM + 1])

    gate_t = _sigmoid(gate_ref[...].T)
    outs = []
    for r in range(NSA_REP):
        cols = slice(r * tq, (r + 1) * tq)

        def gate_row(branch):
            lo_row = 3 * r + branch
            hi_row = 3 * (NSA_REP + r) + branch
            return jnp.where(g == 0, gate_t[lo_row:lo_row + 1], gate_t[hi_row:hi_row + 1])

        outs.append(gate_row(0) * o_cmp[:, cols] + gate_row(1) * o_slc[:, cols]
                    + gate_row(2) * o_win[:, cols])
    o_ref[...] = jnp.concatenate(outs, axis=0).T.astype(BF16)


def _nsa(q_pad, gates, kc, vct, kv_pad, kcols, vtail, band, cmpmask, ovt, *, B, S, tq=NSA_TQ, tk=512):
    G = NSA_GROUPS
    n_ck = kc.shape[1]
    n_slc = ovt.shape[0]
    rows = NSA_REP * tq
    nq = S // tq
    n_chunks = n_slc * SLC_LEN // tk
    per_bg = lambda shape: pl.BlockSpec((None,) + shape, lambda b, g, i: (b * G + g, 0, 0))
    full = lambda a: pl.BlockSpec(a.shape, lambda b, g, i: (0,) * a.ndim)
    kv_kind = lambda kind: pl.BlockSpec((S, LANES), lambda b, g, i: (b, kind * G + g))
    return pl.pallas_call(
        functools.partial(_nsa_kernel, tq=tq, tk=tk),
        grid=(B, G, nq),
        in_specs=[
            pl.BlockSpec((tq, NSA_REP * LANES), lambda b, g, i: (b * nq + i, g)),
            pl.BlockSpec((tq, LANES), lambda b, g, i: (b * nq + i, 0)),
            per_bg((n_ck, LANES)), per_bg((VT_ROWS, n_ck)),
            kv_kind(2), kv_kind(3), kv_kind(4), kv_kind(5),
            full(kcols), full(vtail), full(band), full(cmpmask), full(ovt),
        ],
        out_specs=pl.BlockSpec((tq, NSA_REP * HEAD_DIM), lambda b, g, i: (b * nq + i, g)),
        out_shape=jax.ShapeDtypeStruct((B * S, Q_NSA), BF16),
        scratch_shapes=[pltpu.VMEM((n_slc, rows), F32), pltpu.VMEM((1, rows), F32),
                        pltpu.VMEM((VT_ROWS, rows), F32),
                        pltpu.VMEM((S, LANES), BF16), pltpu.VMEM((VT_ROWS, S), BF16),
                        pltpu.VMEM((S, LANES), BF16), pltpu.VMEM((VT_ROWS, S), BF16),
                        pltpu.SMEM((n_chunks,), jnp.int32), pltpu.SMEM((n_chunks + 1,), jnp.int32)],
        compiler_params=_params(("arbitrary", "arbitrary", "arbitrary")),
        name="nsa",
    )(q_pad, gates, kc, vct, kv_pad, kv_pad, kv_pad, kv_pad, kcols, vtail, band, cmpmask, ovt)


def _dil_attn_kernel(q_ref, k_ref, v_ref, dist_ref, o_ref, lse_ref, vt_scr, *, tq, width, max_back, slopes):
    i = pl.program_id(1)
    L = k_ref.shape[0]
    t_chunk = min(L, 512)

    @pl.when(i == 0)
    def _():
        for pair in range(DIL_OUT // LANES):
            for c in range(L // t_chunk):
                v = v_ref[c * t_chunk:(c + 1) * t_chunk, pair * LANES:(pair + 1) * LANES].astype(F32)
                vt_scr[pair * LANES:(pair + 1) * LANES, c * t_chunk:(c + 1) * t_chunk] = v.T.astype(BF16)

    back = jnp.minimum(i * tq, max_back)
    ks = pl.multiple_of(i * tq - back, max_back)
    dist = dist_ref[pl.ds(pl.multiple_of(max_back - back, max_back), width), :]
    lane = lax.broadcasted_iota(jnp.int32, (1, LANES), 1)
    outs, lses = [], []
    for h in range(DIL_HPG):
        pair, half = divmod(h, 2)
        in_head = (lane >= half * HEAD_DIM) & (lane < (half + 1) * HEAD_DIM)
        q = q_ref[:, pair * LANES:(pair + 1) * LANES]
        q = jnp.where(in_head, q, jnp.zeros_like(q))
        k = k_ref[pl.ds(ks, width), pair * LANES:(pair + 1) * LANES]
        s_t = _dot_nt(k, q) - (slopes[h] * LOG2E) * dist
        m = jnp.max(s_t, axis=0, keepdims=True)
        p = jnp.exp2(s_t - m)
        den = jnp.sum(p, axis=0, keepdims=True)
        o_t =_dot(vt_scr[h * HEAD_DIM:(h + 1) * HEAD_DIM, pl.ds(ks, width)], p.astype(BF16))
        outs.append(o_t * (1.0 / den))
        lses.append(jnp.broadcast_to((m + jnp.log2(den)) * (1.0 / LOG2E), (HEAD_DIM, tq)))
    o_ref[...] = jnp.concatenate(outs, axis=0).T
    lse_ref[...] = jnp.concatenate(lses, axis=0).T


def _dil_attn(qkv, dist_tbl, *, max_back, slopes, tq):
    N, L, _ = qkv.shape
    width = dist_tbl.shape[0] - max_back
    kern = functools.partial(_dil_attn_kernel, tq=tq, width=width, max_back=max_back, slopes=slopes)
    return pl.pallas_call(
        kern,
        grid=(N, L // tq),
        in_specs=[
            pl.BlockSpec((None, tq, DIL_OUT), lambda n, i: (n, i, 0)),
            pl.BlockSpec((None, L, DIL_OUT), lambda n, i: (n, 0, 1)),
            pl.BlockSpec((None, L, DIL_OUT), lambda n, i: (n, 0, 2)),
            pl.BlockSpec(dist_tbl.shape, lambda n, i: (0, 0)),
        ],
        out_specs=[
            pl.BlockSpec((None, tq, DIL_OUT), lambda n, i: (n, i, 0)),
            pl.BlockSpec((None, tq, DIL_OUT), lambda n, i: (n, i, 0)),
        ],
        out_shape=[
            jax.ShapeDtypeStruct((N, L, DIL_OUT), F32),
            jax.ShapeDtypeStruct((N, L, DIL_OUT), F32),
        ],
        scratch_shapes=[pltpu.VMEM((DIL_OUT, L), BF16)],
        compiler_params=_params(("parallel", "arbitrary")),
        name="dil_attn",
    )(qkv, qkv, qkv, dist_tbl)


def _final_kernel(x_ref, pre_ref, post_ref, onsa_ref, od0_ref, od1_ref, od2_ref, l0_ref, l1_ref, l2_ref,
                  wmerge_ref, wnsa_ref, wdil_ref, wout_ref, o_ref, scr):
    tm = x_ref.shape[0]
    x = x_ref[...]
    h = _rms(x, pre_ref[...]).astype(BF16)
    merge = _dot(h, wmerge_ref[...])
    g_a = _sigmoid(merge[:, :D_MODEL])
    g_b = _sigmoid(merge[:, D_MODEL:])
    y_nsa = _dot(onsa_ref[...], wnsa_ref[...])

    n_blk = DIL_OUT // LANES
    slot = 0
    token_major = []
    for (_, dil), o_in, l_in in zip(DIL_CONFIGS, (od0_ref, od1_ref, od2_ref), (l0_ref, l1_ref, l2_ref)):
        pair = []
        for src in (o_in, l_in):
            if dil == 1:
                pair.append(src[0])
                continue
            for r in range(dil):
                for c in range(n_blk):
                    scr[slot + c, pl.ds(r, tm // dil, stride=dil), :] = src[r, :, c * LANES:(c + 1) * LANES]
            pair.append(jnp.concatenate([scr[slot + c] for c in range(n_blk)], axis=1))
            slot += n_blk
        token_major.append(pair)
    (o0, l0), (o1, l1), (o2, l2) = token_major
    mx = jnp.maximum(jnp.maximum(l0, l1), l2)
    e0, e1, e2 = jnp.exp(l0 - mx), jnp.exp(l1 - mx), jnp.exp(l2 - mx)
    den = e0 + e1 + e2
    o_dil = (e0 / den) * o0 + (e1 / den) * o1 + (e2 / den) * o2
    y_dil = _dot(o_dil.astype(BF16), wdil_ref[...])

    y = g_a * y_nsa + g_b * y_dil
    z = _dot(y.astype(BF16), wout_ref[...])
    o_ref[...] = x + _rms(z, post_ref[...])


def _final(x2d, pre, post, onsa, od, ld, wmerge, wnsa, wdil, wout, *, S, tm=256):
    T = x2d.shape[0]
    nt = S // tm
    tok = lambda w: pl.BlockSpec((tm, w), lambda i: (i, 0))
    full = lambda a: pl.BlockSpec(a.shape, lambda i: (0,) * a.ndim)
    strided = [pl.BlockSpec((None, dil, tm // dil, DIL_OUT), lambda i: (i // nt, 0, i % nt, 0))
               for _, dil in DIL_CONFIGS]
    n_strided = sum(1 for _, dil in DIL_CONFIGS if dil > 1)
    return pl.pallas_call(
        _final_kernel,
        grid=(T // tm,),
        in_specs=[tok(D_MODEL), full(pre), full(post), tok(Q_NSA)] + strided + strided
        + [full(wmerge), full(wnsa), full(wdil), full(wout)],
        out_specs=tok(D_MODEL),
        out_shape=jax.ShapeDtypeStruct((T, D_MODEL), F32),
        scratch_shapes=[pltpu.VMEM((2 * n_strided * (DIL_OUT // LANES), tm, LANES), F32)],
        compiler_params=_params(("parallel",)),
        name="final",
    )(x2d, pre, post, onsa, od[0], od[1], od[2], ld[0], ld[1], ld[2], wmerge, wnsa, wdil, wout)


def _overlap_t(n_ck, n_slc):
    c0 = np.arange(n_ck)[None, :] * CMP_STRIDE
    s0 = np.arange(n_slc)[:, None] * SLC_LEN
    ov = np.clip(np.minimum(c0 + CMP_LEN, s0 + SLC_LEN) - np.maximum(c0, s0), 0, None) / CMP_LEN
    ov[:, n_ck - 1] = 0.0
    return ov.astype(np.float32)


def _pos_columns(pos):
    cols = np.zeros((len(pos), LANES), np.float32)
    for t, vals in enumerate((pos // SLC_LEN, pos % SLC_LEN, np.ones(len(pos)))):
        for u in range(3):
            cols[:, HEAD_DIM + 3 * t + u] = vals
    return cols


def _ones_row_tail(n):
    tail = np.zeros((VT_ROWS, n), np.float32)
    tail[HEAD_DIM] = 1.0
    return tail


def _band_table(tq):
    u = np.arange(2 * NSA_WINDOW + tq)[:, None]
    r = (np.arange(NSA_REP * tq) % tq)[None, :]
    dist = NSA_WINDOW + r - u
    return np.where((dist >= 0) & (dist <= NSA_WINDOW - 1), 0.0, NEG).astype(np.float32)


def _dil_dist_table(tq, width, max_back, dil):
    u = np.arange(width + max_back)[:, None]
    r = np.arange(tq)[None, :]
    dist = max_back + r - u
    return np.where((dist >= 0) & (dist <= max_back), dist * dil, -NEG).astype(np.float32)


def _cmp_mask_table(n_ck, tq):
    u = np.arange(2 * n_ck - 8)[:, None]
    r = (np.arange(NSA_REP * tq) % tq)[None, :]
    end = (u - (n_ck - 8)) * CMP_STRIDE + CMP_LEN - 1
    return np.where(end <= r, 0.0, NEG).astype(np.float32)


def _mixer(x2d, B, S, mix_pre, mix_post, w_in, pe_k, w_ck1, w_ck2, pe_v, w_cv1, w_cv2,
           w_nsa_o, w_dil_o, w_mix_out):
    G = NSA_GROUPS
    BG = B * G
    assert S % (16 * 128) == 0 and NSA_TQ // CMP_STRIDE == 8
    o_q, o_kv, o_g, o_d, o_m = np.cumsum([0, Q_NSA, 6 * KV_NSA, GATE_NSA, 3 * QKV_DIL]).tolist()
    wq = (w_in[:, o_q:o_kv] * (SCALE * LOG2E)).reshape(D_MODEL, NSA_HEADS, HEAD_DIM)
    wq = jnp.pad(wq, ((0, 0), (0, 0), (0, LANES - HEAD_DIM))).reshape(D_MODEL, NSA_HEADS * LANES)
    w_gate = jnp.pad(w_in[:, o_g:o_d], ((0, 0), (0, LANES - GATE_NSA)))
    wd = w_in[:, o_d:o_m].reshape(D_MODEL, 3, len(DIL_CONFIGS), DIL_OUT)
    wd = (wd * jnp.asarray([SCALE * LOG2E, 1.0, 1.0], F32)[None, :, None, None]).transpose(0, 2, 1, 3)
    wd = wd.reshape(D_MODEL, 3 * QKV_DIL)
    wkv = w_in[:, o_kv:o_g].reshape(D_MODEL, 6 * G, HEAD_DIM)
    wkv = jnp.pad(wkv, ((0, 0), (0, 0), (0, LANES - HEAD_DIM))).reshape(D_MODEL, 6 * G * LANES)
    q_pad, kv_pad, gates, *qkv_d = _proj(x2d, mix_pre[None, :], wq.astype(BF16), wkv.astype(BF16),
                                         wd.astype(BF16), w_gate.astype(BF16), B=B, S=S)

    n_ck = S // CMP_STRIDE
    half = CMP_STRIDE * HEAD_DIM
    def chunks(kind):
        per_g = [kv_pad[:, (kind * G + g) * LANES:(kind * G + g) * LANES + HEAD_DIM].reshape(B, 1, n_ck, half)
                 for g in range(G)]
        return jnp.concatenate(per_g, axis=1).reshape(BG, n_ck, half)
    w1 = jnp.stack([w_ck1, w_cv1])
    w1cat = jnp.concatenate([w1[:, :half], w1[:, half:]], axis=2).astype(BF16)
    pe = jnp.stack([pe_k.reshape(1, -1), pe_v.reshape(1, -1)]).astype(BF16)
    w2 = jnp.pad(jnp.stack([w_ck2, w_cv2]), ((0, 0), (0, 0), (0, LANES - HEAD_DIM))).astype(BF16)
    cmp_pos = np.arange(n_ck) * CMP_STRIDE + CMP_LEN - 1
    kc, vct = _cmp_mlp(chunks(0), chunks(1), pe, w1cat, w1.astype(BF16), w2,
                       jnp.asarray(_pos_columns(cmp_pos), BF16), jnp.asarray(_ones_row_tail(n_ck), BF16))

    band = jnp.asarray(_band_table(NSA_TQ))
    cmpmask = jnp.asarray(_cmp_mask_table(n_ck, NSA_TQ))
    ovt = jnp.asarray(_overlap_t(n_ck, S // SLC_LEN), BF16)
    o_nsa = _nsa(q_pad, gates, kc, vct, kv_pad, jnp.asarray(_pos_columns(np.arange(S)), BF16),
                 jnp.asarray(_ones_row_tail(S), BF16), band, cmpmask, ovt, B=B, S=S)

    od, ld = [], []
    for gi, (window, dil) in enumerate(DIL_CONFIGS):
        Ls = S // dil
        max_back = window // dil
        tq = min(DIL_TQ, Ls)
        width = tq + max_back if Ls > tq else tq
        o, lse = _dil_attn(qkv_d[gi].reshape(B * dil, Ls, DIL_QKV),
                           jnp.asarray(_dil_dist_table(tq, width, max_back, dil)),
                           max_back=max_back, slopes=DIL_SLOPES[gi * DIL_HPG:(gi + 1) * DIL_HPG], tq=tq)
        od.append(o.reshape(B, dil, Ls, DIL_OUT))
        ld.append(lse.reshape(B, dil, Ls, DIL_OUT))

    return _final(x2d, mix_pre[None, :], mix_post[None, :], o_nsa, od, ld, w_in[:, o_m:].astype(BF16),
                  w_nsa_o.astype(BF16), w_dil_o.astype(BF16), w_mix_out.astype(BF16), S=S)


def kernel(x, ffn1_pre, ffn1_post, ffn1_w_gate, ffn1_w_up, ffn1_w_down, mix_pre, mix_post, w_in, nsa_pe_k, nsa_w_ck1, nsa_w_ck2, nsa_pe_v, nsa_w_cv1, nsa_w_cv2, w_nsa_o, w_dil_o, w_mix_out, ffn2_pre, ffn2_post, ffn2_w_gate, ffn2_w_up, ffn2_w_down):
    B, S, _ = x.shape
    x2d = x.reshape(B * S, D_MODEL)
    for l in range(ffn1_pre.shape[0]):
        x2d = _ffn(x2d, ffn1_pre[l][None, :], ffn1_post[l][None, :], ffn1_w_gate[l].astype(BF16),
                   ffn1_w_up[l].astype(BF16), ffn1_w_down[l].astype(BF16))
        x2d = _mixer(x2d, B, S, mix_pre[l], mix_post[l], w_in[l], nsa_pe_k[l], nsa_w_ck1[l], nsa_w_ck2[l],
                     nsa_pe_v[l], nsa_w_cv1[l], nsa_w_cv2[l], w_nsa_o[l], w_dil_o[l], w_mix_out[l])
        x2d = _ffn(x2d, ffn2_pre[l][None, :], ffn2_post[l][None, :], ffn2_w_gate[l].astype(BF16),
                   ffn2_w_up[l].astype(BF16), ffn2_w_down[l].astype(BF16))
    return x2d.reshape(B, S, D_MODEL)
```

```python
import functools
import math

import numpy as np
import jax
import jax.numpy as jnp
from jax import lax
from jax.experimental import pallas as pl
from jax.experimental.pallas import tpu as pltpu

F32 = jnp.float32
BF16 = jnp.bfloat16

D_MODEL = 1024
HEAD_DIM = 64
RMS_EPS = 1e-6
NSA_HEADS = 8
NSA_GROUPS = 2
NSA_REP = NSA_HEADS // NSA_GROUPS
CMP_LEN = 32
CMP_STRIDE = 16
CMP_HIDDEN = 256
SLC_LEN = 64
N_SEL = 16
NSA_WINDOW = 512
DIL_CONFIGS = ((128, 1), (512, 4), (2048, 16))
DIL_HPG = 4
DIL_HEADS = DIL_HPG * len(DIL_CONFIGS)
D_FF = 2816
Q_NSA = NSA_HEADS * HEAD_DIM
KV_NSA = NSA_GROUPS * HEAD_DIM
GATE_NSA = NSA_HEADS * 3
QKV_DIL = DIL_HEADS * HEAD_DIM
DIL_OUT = DIL_HPG * HEAD_DIM
DIL_QKV = 3 * DIL_OUT
DIL_TQ = 512
LANES = 128
NEG = -1e30
SCALE = HEAD_DIM ** -0.5
LOG2E = math.log2(math.e)
VMEM_LIMIT = 56 * 1024 * 1024
NSA_TQ = 128
VT_ROWS = 80


def _slopes(n):
    return [float(np.float32(2.0 ** (-8.0 * (i + 1) / n))) for i in range(n)]


NSA_SLOPES = _slopes(NSA_HEADS)
DIL_SLOPES = _slopes(DIL_HEADS)


def _params(sem):
    return pltpu.CompilerParams(dimension_semantics=sem, vmem_limit_bytes=VMEM_LIMIT)


def _rms(x, g):
    ms = jnp.mean(x * x, axis=-1, keepdims=True)
    return x * lax.rsqrt(ms + RMS_EPS) * g


def _sigmoid(x):
    return 1.0 / (1.0 + jnp.exp(-x))


def _dot(a, b):
    return jnp.dot(a, b, preferred_element_type=F32)


def _dot_nt(a, b):
    return lax.dot_general(a, b, (((1,), (1,)), ((), ())), preferred_element_type=F32)


def _split3(a):
    hi = a.astype(BF16)
    r1 = a - hi.astype(F32)
    mid = r1.astype(BF16)
    lo = (r1 - mid.astype(F32)).astype(BF16)
    return hi, mid, lo


def _ffn_kernel(x_ref, pre_ref, post_ref, wg_ref, wu_ref, wd_ref, o_ref):
    x = x_ref[...]
    h = _rms(x, pre_ref[...]).astype(BF16)
    g = _dot(h, wg_ref[...])
    u = _dot(h, wu_ref[...])
    a = (g * _sigmoid(g) * u).astype(BF16)
    o_ref[...] = x + 0.5 * _rms(_dot(a, wd_ref[...]), post_ref[...])


def _resident(a):
    return pl.BlockSpec(a.shape, lambda *_: (0,) * a.ndim, pipeline_mode=pl.Buffered(1))


def _ffn(x2d, pre, post, wg, wu, wd, *, tm=512):
    T = x2d.shape[0]
    return pl.pallas_call(
        _ffn_kernel,
        grid=(T // tm,),
        in_specs=[pl.BlockSpec((tm, D_MODEL), lambda i: (i, 0)), _resident(pre), _resident(post),
                  _resident(wg), _resident(wu), _resident(wd)],
        out_specs=pl.BlockSpec((tm, D_MODEL), lambda i: (i, 0)),
        out_shape=jax.ShapeDtypeStruct((T, D_MODEL), F32),
        compiler_params=_params(("parallel",)),
        name="ffn",
    )(x2d, pre, post, wg, wu, wd)


def _proj_kernel(x_ref, pre_ref, wq_ref, wkv_ref, wd_ref, wgate_ref, q_ref, kv_ref, g_ref, *rest):
    d_refs, scr = rest[:-1], rest[-1]
    tm = x_ref.shape[0]
    h = _rms(x_ref[...], pre_ref[...]).astype(BF16)
    q_ref[...] = _dot(h, wq_ref[...]).astype(BF16)
    kv_ref[...] = _dot(h, wkv_ref[...]).astype(BF16)
    g_ref[...] = _dot(h, wgate_ref[...])
    n_blk = DIL_QKV // LANES
    for gi, (_, dil) in enumerate(DIL_CONFIGS):
        res = _dot(h, wd_ref[:, gi * DIL_QKV:(gi + 1) * DIL_QKV])
        if dil == 1:
            d_refs[gi][0] = res.astype(BF16)
            continue
        for c in range(n_blk):
            scr[c] = res[:, c * LANES:(c + 1) * LANES]
        for r in range(dil):
            for c in range(n_blk):
                rows = scr[c, pl.ds(r, tm // dil, stride=dil), :]
                d_refs[gi][r, :, c * LANES:(c + 1) * LANES] = rows.astype(BF16)


def _proj(x2d, pre, wq, wkv, wd, wgate, *, B, S, tm=512):
    T = x2d.shape[0]
    nt = S // tm
    tok = lambda w: pl.BlockSpec((tm, w), lambda i: (i, 0))
    full = lambda a: pl.BlockSpec(a.shape, lambda i: (0,) * a.ndim)
    d_specs = [pl.BlockSpec((None, dil, tm // dil, DIL_QKV), lambda i: (i // nt, 0, i % nt, 0))
               for _, dil in DIL_CONFIGS]
    d_shapes = [jax.ShapeDtypeStruct((B, dil, S // dil, DIL_QKV), BF16) for _, dil in DIL_CONFIGS]
    return pl.pallas_call(
        _proj_kernel,
        grid=(T // tm,),
        in_specs=[tok(D_MODEL), _resident(pre), _resident(wq), _resident(wkv), _resident(wd), _resident(wgate)],
        out_specs=[tok(wq.shape[1]), tok(wkv.shape[1]), tok(LANES)] + d_specs,
        out_shape=[jax.ShapeDtypeStruct((T, wq.shape[1]), BF16), jax.ShapeDtypeStruct((T, wkv.shape[1]), BF16),
                   jax.ShapeDtypeStruct((T, LANES), F32)] + d_shapes,
        scratch_shapes=[pltpu.VMEM((DIL_QKV // LANES, tm, LANES), F32)],
        compiler_params=_params(("parallel",)),
        name="proj",
    )(x2d, pre, wq, wkv, wd, wgate)


def _gelu_tanh(x):
    c = math.sqrt(2.0 / math.pi)
    return x * (0.5 * (1.0 + jnp.tanh(c * (x + 0.044715 * (x * x * x)))))


def _cmp_mlp_kernel(xk_ref, xv_ref, pe_ref, w1cat_ref, w1_ref, w2_ref, kcols_ref, vtail_ref, kc_ref, vct_ref):
    n_chunks = xk_ref.shape[0]

    def mlp(x_ref, t):
        ab = _dot(x_ref[...], w1cat_ref[t])
        first = ab[:, :CMP_HIDDEN]
        second = pltpu.roll(ab[:, CMP_HIDDEN:], n_chunks - 1, 0)
        pe8 = jnp.broadcast_to(pe_ref[t], (8, pe_ref.shape[2]))
        bias = _dot(pe8, w1_ref[t])[0:1]
        hid = _gelu_tanh(first + second + bias).astype(BF16)
        out = _dot(hid, w2_ref[t])
        row = lax.broadcasted_iota(jnp.int32, out.shape, 0)
        return jnp.where(row < n_chunks - 1, out, 0.0)

    kc_ref[...] = mlp(xk_ref, 0).astype(BF16) + kcols_ref[...]
    vct_ref[...] = mlp(xv_ref, 1).T[0:VT_ROWS].astype(BF16) + vtail_ref[...]


def _cmp_mlp(xk, xv, pe, w1cat, w1, w2, kcols, vtail):
    BG, n_chunks, width = xk.shape
    per_n = lambda shape: pl.BlockSpec((None,) + shape, lambda n: (n, 0, 0))
    full = lambda a: pl.BlockSpec(a.shape, lambda n: (0,) * a.ndim)
    return pl.pallas_call(
        _cmp_mlp_kernel,
        grid=(BG,),
        in_specs=[per_n((n_chunks, width)), per_n((n_chunks, width)), full(pe), full(w1cat), full(w1),
                  full(w2), full(kcols), full(vtail)],
        out_specs=[per_n((n_chunks, LANES)), per_n((VT_ROWS, n_chunks))],
        out_shape=[jax.ShapeDtypeStruct((BG, n_chunks, LANES), BF16),
                   jax.ShapeDtypeStruct((BG, VT_ROWS, n_chunks), BF16)],
        compiler_params=_params(("parallel",)),
        name="cmp_mlp",
    )(xk, xv, pe, w1cat, w1, w2, kcols, vtail)


def _exp2_bf16(x):
    return jnp.exp2(x.astype(BF16))


def _online_update(s_t, v_t, m, acc):
    m_new = jnp.maximum(m, jnp.max(s_t, axis=0, keepdims=True))
    alpha = jnp.exp2(m - m_new)
    return m_new, alpha * acc + _dot(v_t, _exp2_bf16(s_t - m_new))


def _nsa_kernel(q_ref, gate_ref, kc_ref, vct_ref, kslc_ref, vslc_ref, kwin_ref, vwin_ref, kcols_ref,
                vtail_ref, band_ref, cmpmask_ref, ovt_ref, o_ref,
                selbias_scr, m_scr, acc_scr, ks_ref, vst_ref, kw_ref, vwt_ref, need_smem, list_smem,
                *, tq, tk):
    g = pl.program_id(1)
    i = pl.program_id(2)
    rows = NSA_REP * tq
    n_ck = kc_ref.shape[0]
    n_slc = ovt_ref.shape[0]
    S = kslc_ref.shape[0]
    q0 = i * tq

    @pl.when(i == 0)
    def _():
        ks_ref[...] = kslc_ref[...] + kcols_ref[...]
        kw_ref[...] = kwin_ref[...] + kcols_ref[...]
        t_chunk = min(S, 512)
        for src, dst in ((vslc_ref, vst_ref), (vwin_ref, vwt_ref)):
            dst[HEAD_DIM:VT_ROWS, :] = vtail_ref[HEAD_DIM:VT_ROWS, :]
            for c in range(S // t_chunk):
                v = src[c * t_chunk:(c + 1) * t_chunk, :].astype(F32)
                dst[0:HEAD_DIM, c * t_chunk:(c + 1) * t_chunk] = v.T[0:HEAD_DIM].astype(BF16)

    lane = lax.broadcasted_iota(jnp.int32, (1, LANES), 1)
    slabs = []
    for r in range(NSA_REP):
        sl = jnp.where(g == 0, NSA_SLOPES[r], NSA_SLOPES[NSA_REP + r]) * LOG2E
        meta = jnp.zeros((1, LANES), F32)
        for t, coef in enumerate((SLC_LEN * sl, sl, -sl * q0.astype(F32))):
            terms = _split3(jnp.full((1, LANES), coef, F32))
            for u in range(3):
                meta = jnp.where(lane == HEAD_DIM + 3 * t + u, terms[u].astype(F32), meta)
        slabs.append(q_ref[:, r * LANES:(r + 1) * LANES] + meta.astype(BF16))
    q_aug = jnp.concatenate(slabs, axis=0)

    width = NSA_WINDOW + tq
    back = jnp.minimum(q0, NSA_WINDOW)
    ws = pl.multiple_of(q0 - back, tq)
    qs = pl.multiple_of(q0, tq)
    s_all = _dot_nt(jnp.concatenate([kc_ref[...], kw_ref[pl.ds(ws, width), :], ks_ref[pl.ds(qs, tq), :]],
                                    axis=0), q_aug)

    s_t = s_all[0:n_ck]
    cstart = pl.multiple_of((n_ck - 8) - (tq // CMP_STRIDE) * i, 8)
    s_t = s_t + cmpmask_ref[pl.ds(cstart, n_ck), :]
    m = jnp.max(s_t, axis=0, keepdims=True)
    any_key = m > 0.5 * NEG
    p = jnp.exp2(s_t - m)
    den = jnp.sum(p, axis=0, keepdims=True)
    p = p * jnp.where(any_key, 1.0 / den, 0.0)
    o_cmp = _dot(vct_ref[0:HEAD_DIM, :], p.astype(BF16))

    psum = p[:, 0:tq]
    for r in range(1, NSA_REP):
        psum = psum + p[:, r * tq:(r + 1) * tq]
    hi, mid, _ = _split3(psum)
    ovt = ovt_ref[...]
    imp = _dot(ovt, hi) + _dot(ovt, mid)
    pos = q0 + lax.broadcasted_iota(jnp.int32, (1, tq), 1)
    own = pos // SLC_LEN
    jb = lax.broadcasted_iota(jnp.int32, (n_slc, 1), 0)
    jbf = jb.astype(F32)
    forced = (jb == 0) | (jb == own) | (jb == own - 1)
    valid = jb * SLC_LEN <= pos
    vals = jnp.where(valid & jnp.logical_not(forced), imp, -jnp.inf)
    sel = jnp.where(forced, 1.0, 0.0)

    s_w = s_all[n_ck:n_ck + width] + band_ref[pl.ds(pl.multiple_of(NSA_WINDOW - back, tq), width), :]
    m_w = jnp.max(s_w, axis=0, keepdims=True)
    s_d = s_all[n_ck + width:n_ck + width + tq] + band_ref[NSA_WINDOW:NSA_WINDOW + tq, :]
    m_d = jnp.max(s_d, axis=0, keepdims=True)
    slices = ([(s_w, m_w, r0) for r0 in range(0, width, SLC_LEN)]
              + [(s_d, m_d, r0) for r0 in range(0, tq, SLC_LEN)])
    probs = []

    for it in range(max(N_SEL - 3, len(slices))):
        if it < N_SEL - 3:
            mx = jnp.max(vals, axis=0, keepdims=True)
            cand = jnp.where(vals == mx, jbf, 1e9)
            idx = jnp.min(cand, axis=0, keepdims=True)
            pick = (jbf == idx) & (mx > -jnp.inf)
            sel = jnp.where(pick, 1.0, sel)
            vals = jnp.where(pick, -jnp.inf, vals)
        if it < len(slices):
            src, m_src, r0 = slices[it]
            probs.append(_exp2_bf16(src[r0:r0 + SLC_LEN] - m_src))
    n_w = width // SLC_LEN
    acc_w = _dot(vwt_ref[:, pl.ds(ws, width)], jnp.concatenate(probs[:n_w], axis=0))
    o_win = acc_w[0:HEAD_DIM] * (1.0 / acc_w[HEAD_DIM:HEAD_DIM + 1])
    m_scr[...] = m_d
    acc_scr[...] = _dot(vst_ref[:, pl.ds(qs, tq)], jnp.concatenate(probs[n_w:], axis=0))
    sbias = jnp.where((sel > 0.5) & (jb < q0 // SLC_LEN), 0.0, NEG)
    selbias_scr[...] = jnp.concatenate([sbias] * NSA_REP, axis=1)
    blocks_per_chunk = tk // SLC_LEN
    for c in range(n_slc // blocks_per_chunk):
        chunk_max = jnp.max(sbias[c * blocks_per_chunk:(c + 1) * blocks_per_chunk])
        need_smem[c] = (chunk_max > 0.5 * NEG).astype(jnp.int32)
    n_need = jnp.int32(0)
    for c in range(n_slc // blocks_per_chunk):
        list_smem[n_need] = jnp.int32(c)
        n_need = n_need + need_smem[c]

    def masked(s_c, c):
        parts = []
        for jj in range(blocks_per_chunk):
            brow = selbias_scr[pl.ds(c * blocks_per_chunk + jj, 1), :]
            parts.append(s_c[jj * SLC_LEN:(jj + 1) * SLC_LEN] + brow)
        return jnp.concatenate(parts, axis=0)

    def chunk_update(c, m_run, acc):
        ks = pl.multiple_of(c * tk, tk)
        s_c = _dot_nt(ks_ref[pl.ds(ks, tk), :], q_aug)
        return _online_update(masked(s_c, c), vst_ref[:, pl.ds(ks, tk)], m_run, acc)

    def pair_body(t, carry):
        ca, cb = list_smem[2 * t], list_smem[2 * t + 1]
        ka = pl.multiple_of(ca * tk, tk)
        kb = pl.multiple_of(cb * tk, tk)
        keys = jnp.concatenate([ks_ref[pl.ds(ka, tk), :], ks_ref[pl.ds(kb, tk), :]], axis=0)
        s_ab = _dot_nt(keys, q_aug)
        s_ab = jnp.concatenate([masked(s_ab[0:tk], ca), masked(s_ab[tk:2 * tk], cb)], axis=0)
        v_ab = jnp.concatenate([vst_ref[:, pl.ds(ka, tk)], vst_ref[:, pl.ds(kb, tk)]], axis=1)
        m_run, acc = _online_update(s_ab, v_ab, m_scr[...], acc_scr[...])
        m_scr[...] = m_run
        acc_scr[...] = acc
        return carry

    lax.fori_loop(0, n_need // 2, pair_body, 0)

    @pl.when(n_need % 2 == 1)
    def _():
        m_run, acc = chunk_update(list_smem[n_need - 1], m_scr[...], acc_scr[...])
        m_scr[...] = m_run
        acc_scr[...] = acc

    acc = acc_scr[...]
    o_slc = acc[0:HEAD_DIM] * (1.0 / acc[HEAD_DIM:HEAD_DIM + 1])

    gate_t = _sigmoid(gate_ref[...].T)
    outs = []
    for r in range(NSA_REP):
        cols = slice(r * tq, (r + 1) * tq)

        def gate_row(branch):
            lo_row = 3 * r + branch
            hi_row = 3 * (NSA_REP + r) + branch
            return jnp.where(g == 0, gate_t[lo_row:lo_row + 1], gate_t[hi_row:hi_row + 1])

        outs.append(gate_row(0) * o_cmp[:, cols] + gate_row(1) * o_slc[:, cols]
                    + gate_row(2) * o_win[:, cols])
    o_ref[...] = jnp.concatenate(outs, axis=0).T.astype(BF16)


def _nsa(q_pad, gates, kc, vct, kv_pad, kcols, vtail, band, cmpmask, ovt, *, B, S, tq=NSA_TQ, tk=512):
    G = NSA_GROUPS
    n_ck = kc.shape[1]
    n_slc = ovt.shape[0]
    rows = NSA_REP * tq
    nq = S // tq
    n_chunks = n_slc * SLC_LEN // tk
    per_bg = lambda shape: pl.BlockSpec((None,) + shape, lambda b, g, i: (b * G + g, 0, 0))
    full = lambda a: pl.BlockSpec(a.shape, lambda b, g, i: (0,) * a.ndim)
    kv_kind = lambda kind: pl.BlockSpec((S, LANES), lambda b, g, i: (b, kind * G + g))
    return pl.pallas_call(
        functools.partial(_nsa_kernel, tq=tq, tk=tk),
        grid=(B, G, nq),
        in_specs=[
            pl.BlockSpec((tq, NSA_REP * LANES), lambda b, g, i: (b * nq + i, g)),
            pl.BlockSpec((tq, LANES), lambda b, g, i: (b * nq + i, 0)),
            per_bg((n_ck, LANES)), per_bg((VT_ROWS, n_ck)),
            kv_kind(2), kv_kind(3), kv_kind(4), kv_kind(5),
            full(kcols), full(vtail), full(band), full(cmpmask), full(ovt),
        ],
        out_specs=pl.BlockSpec((tq, NSA_REP * HEAD_DIM), lambda b, g, i: (b * nq + i, g)),
        out_shape=jax.ShapeDtypeStruct((B * S, Q_NSA), BF16),
        scratch_shapes=[pltpu.VMEM((n_slc, rows), F32), pltpu.VMEM((1, rows), F32),
                        pltpu.VMEM((VT_ROWS, rows), F32),
                        pltpu.VMEM((S, LANES), BF16), pltpu.VMEM((VT_ROWS, S), BF16),
                        pltpu.VMEM((S, LANES), BF16), pltpu.VMEM((VT_ROWS, S), BF16),
                        pltpu.SMEM((n_chunks,), jnp.int32), pltpu.SMEM((n_chunks + 1,), jnp.int32)],
        compiler_params=_params(("arbitrary", "arbitrary", "arbitrary")),
        name="nsa",
    )(q_pad, gates, kc, vct, kv_pad, kv_pad, kv_pad, kv_pad, kcols, vtail, band, cmpmask, ovt)


def _dil_attn_kernel(q_ref, k_ref, v_ref, dist_ref, o_ref, lse_ref, vt_scr, *, tq, width, max_back, slopes):
    i = pl.program_id(1)
    nb, L, _ = k_ref.shape
    t_chunk = min(L, 512)

    @pl.when(i == 0)
    def _():
        for b in range(nb):
            for pair in range(DIL_OUT // LANES):
                for c in range(L // t_chunk):
                    v = v_ref[b, c * t_chunk:(c + 1) * t_chunk, pair * LANES:(pair + 1) * LANES].astype(F32)
                    vt_scr[b, pair * LANES:(pair + 1) * LANES, c * t_chunk:(c + 1) * t_chunk] = v.T.astype(BF16)

    back = jnp.minimum(i * tq, max_back)
    ks = pl.multiple_of(i * tq - back, max_back)
    dist = dist_ref[pl.ds(pl.multiple_of(max_back - back, max_back), width), :]
    lane = lax.broadcasted_iota(jnp.int32, (1, LANES), 1)
    chains = [(b, h) for b in range(nb) for h in range(DIL_HPG)]
    scores = []
    for b, h in chains:
        pair, half = divmod(h, 2)
        in_head = (lane >= half * HEAD_DIM) & (lane < (half + 1) * HEAD_DIM)
        q = q_ref[b, :, pair * LANES:(pair + 1) * LANES]
        q = jnp.where(in_head, q, jnp.zeros_like(q))
        k = k_ref[b, pl.ds(ks, width), pair * LANES:(pair + 1) * LANES]
        scores.append(_dot_nt(k, q) - (slopes[h] * LOG2E) * dist)
    maxes = [jnp.max(s_t, axis=0, keepdims=True) for s_t in scores]
    probs = [jnp.exp2(s_t - m) for s_t, m in zip(scores, maxes)]
    dens = [jnp.sum(p, axis=0, keepdims=True) for p in probs]
    outs = [_dot(vt_scr[b, h * HEAD_DIM:(h + 1) * HEAD_DIM, pl.ds(ks, width)], p.astype(BF16)) * (1.0 / den)
            for (b, h), p, den in zip(chains, probs, dens)]
    lses = [jnp.broadcast_to((m + jnp.log2(den)) * (1.0 / LOG2E), (HEAD_DIM, tq)) for m, den in zip(maxes, dens)]
    for b in range(nb):
        o_ref[b] = jnp.concatenate(outs[b * DIL_HPG:(b + 1) * DIL_HPG], axis=0).T
        lse_ref[b] = jnp.concatenate(lses[b * DIL_HPG:(b + 1) * DIL_HPG], axis=0).T


def _dil_attn(qkv, dist_tbl, *, max_back, slopes, tq, nb):
    N, L, _ = qkv.shape
    width = dist_tbl.shape[0] - max_back
    kern = functools.partial(_dil_attn_kernel, tq=tq, width=width, max_back=max_back, slopes=slopes)
    return pl.pallas_call(
        kern,
        grid=(N // nb, L // tq),
        in_specs=[
            pl.BlockSpec((nb, tq, DIL_OUT), lambda n, i: (n, i, 0)),
            pl.BlockSpec((nb, L, DIL_OUT), lambda n, i: (n, 0, 1)),
            pl.BlockSpec((nb, L, DIL_OUT), lambda n, i: (n, 0, 2)),
            pl.BlockSpec(dist_tbl.shape, lambda n, i: (0, 0)),
        ],
        out_specs=[
            pl.BlockSpec((nb, tq, DIL_OUT), lambda n, i: (n, i, 0)),
            pl.BlockSpec((nb, tq, DIL_OUT), lambda n, i: (n, i, 0)),
        ],
        out_shape=[
            jax.ShapeDtypeStruct((N, L, DIL_OUT), F32),
            jax.ShapeDtypeStruct((N, L, DIL_OUT), F32),
        ],
        scratch_shapes=[pltpu.VMEM((nb, DIL_OUT, L), BF16)],
        compiler_params=_params(("parallel", "arbitrary")),
        name="dil_attn",
    )(qkv, qkv, qkv, dist_tbl)


def _final_kernel(x_ref, pre_ref, post_ref, onsa_ref, od0_ref, od1_ref, od2_ref, l0_ref, l1_ref, l2_ref,
                  wmerge_ref, wnsa_ref, wdil_ref, wout_ref, o_ref, scr):
    tm = x_ref.shape[0]
    x = x_ref[...]
    h = _rms(x, pre_ref[...]).astype(BF16)
    merge = _dot(h, wmerge_ref[...])
    g_a = _sigmoid(merge[:, :D_MODEL])
    g_b = _sigmoid(merge[:, D_MODEL:])
    y_nsa = _dot(onsa_ref[...], wnsa_ref[...])

    n_blk = DIL_OUT // LANES
    slot = 0
    token_major = []
    for (_, dil), o_in, l_in in zip(DIL_CONFIGS, (od0_ref, od1_ref, od2_ref), (l0_ref, l1_ref, l2_ref)):
        pair = []
        for src in (o_in, l_in):
            if dil == 1:
                pair.append(src[0])
                continue
            for r in range(dil):
                for c in range(n_blk):
                    scr[slot + c, pl.ds(r, tm // dil, stride=dil), :] = src[r, :, c * LANES:(c + 1) * LANES]
            pair.append(jnp.concatenate([scr[slot + c] for c in range(n_blk)], axis=1))
            slot += n_blk
        token_major.append(pair)
    (o0, l0), (o1, l1), (o2, l2) = token_major
    mx = jnp.maximum(jnp.maximum(l0, l1), l2)
    e0, e1, e2 = jnp.exp(l0 - mx), jnp.exp(l1 - mx), jnp.exp(l2 - mx)
    den = e0 + e1 + e2
    o_dil = (e0 / den) * o0 + (e1 / den) * o1 + (e2 / den) * o2
    y_dil = _dot(o_dil.astype(BF16), wdil_ref[...])

    y = g_a * y_nsa + g_b * y_dil
    z = _dot(y.astype(BF16), wout_ref[...])
    o_ref[...] = x + _rms(z, post_ref[...])


def _final(x2d, pre, post, onsa, od, ld, wmerge, wnsa, wdil, wout, *, S, tm=512):
    T = x2d.shape[0]
    nt = S // tm
    tok = lambda w: pl.BlockSpec((tm, w), lambda i: (i, 0))
    full = lambda a: pl.BlockSpec(a.shape, lambda i: (0,) * a.ndim)
    strided = [pl.BlockSpec((None, dil, tm // dil, DIL_OUT), lambda i: (i // nt, 0, i % nt, 0))
               for _, dil in DIL_CONFIGS]
    n_strided = sum(1 for _, dil in DIL_CONFIGS if dil > 1)
    return pl.pallas_call(
        _final_kernel,
        grid=(T // tm,),
        in_specs=[tok(D_MODEL), _resident(pre), _resident(post), tok(Q_NSA)] + strided + strided
        + [_resident(wmerge), _resident(wnsa), _resident(wdil), _resident(wout)],
        out_specs=tok(D_MODEL),
        out_shape=jax.ShapeDtypeStruct((T, D_MODEL), F32),
        scratch_shapes=[pltpu.VMEM((2 * n_strided * (DIL_OUT // LANES), tm, LANES), F32)],
        compiler_params=_params(("parallel",)),
        name="final",
    )(x2d, pre, post, onsa, od[0], od[1], od[2], ld[0], ld[1], ld[2], wmerge, wnsa, wdil, wout)


def _overlap_t(n_ck, n_slc):
    c0 = np.arange(n_ck)[None, :] * CMP_STRIDE
    s0 = np.arange(n_slc)[:, None] * SLC_LEN
    ov = np.clip(np.minimum(c0 + CMP_LEN, s0 + SLC_LEN) - np.maximum(c0, s0), 0, None) / CMP_LEN
    ov[:, n_ck - 1] = 0.0
    return ov.astype(np.float32)


def _pos_columns(pos):
    cols = np.zeros((len(pos), LANES), np.float32)
    for t, vals in enumerate((pos // SLC_LEN, pos % SLC_LEN, np.ones(len(pos)))):
        for u in range(3):
            cols[:, HEAD_DIM + 3 * t + u] = vals
    return cols


def _ones_row_tail(n):
    tail = np.zeros((VT_ROWS, n), np.float32)
    tail[HEAD_DIM] = 1.0
    return tail


def _band_table(tq):
    u = np.arange(2 * NSA_WINDOW + tq)[:, None]
    r = (np.arange(NSA_REP * tq) % tq)[None, :]
    dist = NSA_WINDOW + r - u
    return np.where((dist >= 0) & (dist <= NSA_WINDOW - 1), 0.0, NEG).astype(np.float32)


def _dil_dist_table(tq, width, max_back, dil):
    u = np.arange(width + max_back)[:, None]
    r = np.arange(tq)[None, :]
    dist = max_back + r - u
    return np.where((dist >= 0) & (dist <= max_back), dist * dil, -NEG).astype(np.float32)


def _cmp_mask_table(n_ck, tq):
    u = np.arange(2 * n_ck - 8)[:, None]
    r = (np.arange(NSA_REP * tq) % tq)[None, :]
    end = (u - (n_ck - 8)) * CMP_STRIDE + CMP_LEN - 1
    return np.where(end <= r, 0.0, NEG).astype(np.float32)


def _mixer(x2d, B, S, mix_pre, mix_post, w_in, pe_k, w_ck1, w_ck2, pe_v, w_cv1, w_cv2,
           w_nsa_o, w_dil_o, w_mix_out):
    G = NSA_GROUPS
    BG = B * G
    assert S % (16 * 128) == 0 and NSA_TQ // CMP_STRIDE == 8
    o_q, o_kv, o_g, o_d, o_m = np.cumsum([0, Q_NSA, 6 * KV_NSA, GATE_NSA, 3 * QKV_DIL]).tolist()
    wq = (w_in[:, o_q:o_kv] * (SCALE * LOG2E)).reshape(D_MODEL, NSA_HEADS, HEAD_DIM)
    wq = jnp.pad(wq, ((0, 0), (0, 0), (0, LANES - HEAD_DIM))).reshape(D_MODEL, NSA_HEADS * LANES)
    w_gate = jnp.pad(w_in[:, o_g:o_d], ((0, 0), (0, LANES - GATE_NSA)))
    wd = w_in[:, o_d:o_m].reshape(D_MODEL, 3, len(DIL_CONFIGS), DIL_OUT)
    wd = (wd * jnp.asarray([SCALE * LOG2E, 1.0, 1.0], F32)[None, :, None, None]).transpose(0, 2, 1, 3)
    wd = wd.reshape(D_MODEL, 3 * QKV_DIL)
    wkv = w_in[:, o_kv:o_g].reshape(D_MODEL, 6 * G, HEAD_DIM)
    wkv = jnp.pad(wkv, ((0, 0), (0, 0), (0, LANES - HEAD_DIM))).reshape(D_MODEL, 6 * G * LANES)
    q_pad, kv_pad, gates, *qkv_d = _proj(x2d, mix_pre[None, :], wq.astype(BF16), wkv.astype(BF16),
                                         wd.astype(BF16), w_gate.astype(BF16), B=B, S=S)

    n_ck = S // CMP_STRIDE
    half = CMP_STRIDE * HEAD_DIM
    def chunks(kind):
        per_g = [kv_pad[:, (kind * G + g) * LANES:(kind * G + g) * LANES + HEAD_DIM].reshape(B, 1, n_ck, half)
                 for g in range(G)]
        return jnp.concatenate(per_g, axis=1).reshape(BG, n_ck, half)
    w1 = jnp.stack([w_ck1, w_cv1])
    w1cat = jnp.concatenate([w1[:, :half], w1[:, half:]], axis=2).astype(BF16)
    pe = jnp.stack([pe_k.reshape(1, -1), pe_v.reshape(1, -1)]).astype(BF16)
    w2 = jnp.pad(jnp.stack([w_ck2, w_cv2]), ((0, 0), (0, 0), (0, LANES - HEAD_DIM))).astype(BF16)
    cmp_pos = np.arange(n_ck) * CMP_STRIDE + CMP_LEN - 1
    kc, vct = _cmp_mlp(chunks(0), chunks(1), pe, w1cat, w1.astype(BF16), w2,
                       jnp.asarray(_pos_columns(cmp_pos), BF16), jnp.asarray(_ones_row_tail(n_ck), BF16))

    band = jnp.asarray(_band_table(NSA_TQ))
    cmpmask = jnp.asarray(_cmp_mask_table(n_ck, NSA_TQ))
    ovt = jnp.asarray(_overlap_t(n_ck, S // SLC_LEN), BF16)
    o_nsa = _nsa(q_pad, gates, kc, vct, kv_pad, jnp.asarray(_pos_columns(np.arange(S)), BF16),
                 jnp.asarray(_ones_row_tail(S), BF16), band, cmpmask, ovt, B=B, S=S)

    od, ld = [], []
    for gi, (window, dil) in enumerate(DIL_CONFIGS):
        Ls = S // dil
        max_back = window // dil
        tq = min(DIL_TQ, Ls)
        width = tq + max_back if Ls > tq else tq
        o, lse = _dil_attn(qkv_d[gi].reshape(B * dil, Ls, DIL_QKV),
                           jnp.asarray(_dil_dist_table(tq, width, max_back, dil)),
                           max_back=max_back, slopes=DIL_SLOPES[gi * DIL_HPG:(gi + 1) * DIL_HPG], tq=tq,
                           nb=DIL_TQ // tq)
        od.append(o.reshape(B, dil, Ls, DIL_OUT))
        ld.append(lse.reshape(B, dil, Ls, DIL_OUT))

    return _final(x2d, mix_pre[None, :], mix_post[None, :], o_nsa, od, ld, w_in[:, o_m:].astype(BF16),
                  w_nsa_o.astype(BF16), w_dil_o.astype(BF16), w_mix_out.astype(BF16), S=S)


def kernel(x, ffn1_pre, ffn1_post, ffn1_w_gate, ffn1_w_up, ffn1_w_down, mix_pre, mix_post, w_in, nsa_pe_k, nsa_w_ck1, nsa_w_ck2, nsa_pe_v, nsa_w_cv1, nsa_w_cv2, w_nsa_o, w_dil_o, w_mix_out, ffn2_pre, ffn2_post, ffn2_w_gate, ffn2_w_up, ffn2_w_down):
    B, S, _ = x.shape
    x2d = x.reshape(B * S, D_MODEL)
    for l in range(ffn1_pre.shape[0]):
        x2d = _ffn(x2d, ffn1_pre[l][None, :], ffn1_post[l][None, :], ffn1_w_gate[l].astype(BF16),
                   ffn1_w_up[l].astype(BF16), ffn1_w_down[l].astype(BF16))
        x2d = _mixer(x2d, B, S, mix_pre[l], mix_post[l], w_in[l], nsa_pe_k[l], nsa_w_ck1[l], nsa_w_ck2[l],
                     nsa_pe_v[l], nsa_w_cv1[l], nsa_w_cv2[l], w_nsa_o[l], w_dil_o[l], w_mix_out[l])
        x2d = _ffn(x2d, ffn2_pre[l][None, :], ffn2_post[l][None, :], ffn2_w_gate[l].astype(BF16),
                   ffn2_w_up[l].astype(BF16), ffn2_w_down[l].astype(BF16))
    return x2d.reshape(B, S, D_MODEL)
```

```python
import functools
import math

import numpy as np
import jax
import jax.numpy as jnp
from jax import lax
from jax.experimental import pallas as pl
from jax.experimental.pallas import tpu as pltpu

F32 = jnp.float32
BF16 = jnp.bfloat16

D_MODEL = 1024
HEAD_DIM = 64
RMS_EPS = 1e-6
NSA_HEADS = 8
NSA_GROUPS = 2
NSA_REP = NSA_HEADS // NSA_GROUPS
CMP_LEN = 32
CMP_STRIDE = 16
CMP_HIDDEN = 256
SLC_LEN = 64
N_SEL = 16
NSA_WINDOW = 512
DIL_CONFIGS = ((128, 1), (512, 4), (2048, 16))
DIL_HPG = 4
DIL_HEADS = DIL_HPG * len(DIL_CONFIGS)
D_FF = 2816
Q_NSA = NSA_HEADS * HEAD_DIM
KV_NSA = NSA_GROUPS * HEAD_DIM
GATE_NSA = NSA_HEADS * 3
QKV_DIL = DIL_HEADS * HEAD_DIM
DIL_OUT = DIL_HPG * HEAD_DIM
DIL_QKV = 3 * DIL_OUT
DIL_TQ = 512
LANES = 128
NEG = -1e30
SCALE = HEAD_DIM ** -0.5
LOG2E = math.log2(math.e)
VMEM_LIMIT = 56 * 1024 * 1024
NSA_TQ = 128
NSA_TILES_PER_STEP = 4
VT_ROWS = 80


def _slopes(n):
    return [float(np.float32(2.0 ** (-8.0 * (i + 1) / n))) for i in range(n)]


NSA_SLOPES = _slopes(NSA_HEADS)
DIL_SLOPES = _slopes(DIL_HEADS)


def _params(sem):
    return pltpu.CompilerParams(dimension_semantics=sem, vmem_limit_bytes=VMEM_LIMIT)


def _rms(x, g):
    ms = jnp.mean(x * x, axis=-1, keepdims=True)
    return x * lax.rsqrt(ms + RMS_EPS) * g


def _sigmoid(x):
    return 1.0 / (1.0 + jnp.exp(-x))


def _dot(a, b):
    return jnp.dot(a, b, preferred_element_type=F32)


def _dot_nt(a, b):
    return lax.dot_general(a, b, (((1,), (1,)), ((), ())), preferred_element_type=F32)


def _split3(a):
    hi = a.astype(BF16)
    r1 = a - hi.astype(F32)
    mid = r1.astype(BF16)
    lo = (r1 - mid.astype(F32)).astype(BF16)
    return hi, mid, lo


def _ffn_kernel(x_ref, pre_ref, post_ref, wg_ref, wu_ref, wd_ref, o_ref):
    x = x_ref[...]
    h = _rms(x, pre_ref[...]).astype(BF16)
    g = _dot(h, wg_ref[...])
    u = _dot(h, wu_ref[...])
    a = (g * _sigmoid(g) * u).astype(BF16)
    o_ref[...] = x + 0.5 * _rms(_dot(a, wd_ref[...]), post_ref[...])


def _resident(a):
    return pl.BlockSpec(a.shape, lambda *_: (0,) * a.ndim, pipeline_mode=pl.Buffered(1))


def _ffn(x2d, pre, post, wg, wu, wd, *, tm=512):
    T = x2d.shape[0]
    return pl.pallas_call(
        _ffn_kernel,
        grid=(T // tm,),
        in_specs=[pl.BlockSpec((tm, D_MODEL), lambda i: (i, 0)), _resident(pre), _resident(post),
                  _resident(wg), _resident(wu), _resident(wd)],
        out_specs=pl.BlockSpec((tm, D_MODEL), lambda i: (i, 0)),
        out_shape=jax.ShapeDtypeStruct((T, D_MODEL), F32),
        compiler_params=_params(("parallel",)),
        name="ffn",
    )(x2d, pre, post, wg, wu, wd)


def _proj_kernel(x_ref, pre_ref, wq_ref, wkv_ref, wd_ref, wgate_ref, q_ref, kv_ref, g_ref, *rest):
    d_refs, scr = rest[:-1], rest[-1]
    tm = x_ref.shape[0]
    h = _rms(x_ref[...], pre_ref[...]).astype(BF16)
    q_ref[...] = _dot(h, wq_ref[...]).astype(BF16)
    kv_ref[...] = _dot(h, wkv_ref[...]).astype(BF16)
    g_ref[...] = _dot(h, wgate_ref[...])
    n_blk = DIL_QKV // LANES
    for gi, (_, dil) in enumerate(DIL_CONFIGS):
        res = _dot(h, wd_ref[:, gi * DIL_QKV:(gi + 1) * DIL_QKV])
        if dil == 1:
            d_refs[gi][0] = res.astype(BF16)
            continue
        for c in range(n_blk):
            scr[c] = res[:, c * LANES:(c + 1) * LANES]
        for r in range(dil):
            for c in range(n_blk):
                rows = scr[c, pl.ds(r, tm // dil, stride=dil), :]
                d_refs[gi][r, :, c * LANES:(c + 1) * LANES] = rows.astype(BF16)


def _proj(x2d, pre, wq, wkv, wd, wgate, *, B, S, tm=512):
    T = x2d.shape[0]
    nt = S // tm
    tok = lambda w: pl.BlockSpec((tm, w), lambda i: (i, 0))
    full = lambda a: pl.BlockSpec(a.shape, lambda i: (0,) * a.ndim)
    d_specs = [pl.BlockSpec((None, dil, tm // dil, DIL_QKV), lambda i: (i // nt, 0, i % nt, 0))
               for _, dil in DIL_CONFIGS]
    d_shapes = [jax.ShapeDtypeStruct((B, dil, S // dil, DIL_QKV), BF16) for _, dil in DIL_CONFIGS]
    return pl.pallas_call(
        _proj_kernel,
        grid=(T // tm,),
        in_specs=[tok(D_MODEL), _resident(pre), _resident(wq), _resident(wkv), _resident(wd), _resident(wgate)],
        out_specs=[tok(wq.shape[1]), tok(wkv.shape[1]), tok(LANES)] + d_specs,
        out_shape=[jax.ShapeDtypeStruct((T, wq.shape[1]), BF16), jax.ShapeDtypeStruct((T, wkv.shape[1]), BF16),
                   jax.ShapeDtypeStruct((T, LANES), F32)] + d_shapes,
        scratch_shapes=[pltpu.VMEM((DIL_QKV // LANES, tm, LANES), F32)],
        compiler_params=_params(("parallel",)),
        name="proj",
    )(x2d, pre, wq, wkv, wd, wgate)


def _gelu_tanh(x):
    c = math.sqrt(2.0 / math.pi)
    return x * (0.5 * (1.0 + jnp.tanh(c * (x + 0.044715 * (x * x * x)))))


def _cmp_mlp_kernel(xk_ref, xv_ref, pe_ref, w1cat_ref, w1_ref, w2_ref, kcols_ref, vtail_ref, kc_ref, vct_ref):
    n_chunks = xk_ref.shape[0]

    def mlp(x_ref, t):
        ab = _dot(x_ref[...], w1cat_ref[t])
        first = ab[:, :CMP_HIDDEN]
        second = pltpu.roll(ab[:, CMP_HIDDEN:], n_chunks - 1, 0)
        pe8 = jnp.broadcast_to(pe_ref[t], (8, pe_ref.shape[2]))
        bias = _dot(pe8, w1_ref[t])[0:1]
        hid = _gelu_tanh(first + second + bias).astype(BF16)
        out = _dot(hid, w2_ref[t])
        row = lax.broadcasted_iota(jnp.int32, out.shape, 0)
        return jnp.where(row < n_chunks - 1, out, 0.0)

    kc_ref[...] = mlp(xk_ref, 0).astype(BF16) + kcols_ref[...]
    vct_ref[...] = mlp(xv_ref, 1).T[0:VT_ROWS].astype(BF16) + vtail_ref[...]


def _cmp_mlp(xk, xv, pe, w1cat, w1, w2, kcols, vtail):
    BG, n_chunks, width = xk.shape
    per_n = lambda shape: pl.BlockSpec((None,) + shape, lambda n: (n, 0, 0))
    full = lambda a: pl.BlockSpec(a.shape, lambda n: (0,) * a.ndim)
    return pl.pallas_call(
        _cmp_mlp_kernel,
        grid=(BG,),
        in_specs=[per_n((n_chunks, width)), per_n((n_chunks, width)), full(pe), full(w1cat), full(w1),
                  full(w2), full(kcols), full(vtail)],
        out_specs=[per_n((n_chunks, LANES)), per_n((VT_ROWS, n_chunks))],
        out_shape=[jax.ShapeDtypeStruct((BG, n_chunks, LANES), BF16),
                   jax.ShapeDtypeStruct((BG, VT_ROWS, n_chunks), BF16)],
        compiler_params=_params(("parallel",)),
        name="cmp_mlp",
    )(xk, xv, pe, w1cat, w1, w2, kcols, vtail)


def _exp2_bf16(x):
    return jnp.exp2(x.astype(BF16))


def _online_update(s_t, v_t, m, acc):
    m_new = jnp.maximum(m, jnp.max(s_t, axis=0, keepdims=True))
    alpha = jnp.exp2(m - m_new)
    return m_new, alpha * acc + _dot(v_t, _exp2_bf16(s_t - m_new))


def _nsa_kernel(q_ref, gate_ref, kc_ref, vct_ref, kslc_ref, vslc_ref, kwin_ref, vwin_ref, kcols_ref,
                vtail_ref, band_ref, cmpmask_ref, ovt_ref, o_ref,
                selbias_scr, m_scr, acc_scr, ks_ref, vst_ref, kw_ref, vwt_ref, need_smem, list_smem,
                *, tq, tk, nt):
    g = pl.program_id(1)
    i = pl.program_id(2)
    n_ck = kc_ref.shape[0]
    n_slc = ovt_ref.shape[0]
    S = kslc_ref.shape[0]

    @pl.when(i == 0)
    def _():
        ks_ref[...] = kslc_ref[...] + kcols_ref[...]
        kw_ref[...] = kwin_ref[...] + kcols_ref[...]
        t_chunk = min(S, 512)
        for src, dst in ((vslc_ref, vst_ref), (vwin_ref, vwt_ref)):
            dst[HEAD_DIM:VT_ROWS, :] = vtail_ref[HEAD_DIM:VT_ROWS, :]
            for c in range(S // t_chunk):
                v = src[c * t_chunk:(c + 1) * t_chunk, :].astype(F32)
                dst[0:HEAD_DIM, c * t_chunk:(c + 1) * t_chunk] = v.T[0:HEAD_DIM].astype(BF16)

    tiles = range(nt)
    lane = lax.broadcasted_iota(jnp.int32, (1, LANES), 1)
    width = NSA_WINDOW + tq
    blocks_per_chunk = tk // SLC_LEN
    n_w = width // SLC_LEN
    jb = lax.broadcasted_iota(jnp.int32, (n_slc, 1), 0)
    jbf = jb.astype(F32)
    q0s = [(i * nt + t) * tq for t in tiles]
    backs = [jnp.minimum(q0, NSA_WINDOW) for q0 in q0s]
    wss = [pl.multiple_of(q0 - back, tq) for q0, back in zip(q0s, backs)]
    qss = [pl.multiple_of(q0, tq) for q0 in q0s]

    q_augs = []
    for t in tiles:
        slabs = []
        for r in range(NSA_REP):
            sl = jnp.where(g == 0, NSA_SLOPES[r], NSA_SLOPES[NSA_REP + r]) * LOG2E
            meta = jnp.zeros((1, LANES), F32)
            for c, coef in enumerate((SLC_LEN * sl, sl, -sl * q0s[t].astype(F32))):
                terms = _split3(jnp.full((1, LANES), coef, F32))
                for u in range(3):
                    meta = jnp.where(lane == HEAD_DIM + 3 * c + u, terms[u].astype(F32), meta)
            slabs.append(q_ref[t * tq:(t + 1) * tq, r * LANES:(r + 1) * LANES] + meta.astype(BF16))
        q_augs.append(jnp.concatenate(slabs, axis=0))

    s_alls = [_dot_nt(jnp.concatenate([kc_ref[...], kw_ref[pl.ds(wss[t], width), :],
                                       ks_ref[pl.ds(qss[t], tq), :]], axis=0), q_augs[t]) for t in tiles]

    s_cs = [s_alls[t][0:n_ck] + cmpmask_ref[pl.ds(pl.multiple_of(
        (n_ck - 8) - (tq // CMP_STRIDE) * (i * nt + t), 8), n_ck), :] for t in tiles]
    m_cs = [jnp.max(s, axis=0, keepdims=True) for s in s_cs]
    p_cs = [jnp.exp2(s - m) for s, m in zip(s_cs, m_cs)]
    den_cs = [jnp.sum(p, axis=0, keepdims=True) for p in p_cs]
    p_cs = [p * jnp.where(m > 0.5 * NEG, 1.0 / den, 0.0) for p, m, den in zip(p_cs, m_cs, den_cs)]
    o_cmps = [_dot(vct_ref[0:HEAD_DIM, :], p.astype(BF16)) for p in p_cs]

    ovt = ovt_ref[...]
    imps = []
    for p in p_cs:
        psum = p[:, 0:tq]
        for r in range(1, NSA_REP):
            psum = psum + p[:, r * tq:(r + 1) * tq]
        hi, mid, _ = _split3(psum)
        imps.append(_dot(ovt, hi) + _dot(ovt, mid))
    vals, sels = [], []
    for t in tiles:
        pos = q0s[t] + lax.broadcasted_iota(jnp.int32, (1, tq), 1)
        own = pos // SLC_LEN
        forced = (jb == 0) | (jb == own) | (jb == own - 1)
        valid = jb * SLC_LEN <= pos
        vals.append(jnp.where(valid & jnp.logical_not(forced), imps[t], -jnp.inf))
        sels.append(jnp.where(forced, 1.0, 0.0))

    s_ws = [s_alls[t][n_ck:n_ck + width]
            + band_ref[pl.ds(pl.multiple_of(NSA_WINDOW - backs[t], tq), width), :] for t in tiles]
    m_ws = [jnp.max(s, axis=0, keepdims=True) for s in s_ws]
    s_ds = [s_alls[t][n_ck + width:n_ck + width + tq] + band_ref[NSA_WINDOW:NSA_WINDOW + tq, :] for t in tiles]
    m_ds = [jnp.max(s, axis=0, keepdims=True) for s in s_ds]
    slices = [([(s_ws[t], m_ws[t], r0) for r0 in range(0, width, SLC_LEN)]
               + [(s_ds[t], m_ds[t], r0) for r0 in range(0, tq, SLC_LEN)]) for t in tiles]
    probs = [[] for _ in tiles]

    for it in range(max(N_SEL - 3, len(slices[0]))):
        for t in tiles:
            if it < N_SEL - 3:
                mx = jnp.max(vals[t], axis=0, keepdims=True)
                cand = jnp.where(vals[t] == mx, jbf, 1e9)
                idx = jnp.min(cand, axis=0, keepdims=True)
                pick = (jbf == idx) & (mx > -jnp.inf)
                sels[t] = jnp.where(pick, 1.0, sels[t])
                vals[t] = jnp.where(pick, -jnp.inf, vals[t])
            if it < len(slices[t]):
                src, m_src, r0 = slices[t][it]
                probs[t].append(_exp2_bf16(src[r0:r0 + SLC_LEN] - m_src))
    o_wins = []
    for t in tiles:
        acc_w = _dot(vwt_ref[:, pl.ds(wss[t], width)], jnp.concatenate(probs[t][:n_w], axis=0))
        o_wins.append(acc_w[0:HEAD_DIM] * (1.0 / acc_w[HEAD_DIM:HEAD_DIM + 1]))
        m_scr[t] = m_ds[t]
        acc_scr[t] = _dot(vst_ref[:, pl.ds(qss[t], tq)], jnp.concatenate(probs[t][n_w:], axis=0))

    n_needs = []
    for t in tiles:
        sbias = jnp.where((sels[t] > 0.5) & (jb < q0s[t] // SLC_LEN), 0.0, NEG)
        selbias_scr[t] = jnp.concatenate([sbias] * NSA_REP, axis=1)
        for c in range(n_slc // blocks_per_chunk):
            chunk_max = jnp.max(sbias[c * blocks_per_chunk:(c + 1) * blocks_per_chunk])
            need_smem[t, c] = (chunk_max > 0.5 * NEG).astype(jnp.int32)
    for t in tiles:
        n_need = jnp.int32(0)
        for c in range(n_slc // blocks_per_chunk):
            list_smem[t, n_need] = jnp.int32(c)
            n_need = n_need + need_smem[t, c]
        n_needs.append(n_need)

    o_slcs = []
    for t in tiles:
        q_aug, n_need = q_augs[t], n_needs[t]

        def masked(s_c, c, t=t):
            parts = []
            for jj in range(blocks_per_chunk):
                brow = selbias_scr[t, pl.ds(c * blocks_per_chunk + jj, 1), :]
                parts.append(s_c[jj * SLC_LEN:(jj + 1) * SLC_LEN] + brow)
            return jnp.concatenate(parts, axis=0)

        def pair_body(j, carry, t=t, q_aug=q_aug, masked=masked):
            ca, cb = list_smem[t, 2 * j], list_smem[t, 2 * j + 1]
            ka = pl.multiple_of(ca * tk, tk)
            kb = pl.multiple_of(cb * tk, tk)
            keys = jnp.concatenate([ks_ref[pl.ds(ka, tk), :], ks_ref[pl.ds(kb, tk), :]], axis=0)
            s_ab = _dot_nt(keys, q_aug)
            s_ab = jnp.concatenate([masked(s_ab[0:tk], ca), masked(s_ab[tk:2 * tk], cb)], axis=0)
            v_ab = jnp.concatenate([vst_ref[:, pl.ds(ka, tk)], vst_ref[:, pl.ds(kb, tk)]], axis=1)
            m_run, acc = _online_update(s_ab, v_ab, m_scr[t], acc_scr[t])
            m_scr[t] = m_run
            acc_scr[t] = acc
            return carry

        lax.fori_loop(0, n_need // 2, pair_body, 0)

        @pl.when(n_need % 2 == 1)
        def _(t=t, q_aug=q_aug, n_need=n_need, masked=masked):
            c = list_smem[t, n_need - 1]
            ks = pl.multiple_of(c * tk, tk)
            s_c = _dot_nt(ks_ref[pl.ds(ks, tk), :], q_aug)
            m_run, acc = _online_update(masked(s_c, c), vst_ref[:, pl.ds(ks, tk)], m_scr[t], acc_scr[t])
            m_scr[t] = m_run
            acc_scr[t] = acc

    for t in tiles:
        acc = acc_scr[t]
        o_slc = acc[0:HEAD_DIM] * (1.0 / acc[HEAD_DIM:HEAD_DIM + 1])
        gate_t = _sigmoid(gate_ref[t * tq:(t + 1) * tq, :].T)
        outs = []
        for r in range(NSA_REP):
            cols = slice(r * tq, (r + 1) * tq)

            def gate_row(branch):
                lo_row = 3 * r + branch
                hi_row = 3 * (NSA_REP + r) + branch
                return jnp.where(g == 0, gate_t[lo_row:lo_row + 1], gate_t[hi_row:hi_row + 1])

            outs.append(gate_row(0) * o_cmps[t][:, cols] + gate_row(1) * o_slc[:, cols]
                        + gate_row(2) * o_wins[t][:, cols])
        o_ref[t * tq:(t + 1) * tq, :] = jnp.concatenate(outs, axis=0).T.astype(BF16)


def _nsa(q_pad, gates, kc, vct, kv_pad, kcols, vtail, band, cmpmask, ovt, *, B, S, tq=NSA_TQ, tk=512,
         nt=NSA_TILES_PER_STEP):
    G = NSA_GROUPS
    n_ck = kc.shape[1]
    n_slc = ovt.shape[0]
    rows = NSA_REP * tq
    nq = S // (tq * nt)
    n_chunks = n_slc * SLC_LEN // tk
    per_bg = lambda shape: pl.BlockSpec((None,) + shape, lambda b, g, i: (b * G + g, 0, 0))
    full = lambda a: pl.BlockSpec(a.shape, lambda b, g, i: (0,) * a.ndim)
    kv_kind = lambda kind: pl.BlockSpec((S, LANES), lambda b, g, i: (b, kind * G + g))
    return pl.pallas_call(
        functools.partial(_nsa_kernel, tq=tq, tk=tk, nt=nt),
        grid=(B, G, nq),
        in_specs=[
            pl.BlockSpec((nt * tq, NSA_REP * LANES), lambda b, g, i: (b * nq + i, g)),
            pl.BlockSpec((nt * tq, LANES), lambda b, g, i: (b * nq + i, 0)),
            per_bg((n_ck, LANES)), per_bg((VT_ROWS, n_ck)),
            kv_kind(2), kv_kind(3), kv_kind(4), kv_kind(5),
            full(kcols), full(vtail), full(band), full(cmpmask), full(ovt),
        ],
        out_specs=pl.BlockSpec((nt * tq, NSA_REP * HEAD_DIM), lambda b, g, i: (b * nq + i, g)),
        out_shape=jax.ShapeDtypeStruct((B * S, Q_NSA), BF16),
        scratch_shapes=[pltpu.VMEM((nt, n_slc, rows), F32), pltpu.VMEM((nt, 1, rows), F32),
                        pltpu.VMEM((nt, VT_ROWS, rows), F32),
                        pltpu.VMEM((S, LANES), BF16), pltpu.VMEM((VT_ROWS, S), BF16),
                        pltpu.VMEM((S, LANES), BF16), pltpu.VMEM((VT_ROWS, S), BF16),
                        pltpu.SMEM((nt, n_chunks), jnp.int32), pltpu.SMEM((nt, n_chunks + 1), jnp.int32)],
        compiler_params=_params(("arbitrary", "arbitrary", "arbitrary")),
        name="nsa",
    )(q_pad, gates, kc, vct, kv_pad, kv_pad, kv_pad, kv_pad, kcols, vtail, band, cmpmask, ovt)


def _dil_attn_kernel(q_ref, k_ref, v_ref, dist_ref, o_ref, lse_ref, vt_scr, *, tq, width, max_back, slopes):
    i = pl.program_id(1)
    nb, L, _ = k_ref.shape
    t_chunk = min(L, 512)

    @pl.when(i == 0)
    def _():
        for b in range(nb):
            for pair in range(DIL_OUT // LANES):
                for c in range(L // t_chunk):
                    v = v_ref[b, c * t_chunk:(c + 1) * t_chunk, pair * LANES:(pair + 1) * LANES].astype(F32)
                    vt_scr[b, pair * LANES:(pair + 1) * LANES, c * t_chunk:(c + 1) * t_chunk] = v.T.astype(BF16)

    back = jnp.minimum(i * tq, max_back)
    ks = pl.multiple_of(i * tq - back, max_back)
    dist = dist_ref[pl.ds(pl.multiple_of(max_back - back, max_back), width), :]
    lane = lax.broadcasted_iota(jnp.int32, (1, LANES), 1)
    chains = [(b, h) for b in range(nb) for h in range(DIL_HPG)]
    scores = []
    for b, h in chains:
        pair, half = divmod(h, 2)
        in_head = (lane >= half * HEAD_DIM) & (lane < (half + 1) * HEAD_DIM)
        q = q_ref[b, :, pair * LANES:(pair + 1) * LANES]
        q = jnp.where(in_head, q, jnp.zeros_like(q))
        k = k_ref[b, pl.ds(ks, width), pair * LANES:(pair + 1) * LANES]
        scores.append(_dot_nt(k, q) - (slopes[h] * LOG2E) * dist)
    maxes = [jnp.max(s_t, axis=0, keepdims=True) for s_t in scores]
    probs = [jnp.exp2(s_t - m) for s_t, m in zip(scores, maxes)]
    dens = [jnp.sum(p, axis=0, keepdims=True) for p in probs]
    outs = [_dot(vt_scr[b, h * HEAD_DIM:(h + 1) * HEAD_DIM, pl.ds(ks, width)], p.astype(BF16)) * (1.0 / den)
            for (b, h), p, den in zip(chains, probs, dens)]
    lses = [jnp.broadcast_to((m + jnp.log2(den)) * (1.0 / LOG2E), (HEAD_DIM, tq)) for m, den in zip(maxes, dens)]
    for b in range(nb):
        o_ref[b] = jnp.concatenate(outs[b * DIL_HPG:(b + 1) * DIL_HPG], axis=0).T
        lse_ref[b] = jnp.concatenate(lses[b * DIL_HPG:(b + 1) * DIL_HPG], axis=0).T


def _dil_attn(qkv, dist_tbl, *, max_back, slopes, tq, nb):
    N, L, _ = qkv.shape
    width = dist_tbl.shape[0] - max_back
    kern = functools.partial(_dil_attn_kernel, tq=tq, width=width, max_back=max_back, slopes=slopes)
    return pl.pallas_call(
        kern,
        grid=(N // nb, L // tq),
        in_specs=[
            pl.BlockSpec((nb, tq, DIL_OUT), lambda n, i: (n, i, 0)),
            pl.BlockSpec((nb, L, DIL_OUT), lambda n, i: (n, 0, 1)),
            pl.BlockSpec((nb, L, DIL_OUT), lambda n, i: (n, 0, 2)),
            pl.BlockSpec(dist_tbl.shape, lambda n, i: (0, 0)),
        ],
        out_specs=[
            pl.BlockSpec((nb, tq, DIL_OUT), lambda n, i: (n, i, 0)),
            pl.BlockSpec((nb, tq, DIL_OUT), lambda n, i: (n, i, 0)),
        ],
        out_shape=[
            jax.ShapeDtypeStruct((N, L, DIL_OUT), F32),
            jax.ShapeDtypeStruct((N, L, DIL_OUT), F32),
        ],
        scratch_shapes=[pltpu.VMEM((nb, DIL_OUT, L), BF16)],
        compiler_params=_params(("parallel", "arbitrary")),
        name="dil_attn",
    )(qkv, qkv, qkv, dist_tbl)


def _final_kernel(x_ref, pre_ref, post_ref, onsa_ref, od0_ref, od1_ref, od2_ref, l0_ref, l1_ref, l2_ref,
                  wmerge_ref, wnsa_ref, wdil_ref, wout_ref, o_ref, scr):
    tm = x_ref.shape[0]
    x = x_ref[...]
    h = _rms(x, pre_ref[...]).astype(BF16)
    merge = _dot(h, wmerge_ref[...])
    g_a = _sigmoid(merge[:, :D_MODEL])
    g_b = _sigmoid(merge[:, D_MODEL:])
    y_nsa = _dot(onsa_ref[...], wnsa_ref[...])

    n_blk = DIL_OUT // LANES
    slot = 0
    token_major = []
    for (_, dil), o_in, l_in in zip(DIL_CONFIGS, (od0_ref, od1_ref, od2_ref), (l0_ref, l1_ref, l2_ref)):
        pair = []
        for src in (o_in, l_in):
            if dil == 1:
                pair.append(src[0])
                continue
            for r in range(dil):
                for c in range(n_blk):
                    scr[slot + c, pl.ds(r, tm // dil, stride=dil), :] = src[r, :, c * LANES:(c + 1) * LANES]
            pair.append(jnp.concatenate([scr[slot + c] for c in range(n_blk)], axis=1))
            slot += n_blk
        token_major.append(pair)
    (o0, l0), (o1, l1), (o2, l2) = token_major
    mx = jnp.maximum(jnp.maximum(l0, l1), l2)
    e0, e1, e2 = jnp.exp(l0 - mx), jnp.exp(l1 - mx), jnp.exp(l2 - mx)
    den = e0 + e1 + e2
    o_dil = (e0 / den) * o0 + (e1 / den) * o1 + (e2 / den) * o2
    y_dil = _dot(o_dil.astype(BF16), wdil_ref[...])

    y = g_a * y_nsa + g_b * y_dil
    z = _dot(y.astype(BF16), wout_ref[...])
    o_ref[...] = x + _rms(z, post_ref[...])


def _final(x2d, pre, post, onsa, od, ld, wmerge, wnsa, wdil, wout, *, S, tm=512):
    T = x2d.shape[0]
    nt = S // tm
    tok = lambda w: pl.BlockSpec((tm, w), lambda i: (i, 0))
    full = lambda a: pl.BlockSpec(a.shape, lambda i: (0,) * a.ndim)
    strided = [pl.BlockSpec((None, dil, tm // dil, DIL_OUT), lambda i: (i // nt, 0, i % nt, 0))
               for _, dil in DIL_CONFIGS]
    n_strided = sum(1 for _, dil in DIL_CONFIGS if dil > 1)
    return pl.pallas_call(
        _final_kernel,
        grid=(T // tm,),
        in_specs=[tok(D_MODEL), _resident(pre), _resident(post), tok(Q_NSA)] + strided + strided
        + [_resident(wmerge), _resident(wnsa), _resident(wdil), _resident(wout)],
        out_specs=tok(D_MODEL),
        out_shape=jax.ShapeDtypeStruct((T, D_MODEL), F32),
        scratch_shapes=[pltpu.VMEM((2 * n_strided * (DIL_OUT // LANES), tm, LANES), F32)],
        compiler_params=_params(("parallel",)),
        name="final",
    )(x2d, pre, post, onsa, od[0], od[1], od[2], ld[0], ld[1], ld[2], wmerge, wnsa, wdil, wout)


def _overlap_t(n_ck, n_slc):
    c0 = np.arange(n_ck)[None, :] * CMP_STRIDE
    s0 = np.arange(n_slc)[:, None] * SLC_LEN
    ov = np.clip(np.minimum(c0 + CMP_LEN, s0 + SLC_LEN) - np.maximum(c0, s0), 0, None) / CMP_LEN
    ov[:, n_ck - 1] = 0.0
    return ov.astype(np.float32)


def _pos_columns(pos):
    cols = np.zeros((len(pos), LANES), np.float32)
    for t, vals in enumerate((pos // SLC_LEN, pos % SLC_LEN, np.ones(len(pos)))):
        for u in range(3):
            cols[:, HEAD_DIM + 3 * t + u] = vals
    return cols


def _ones_row_tail(n):
    tail = np.zeros((VT_ROWS, n), np.float32)
    tail[HEAD_DIM] = 1.0
    return tail


def _band_table(tq):
    u = np.arange(2 * NSA_WINDOW + tq)[:, None]
    r = (np.arange(NSA_REP * tq) % tq)[None, :]
    dist = NSA_WINDOW + r - u
    return np.where((dist >= 0) & (dist <= NSA_WINDOW - 1), 0.0, NEG).astype(np.float32)


def _dil_dist_table(tq, width, max_back, dil):
    u = np.arange(width + max_back)[:, None]
    r = np.arange(tq)[None, :]
    dist = max_back + r - u
    return np.where((dist >= 0) & (dist <= max_back), dist * dil, -NEG).astype(np.float32)


def _cmp_mask_table(n_ck, tq):
    u = np.arange(2 * n_ck - 8)[:, None]
    r = (np.arange(NSA_REP * tq) % tq)[None, :]
    end = (u - (n_ck - 8)) * CMP_STRIDE + CMP_LEN - 1
    return np.where(end <= r, 0.0, NEG).astype(np.float32)


def _mixer(x2d, B, S, mix_pre, mix_post, w_in, pe_k, w_ck1, w_ck2, pe_v, w_cv1, w_cv2,
           w_nsa_o, w_dil_o, w_mix_out):
    G = NSA_GROUPS
    BG = B * G
    assert S % (16 * 128) == 0 and NSA_TQ // CMP_STRIDE == 8
    o_q, o_kv, o_g, o_d, o_m = np.cumsum([0, Q_NSA, 6 * KV_NSA, GATE_NSA, 3 * QKV_DIL]).tolist()
    wq = (w_in[:, o_q:o_kv] * (SCALE * LOG2E)).reshape(D_MODEL, NSA_HEADS, HEAD_DIM)
    wq = jnp.pad(wq, ((0, 0), (0, 0), (0, LANES - HEAD_DIM))).reshape(D_MODEL, NSA_HEADS * LANES)
    w_gate = jnp.pad(w_in[:, o_g:o_d], ((0, 0), (0, LANES - GATE_NSA)))
    wd = w_in[:, o_d:o_m].reshape(D_MODEL, 3, len(DIL_CONFIGS), DIL_OUT)
    wd = (wd * jnp.asarray([SCALE * LOG2E, 1.0, 1.0], F32)[None, :, None, None]).transpose(0, 2, 1, 3)
    wd = wd.reshape(D_MODEL, 3 * QKV_DIL)
    wkv = w_in[:, o_kv:o_g].reshape(D_MODEL, 6 * G, HEAD_DIM)
    wkv = jnp.pad(wkv, ((0, 0), (0, 0), (0, LANES - HEAD_DIM))).reshape(D_MODEL, 6 * G * LANES)
    q_pad, kv_pad, gates, *qkv_d = _proj(x2d, mix_pre[None, :], wq.astype(BF16), wkv.astype(BF16),
                                         wd.astype(BF16), w_gate.astype(BF16), B=B, S=S)

    n_ck = S // CMP_STRIDE
    half = CMP_STRIDE * HEAD_DIM
    def chunks(kind):
        per_g = [kv_pad[:, (kind * G + g) * LANES:(kind * G + g) * LANES + HEAD_DIM].reshape(B, 1, n_ck, half)
                 for g in range(G)]
        return jnp.concatenate(per_g, axis=1).reshape(BG, n_ck, half)
    w1 = jnp.stack([w_ck1, w_cv1])
    w1cat = jnp.concatenate([w1[:, :half], w1[:, half:]], axis=2).astype(BF16)
    pe = jnp.stack([pe_k.reshape(1, -1), pe_v.reshape(1, -1)]).astype(BF16)
    w2 = jnp.pad(jnp.stack([w_ck2, w_cv2]), ((0, 0), (0, 0), (0, LANES - HEAD_DIM))).astype(BF16)
    cmp_pos = np.arange(n_ck) * CMP_STRIDE + CMP_LEN - 1
    kc, vct = _cmp_mlp(chunks(0), chunks(1), pe, w1cat, w1.astype(BF16), w2,
                       jnp.asarray(_pos_columns(cmp_pos), BF16), jnp.asarray(_ones_row_tail(n_ck), BF16))

    band = jnp.asarray(_band_table(NSA_TQ))
    cmpmask = jnp.asarray(_cmp_mask_table(n_ck, NSA_TQ))
    ovt = jnp.asarray(_overlap_t(n_ck, S // SLC_LEN), BF16)
    o_nsa = _nsa(q_pad, gates, kc, vct, kv_pad, jnp.asarray(_pos_columns(np.arange(S)), BF16),
                 jnp.asarray(_ones_row_tail(S), BF16), band, cmpmask, ovt, B=B, S=S)

    od, ld = [], []
    for gi, (window, dil) in enumerate(DIL_CONFIGS):
        Ls = S // dil
        max_back = window // dil
        tq = min(DIL_TQ, Ls)
        width = tq + max_back if Ls > tq else tq
        o, lse = _dil_attn(qkv_d[gi].reshape(B * dil, Ls, DIL_QKV),
                           jnp.asarray(_dil_dist_table(tq, width, max_back, dil)),
                           max_back=max_back, slopes=DIL_SLOPES[gi * DIL_HPG:(gi + 1) * DIL_HPG], tq=tq,
                           nb=DIL_TQ // tq)
        od.append(o.reshape(B, dil, Ls, DIL_OUT))
        ld.append(lse.reshape(B, dil, Ls, DIL_OUT))

    return _final(x2d, mix_pre[None, :], mix_post[None, :], o_nsa, od, ld, w_in[:, o_m:].astype(BF16),
                  w_nsa_o.astype(BF16), w_dil_o.astype(BF16), w_mix_out.astype(BF16), S=S)


def kernel(x, ffn1_pre, ffn1_post, ffn1_w_gate, ffn1_w_up, ffn1_w_down, mix_pre, mix_post, w_in, nsa_pe_k, nsa_w_ck1, nsa_w_ck2, nsa_pe_v, nsa_w_cv1, nsa_w_cv2, w_nsa_o, w_dil_o, w_mix_out, ffn2_pre, ffn2_post, ffn2_w_gate, ffn2_w_up, ffn2_w_down):
    B, S, _ = x.shape
    x2d = x.reshape(B * S, D_MODEL)
    for l in range(ffn1_pre.shape[0]):
        x2d = _ffn(x2d, ffn1_pre[l][None, :], ffn1_post[l][None, :], ffn1_w_gate[l].astype(BF16),
                   ffn1_w_up[l].astype(BF16), ffn1_w_down[l].astype(BF16))
        x2d = _mixer(x2d, B, S, mix_pre[l], mix_post[l], w_in[l], nsa_pe_k[l], nsa_w_ck1[l], nsa_w_ck2[l],
                     nsa_pe_v[l], nsa_w_cv1[l], nsa_w_cv2[l], w_nsa_o[l], w_dil_o[l], w_mix_out[l])
        x2d = _ffn(x2d, ffn2_pre[l][None, :], ffn2_post[l][None, :], ffn2_w_gate[l].astype(BF16),
                   ffn2_w_up[l].astype(BF16), ffn2_w_down[l].astype(BF16))
    return x2d.reshape(B, S, D_MODEL)
```

```python
import functools
import math

import numpy as np
import jax
import jax.numpy as jnp
from jax import lax
from jax.experimental import pallas as pl
from jax.experimental.pallas import tpu as pltpu

F32 = jnp.float32
BF16 = jnp.bfloat16

D_MODEL = 1024
HEAD_DIM = 64
RMS_EPS = 1e-6
NSA_HEADS = 8
NSA_GROUPS = 2
NSA_REP = NSA_HEADS // NSA_GROUPS
CMP_LEN = 32
CMP_STRIDE = 16
CMP_HIDDEN = 256
SLC_LEN = 64
N_SEL = 16
NSA_WINDOW = 512
DIL_CONFIGS = ((128, 1), (512, 4), (2048, 16))
DIL_HPG = 4
DIL_HEADS = DIL_HPG * len(DIL_CONFIGS)
D_FF = 2816
Q_NSA = NSA_HEADS * HEAD_DIM
KV_NSA = NSA_GROUPS * HEAD_DIM
GATE_NSA = NSA_HEADS * 3
QKV_DIL = DIL_HEADS * HEAD_DIM
DIL_OUT = DIL_HPG * HEAD_DIM
DIL_QKV = 3 * DIL_OUT
DIL_TQ = 512
LANES = 128
NEG = -1e30
SCALE = HEAD_DIM ** -0.5
LOG2E = math.log2(math.e)
VMEM_LIMIT = 56 * 1024 * 1024
NSA_TQ = 128
NSA_TILES_PER_STEP = 4
VT_ROWS = 80


def _slopes(n):
    return [float(np.float32(2.0 ** (-8.0 * (i + 1) / n))) for i in range(n)]


NSA_SLOPES = _slopes(NSA_HEADS)
DIL_SLOPES = _slopes(DIL_HEADS)


def _params(sem):
    return pltpu.CompilerParams(dimension_semantics=sem, vmem_limit_bytes=VMEM_LIMIT)


def _rms(x, g):
    ms = jnp.mean(x * x, axis=-1, keepdims=True)
    return x * lax.rsqrt(ms + RMS_EPS) * g


def _sigmoid(x):
    return 1.0 / (1.0 + jnp.exp(-x))


def _dot(a, b):
    return jnp.dot(a, b, preferred_element_type=F32)


def _dot_nt(a, b):
    return lax.dot_general(a, b, (((1,), (1,)), ((), ())), preferred_element_type=F32)


def _split3(a):
    hi = a.astype(BF16)
    r1 = a - hi.astype(F32)
    mid = r1.astype(BF16)
    lo = (r1 - mid.astype(F32)).astype(BF16)
    return hi, mid, lo


def _ffn_kernel(x_ref, pre_ref, post_ref, wg_ref, wu_ref, wd_ref, o_ref):
    x = x_ref[...]
    h = _rms(x, pre_ref[...]).astype(BF16)
    g = _dot(h, wg_ref[...])
    u = _dot(h, wu_ref[...])
    a = (g * _sigmoid(g) * u).astype(BF16)
    o_ref[...] = x + 0.5 * _rms(_dot(a, wd_ref[...]), post_ref[...])


def _resident(a):
    return pl.BlockSpec(a.shape, lambda *_: (0,) * a.ndim, pipeline_mode=pl.Buffered(1))


def _ffn(x2d, pre, post, wg, wu, wd, *, tm=512):
    T = x2d.shape[0]
    return pl.pallas_call(
        _ffn_kernel,
        grid=(T // tm,),
        in_specs=[pl.BlockSpec((tm, D_MODEL), lambda i: (i, 0)), _resident(pre), _resident(post),
                  _resident(wg), _resident(wu), _resident(wd)],
        out_specs=pl.BlockSpec((tm, D_MODEL), lambda i: (i, 0)),
        out_shape=jax.ShapeDtypeStruct((T, D_MODEL), F32),
        compiler_params=_params(("parallel",)),
        name="ffn",
    )(x2d, pre, post, wg, wu, wd)


def _proj_kernel(x_ref, pre_ref, wq_ref, wkv_ref, wd_ref, wgate_ref, q_ref, kv_ref, g_ref, *rest):
    d_refs, scr = rest[:-1], rest[-1]
    tm = x_ref.shape[0]
    h = _rms(x_ref[...], pre_ref[...]).astype(BF16)
    q_ref[...] = _dot(h, wq_ref[...]).astype(BF16)
    kv_ref[...] = _dot(h, wkv_ref[...]).astype(BF16)
    g_ref[...] = _dot(h, wgate_ref[...])
    n_blk = DIL_QKV // LANES
    for gi, (_, dil) in enumerate(DIL_CONFIGS):
        res = _dot(h, wd_ref[:, gi * DIL_QKV:(gi + 1) * DIL_QKV])
        if dil == 1:
            d_refs[gi][0] = res.astype(BF16)
            continue
        for c in range(n_blk):
            scr[c] = res[:, c * LANES:(c + 1) * LANES]
        for r in range(dil):
            for c in range(n_blk):
                rows = scr[c, pl.ds(r, tm // dil, stride=dil), :]
                d_refs[gi][r, :, c * LANES:(c + 1) * LANES] = rows.astype(BF16)


def _proj(x2d, pre, wq, wkv, wd, wgate, *, B, S, tm=512):
    T = x2d.shape[0]
    nt = S // tm
    tok = lambda w: pl.BlockSpec((tm, w), lambda i: (i, 0))
    full = lambda a: pl.BlockSpec(a.shape, lambda i: (0,) * a.ndim)
    d_specs = [pl.BlockSpec((None, dil, tm // dil, DIL_QKV), lambda i: (i // nt, 0, i % nt, 0))
               for _, dil in DIL_CONFIGS]
    d_shapes = [jax.ShapeDtypeStruct((B, dil, S // dil, DIL_QKV), BF16) for _, dil in DIL_CONFIGS]
    return pl.pallas_call(
        _proj_kernel,
        grid=(T // tm,),
        in_specs=[tok(D_MODEL), _resident(pre), _resident(wq), _resident(wkv), _resident(wd), _resident(wgate)],
        out_specs=[tok(wq.shape[1]), tok(wkv.shape[1]), tok(LANES)] + d_specs,
        out_shape=[jax.ShapeDtypeStruct((T, wq.shape[1]), BF16), jax.ShapeDtypeStruct((T, wkv.shape[1]), BF16),
                   jax.ShapeDtypeStruct((T, LANES), F32)] + d_shapes,
        scratch_shapes=[pltpu.VMEM((DIL_QKV // LANES, tm, LANES), F32)],
        compiler_params=_params(("parallel",)),
        name="proj",
    )(x2d, pre, wq, wkv, wd, wgate)


def _gelu_tanh(x):
    c = math.sqrt(2.0 / math.pi)
    return x * (0.5 * (1.0 + jnp.tanh(c * (x + 0.044715 * (x * x * x)))))


def _cmp_mlp_kernel(xk_ref, xv_ref, pe_ref, w1cat_ref, w1_ref, w2_ref, kcols_ref, vtail_ref, kc_ref, vct_ref):
    n_chunks = xk_ref.shape[0]

    def mlp(x_ref, t):
        ab = _dot(x_ref[...], w1cat_ref[t])
        first = ab[:, :CMP_HIDDEN]
        second = pltpu.roll(ab[:, CMP_HIDDEN:], n_chunks - 1, 0)
        pe8 = jnp.broadcast_to(pe_ref[t], (8, pe_ref.shape[2]))
        bias = _dot(pe8, w1_ref[t])[0:1]
        hid = _gelu_tanh(first + second + bias).astype(BF16)
        out = _dot(hid, w2_ref[t])
        row = lax.broadcasted_iota(jnp.int32, out.shape, 0)
        return jnp.where(row < n_chunks - 1, out, 0.0)

    kc_ref[...] = mlp(xk_ref, 0).astype(BF16) + kcols_ref[...]
    vct_ref[...] = mlp(xv_ref, 1).T[0:VT_ROWS].astype(BF16) + vtail_ref[...]


def _cmp_mlp(xk, xv, pe, w1cat, w1, w2, kcols, vtail):
    BG, n_chunks, width = xk.shape
    per_n = lambda shape: pl.BlockSpec((None,) + shape, lambda n: (n, 0, 0))
    full = lambda a: pl.BlockSpec(a.shape, lambda n: (0,) * a.ndim)
    return pl.pallas_call(
        _cmp_mlp_kernel,
        grid=(BG,),
        in_specs=[per_n((n_chunks, width)), per_n((n_chunks, width)), full(pe), full(w1cat), full(w1),
                  full(w2), full(kcols), full(vtail)],
        out_specs=[per_n((n_chunks, LANES)), per_n((VT_ROWS, n_chunks))],
        out_shape=[jax.ShapeDtypeStruct((BG, n_chunks, LANES), BF16),
                   jax.ShapeDtypeStruct((BG, VT_ROWS, n_chunks), BF16)],
        compiler_params=_params(("parallel",)),
        name="cmp_mlp",
    )(xk, xv, pe, w1cat, w1, w2, kcols, vtail)


def _exp2_bf16(x):
    return jnp.exp2(x.astype(BF16))


def _low_half(x2, half):
    lane = lax.broadcasted_iota(jnp.int32, (1, LANES), 1)
    xf = x2.astype(F32)
    if half == 1:
        xf = pltpu.roll(xf, HEAD_DIM, 1)
    return jnp.where(lane < HEAD_DIM, xf, 0.0).astype(BF16)


def _online_update(s_t, v_t, m, acc):
    m_new = jnp.maximum(m, jnp.max(s_t, axis=0, keepdims=True))
    alpha = jnp.exp2(m - m_new)
    return m_new, alpha * acc + _dot(v_t, _exp2_bf16(s_t - m_new))


def _nsa_kernel(q_ref, gate_ref, kc_ref, vct_ref, kslc_ref, vslc_ref, kwin_ref, vwin_ref, kcols_ref,
                vtail_ref, band_ref, cmpmask_ref, ovt_ref, o_ref,
                selbias_scr, m_scr, acc_scr, ks_ref, vst_ref, kw_ref, vwt_ref, need_smem, list_smem,
                *, tq, tk, nt):
    g = pl.program_id(1)
    i = pl.program_id(2)
    n_ck = kc_ref.shape[0]
    n_slc = ovt_ref.shape[0]
    S = kslc_ref.shape[0]

    @pl.when(i == 0)
    def _():
        t_chunk = min(S, 512)
        for src, dst in ((kslc_ref, ks_ref), (kwin_ref, kw_ref)):
            for c in range(S // t_chunk):
                rows_c = slice(c * t_chunk, (c + 1) * t_chunk)
                k2 = src[rows_c, :]
                mine = jnp.where(g == 0, _low_half(k2, 0), _low_half(k2, 1))
                dst[rows_c, :] = mine + kcols_ref[rows_c, :]
        for src, dst in ((vslc_ref, vst_ref), (vwin_ref, vwt_ref)):
            dst[HEAD_DIM:VT_ROWS, :] = vtail_ref[HEAD_DIM:VT_ROWS, :]
            for c in range(S // t_chunk):
                v = src[c * t_chunk:(c + 1) * t_chunk, :].astype(F32)
                v_t = v.T
                mine = jnp.where(g == 0, v_t[0:HEAD_DIM], v_t[HEAD_DIM:2 * HEAD_DIM])
                dst[0:HEAD_DIM, c * t_chunk:(c + 1) * t_chunk] = mine.astype(BF16)

    tiles = range(nt)
    lane = lax.broadcasted_iota(jnp.int32, (1, LANES), 1)
    width = NSA_WINDOW + tq
    blocks_per_chunk = tk // SLC_LEN
    n_w = width // SLC_LEN
    jb = lax.broadcasted_iota(jnp.int32, (n_slc, 1), 0)
    jbf = jb.astype(F32)
    q0s = [(i * nt + t) * tq for t in tiles]
    backs = [jnp.minimum(q0, NSA_WINDOW) for q0 in q0s]
    wss = [pl.multiple_of(q0 - back, tq) for q0, back in zip(q0s, backs)]
    qss = [pl.multiple_of(q0, tq) for q0 in q0s]

    q_augs = []
    for t in tiles:
        slabs = []
        for r in range(NSA_REP):
            sl = jnp.where(g == 0, NSA_SLOPES[r], NSA_SLOPES[NSA_REP + r]) * LOG2E
            meta = jnp.zeros((1, LANES), F32)
            for c, coef in enumerate((SLC_LEN * sl, sl, -sl * q0s[t].astype(F32))):
                terms = _split3(jnp.full((1, LANES), coef, F32))
                for u in range(3):
                    meta = jnp.where(lane == HEAD_DIM + 3 * c + u, terms[u].astype(F32), meta)
            q2 = q_ref[t * tq:(t + 1) * tq, (r // 2) * LANES:(r // 2 + 1) * LANES]
            slabs.append(_low_half(q2, r % 2) + meta.astype(BF16))
        q_augs.append(jnp.concatenate(slabs, axis=0))

    s_alls = [_dot_nt(jnp.concatenate([kc_ref[...], kw_ref[pl.ds(wss[t], width), :],
                                       ks_ref[pl.ds(qss[t], tq), :]], axis=0), q_augs[t]) for t in tiles]

    s_cs = [s_alls[t][0:n_ck] + cmpmask_ref[pl.ds(pl.multiple_of(
        (n_ck - 8) - (tq // CMP_STRIDE) * (i * nt + t), 8), n_ck), :] for t in tiles]
    m_cs = [jnp.max(s, axis=0, keepdims=True) for s in s_cs]
    p_cs = [jnp.exp2(s - m) for s, m in zip(s_cs, m_cs)]
    den_cs = [jnp.sum(p, axis=0, keepdims=True) for p in p_cs]
    p_cs = [p * jnp.where(m > 0.5 * NEG, 1.0 / den, 0.0) for p, m, den in zip(p_cs, m_cs, den_cs)]
    o_cmps = [_dot(vct_ref[0:HEAD_DIM, :], p.astype(BF16)) for p in p_cs]

    ovt = ovt_ref[...]
    imps = []
    for p in p_cs:
        psum = p[:, 0:tq]
        for r in range(1, NSA_REP):
            psum = psum + p[:, r * tq:(r + 1) * tq]
        hi, mid, _ = _split3(psum)
        imps.append(_dot(ovt, hi) + _dot(ovt, mid))
    vals, sels = [], []
    for t in tiles:
        pos = q0s[t] + lax.broadcasted_iota(jnp.int32, (1, tq), 1)
        own = pos // SLC_LEN
        forced = (jb == 0) | (jb == own) | (jb == own - 1)
        valid = jb * SLC_LEN <= pos
        vals.append(jnp.where(valid & jnp.logical_not(forced), imps[t], -jnp.inf))
        sels.append(jnp.where(forced, 1.0, 0.0))

    s_ws = [s_alls[t][n_ck:n_ck + width]
            + band_ref[pl.ds(pl.multiple_of(NSA_WINDOW - backs[t], tq), width), :] for t in tiles]
    m_ws = [jnp.max(s, axis=0, keepdims=True) for s in s_ws]
    s_ds = [s_alls[t][n_ck + width:n_ck + width + tq] + band_ref[NSA_WINDOW:NSA_WINDOW + tq, :] for t in tiles]
    m_ds = [jnp.max(s, axis=0, keepdims=True) for s in s_ds]
    slices = [([(s_ws[t], m_ws[t], r0) for r0 in range(0, width, SLC_LEN)]
               + [(s_ds[t], m_ds[t], r0) for r0 in range(0, tq, SLC_LEN)]) for t in tiles]
    probs = [[] for _ in tiles]

    for it in range(max(N_SEL - 3, len(slices[0]))):
        for t in tiles:
            if it < N_SEL - 3:
                mx = jnp.max(vals[t], axis=0, keepdims=True)
                cand = jnp.where(vals[t] == mx, jbf, 1e9)
                idx = jnp.min(cand, axis=0, keepdims=True)
                pick = (jbf == idx) & (mx > -jnp.inf)
                sels[t] = jnp.where(pick, 1.0, sels[t])
                vals[t] = jnp.where(pick, -jnp.inf, vals[t])
            if it < len(slices[t]):
                src, m_src, r0 = slices[t][it]
                probs[t].append(_exp2_bf16(src[r0:r0 + SLC_LEN] - m_src))
    o_wins = []
    for t in tiles:
        acc_w = _dot(vwt_ref[:, pl.ds(wss[t], width)], jnp.concatenate(probs[t][:n_w], axis=0))
        o_wins.append(acc_w[0:HEAD_DIM] * (1.0 / acc_w[HEAD_DIM:HEAD_DIM + 1]))
        m_scr[t] = m_ds[t]
        acc_scr[t] = _dot(vst_ref[:, pl.ds(qss[t], tq)], jnp.concatenate(probs[t][n_w:], axis=0))

    n_needs = []
    for t in tiles:
        sbias = jnp.where((sels[t] > 0.5) & (jb < q0s[t] // SLC_LEN), 0.0, NEG)
        selbias_scr[t] = jnp.concatenate([sbias] * NSA_REP, axis=1)
        for c in range(n_slc // blocks_per_chunk):
            chunk_max = jnp.max(sbias[c * blocks_per_chunk:(c + 1) * blocks_per_chunk])
            need_smem[t, c] = (chunk_max > 0.5 * NEG).astype(jnp.int32)
    for t in tiles:
        n_need = jnp.int32(0)
        for c in range(n_slc // blocks_per_chunk):
            list_smem[t, n_need] = jnp.int32(c)
            n_need = n_need + need_smem[t, c]
        n_needs.append(n_need)

    o_slcs = []
    for t in tiles:
        q_aug, n_need = q_augs[t], n_needs[t]

        def masked(s_c, c, t=t):
            parts = []
            for jj in range(blocks_per_chunk):
                brow = selbias_scr[t, pl.ds(c * blocks_per_chunk + jj, 1), :]
                parts.append(s_c[jj * SLC_LEN:(jj + 1) * SLC_LEN] + brow)
            return jnp.concatenate(parts, axis=0)

        def pair_body(j, carry, t=t, q_aug=q_aug, masked=masked):
            ca, cb = list_smem[t, 2 * j], list_smem[t, 2 * j + 1]
            ka = pl.multiple_of(ca * tk, tk)
            kb = pl.multiple_of(cb * tk, tk)
            keys = jnp.concatenate([ks_ref[pl.ds(ka, tk), :], ks_ref[pl.ds(kb, tk), :]], axis=0)
            s_ab = _dot_nt(keys, q_aug)
            s_ab = jnp.concatenate([masked(s_ab[0:tk], ca), masked(s_ab[tk:2 * tk], cb)], axis=0)
            v_ab = jnp.concatenate([vst_ref[:, pl.ds(ka, tk)], vst_ref[:, pl.ds(kb, tk)]], axis=1)
            m_run, acc = _online_update(s_ab, v_ab, m_scr[t], acc_scr[t])
            m_scr[t] = m_run
            acc_scr[t] = acc
            return carry

        lax.fori_loop(0, n_need // 2, pair_body, 0)

        @pl.when(n_need % 2 == 1)
        def _(t=t, q_aug=q_aug, n_need=n_need, masked=masked):
            c = list_smem[t, n_need - 1]
            ks = pl.multiple_of(c * tk, tk)
            s_c = _dot_nt(ks_ref[pl.ds(ks, tk), :], q_aug)
            m_run, acc = _online_update(masked(s_c, c), vst_ref[:, pl.ds(ks, tk)], m_scr[t], acc_scr[t])
            m_scr[t] = m_run
            acc_scr[t] = acc

    for t in tiles:
        acc = acc_scr[t]
        o_slc = acc[0:HEAD_DIM] * (1.0 / acc[HEAD_DIM:HEAD_DIM + 1])
        gate_t = _sigmoid(gate_ref[t * tq:(t + 1) * tq, :].T)
        outs = []
        for r in range(NSA_REP):
            cols = slice(r * tq, (r + 1) * tq)

            def gate_row(branch):
                lo_row = 3 * r + branch
                hi_row = 3 * (NSA_REP + r) + branch
                return jnp.where(g == 0, gate_t[lo_row:lo_row + 1], gate_t[hi_row:hi_row + 1])

            outs.append(gate_row(0) * o_cmps[t][:, cols] + gate_row(1) * o_slc[:, cols]
                        + gate_row(2) * o_wins[t][:, cols])
        o_ref[t * tq:(t + 1) * tq, :] = jnp.concatenate(outs, axis=0).T.astype(BF16)


def _nsa(q_pad, gates, kc, vct, kv_pad, kcols, vtail, band, cmpmask, ovt, *, B, S, tq=NSA_TQ, tk=512,
         nt=NSA_TILES_PER_STEP):
    G = NSA_GROUPS
    n_ck = kc.shape[1]
    n_slc = ovt.shape[0]
    rows = NSA_REP * tq
    nq = S // (tq * nt)
    n_chunks = n_slc * SLC_LEN // tk
    per_bg = lambda shape: pl.BlockSpec((None,) + shape, lambda b, g, i: (b * G + g, 0, 0))
    full = lambda a: pl.BlockSpec(a.shape, lambda b, g, i: (0,) * a.ndim)
    kv_kind = lambda kind: pl.BlockSpec((S, LANES), lambda b, g, i: (b, kind))
    return pl.pallas_call(
        functools.partial(_nsa_kernel, tq=tq, tk=tk, nt=nt),
        grid=(B, G, nq),
        in_specs=[
            pl.BlockSpec((nt * tq, NSA_REP * HEAD_DIM), lambda b, g, i: (b * nq + i, g)),
            pl.BlockSpec((nt * tq, LANES), lambda b, g, i: (b * nq + i, 0)),
            per_bg((n_ck, LANES)), per_bg((VT_ROWS, n_ck)),
            kv_kind(2), kv_kind(3), kv_kind(4), kv_kind(5),
            full(kcols), full(vtail), full(band), full(cmpmask), full(ovt),
        ],
        out_specs=pl.BlockSpec((nt * tq, NSA_REP * HEAD_DIM), lambda b, g, i: (b * nq + i, g)),
        out_shape=jax.ShapeDtypeStruct((B * S, Q_NSA), BF16),
        scratch_shapes=[pltpu.VMEM((nt, n_slc, rows), F32), pltpu.VMEM((nt, 1, rows), F32),
                        pltpu.VMEM((nt, VT_ROWS, rows), F32),
                        pltpu.VMEM((S, LANES), BF16), pltpu.VMEM((VT_ROWS, S), BF16),
                        pltpu.VMEM((S, LANES), BF16), pltpu.VMEM((VT_ROWS, S), BF16),
                        pltpu.SMEM((nt, n_chunks), jnp.int32), pltpu.SMEM((nt, n_chunks + 1), jnp.int32)],
        compiler_params=_params(("arbitrary", "arbitrary", "arbitrary")),
        name="nsa",
    )(q_pad, gates, kc, vct, kv_pad, kv_pad, kv_pad, kv_pad, kcols, vtail, band, cmpmask, ovt)


def _dil_attn_kernel(q_ref, k_ref, v_ref, dist_ref, o_ref, lse_ref, vt_scr, *, tq, width, max_back, slopes):
    i = pl.program_id(1)
    nb, L, _ = k_ref.shape
    t_chunk = min(L, 512)

    @pl.when(i == 0)
    def _():
        tail_row = lax.broadcasted_iota(jnp.int32, (VT_ROWS - HEAD_DIM, L), 0)
        tail = jnp.where(tail_row == 0, 1.0, 0.0).astype(BF16)
        for b in range(nb):
            for h in range(DIL_HPG):
                vt_scr[b, h * VT_ROWS + HEAD_DIM:(h + 1) * VT_ROWS, :] = tail
            for pair in range(DIL_OUT // LANES):
                for c in range(L // t_chunk):
                    cols = slice(c * t_chunk, (c + 1) * t_chunk)
                    v_t = v_ref[b, cols, pair * LANES:(pair + 1) * LANES].astype(F32).T.astype(BF16)
                    for half in range(2):
                        h = 2 * pair + half
                        vt_scr[b, h * VT_ROWS:h * VT_ROWS + HEAD_DIM, cols] = v_t[half * HEAD_DIM:(half + 1) * HEAD_DIM]

    back = jnp.minimum(i * tq, max_back)
    ks = pl.multiple_of(i * tq - back, max_back)
    dist = dist_ref[pl.ds(pl.multiple_of(max_back - back, max_back), width), :]
    lane = lax.broadcasted_iota(jnp.int32, (1, LANES), 1)
    chains = [(b, h) for b in range(nb) for h in range(DIL_HPG)]
    scores = []
    for b, h in chains:
        pair, half = divmod(h, 2)
        in_head = (lane >= half * HEAD_DIM) & (lane < (half + 1) * HEAD_DIM)
        q = q_ref[b, :, pair * LANES:(pair + 1) * LANES]
        q = jnp.where(in_head, q, jnp.zeros_like(q))
        k = k_ref[b, pl.ds(ks, width), pair * LANES:(pair + 1) * LANES]
        scores.append(_dot_nt(k, q) - (slopes[h] * LOG2E) * dist)
    maxes = [jnp.max(s_t, axis=0, keepdims=True) for s_t in scores]
    probs = [_exp2_bf16(s_t - m) for s_t, m in zip(scores, maxes)]
    accs = [_dot(vt_scr[b, h * VT_ROWS:(h + 1) * VT_ROWS, pl.ds(ks, width)], p) for (b, h), p in zip(chains, probs)]
    dens = [acc[HEAD_DIM:HEAD_DIM + 1] for acc in accs]
    outs = [acc[0:HEAD_DIM] * (1.0 / den) for acc, den in zip(accs, dens)]
    lses = [jnp.broadcast_to((m + jnp.log2(den)) * (1.0 / LOG2E), (HEAD_DIM, tq)) for m, den in zip(maxes, dens)]
    for b in range(nb):
        o_ref[b] = jnp.concatenate(outs[b * DIL_HPG:(b + 1) * DIL_HPG], axis=0).T
        lse_ref[b] = jnp.concatenate(lses[b * DIL_HPG:(b + 1) * DIL_HPG], axis=0).T


def _dil_attn(qkv, dist_tbl, *, max_back, slopes, tq, nb):
    N, L, _ = qkv.shape
    width = dist_tbl.shape[0] - max_back
    kern = functools.partial(_dil_attn_kernel, tq=tq, width=width, max_back=max_back, slopes=slopes)
    return pl.pallas_call(
        kern,
        grid=(N // nb, L // tq),
        in_specs=[
            pl.BlockSpec((nb, tq, DIL_OUT), lambda n, i: (n, i, 0)),
            pl.BlockSpec((nb, L, DIL_OUT), lambda n, i: (n, 0, 1)),
            pl.BlockSpec((nb, L, DIL_OUT), lambda n, i: (n, 0, 2)),
            pl.BlockSpec(dist_tbl.shape, lambda n, i: (0, 0)),
        ],
        out_specs=[
            pl.BlockSpec((nb, tq, DIL_OUT), lambda n, i: (n, i, 0)),
            pl.BlockSpec((nb, tq, DIL_OUT), lambda n, i: (n, i, 0)),
        ],
        out_shape=[
            jax.ShapeDtypeStruct((N, L, DIL_OUT), F32),
            jax.ShapeDtypeStruct((N, L, DIL_OUT), F32),
        ],
        scratch_shapes=[pltpu.VMEM((nb, DIL_HPG * VT_ROWS, L), BF16)],
        compiler_params=_params(("parallel", "arbitrary")),
        name="dil_attn",
    )(qkv, qkv, qkv, dist_tbl)


def _final_kernel(x_ref, pre_ref, post_ref, onsa_ref, od0_ref, od1_ref, od2_ref, l0_ref, l1_ref, l2_ref,
                  wmerge_ref, wnsa_ref, wdil_ref, wout_ref, o_ref, scr):
    tm = x_ref.shape[0]
    x = x_ref[...]
    h = _rms(x, pre_ref[...]).astype(BF16)
    merge = _dot(h, wmerge_ref[...])
    g_a = _sigmoid(merge[:, :D_MODEL])
    g_b = _sigmoid(merge[:, D_MODEL:])
    y_nsa = _dot(onsa_ref[...], wnsa_ref[...])

    n_blk = DIL_OUT // LANES
    slot = 0
    token_major = []
    for (_, dil), o_in, l_in in zip(DIL_CONFIGS, (od0_ref, od1_ref, od2_ref), (l0_ref, l1_ref, l2_ref)):
        pair = []
        for src in (o_in, l_in):
            if dil == 1:
                pair.append(src[0])
                continue
            for r in range(dil):
                for c in range(n_blk):
                    scr[slot + c, pl.ds(r, tm // dil, stride=dil), :] = src[r, :, c * LANES:(c + 1) * LANES]
            pair.append(jnp.concatenate([scr[slot + c] for c in range(n_blk)], axis=1))
            slot += n_blk
        token_major.append(pair)
    (o0, l0), (o1, l1), (o2, l2) = token_major
    mx = jnp.maximum(jnp.maximum(l0, l1), l2)
    e0, e1, e2 = jnp.exp(l0 - mx), jnp.exp(l1 - mx), jnp.exp(l2 - mx)
    den = e0 + e1 + e2
    o_dil = (e0 / den) * o0 + (e1 / den) * o1 + (e2 / den) * o2
    y_dil = _dot(o_dil.astype(BF16), wdil_ref[...])

    y = g_a * y_nsa + g_b * y_dil
    z = _dot(y.astype(BF16), wout_ref[...])
    o_ref[...] = x + _rms(z, post_ref[...])


def _final(x2d, pre, post, onsa, od, ld, wmerge, wnsa, wdil, wout, *, S, tm=512):
    T = x2d.shape[0]
    nt = S // tm
    tok = lambda w: pl.BlockSpec((tm, w), lambda i: (i, 0))
    full = lambda a: pl.BlockSpec(a.shape, lambda i: (0,) * a.ndim)
    strided = [pl.BlockSpec((None, dil, tm // dil, DIL_OUT), lambda i: (i // nt, 0, i % nt, 0))
               for _, dil in DIL_CONFIGS]
    n_strided = sum(1 for _, dil in DIL_CONFIGS if dil > 1)
    return pl.pallas_call(
        _final_kernel,
        grid=(T // tm,),
        in_specs=[tok(D_MODEL), _resident(pre), _resident(post), tok(Q_NSA)] + strided + strided
        + [_resident(wmerge), _resident(wnsa), _resident(wdil), _resident(wout)],
        out_specs=tok(D_MODEL),
        out_shape=jax.ShapeDtypeStruct((T, D_MODEL), F32),
        scratch_shapes=[pltpu.VMEM((2 * n_strided * (DIL_OUT // LANES), tm, LANES), F32)],
        compiler_params=_params(("parallel",)),
        name="final",
    )(x2d, pre, post, onsa, od[0], od[1], od[2], ld[0], ld[1], ld[2], wmerge, wnsa, wdil, wout)


def _overlap_t(n_ck, n_slc):
    c0 = np.arange(n_ck)[None, :] * CMP_STRIDE
    s0 = np.arange(n_slc)[:, None] * SLC_LEN
    ov = np.clip(np.minimum(c0 + CMP_LEN, s0 + SLC_LEN) - np.maximum(c0, s0), 0, None) / CMP_LEN
    ov[:, n_ck - 1] = 0.0
    return ov.astype(np.float32)


def _pos_columns(pos):
    cols = np.zeros((len(pos), LANES), np.float32)
    for t, vals in enumerate((pos // SLC_LEN, pos % SLC_LEN, np.ones(len(pos)))):
        for u in range(3):
            cols[:, HEAD_DIM + 3 * t + u] = vals
    return cols


def _ones_row_tail(n):
    tail = np.zeros((VT_ROWS, n), np.float32)
    tail[HEAD_DIM] = 1.0
    return tail


def _band_table(tq):
    u = np.arange(2 * NSA_WINDOW + tq)[:, None]
    r = (np.arange(NSA_REP * tq) % tq)[None, :]
    dist = NSA_WINDOW + r - u
    return np.where((dist >= 0) & (dist <= NSA_WINDOW - 1), 0.0, NEG).astype(np.float32)


def _dil_dist_table(tq, width, max_back, dil):
    u = np.arange(width + max_back)[:, None]
    r = np.arange(tq)[None, :]
    dist = max_back + r - u
    return np.where((dist >= 0) & (dist <= max_back), dist * dil, -NEG).astype(np.float32)


def _cmp_mask_table(n_ck, tq):
    u = np.arange(2 * n_ck - 8)[:, None]
    r = (np.arange(NSA_REP * tq) % tq)[None, :]
    end = (u - (n_ck - 8)) * CMP_STRIDE + CMP_LEN - 1
    return np.where(end <= r, 0.0, NEG).astype(np.float32)


def _mixer(x2d, B, S, mix_pre, mix_post, w_in, pe_k, w_ck1, w_ck2, pe_v, w_cv1, w_cv2,
           w_nsa_o, w_dil_o, w_mix_out):
    G = NSA_GROUPS
    BG = B * G
    assert S % (16 * 128) == 0 and NSA_TQ // CMP_STRIDE == 8
    o_q, o_kv, o_g, o_d, o_m = np.cumsum([0, Q_NSA, 6 * KV_NSA, GATE_NSA, 3 * QKV_DIL]).tolist()
    wq = w_in[:, o_q:o_kv] * (SCALE * LOG2E)
    w_gate = jnp.pad(w_in[:, o_g:o_d], ((0, 0), (0, LANES - GATE_NSA)))
    wd = w_in[:, o_d:o_m].reshape(D_MODEL, 3, len(DIL_CONFIGS), DIL_OUT)
    wd = (wd * jnp.asarray([SCALE * LOG2E, 1.0, 1.0], F32)[None, :, None, None]).transpose(0, 2, 1, 3)
    wd = wd.reshape(D_MODEL, 3 * QKV_DIL)
    q_pad, kv_pad, gates, *qkv_d = _proj(x2d, mix_pre[None, :], wq.astype(BF16), w_in[:, o_kv:o_g].astype(BF16),
                                         wd.astype(BF16), w_gate.astype(BF16), B=B, S=S)

    n_ck = S // CMP_STRIDE
    half = CMP_STRIDE * HEAD_DIM
    def chunks(kind):
        per_g = [kv_pad[:, (kind * G + g) * HEAD_DIM:(kind * G + g + 1) * HEAD_DIM].reshape(B, 1, n_ck, half)
                 for g in range(G)]
        return jnp.concatenate(per_g, axis=1).reshape(BG, n_ck, half)
    w1 = jnp.stack([w_ck1, w_cv1])
    w1cat = jnp.concatenate([w1[:, :half], w1[:, half:]], axis=2).astype(BF16)
    pe = jnp.stack([pe_k.reshape(1, -1), pe_v.reshape(1, -1)]).astype(BF16)
    w2 = jnp.pad(jnp.stack([w_ck2, w_cv2]), ((0, 0), (0, 0), (0, LANES - HEAD_DIM))).astype(BF16)
    cmp_pos = np.arange(n_ck) * CMP_STRIDE + CMP_LEN - 1
    kc, vct = _cmp_mlp(chunks(0), chunks(1), pe, w1cat, w1.astype(BF16), w2,
                       jnp.asarray(_pos_columns(cmp_pos), BF16), jnp.asarray(_ones_row_tail(n_ck), BF16))

    band = jnp.asarray(_band_table(NSA_TQ))
    cmpmask = jnp.asarray(_cmp_mask_table(n_ck, NSA_TQ))
    ovt = jnp.asarray(_overlap_t(n_ck, S // SLC_LEN), BF16)
    o_nsa = _nsa(q_pad, gates, kc, vct, kv_pad, jnp.asarray(_pos_columns(np.arange(S)), BF16),
                 jnp.asarray(_ones_row_tail(S), BF16), band, cmpmask, ovt, B=B, S=S)

    od, ld = [], []
    for gi, (window, dil) in enumerate(DIL_CONFIGS):
        Ls = S // dil
        max_back = window // dil
        tq = min(DIL_TQ, Ls)
        width = tq + max_back if Ls > tq else tq
        o, lse = _dil_attn(qkv_d[gi].reshape(B * dil, Ls, DIL_QKV),
                           jnp.asarray(_dil_dist_table(tq, width, max_back, dil)),
                           max_back=max_back, slopes=DIL_SLOPES[gi * DIL_HPG:(gi + 1) * DIL_HPG], tq=tq,
                           nb=DIL_TQ // tq)
        od.append(o.reshape(B, dil, Ls, DIL_OUT))
        ld.append(lse.reshape(B, dil, Ls, DIL_OUT))

    return _final(x2d, mix_pre[None, :], mix_post[None, :], o_nsa, od, ld, w_in[:, o_m:].astype(BF16),
                  w_nsa_o.astype(BF16), w_dil_o.astype(BF16), w_mix_out.astype(BF16), S=S)


def kernel(x, ffn1_pre, ffn1_post, ffn1_w_gate, ffn1_w_up, ffn1_w_down, mix_pre, mix_post, w_in, nsa_pe_k, nsa_w_ck1, nsa_w_ck2, nsa_pe_v, nsa_w_cv1, nsa_w_cv2, w_nsa_o, w_dil_o, w_mix_out, ffn2_pre, ffn2_post, ffn2_w_gate, ffn2_w_up, ffn2_w_down):
    B, S, _ = x.shape
    x2d = x.reshape(B * S, D_MODEL)
    for l in range(ffn1_pre.shape[0]):
        x2d = _ffn(x2d, ffn1_pre[l][None, :], ffn1_post[l][None, :], ffn1_w_gate[l].astype(BF16),
                   ffn1_w_up[l].astype(BF16), ffn1_w_down[l].astype(BF16))
        x2d = _mixer(x2d, B, S, mix_pre[l], mix_post[l], w_in[l], nsa_pe_k[l], nsa_w_ck1[l], nsa_w_ck2[l],
                     nsa_pe_v[l], nsa_w_cv1[l], nsa_w_cv2[l], w_nsa_o[l], w_dil_o[l], w_mix_out[l])
        x2d = _ffn(x2d, ffn2_pre[l][None, :], ffn2_post[l][None, :], ffn2_w_gate[l].astype(BF16),
                   ffn2_w_up[l].astype(BF16), ffn2_w_down[l].astype(BF16))
    return x2d.reshape(B, S, D_MODEL)
```

```python
import functools
import math

import numpy as np
import jax
import jax.numpy as jnp
from jax import lax
from jax.experimental import pallas as pl
from jax.experimental.pallas import tpu as pltpu

F32 = jnp.float32
BF16 = jnp.bfloat16

D_MODEL = 1024
HEAD_DIM = 64
RMS_EPS = 1e-6
NSA_HEADS = 8
NSA_GROUPS = 2
NSA_REP = NSA_HEADS // NSA_GROUPS
CMP_LEN = 32
CMP_STRIDE = 16
CMP_HIDDEN = 256
SLC_LEN = 64
N_SEL = 16
NSA_WINDOW = 512
DIL_CONFIGS = ((128, 1), (512, 4), (2048, 16))
DIL_HPG = 4
DIL_HEADS = DIL_HPG * len(DIL_CONFIGS)
D_FF = 2816
Q_NSA = NSA_HEADS * HEAD_DIM
KV_NSA = NSA_GROUPS * HEAD_DIM
GATE_NSA = NSA_HEADS * 3
QKV_DIL = DIL_HEADS * HEAD_DIM
DIL_OUT = DIL_HPG * HEAD_DIM
DIL_QKV = 3 * DIL_OUT
DIL_TQ = 512
LANES = 128
NEG = -1e30
SCALE = HEAD_DIM ** -0.5
LOG2E = math.log2(math.e)
VMEM_LIMIT = 56 * 1024 * 1024
NSA_TQ = 128
NSA_TILES_PER_STEP = 4
VT_ROWS = 80


def _slopes(n):
    return [float(np.float32(2.0 ** (-8.0 * (i + 1) / n))) for i in range(n)]


NSA_SLOPES = _slopes(NSA_HEADS)
DIL_SLOPES = _slopes(DIL_HEADS)


def _params(sem):
    return pltpu.CompilerParams(dimension_semantics=sem, vmem_limit_bytes=VMEM_LIMIT)


def _rms(x, g):
    ms = jnp.mean(x * x, axis=-1, keepdims=True)
    return x * lax.rsqrt(ms + RMS_EPS) * g


def _sigmoid(x):
    return 1.0 / (1.0 + jnp.exp(-x))


def _dot(a, b):
    return jnp.dot(a, b, preferred_element_type=F32)


def _dot_nt(a, b):
    return lax.dot_general(a, b, (((1,), (1,)), ((), ())), preferred_element_type=F32)


def _split3(a):
    hi = a.astype(BF16)
    r1 = a - hi.astype(F32)
    mid = r1.astype(BF16)
    lo = (r1 - mid.astype(F32)).astype(BF16)
    return hi, mid, lo


def _ffn_kernel(x_ref, pre_ref, post_ref, wg_ref, wu_ref, wd_ref, o_ref):
    x = x_ref[...]
    h = _rms(x, pre_ref[...]).astype(BF16)
    g = _dot(h, wg_ref[...])
    u = _dot(h, wu_ref[...])
    a = (g * _sigmoid(g) * u).astype(BF16)
    o_ref[...] = x + 0.5 * _rms(_dot(a, wd_ref[...]), post_ref[...])


def _resident(a):
    return pl.BlockSpec(a.shape, lambda *_: (0,) * a.ndim, pipeline_mode=pl.Buffered(1))


def _ffn(x2d, pre, post, wg, wu, wd, *, tm=512):
    T = x2d.shape[0]
    return pl.pallas_call(
        _ffn_kernel,
        grid=(T // tm,),
        in_specs=[pl.BlockSpec((tm, D_MODEL), lambda i: (i, 0)), _resident(pre), _resident(post),
                  _resident(wg), _resident(wu), _resident(wd)],
        out_specs=pl.BlockSpec((tm, D_MODEL), lambda i: (i, 0)),
        out_shape=jax.ShapeDtypeStruct((T, D_MODEL), F32),
        compiler_params=_params(("parallel",)),
        name="ffn",
    )(x2d, pre, post, wg, wu, wd)


def _proj_kernel(x_ref, pre_ref, wq_ref, wkv_ref, wd_ref, wgate_ref, q_ref, kv_ref, g_ref, *rest):
    d_refs, scr = rest[:-1], rest[-1]
    tm = x_ref.shape[0]
    h = _rms(x_ref[...], pre_ref[...]).astype(BF16)
    q_ref[...] = _dot(h, wq_ref[...]).astype(BF16)
    kv_ref[...] = _dot(h, wkv_ref[...]).astype(BF16)
    g_ref[...] = _dot(h, wgate_ref[...])
    n_blk = DIL_QKV // LANES
    for gi, (_, dil) in enumerate(DIL_CONFIGS):
        res = _dot(h, wd_ref[:, gi * DIL_QKV:(gi + 1) * DIL_QKV])
        if dil == 1:
            d_refs[gi][0] = res.astype(BF16)
            continue
        for c in range(n_blk):
            scr[c] = res[:, c * LANES:(c + 1) * LANES]
        for r in range(dil):
            for c in range(n_blk):
                rows = scr[c, pl.ds(r, tm // dil, stride=dil), :]
                d_refs[gi][r, :, c * LANES:(c + 1) * LANES] = rows.astype(BF16)


def _proj(x2d, pre, wq, wkv, wd, wgate, *, B, S, tm=512):
    T = x2d.shape[0]
    nt = S // tm
    tok = lambda w: pl.BlockSpec((tm, w), lambda i: (i, 0))
    full = lambda a: pl.BlockSpec(a.shape, lambda i: (0,) * a.ndim)
    d_specs = [pl.BlockSpec((None, dil, tm // dil, DIL_QKV), lambda i: (i // nt, 0, i % nt, 0))
               for _, dil in DIL_CONFIGS]
    d_shapes = [jax.ShapeDtypeStruct((B, dil, S // dil, DIL_QKV), BF16) for _, dil in DIL_CONFIGS]
    return pl.pallas_call(
        _proj_kernel,
        grid=(T // tm,),
        in_specs=[tok(D_MODEL), _resident(pre), _resident(wq), _resident(wkv), _resident(wd), _resident(wgate)],
        out_specs=[tok(wq.shape[1]), tok(wkv.shape[1]), tok(LANES)] + d_specs,
        out_shape=[jax.ShapeDtypeStruct((T, wq.shape[1]), BF16), jax.ShapeDtypeStruct((T, wkv.shape[1]), BF16),
                   jax.ShapeDtypeStruct((T, LANES), F32)] + d_shapes,
        scratch_shapes=[pltpu.VMEM((DIL_QKV // LANES, tm, LANES), F32)],
        compiler_params=_params(("parallel",)),
        name="proj",
    )(x2d, pre, wq, wkv, wd, wgate)


def _gelu_tanh(x):
    c = math.sqrt(2.0 / math.pi)
    return x * (0.5 * (1.0 + jnp.tanh(c * (x + 0.044715 * (x * x * x)))))


def _cmp_mlp_kernel(xk_ref, xv_ref, pe_ref, w1cat_ref, w1_ref, w2_ref, kcols_ref, vtail_ref, kc_ref, vct_ref):
    n_chunks = xk_ref.shape[0]

    def mlp(x_ref, t):
        ab = _dot(x_ref[...], w1cat_ref[t])
        first = ab[:, :CMP_HIDDEN]
        second = pltpu.roll(ab[:, CMP_HIDDEN:], n_chunks - 1, 0)
        pe8 = jnp.broadcast_to(pe_ref[t], (8, pe_ref.shape[2]))
        bias = _dot(pe8, w1_ref[t])[0:1]
        hid = _gelu_tanh(first + second + bias).astype(BF16)
        out = _dot(hid, w2_ref[t])
        row = lax.broadcasted_iota(jnp.int32, out.shape, 0)
        return jnp.where(row < n_chunks - 1, out, 0.0)

    kc_ref[...] = mlp(xk_ref, 0).astype(BF16) + kcols_ref[...]
    vct_ref[...] = mlp(xv_ref, 1).T[0:VT_ROWS].astype(BF16) + vtail_ref[...]


def _cmp_mlp(xk, xv, pe, w1cat, w1, w2, kcols, vtail):
    BG, n_chunks, width = xk.shape
    per_n = lambda shape: pl.BlockSpec((None,) + shape, lambda n: (n, 0, 0))
    full = lambda a: pl.BlockSpec(a.shape, lambda n: (0,) * a.ndim)
    return pl.pallas_call(
        _cmp_mlp_kernel,
        grid=(BG,),
        in_specs=[per_n((n_chunks, width)), per_n((n_chunks, width)), full(pe), full(w1cat), full(w1),
                  full(w2), full(kcols), full(vtail)],
        out_specs=[per_n((n_chunks, LANES)), per_n((VT_ROWS, n_chunks))],
        out_shape=[jax.ShapeDtypeStruct((BG, n_chunks, LANES), BF16),
                   jax.ShapeDtypeStruct((BG, VT_ROWS, n_chunks), BF16)],
        compiler_params=_params(("parallel",)),
        name="cmp_mlp",
    )(xk, xv, pe, w1cat, w1, w2, kcols, vtail)


def _exp2_bf16(x):
    return jnp.exp2(x.astype(BF16))


def _low_half(x2, half):
    lane = lax.broadcasted_iota(jnp.int32, (1, LANES), 1)
    xf = x2.astype(F32)
    if half == 1:
        xf = pltpu.roll(xf, HEAD_DIM, 1)
    return jnp.where(lane < HEAD_DIM, xf, 0.0).astype(BF16)


def _nsa_kernel(q_ref, gate_ref, kc_ref, vct_ref, kslc_ref, vslc_ref, kwin_ref, vwin_ref, kcols_ref,
                vtail_ref, band_ref, cmpmask_ref, ovt_ref, o_ref,
                selbias_scr, m_scr, acc_scr, ks_ref, vst_ref, kw_ref, vwt_ref, need_smem, list_smem,
                *, tq, tk, nt):
    g = pl.program_id(1)
    i = pl.program_id(2)
    n_ck = kc_ref.shape[0]
    n_slc = ovt_ref.shape[0]
    S = kslc_ref.shape[0]

    @pl.when(i == 0)
    def _():
        t_chunk = min(S, 512)
        for src, dst in ((kslc_ref, ks_ref), (kwin_ref, kw_ref)):
            for c in range(S // t_chunk):
                rows_c = slice(c * t_chunk, (c + 1) * t_chunk)
                k2 = src[rows_c, :]
                mine = jnp.where(g == 0, _low_half(k2, 0), _low_half(k2, 1))
                dst[rows_c, :] = mine + kcols_ref[rows_c, :]
        for src, dst in ((vslc_ref, vst_ref), (vwin_ref, vwt_ref)):
            dst[HEAD_DIM:VT_ROWS, :] = vtail_ref[HEAD_DIM:VT_ROWS, :]
            for c in range(S // t_chunk):
                v = src[c * t_chunk:(c + 1) * t_chunk, :].astype(F32)
                v_t = v.T
                mine = jnp.where(g == 0, v_t[0:HEAD_DIM], v_t[HEAD_DIM:2 * HEAD_DIM])
                dst[0:HEAD_DIM, c * t_chunk:(c + 1) * t_chunk] = mine.astype(BF16)

    tiles = range(nt)
    lane = lax.broadcasted_iota(jnp.int32, (1, LANES), 1)
    width = NSA_WINDOW + tq
    blocks_per_chunk = tk // SLC_LEN
    n_w = width // SLC_LEN
    jb = lax.broadcasted_iota(jnp.int32, (n_slc, 1), 0)
    jbf = jb.astype(F32)
    q0s = [(i * nt + t) * tq for t in tiles]
    backs = [jnp.minimum(q0, NSA_WINDOW) for q0 in q0s]
    wss = [pl.multiple_of(q0 - back, tq) for q0, back in zip(q0s, backs)]
    qss = [pl.multiple_of(q0, tq) for q0 in q0s]

    q_augs = []
    for t in tiles:
        slabs = []
        for r in range(NSA_REP):
            sl = jnp.where(g == 0, NSA_SLOPES[r], NSA_SLOPES[NSA_REP + r]) * LOG2E
            meta = jnp.zeros((1, LANES), F32)
            for c, coef in enumerate((SLC_LEN * sl, sl, -sl * q0s[t].astype(F32))):
                terms = _split3(jnp.full((1, LANES), coef, F32))
                for u in range(3):
                    meta = jnp.where(lane == HEAD_DIM + 3 * c + u, terms[u].astype(F32), meta)
            q2 = q_ref[t * tq:(t + 1) * tq, (r // 2) * LANES:(r // 2 + 1) * LANES]
            slabs.append(_low_half(q2, r % 2) + meta.astype(BF16))
        q_augs.append(jnp.concatenate(slabs, axis=0))

    s_alls = [_dot_nt(jnp.concatenate([kc_ref[...], kw_ref[pl.ds(wss[t], width), :],
                                       ks_ref[pl.ds(qss[t], tq), :]], axis=0), q_augs[t]) for t in tiles]

    s_cs = [s_alls[t][0:n_ck] + cmpmask_ref[pl.ds(pl.multiple_of(
        (n_ck - 8) - (tq // CMP_STRIDE) * (i * nt + t), 8), n_ck), :] for t in tiles]
    m_cs = [jnp.max(s, axis=0, keepdims=True) for s in s_cs]
    p_cs = [jnp.exp2(s - m) for s, m in zip(s_cs, m_cs)]
    den_cs = [jnp.sum(p, axis=0, keepdims=True) for p in p_cs]
    p_cs = [p * jnp.where(m > 0.5 * NEG, 1.0 / den, 0.0) for p, m, den in zip(p_cs, m_cs, den_cs)]
    o_cmps = [_dot(vct_ref[0:HEAD_DIM, :], p.astype(BF16)) for p in p_cs]

    ovt = ovt_ref[...]
    imps = []
    for p in p_cs:
        psum = p[:, 0:tq]
        for r in range(1, NSA_REP):
            psum = psum + p[:, r * tq:(r + 1) * tq]
        hi, mid, _ = _split3(psum)
        imps.append(_dot(ovt, hi) + _dot(ovt, mid))
    vals, sels = [], []
    for t in tiles:
        pos = q0s[t] + lax.broadcasted_iota(jnp.int32, (1, tq), 1)
        own = pos // SLC_LEN
        forced = (jb == 0) | (jb == own) | (jb == own - 1)
        valid = jb * SLC_LEN <= pos
        vals.append(jnp.where(valid & jnp.logical_not(forced), imps[t], -jnp.inf))
        sels.append(jnp.where(forced, 1.0, 0.0))

    s_ws = [s_alls[t][n_ck:n_ck + width]
            + band_ref[pl.ds(pl.multiple_of(NSA_WINDOW - backs[t], tq), width), :] for t in tiles]
    m_ws = [jnp.max(s, axis=0, keepdims=True) for s in s_ws]
    s_ds = [s_alls[t][n_ck + width:n_ck + width + tq] + band_ref[NSA_WINDOW:NSA_WINDOW + tq, :] for t in tiles]
    m_ds = [jnp.max(s, axis=0, keepdims=True) for s in s_ds]
    slices = [([(s_ws[t], m_ws[t], r0) for r0 in range(0, width, SLC_LEN)]
               + [(s_ds[t], m_ds[t], r0) for r0 in range(0, tq, SLC_LEN)]) for t in tiles]
    probs = [[] for _ in tiles]

    for it in range(max(N_SEL - 3, len(slices[0]))):
        for t in tiles:
            if it < N_SEL - 3:
                mx = jnp.max(vals[t], axis=0, keepdims=True)
                cand = jnp.where(vals[t] == mx, jbf, 1e9)
                idx = jnp.min(cand, axis=0, keepdims=True)
                pick = (jbf == idx) & (mx > -jnp.inf)
                sels[t] = jnp.where(pick, 1.0, sels[t])
                vals[t] = jnp.where(pick, -jnp.inf, vals[t])
            if it < len(slices[t]):
                src, m_src, r0 = slices[t][it]
                probs[t].append(_exp2_bf16(src[r0:r0 + SLC_LEN] - m_src))
    o_wins = []
    for t in tiles:
        acc_w = _dot(vwt_ref[:, pl.ds(wss[t], width)], jnp.concatenate(probs[t][:n_w], axis=0))
        o_wins.append(acc_w[0:HEAD_DIM] * (1.0 / acc_w[HEAD_DIM:HEAD_DIM + 1]))
        m_scr[t] = m_ds[t]
        acc_scr[t] = _dot(vst_ref[:, pl.ds(qss[t], tq)], jnp.concatenate(probs[t][n_w:], axis=0))

    n_needs = []
    for t in tiles:
        sbias = jnp.where((sels[t] > 0.5) & (jb < q0s[t] // SLC_LEN), 0.0, NEG)
        selbias_scr[t] = jnp.concatenate([sbias] * NSA_REP, axis=1)
        for c in range(n_slc // blocks_per_chunk):
            chunk_max = jnp.max(sbias[c * blocks_per_chunk:(c + 1) * blocks_per_chunk])
            need_smem[t, c] = (chunk_max > 0.5 * NEG).astype(jnp.int32)
    for t in tiles:
        n_need = jnp.int32(0)
        for k in range(n_slc // blocks_per_chunk + 1):
            list_smem[t, k] = jnp.int32(0)
        for c in range(n_slc // blocks_per_chunk):
            list_smem[t, n_need] = jnp.int32(c)
            n_need = n_need + need_smem[t, c]
        n_needs.append(n_need)

    def entry(t, k):
        live = k < n_needs[t]
        c = jnp.where(live, list_smem[t, k], 0)
        return c, jnp.where(live, 0.0, NEG)

    def masked(s_c, t, c, pad):
        parts = []
        for jj in range(blocks_per_chunk):
            brow = selbias_scr[t, pl.ds(c * blocks_per_chunk + jj, 1), :] + pad
            parts.append(s_c[jj * SLC_LEN:(jj + 1) * SLC_LEN] + brow)
        return jnp.concatenate(parts, axis=0)

    for ta in range(0, nt, 2):
        group = list(range(ta, min(ta + 2, nt)))

        def joint_body(j, carry, group=group):
            ents = [(entry(t, 2 * j), entry(t, 2 * j + 1)) for t in group]
            offs = [(pl.multiple_of(ea[0] * tk, tk), pl.multiple_of(eb[0] * tk, tk)) for ea, eb in ents]
            scores = [_dot_nt(jnp.concatenate([ks_ref[pl.ds(ka, tk), :], ks_ref[pl.ds(kb, tk), :]], axis=0),
                              q_augs[t]) for t, (ka, kb) in zip(group, offs)]
            scores = [jnp.concatenate([masked(s[0:tk], t, ea[0], ea[1]), masked(s[tk:2 * tk], t, eb[0], eb[1])],
                                      axis=0) for s, t, (ea, eb) in zip(scores, group, ents)]
            m_old = [m_scr[t] for t in group]
            m_new = [jnp.maximum(m, jnp.max(s, axis=0, keepdims=True)) for m, s in zip(m_old, scores)]
            probs = [_exp2_bf16(s - m) for s, m in zip(scores, m_new)]
            for t, (ka, kb), p, mo, mn in zip(group, offs, probs, m_old, m_new):
                v_ab = jnp.concatenate([vst_ref[:, pl.ds(ka, tk)], vst_ref[:, pl.ds(kb, tk)]], axis=1)
                acc_scr[t] = jnp.exp2(mo - mn) * acc_scr[t] + _dot(v_ab, p)
                m_scr[t] = mn
            return carry

        trips = functools.reduce(jnp.maximum, [(n_needs[t] + 1) // 2 for t in group])
        lax.fori_loop(0, trips, joint_body, 0)

    for t in tiles:
        acc = acc_scr[t]
        o_slc = acc[0:HEAD_DIM] * (1.0 / acc[HEAD_DIM:HEAD_DIM + 1])
        gate_t = _sigmoid(gate_ref[t * tq:(t + 1) * tq, :].T)
        outs = []
        for r in range(NSA_REP):
            cols = slice(r * tq, (r + 1) * tq)

            def gate_row(branch):
                lo_row = 3 * r + branch
                hi_row = 3 * (NSA_REP + r) + branch
                return jnp.where(g == 0, gate_t[lo_row:lo_row + 1], gate_t[hi_row:hi_row + 1])

            outs.append(gate_row(0) * o_cmps[t][:, cols] + gate_row(1) * o_slc[:, cols]
                        + gate_row(2) * o_wins[t][:, cols])
        o_ref[t * tq:(t + 1) * tq, :] = jnp.concatenate(outs, axis=0).T.astype(BF16)


def _nsa(q_pad, gates, kc, vct, kv_pad, kcols, vtail, band, cmpmask, ovt, *, B, S, tq=NSA_TQ, tk=512,
         nt=NSA_TILES_PER_STEP):
    G = NSA_GROUPS
    n_ck = kc.shape[1]
    n_slc = ovt.shape[0]
    rows = NSA_REP * tq
    nq = S // (tq * nt)
    n_chunks = n_slc * SLC_LEN // tk
    per_bg = lambda shape: pl.BlockSpec((None,) + shape, lambda b, g, i: (b * G + g, 0, 0))
    full = lambda a: pl.BlockSpec(a.shape, lambda b, g, i: (0,) * a.ndim)
    kv_kind = lambda kind: pl.BlockSpec((S, LANES), lambda b, g, i: (b, kind))
    return pl.pallas_call(
        functools.partial(_nsa_kernel, tq=tq, tk=tk, nt=nt),
        grid=(B, G, nq),
        in_specs=[
            pl.BlockSpec((nt * tq, NSA_REP * HEAD_DIM), lambda b, g, i: (b * nq + i, g)),
            pl.BlockSpec((nt * tq, LANES), lambda b, g, i: (b * nq + i, 0)),
            per_bg((n_ck, LANES)), per_bg((VT_ROWS, n_ck)),
            kv_kind(2), kv_kind(3), kv_kind(4), kv_kind(5),
            full(kcols), full(vtail), full(band), full(cmpmask), full(ovt),
        ],
        out_specs=pl.BlockSpec((nt * tq, NSA_REP * HEAD_DIM), lambda b, g, i: (b * nq + i, g)),
        out_shape=jax.ShapeDtypeStruct((B * S, Q_NSA), BF16),
        scratch_shapes=[pltpu.VMEM((nt, n_slc, rows), F32), pltpu.VMEM((nt, 1, rows), F32),
                        pltpu.VMEM((nt, VT_ROWS, rows), F32),
                        pltpu.VMEM((S, LANES), BF16), pltpu.VMEM((VT_ROWS, S), BF16),
                        pltpu.VMEM((S, LANES), BF16), pltpu.VMEM((VT_ROWS, S), BF16),
                        pltpu.SMEM((nt, n_chunks), jnp.int32), pltpu.SMEM((nt, n_chunks + 1), jnp.int32)],
        compiler_params=_params(("arbitrary", "arbitrary", "arbitrary")),
        name="nsa",
    )(q_pad, gates, kc, vct, kv_pad, kv_pad, kv_pad, kv_pad, kcols, vtail, band, cmpmask, ovt)


def _dil_attn_kernel(q_ref, k_ref, v_ref, dist_ref, o_ref, lse_ref, vt_scr, *, tq, width, max_back, slopes):
    i = pl.program_id(1)
    nb, L, _ = k_ref.shape
    t_chunk = min(L, 512)

    @pl.when(i == 0)
    def _():
        tail_row = lax.broadcasted_iota(jnp.int32, (VT_ROWS - HEAD_DIM, L), 0)
        tail = jnp.where(tail_row == 0, 1.0, 0.0).astype(BF16)
        for b in range(nb):
            for h in range(DIL_HPG):
                vt_scr[b, h * VT_ROWS + HEAD_DIM:(h + 1) * VT_ROWS, :] = tail
            for pair in range(DIL_OUT // LANES):
                for c in range(L // t_chunk):
                    cols = slice(c * t_chunk, (c + 1) * t_chunk)
                    v_t = v_ref[b, cols, pair * LANES:(pair + 1) * LANES].astype(F32).T.astype(BF16)
                    for half in range(2):
                        h = 2 * pair + half
                        vt_scr[b, h * VT_ROWS:h * VT_ROWS + HEAD_DIM, cols] = v_t[half * HEAD_DIM:(half + 1) * HEAD_DIM]

    back = jnp.minimum(i * tq, max_back)
    ks = pl.multiple_of(i * tq - back, max_back)
    dist = dist_ref[pl.ds(pl.multiple_of(max_back - back, max_back), width), :]
    lane = lax.broadcasted_iota(jnp.int32, (1, LANES), 1)
    chains = [(b, h) for b in range(nb) for h in range(DIL_HPG)]
    scores = []
    for b, h in chains:
        pair, half = divmod(h, 2)
        in_head = (lane >= half * HEAD_DIM) & (lane < (half + 1) * HEAD_DIM)
        q = q_ref[b, :, pair * LANES:(pair + 1) * LANES]
        q = jnp.where(in_head, q, jnp.zeros_like(q))
        k = k_ref[b, pl.ds(ks, width), pair * LANES:(pair + 1) * LANES]
        scores.append(_dot_nt(k, q) - (slopes[h] * LOG2E) * dist)
    maxes = [jnp.max(s_t, axis=0, keepdims=True) for s_t in scores]
    probs = [_exp2_bf16(s_t - m) for s_t, m in zip(scores, maxes)]
    accs = [_dot(vt_scr[b, h * VT_ROWS:(h + 1) * VT_ROWS, pl.ds(ks, width)], p) for (b, h), p in zip(chains, probs)]
    dens = [acc[HEAD_DIM:HEAD_DIM + 1] for acc in accs]
    outs = [acc[0:HEAD_DIM] * (1.0 / den) for acc, den in zip(accs, dens)]
    lses = [jnp.broadcast_to((m + jnp.log2(den)) * (1.0 / LOG2E), (HEAD_DIM, tq)) for m, den in zip(maxes, dens)]
    for b in range(nb):
        o_ref[b] = jnp.concatenate(outs[b * DIL_HPG:(b + 1) * DIL_HPG], axis=0).T
        lse_ref[b] = jnp.concatenate(lses[b * DIL_HPG:(b + 1) * DIL_HPG], axis=0).T


def _dil_attn(qkv, dist_tbl, *, max_back, slopes, tq, nb):
    N, L, _ = qkv.shape
    width = dist_tbl.shape[0] - max_back
    kern = functools.partial(_dil_attn_kernel, tq=tq, width=width, max_back=max_back, slopes=slopes)
    return pl.pallas_call(
        kern,
        grid=(N // nb, L // tq),
        in_specs=[
            pl.BlockSpec((nb, tq, DIL_OUT), lambda n, i: (n, i, 0)),
            pl.BlockSpec((nb, L, DIL_OUT), lambda n, i: (n, 0, 1)),
            pl.BlockSpec((nb, L, DIL_OUT), lambda n, i: (n, 0, 2)),
            pl.BlockSpec(dist_tbl.shape, lambda n, i: (0, 0)),
        ],
        out_specs=[
            pl.BlockSpec((nb, tq, DIL_OUT), lambda n, i: (n, i, 0)),
            pl.BlockSpec((nb, tq, DIL_OUT), lambda n, i: (n, i, 0)),
        ],
        out_shape=[
            jax.ShapeDtypeStruct((N, L, DIL_OUT), F32),
            jax.ShapeDtypeStruct((N, L, DIL_OUT), F32),
        ],
        scratch_shapes=[pltpu.VMEM((nb, DIL_HPG * VT_ROWS, L), BF16)],
        compiler_params=_params(("parallel", "arbitrary")),
        name="dil_attn",
    )(qkv, qkv, qkv, dist_tbl)


def _final_kernel(x_ref, pre_ref, post_ref, onsa_ref, od0_ref, od1_ref, od2_ref, l0_ref, l1_ref, l2_ref,
                  wmerge_ref, wnsa_ref, wdil_ref, wout_ref, o_ref, scr):
    tm = x_ref.shape[0]
    x = x_ref[...]
    h = _rms(x, pre_ref[...]).astype(BF16)
    merge = _dot(h, wmerge_ref[...])
    g_a = _sigmoid(merge[:, :D_MODEL])
    g_b = _sigmoid(merge[:, D_MODEL:])
    y_nsa = _dot(onsa_ref[...], wnsa_ref[...])

    n_blk = DIL_OUT // LANES
    slot = 0
    token_major = []
    for (_, dil), o_in, l_in in zip(DIL_CONFIGS, (od0_ref, od1_ref, od2_ref), (l0_ref, l1_ref, l2_ref)):
        pair = []
        for src in (o_in, l_in):
            if dil == 1:
                pair.append(src[0])
                continue
            for r in range(dil):
                for c in range(n_blk):
                    scr[slot + c, pl.ds(r, tm // dil, stride=dil), :] = src[r, :, c * LANES:(c + 1) * LANES]
            pair.append(jnp.concatenate([scr[slot + c] for c in range(n_blk)], axis=1))
            slot += n_blk
        token_major.append(pair)
    (o0, l0), (o1, l1), (o2, l2) = token_major
    mx = jnp.maximum(jnp.maximum(l0, l1), l2)
    e0, e1, e2 = jnp.exp(l0 - mx), jnp.exp(l1 - mx), jnp.exp(l2 - mx)
    den = e0 + e1 + e2
    o_dil = (e0 / den) * o0 + (e1 / den) * o1 + (e2 / den) * o2
    y_dil = _dot(o_dil.astype(BF16), wdil_ref[...])

    y = g_a * y_nsa + g_b * y_dil
    z = _dot(y.astype(BF16), wout_ref[...])
    o_ref[...] = x + _rms(z, post_ref[...])


def _final(x2d, pre, post, onsa, od, ld, wmerge, wnsa, wdil, wout, *, S, tm=512):
    T = x2d.shape[0]
    nt = S // tm
    tok = lambda w: pl.BlockSpec((tm, w), lambda i: (i, 0))
    full = lambda a: pl.BlockSpec(a.shape, lambda i: (0,) * a.ndim)
    strided = [pl.BlockSpec((None, dil, tm // dil, DIL_OUT), lambda i: (i // nt, 0, i % nt, 0))
               for _, dil in DIL_CONFIGS]
    n_strided = sum(1 for _, dil in DIL_CONFIGS if dil > 1)
    return pl.pallas_call(
        _final_kernel,
        grid=(T // tm,),
        in_specs=[tok(D_MODEL), _resident(pre), _resident(post), tok(Q_NSA)] + strided + strided
        + [_resident(wmerge), _resident(wnsa), _resident(wdil), _resident(wout)],
        out_specs=tok(D_MODEL),
        out_shape=jax.ShapeDtypeStruct((T, D_MODEL), F32),
        scratch_shapes=[pltpu.VMEM((2 * n_strided * (DIL_OUT // LANES), tm, LANES), F32)],
        compiler_params=_params(("parallel",)),
        name="final",
    )(x2d, pre, post, onsa, od[0], od[1], od[2], ld[0], ld[1], ld[2], wmerge, wnsa, wdil, wout)


def _overlap_t(n_ck, n_slc):
    c0 = np.arange(n_ck)[None, :] * CMP_STRIDE
    s0 = np.arange(n_slc)[:, None] * SLC_LEN
    ov = np.clip(np.minimum(c0 + CMP_LEN, s0 + SLC_LEN) - np.maximum(c0, s0), 0, None) / CMP_LEN
    ov[:, n_ck - 1] = 0.0
    return ov.astype(np.float32)


def _pos_columns(pos):
    cols = np.zeros((len(pos), LANES), np.float32)
    for t, vals in enumerate((pos // SLC_LEN, pos % SLC_LEN, np.ones(len(pos)))):
        for u in range(3):
            cols[:, HEAD_DIM + 3 * t + u] = vals
    return cols


def _ones_row_tail(n):
    tail = np.zeros((VT_ROWS, n), np.float32)
    tail[HEAD_DIM] = 1.0
    return tail


def _band_table(tq):
    u = np.arange(2 * NSA_WINDOW + tq)[:, None]
    r = (np.arange(NSA_REP * tq) % tq)[None, :]
    dist = NSA_WINDOW + r - u
    return np.where((dist >= 0) & (dist <= NSA_WINDOW - 1), 0.0, NEG).astype(np.float32)


def _dil_dist_table(tq, width, max_back, dil):
    u = np.arange(width + max_back)[:, None]
    r = np.arange(tq)[None, :]
    dist = max_back + r - u
    return np.where((dist >= 0) & (dist <= max_back), dist * dil, -NEG).astype(np.float32)


def _cmp_mask_table(n_ck, tq):
    u = np.arange(2 * n_ck - 8)[:, None]
    r = (np.arange(NSA_REP * tq) % tq)[None, :]
    end = (u - (n_ck - 8)) * CMP_STRIDE + CMP_LEN - 1
    return np.where(end <= r, 0.0, NEG).astype(np.float32)


def _mixer(x2d, B, S, mix_pre, mix_post, w_in, pe_k, w_ck1, w_ck2, pe_v, w_cv1, w_cv2,
           w_nsa_o, w_dil_o, w_mix_out):
    G = NSA_GROUPS
    BG = B * G
    assert S % (16 * 128) == 0 and NSA_TQ // CMP_STRIDE == 8
    o_q, o_kv, o_g, o_d, o_m = np.cumsum([0, Q_NSA, 6 * KV_NSA, GATE_NSA, 3 * QKV_DIL]).tolist()
    wq = w_in[:, o_q:o_kv] * (SCALE * LOG2E)
    w_gate = jnp.pad(w_in[:, o_g:o_d], ((0, 0), (0, LANES - GATE_NSA)))
    wd = w_in[:, o_d:o_m].reshape(D_MODEL, 3, len(DIL_CONFIGS), DIL_OUT)
    wd = (wd * jnp.asarray([SCALE * LOG2E, 1.0, 1.0], F32)[None, :, None, None]).transpose(0, 2, 1, 3)
    wd = wd.reshape(D_MODEL, 3 * QKV_DIL)
    q_pad, kv_pad, gates, *qkv_d = _proj(x2d, mix_pre[None, :], wq.astype(BF16), w_in[:, o_kv:o_g].astype(BF16),
                                         wd.astype(BF16), w_gate.astype(BF16), B=B, S=S)

    n_ck = S // CMP_STRIDE
    half = CMP_STRIDE * HEAD_DIM
    def chunks(kind):
        per_g = [kv_pad[:, (kind * G + g) * HEAD_DIM:(kind * G + g + 1) * HEAD_DIM].reshape(B, 1, n_ck, half)
                 for g in range(G)]
        return jnp.concatenate(per_g, axis=1).reshape(BG, n_ck, half)
    w1 = jnp.stack([w_ck1, w_cv1])
    w1cat = jnp.concatenate([w1[:, :half], w1[:, half:]], axis=2).astype(BF16)
    pe = jnp.stack([pe_k.reshape(1, -1), pe_v.reshape(1, -1)]).astype(BF16)
    w2 = jnp.pad(jnp.stack([w_ck2, w_cv2]), ((0, 0), (0, 0), (0, LANES - HEAD_DIM))).astype(BF16)
    cmp_pos = np.arange(n_ck) * CMP_STRIDE + CMP_LEN - 1
    kc, vct = _cmp_mlp(chunks(0), chunks(1), pe, w1cat, w1.astype(BF16), w2,
                       jnp.asarray(_pos_columns(cmp_pos), BF16), jnp.asarray(_ones_row_tail(n_ck), BF16))

    band = jnp.asarray(_band_table(NSA_TQ))
    cmpmask = jnp.asarray(_cmp_mask_table(n_ck, NSA_TQ))
    ovt = jnp.asarray(_overlap_t(n_ck, S // SLC_LEN), BF16)
    o_nsa = _nsa(q_pad, gates, kc, vct, kv_pad, jnp.asarray(_pos_columns(np.arange(S)), BF16),
                 jnp.asarray(_ones_row_tail(S), BF16), band, cmpmask, ovt, B=B, S=S)

    od, ld = [], []
    for gi, (window, dil) in enumerate(DIL_CONFIGS):
        Ls = S // dil
        max_back = window // dil
        tq = min(DIL_TQ, Ls)
        width = tq + max_back if Ls > tq else tq
        o, lse = _dil_attn(qkv_d[gi].reshape(B * dil, Ls, DIL_QKV),
                           jnp.asarray(_dil_dist_table(tq, width, max_back, dil)),
                           max_back=max_back, slopes=DIL_SLOPES[gi * DIL_HPG:(gi + 1) * DIL_HPG], tq=tq,
                           nb=DIL_TQ // tq)
        od.append(o.reshape(B, dil, Ls, DIL_OUT))
        ld.append(lse.reshape(B, dil, Ls, DIL_OUT))

    return _final(x2d, mix_pre[None, :], mix_post[None, :], o_nsa, od, ld, w_in[:, o_m:].astype(BF16),
                  w_nsa_o.astype(BF16), w_dil_o.astype(BF16), w_mix_out.astype(BF16), S=S)


def kernel(x, ffn1_pre, ffn1_post, ffn1_w_gate, ffn1_w_up, ffn1_w_down, mix_pre, mix_post, w_in, nsa_pe_k, nsa_w_ck1, nsa_w_ck2, nsa_pe_v, nsa_w_cv1, nsa_w_cv2, w_nsa_o, w_dil_o, w_mix_out, ffn2_pre, ffn2_post, ffn2_w_gate, ffn2_w_up, ffn2_w_down):
    B, S, _ = x.shape
    x2d = x.reshape(B * S, D_MODEL)
    for l in range(ffn1_pre.shape[0]):
        x2d = _ffn(x2d, ffn1_pre[l][None, :], ffn1_post[l][None, :], ffn1_w_gate[l].astype(BF16),
                   ffn1_w_up[l].astype(BF16), ffn1_w_down[l].astype(BF16))
        x2d = _mixer(x2d, B, S, mix_pre[l], mix_post[l], w_in[l], nsa_pe_k[l], nsa_w_ck1[l], nsa_w_ck2[l],
                     nsa_pe_v[l], nsa_w_cv1[l], nsa_w_cv2[l], w_nsa_o[l], w_dil_o[l], w_mix_out[l])
        x2d = _ffn(x2d, ffn2_pre[l][None, :], ffn2_post[l][None, :], ffn2_w_gate[l].astype(BF16),
                   ffn2_w_up[l].astype(BF16), ffn2_w_down[l].astype(BF16))
    return x2d.reshape(B, S, D_MODEL)
```

```python
import functools
import math

import numpy as np
import jax
import jax.numpy as jnp
from jax import lax
from jax.experimental import pallas as pl
from jax.experimental.pallas import tpu as pltpu

F32 = jnp.float32
BF16 = jnp.bfloat16

D_MODEL = 1024
HEAD_DIM = 64
RMS_EPS = 1e-6
NSA_HEADS = 8
NSA_GROUPS = 2
NSA_REP = NSA_HEADS // NSA_GROUPS
CMP_LEN = 32
CMP_STRIDE = 16
CMP_HIDDEN = 256
SLC_LEN = 64
N_SEL = 16
NSA_WINDOW = 512
DIL_CONFIGS = ((128, 1), (512, 4), (2048, 16))
DIL_HPG = 4
DIL_HEADS = DIL_HPG * len(DIL_CONFIGS)
D_FF = 2816
Q_NSA = NSA_HEADS * HEAD_DIM
KV_NSA = NSA_GROUPS * HEAD_DIM
GATE_NSA = NSA_HEADS * 3
QKV_DIL = DIL_HEADS * HEAD_DIM
DIL_OUT = DIL_HPG * HEAD_DIM
DIL_QKV = 3 * DIL_OUT
DIL_TQ = 512
LANES = 128
NEG = -1e30
SCALE = HEAD_DIM ** -0.5
LOG2E = math.log2(math.e)
VMEM_LIMIT = 56 * 1024 * 1024
NSA_TQ = 128
NSA_TILES_PER_STEP = 4
VT_ROWS = 80


def _slopes(n):
    return [float(np.float32(2.0 ** (-8.0 * (i + 1) / n))) for i in range(n)]


NSA_SLOPES = _slopes(NSA_HEADS)
DIL_SLOPES = _slopes(DIL_HEADS)


def _params(sem):
    return pltpu.CompilerParams(dimension_semantics=sem, vmem_limit_bytes=VMEM_LIMIT)


def _rms(x, g):
    ms = jnp.mean(x * x, axis=-1, keepdims=True)
    return x * lax.rsqrt(ms + RMS_EPS) * g


def _sigmoid(x):
    return 1.0 / (1.0 + jnp.exp(-x))


def _dot(a, b):
    return jnp.dot(a, b, preferred_element_type=F32)


def _dot_nt(a, b):
    return lax.dot_general(a, b, (((1,), (1,)), ((), ())), preferred_element_type=F32)


def _split3(a):
    hi = a.astype(BF16)
    r1 = a - hi.astype(F32)
    mid = r1.astype(BF16)
    lo = (r1 - mid.astype(F32)).astype(BF16)
    return hi, mid, lo


def _ffn_kernel(x_ref, pre_ref, post_ref, wg_ref, wu_ref, wd_ref, o_ref):
    x = x_ref[...]
    h = _rms(x, pre_ref[...]).astype(BF16)
    g = _dot(h, wg_ref[...])
    u = _dot(h, wu_ref[...])
    a = (g * _sigmoid(g) * u).astype(BF16)
    o_ref[...] = x + 0.5 * _rms(_dot(a, wd_ref[...]), post_ref[...])


def _resident(a):
    return pl.BlockSpec(a.shape, lambda *_: (0,) * a.ndim, pipeline_mode=pl.Buffered(1))


def _ffn(x2d, pre, post, wg, wu, wd, *, tm=512):
    T = x2d.shape[0]
    return pl.pallas_call(
        _ffn_kernel,
        grid=(T // tm,),
        in_specs=[pl.BlockSpec((tm, D_MODEL), lambda i: (i, 0)), _resident(pre), _resident(post),
                  _resident(wg), _resident(wu), _resident(wd)],
        out_specs=pl.BlockSpec((tm, D_MODEL), lambda i: (i, 0)),
        out_shape=jax.ShapeDtypeStruct((T, D_MODEL), F32),
        compiler_params=_params(("parallel",)),
        name="ffn",
    )(x2d, pre, post, wg, wu, wd)


def _proj_kernel(x_ref, pre_ref, wq_ref, wkv_ref, wd_ref, wgate_ref, q_ref, kv_ref, g_ref, *rest):
    d_refs, scr = rest[:-1], rest[-1]
    tm = x_ref.shape[0]
    h = _rms(x_ref[...], pre_ref[...]).astype(BF16)
    q_ref[...] = _dot(h, wq_ref[...]).astype(BF16)
    kv_ref[...] = _dot(h, wkv_ref[...]).astype(BF16)
    g_ref[...] = _dot(h, wgate_ref[...])
    n_blk = DIL_QKV // LANES
    for gi, (_, dil) in enumerate(DIL_CONFIGS):
        parts = [_dot(h, wd_ref[:, t * QKV_DIL + gi * DIL_OUT:t * QKV_DIL + (gi + 1) * DIL_OUT]) for t in range(3)]
        res = jnp.concatenate([parts[0] * (SCALE * LOG2E), parts[1], parts[2]], axis=1)
        if dil == 1:
            d_refs[gi][0] = res.astype(BF16)
            continue
        for c in range(n_blk):
            scr[c] = res[:, c * LANES:(c + 1) * LANES]
        for r in range(dil):
            for c in range(n_blk):
                rows = scr[c, pl.ds(r, tm // dil, stride=dil), :]
                d_refs[gi][r, :, c * LANES:(c + 1) * LANES] = rows.astype(BF16)


def _proj(x2d, pre, wq, wkv, wd, wgate, *, B, S, tm=512):
    T = x2d.shape[0]
    nt = S // tm
    tok = lambda w: pl.BlockSpec((tm, w), lambda i: (i, 0))
    full = lambda a: pl.BlockSpec(a.shape, lambda i: (0,) * a.ndim)
    d_specs = [pl.BlockSpec((None, dil, tm // dil, DIL_QKV), lambda i: (i // nt, 0, i % nt, 0))
               for _, dil in DIL_CONFIGS]
    d_shapes = [jax.ShapeDtypeStruct((B, dil, S // dil, DIL_QKV), BF16) for _, dil in DIL_CONFIGS]
    return pl.pallas_call(
        _proj_kernel,
        grid=(T // tm,),
        in_specs=[tok(D_MODEL), _resident(pre), _resident(wq), _resident(wkv), _resident(wd), _resident(wgate)],
        out_specs=[tok(wq.shape[1]), tok(wkv.shape[1]), tok(LANES)] + d_specs,
        out_shape=[jax.ShapeDtypeStruct((T, wq.shape[1]), BF16), jax.ShapeDtypeStruct((T, wkv.shape[1]), BF16),
                   jax.ShapeDtypeStruct((T, LANES), F32)] + d_shapes,
        scratch_shapes=[pltpu.VMEM((DIL_QKV // LANES, tm, LANES), F32)],
        compiler_params=_params(("parallel",)),
        name="proj",
    )(x2d, pre, wq, wkv, wd, wgate)


def _gelu_tanh(x):
    c = math.sqrt(2.0 / math.pi)
    return x * (0.5 * (1.0 + jnp.tanh(c * (x + 0.044715 * (x * x * x)))))


def _cmp_mlp_kernel(xk_ref, xv_ref, pe_ref, w1cat_ref, w1_ref, w2_ref, kcols_ref, vtail_ref, kc_ref, vct_ref):
    n_chunks = xk_ref.shape[0]

    def mlp(x_ref, t):
        ab = _dot(x_ref[...], w1cat_ref[t])
        first = ab[:, :CMP_HIDDEN]
        second = pltpu.roll(ab[:, CMP_HIDDEN:], n_chunks - 1, 0)
        pe8 = jnp.broadcast_to(pe_ref[t], (8, pe_ref.shape[2]))
        bias = _dot(pe8, w1_ref[t])[0:1]
        hid = _gelu_tanh(first + second + bias).astype(BF16)
        out = _dot(hid, w2_ref[t])
        row = lax.broadcasted_iota(jnp.int32, out.shape, 0)
        return jnp.where(row < n_chunks - 1, out, 0.0)

    kc_ref[...] = mlp(xk_ref, 0).astype(BF16) + kcols_ref[...]
    vct_ref[...] = mlp(xv_ref, 1).T[0:VT_ROWS].astype(BF16) + vtail_ref[...]


def _cmp_mlp(xk, xv, pe, w1cat, w1, w2, kcols, vtail):
    BG, n_chunks, width = xk.shape
    per_n = lambda shape: pl.BlockSpec((None,) + shape, lambda n: (n, 0, 0))
    full = lambda a: pl.BlockSpec(a.shape, lambda n: (0,) * a.ndim)
    return pl.pallas_call(
        _cmp_mlp_kernel,
        grid=(BG,),
        in_specs=[per_n((n_chunks, width)), per_n((n_chunks, width)), full(pe), full(w1cat), full(w1),
                  full(w2), full(kcols), full(vtail)],
        out_specs=[per_n((n_chunks, LANES)), per_n((VT_ROWS, n_chunks))],
        out_shape=[jax.ShapeDtypeStruct((BG, n_chunks, LANES), BF16),
                   jax.ShapeDtypeStruct((BG, VT_ROWS, n_chunks), BF16)],
        compiler_params=_params(("parallel",)),
        name="cmp_mlp",
    )(xk, xv, pe, w1cat, w1, w2, kcols, vtail)


def _exp2_bf16(x):
    return jnp.exp2(x.astype(BF16))


def _low_half(x2, half):
    lane = lax.broadcasted_iota(jnp.int32, (1, LANES), 1)
    xf = x2.astype(F32)
    if half == 1:
        xf = pltpu.roll(xf, HEAD_DIM, 1)
    return jnp.where(lane < HEAD_DIM, xf, 0.0).astype(BF16)


def _nsa_kernel(q_ref, gate_ref, kc_ref, vct_ref, kslc_ref, vslc_ref, kwin_ref, vwin_ref, kcols_ref,
                vtail_ref, band_ref, cmpmask_ref, ovt_ref, o_ref,
                selbias_scr, m_scr, acc_scr, ks_ref, vst_ref, kw_ref, vwt_ref, need_smem, list_smem,
                *, tq, tk, nt):
    g = pl.program_id(1)
    i = pl.program_id(2)
    n_ck = kc_ref.shape[0]
    n_slc = ovt_ref.shape[0]
    S = kslc_ref.shape[0]

    @pl.when(i == 0)
    def _():
        t_chunk = min(S, 512)
        for src, dst in ((kslc_ref, ks_ref), (kwin_ref, kw_ref)):
            for c in range(S // t_chunk):
                rows_c = slice(c * t_chunk, (c + 1) * t_chunk)
                k2 = src[rows_c, :]
                mine = jnp.where(g == 0, _low_half(k2, 0), _low_half(k2, 1))
                dst[rows_c, :] = mine + kcols_ref[rows_c, :]
        for src, dst in ((vslc_ref, vst_ref), (vwin_ref, vwt_ref)):
            dst[HEAD_DIM:VT_ROWS, :] = vtail_ref[HEAD_DIM:VT_ROWS, :]
            for c in range(S // t_chunk):
                v = src[c * t_chunk:(c + 1) * t_chunk, :].astype(F32)
                v_t = v.T
                mine = jnp.where(g == 0, v_t[0:HEAD_DIM], v_t[HEAD_DIM:2 * HEAD_DIM])
                dst[0:HEAD_DIM, c * t_chunk:(c + 1) * t_chunk] = mine.astype(BF16)

    tiles = range(nt)
    lane = lax.broadcasted_iota(jnp.int32, (1, LANES), 1)
    width = NSA_WINDOW + tq
    blocks_per_chunk = tk // SLC_LEN
    n_w = width // SLC_LEN
    jb = lax.broadcasted_iota(jnp.int32, (n_slc, 1), 0)
    jbf = jb.astype(F32)
    q0s = [(i * nt + t) * tq for t in tiles]
    backs = [jnp.minimum(q0, NSA_WINDOW) for q0 in q0s]
    wss = [pl.multiple_of(q0 - back, tq) for q0, back in zip(q0s, backs)]
    qss = [pl.multiple_of(q0, tq) for q0 in q0s]

    q_augs = []
    for t in tiles:
        slabs = []
        for r in range(NSA_REP):
            sl = jnp.where(g == 0, NSA_SLOPES[r], NSA_SLOPES[NSA_REP + r]) * LOG2E
            meta = jnp.zeros((1, LANES), F32)
            for c, coef in enumerate((SLC_LEN * sl, sl, -sl * q0s[t].astype(F32))):
                terms = _split3(jnp.full((1, LANES), coef, F32))
                for u in range(3):
                    meta = jnp.where(lane == HEAD_DIM + 3 * c + u, terms[u].astype(F32), meta)
            q2 = q_ref[t * tq:(t + 1) * tq, (r // 2) * LANES:(r // 2 + 1) * LANES]
            slabs.append(_low_half(q2, r % 2) + meta.astype(BF16))
        q_augs.append(jnp.concatenate(slabs, axis=0))

    s_alls = [_dot_nt(jnp.concatenate([kc_ref[...], kw_ref[pl.ds(wss[t], width), :],
                                       ks_ref[pl.ds(qss[t], tq), :]], axis=0), q_augs[t]) for t in tiles]

    s_cs = [s_alls[t][0:n_ck] + cmpmask_ref[pl.ds(pl.multiple_of(
        (n_ck - 8) - (tq // CMP_STRIDE) * (i * nt + t), 8), n_ck), :] for t in tiles]
    m_cs = [jnp.max(s, axis=0, keepdims=True) for s in s_cs]
    p_cs = [jnp.exp2(s - m) for s, m in zip(s_cs, m_cs)]
    den_cs = [jnp.sum(p, axis=0, keepdims=True) for p in p_cs]
    p_cs = [p * jnp.where(m > 0.5 * NEG, 1.0 / den, 0.0) for p, m, den in zip(p_cs, m_cs, den_cs)]
    o_cmps = [_dot(vct_ref[0:HEAD_DIM, :], p.astype(BF16)) for p in p_cs]

    ovt = ovt_ref[...]
    imps = []
    for p in p_cs:
        psum = p[:, 0:tq]
        for r in range(1, NSA_REP):
            psum = psum + p[:, r * tq:(r + 1) * tq]
        hi, mid, _ = _split3(psum)
        imps.append(_dot(ovt, hi) + _dot(ovt, mid))
    vals, sels = [], []
    for t in tiles:
        pos = q0s[t] + lax.broadcasted_iota(jnp.int32, (1, tq), 1)
        own = pos // SLC_LEN
        forced = (jb == 0) | (jb == own) | (jb == own - 1)
        valid = jb * SLC_LEN <= pos
        vals.append(jnp.where(valid & jnp.logical_not(forced), imps[t], -jnp.inf))
        sels.append(jnp.where(forced, 1.0, 0.0))

    s_ws = [s_alls[t][n_ck:n_ck + width]
            + band_ref[pl.ds(pl.multiple_of(NSA_WINDOW - backs[t], tq), width), :] for t in tiles]
    m_ws = [jnp.max(s, axis=0, keepdims=True) for s in s_ws]
    s_ds = [s_alls[t][n_ck + width:n_ck + width + tq] + band_ref[NSA_WINDOW:NSA_WINDOW + tq, :] for t in tiles]
    m_ds = [jnp.max(s, axis=0, keepdims=True) for s in s_ds]
    slices = [([(s_ws[t], m_ws[t], r0) for r0 in range(0, width, SLC_LEN)]
               + [(s_ds[t], m_ds[t], r0) for r0 in range(0, tq, SLC_LEN)]) for t in tiles]
    probs = [[] for _ in tiles]

    for it in range(max(N_SEL - 3, len(slices[0]))):
        for t in tiles:
            if it < N_SEL - 3:
                mx = jnp.max(vals[t], axis=0, keepdims=True)
                cand = jnp.where(vals[t] == mx, jbf, 1e9)
                idx = jnp.min(cand, axis=0, keepdims=True)
                pick = (jbf == idx) & (mx > -jnp.inf)
                sels[t] = jnp.where(pick, 1.0, sels[t])
                vals[t] = jnp.where(pick, -jnp.inf, vals[t])
            if it < len(slices[t]):
                src, m_src, r0 = slices[t][it]
                probs[t].append(_exp2_bf16(src[r0:r0 + SLC_LEN] - m_src))
    o_wins = []
    for t in tiles:
        acc_w = _dot(vwt_ref[:, pl.ds(wss[t], width)], jnp.concatenate(probs[t][:n_w], axis=0))
        o_wins.append(acc_w[0:HEAD_DIM] * (1.0 / acc_w[HEAD_DIM:HEAD_DIM + 1]))
        m_scr[t] = m_ds[t]
        acc_scr[t] = _dot(vst_ref[:, pl.ds(qss[t], tq)], jnp.concatenate(probs[t][n_w:], axis=0))

    n_needs = []
    for t in tiles:
        sbias = jnp.where((sels[t] > 0.5) & (jb < q0s[t] // SLC_LEN), 0.0, NEG)
        selbias_scr[t] = jnp.concatenate([sbias] * NSA_REP, axis=1)
        for c in range(n_slc // blocks_per_chunk):
            chunk_max = jnp.max(sbias[c * blocks_per_chunk:(c + 1) * blocks_per_chunk])
            need_smem[t, c] = (chunk_max > 0.5 * NEG).astype(jnp.int32)
    for t in tiles:
        n_need = jnp.int32(0)
        for k in range(n_slc // blocks_per_chunk + 1):
            list_smem[t, k] = jnp.int32(0)
        for c in range(n_slc // blocks_per_chunk):
            list_smem[t, n_need] = jnp.int32(c)
            n_need = n_need + need_smem[t, c]
        n_needs.append(n_need)

    def entry(t, k):
        live = k < n_needs[t]
        c = jnp.where(live, list_smem[t, k], 0)
        return c, jnp.where(live, 0.0, NEG)

    def masked(s_c, t, c, pad):
        parts = []
        for jj in range(blocks_per_chunk):
            brow = selbias_scr[t, pl.ds(c * blocks_per_chunk + jj, 1), :] + pad
            parts.append(s_c[jj * SLC_LEN:(jj + 1) * SLC_LEN] + brow)
        return jnp.concatenate(parts, axis=0)

    for ta in range(0, nt, 4):
        group = list(range(ta, min(ta + 4, nt)))

        def joint_body(j, carry, group=group):
            ents = [(entry(t, 2 * j), entry(t, 2 * j + 1)) for t in group]
            offs = [(pl.multiple_of(ea[0] * tk, tk), pl.multiple_of(eb[0] * tk, tk)) for ea, eb in ents]
            scores = [_dot_nt(jnp.concatenate([ks_ref[pl.ds(ka, tk), :], ks_ref[pl.ds(kb, tk), :]], axis=0),
                              q_augs[t]) for t, (ka, kb) in zip(group, offs)]
            scores = [jnp.concatenate([masked(s[0:tk], t, ea[0], ea[1]), masked(s[tk:2 * tk], t, eb[0], eb[1])],
                                      axis=0) for s, t, (ea, eb) in zip(scores, group, ents)]
            m_old = [m_scr[t] for t in group]
            m_new = [jnp.maximum(m, jnp.max(s, axis=0, keepdims=True)) for m, s in zip(m_old, scores)]
            probs = [_exp2_bf16(s - m) for s, m in zip(scores, m_new)]
            for t, (ka, kb), p, mo, mn in zip(group, offs, probs, m_old, m_new):
                v_ab = jnp.concatenate([vst_ref[:, pl.ds(ka, tk)], vst_ref[:, pl.ds(kb, tk)]], axis=1)
                acc_scr[t] = jnp.exp2(mo - mn) * acc_scr[t] + _dot(v_ab, p)
                m_scr[t] = mn
            return carry

        trips = functools.reduce(jnp.maximum, [(n_needs[t] + 1) // 2 for t in group])
        lax.fori_loop(0, trips, joint_body, 0)

    for t in tiles:
        acc = acc_scr[t]
        o_slc = acc[0:HEAD_DIM] * (1.0 / acc[HEAD_DIM:HEAD_DIM + 1])
        gate_t = _sigmoid(gate_ref[t * tq:(t + 1) * tq, :].T)
        outs = []
        for r in range(NSA_REP):
            cols = slice(r * tq, (r + 1) * tq)

            def gate_row(branch):
                lo_row = 3 * r + branch
                hi_row = 3 * (NSA_REP + r) + branch
                return jnp.where(g == 0, gate_t[lo_row:lo_row + 1], gate_t[hi_row:hi_row + 1])

            outs.append(gate_row(0) * o_cmps[t][:, cols] + gate_row(1) * o_slc[:, cols]
                        + gate_row(2) * o_wins[t][:, cols])
        o_ref[t * tq:(t + 1) * tq, :] = jnp.concatenate(outs, axis=0).T.astype(BF16)


def _nsa(q_pad, gates, kc, vct, kv_pad, kcols, vtail, band, cmpmask, ovt, *, B, S, tq=NSA_TQ, tk=512,
         nt=NSA_TILES_PER_STEP):
    G = NSA_GROUPS
    n_ck = kc.shape[1]
    n_slc = ovt.shape[0]
    rows = NSA_REP * tq
    nq = S // (tq * nt)
    n_chunks = n_slc * SLC_LEN // tk
    per_bg = lambda shape: pl.BlockSpec((None,) + shape, lambda b, g, i: (b * G + g, 0, 0))
    full = lambda a: pl.BlockSpec(a.shape, lambda b, g, i: (0,) * a.ndim)
    kv_kind = lambda kind: pl.BlockSpec((S, LANES), lambda b, g, i: (b, kind))
    return pl.pallas_call(
        functools.partial(_nsa_kernel, tq=tq, tk=tk, nt=nt),
        grid=(B, G, nq),
        in_specs=[
            pl.BlockSpec((nt * tq, NSA_REP * HEAD_DIM), lambda b, g, i: (b * nq + i, g)),
            pl.BlockSpec((nt * tq, LANES), lambda b, g, i: (b * nq + i, 0)),
            per_bg((n_ck, LANES)), per_bg((VT_ROWS, n_ck)),
            kv_kind(2), kv_kind(3), kv_kind(4), kv_kind(5),
            full(kcols), full(vtail), full(band), full(cmpmask), full(ovt),
        ],
        out_specs=pl.BlockSpec((nt * tq, NSA_REP * HEAD_DIM), lambda b, g, i: (b * nq + i, g)),
        out_shape=jax.ShapeDtypeStruct((B * S, Q_NSA), BF16),
        scratch_shapes=[pltpu.VMEM((nt, n_slc, rows), F32), pltpu.VMEM((nt, 1, rows), F32),
                        pltpu.VMEM((nt, VT_ROWS, rows), F32),
                        pltpu.VMEM((S, LANES), BF16), pltpu.VMEM((VT_ROWS, S), BF16),
                        pltpu.VMEM((S, LANES), BF16), pltpu.VMEM((VT_ROWS, S), BF16),
                        pltpu.SMEM((nt, n_chunks), jnp.int32), pltpu.SMEM((nt, n_chunks + 1), jnp.int32)],
        compiler_params=_params(("arbitrary", "arbitrary", "arbitrary")),
        name="nsa",
    )(q_pad, gates, kc, vct, kv_pad, kv_pad, kv_pad, kv_pad, kcols, vtail, band, cmpmask, ovt)


def _dil_attn_kernel(q_ref, k_ref, v_ref, dist_ref, o_ref, lse_ref, vt_scr, *, tq, width, max_back, slopes):
    i = pl.program_id(1)
    nb, L, _ = k_ref.shape
    t_chunk = min(L, 512)

    @pl.when(i == 0)
    def _():
        tail_row = lax.broadcasted_iota(jnp.int32, (VT_ROWS - HEAD_DIM, L), 0)
        tail = jnp.where(tail_row == 0, 1.0, 0.0).astype(BF16)
        for b in range(nb):
            for h in range(DIL_HPG):
                vt_scr[b, h * VT_ROWS + HEAD_DIM:(h + 1) * VT_ROWS, :] = tail
            for pair in range(DIL_OUT // LANES):
                for c in range(L // t_chunk):
                    cols = slice(c * t_chunk, (c + 1) * t_chunk)
                    v_t = v_ref[b, cols, pair * LANES:(pair + 1) * LANES].astype(F32).T.astype(BF16)
                    for half in range(2):
                        h = 2 * pair + half
                        vt_scr[b, h * VT_ROWS:h * VT_ROWS + HEAD_DIM, cols] = v_t[half * HEAD_DIM:(half + 1) * HEAD_DIM]

    back = jnp.minimum(i * tq, max_back)
    ks = pl.multiple_of(i * tq - back, max_back)
    dist = dist_ref[pl.ds(pl.multiple_of(max_back - back, max_back), width), :]
    lane = lax.broadcasted_iota(jnp.int32, (1, LANES), 1)
    chains = [(b, h) for b in range(nb) for h in range(DIL_HPG)]
    scores = []
    for b, h in chains:
        pair, half = divmod(h, 2)
        in_head = (lane >= half * HEAD_DIM) & (lane < (half + 1) * HEAD_DIM)
        q = q_ref[b, :, pair * LANES:(pair + 1) * LANES]
        q = jnp.where(in_head, q, jnp.zeros_like(q))
        k = k_ref[b, pl.ds(ks, width), pair * LANES:(pair + 1) * LANES]
        scores.append(_dot_nt(k, q) - (slopes[h] * LOG2E) * dist)
    maxes = [jnp.max(s_t, axis=0, keepdims=True) for s_t in scores]
    probs = [_exp2_bf16(s_t - m) for s_t, m in zip(scores, maxes)]
    accs = [_dot(vt_scr[b, h * VT_ROWS:(h + 1) * VT_ROWS, pl.ds(ks, width)], p) for (b, h), p in zip(chains, probs)]
    dens = [acc[HEAD_DIM:HEAD_DIM + 1] for acc in accs]
    outs = [acc[0:HEAD_DIM] * (1.0 / den) for acc, den in zip(accs, dens)]
    lses = [jnp.broadcast_to((m + jnp.log2(den)) * (1.0 / LOG2E), (HEAD_DIM, tq)) for m, den in zip(maxes, dens)]
    for b in range(nb):
        o_ref[b] = jnp.concatenate(outs[b * DIL_HPG:(b + 1) * DIL_HPG], axis=0).T
        lse_ref[b] = jnp.concatenate(lses[b * DIL_HPG:(b + 1) * DIL_HPG], axis=0).T


def _dil_attn(qkv, dist_tbl, *, max_back, slopes, tq, nb):
    N, L, _ = qkv.shape
    width = dist_tbl.shape[0] - max_back
    kern = functools.partial(_dil_attn_kernel, tq=tq, width=width, max_back=max_back, slopes=slopes)
    return pl.pallas_call(
        kern,
        grid=(N // nb, L // tq),
        in_specs=[
            pl.BlockSpec((nb, tq, DIL_OUT), lambda n, i: (n, i, 0)),
            pl.BlockSpec((nb, L, DIL_OUT), lambda n, i: (n, 0, 1)),
            pl.BlockSpec((nb, L, DIL_OUT), lambda n, i: (n, 0, 2)),
            pl.BlockSpec(dist_tbl.shape, lambda n, i: (0, 0)),
        ],
        out_specs=[
            pl.BlockSpec((nb, tq, DIL_OUT), lambda n, i: (n, i, 0)),
            pl.BlockSpec((nb, tq, DIL_OUT), lambda n, i: (n, i, 0)),
        ],
        out_shape=[
            jax.ShapeDtypeStruct((N, L, DIL_OUT), F32),
            jax.ShapeDtypeStruct((N, L, DIL_OUT), F32),
        ],
        scratch_shapes=[pltpu.VMEM((nb, DIL_HPG * VT_ROWS, L), BF16)],
        compiler_params=_params(("parallel", "arbitrary")),
        name="dil_attn",
    )(qkv, qkv, qkv, dist_tbl)


def _final_kernel(x_ref, pre_ref, post_ref, onsa_ref, od0_ref, od1_ref, od2_ref, l0_ref, l1_ref, l2_ref,
                  wmerge_ref, wnsa_ref, wdil_ref, wout_ref, o_ref, scr):
    tm = x_ref.shape[0]
    x = x_ref[...]
    h = _rms(x, pre_ref[...]).astype(BF16)
    merge = _dot(h, wmerge_ref[...])
    g_a = _sigmoid(merge[:, :D_MODEL])
    g_b = _sigmoid(merge[:, D_MODEL:])
    y_nsa = _dot(onsa_ref[...], wnsa_ref[...])

    n_blk = DIL_OUT // LANES
    slot = 0
    token_major = []
    for (_, dil), o_in, l_in in zip(DIL_CONFIGS, (od0_ref, od1_ref, od2_ref), (l0_ref, l1_ref, l2_ref)):
        pair = []
        for src in (o_in, l_in):
            if dil == 1:
                pair.append(src[0])
                continue
            for r in range(dil):
                for c in range(n_blk):
                    scr[slot + c, pl.ds(r, tm // dil, stride=dil), :] = src[r, :, c * LANES:(c + 1) * LANES]
            pair.append(jnp.concatenate([scr[slot + c] for c in range(n_blk)], axis=1))
            slot += n_blk
        token_major.append(pair)
    (o0, l0), (o1, l1), (o2, l2) = token_major
    mx = jnp.maximum(jnp.maximum(l0, l1), l2)
    e0, e1, e2 = jnp.exp(l0 - mx), jnp.exp(l1 - mx), jnp.exp(l2 - mx)
    den = e0 + e1 + e2
    o_dil = (e0 / den) * o0 + (e1 / den) * o1 + (e2 / den) * o2
    y_dil = _dot(o_dil.astype(BF16), wdil_ref[...])

    y = g_a * y_nsa + g_b * y_dil
    z = _dot(y.astype(BF16), wout_ref[...])
    o_ref[...] = x + _rms(z, post_ref[...])


def _final(x2d, pre, post, onsa, od, ld, wmerge, wnsa, wdil, wout, *, S, tm=512):
    T = x2d.shape[0]
    nt = S // tm
    tok = lambda w: pl.BlockSpec((tm, w), lambda i: (i, 0))
    full = lambda a: pl.BlockSpec(a.shape, lambda i: (0,) * a.ndim)
    strided = [pl.BlockSpec((None, dil, tm // dil, DIL_OUT), lambda i: (i // nt, 0, i % nt, 0))
               for _, dil in DIL_CONFIGS]
    n_strided = sum(1 for _, dil in DIL_CONFIGS if dil > 1)
    return pl.pallas_call(
        _final_kernel,
        grid=(T // tm,),
        in_specs=[tok(D_MODEL), _resident(pre), _resident(post), tok(Q_NSA)] + strided + strided
        + [_resident(wmerge), _resident(wnsa), _resident(wdil), _resident(wout)],
        out_specs=tok(D_MODEL),
        out_shape=jax.ShapeDtypeStruct((T, D_MODEL), F32),
        scratch_shapes=[pltpu.VMEM((2 * n_strided * (DIL_OUT // LANES), tm, LANES), F32)],
        compiler_params=_params(("parallel",)),
        name="final",
    )(x2d, pre, post, onsa, od[0], od[1], od[2], ld[0], ld[1], ld[2], wmerge, wnsa, wdil, wout)


def _overlap_t(n_ck, n_slc):
    c0 = np.arange(n_ck)[None, :] * CMP_STRIDE
    s0 = np.arange(n_slc)[:, None] * SLC_LEN
    ov = np.clip(np.minimum(c0 + CMP_LEN, s0 + SLC_LEN) - np.maximum(c0, s0), 0, None) / CMP_LEN
    ov[:, n_ck - 1] = 0.0
    return ov.astype(np.float32)


def _pos_columns(pos):
    cols = np.zeros((len(pos), LANES), np.float32)
    for t, vals in enumerate((pos // SLC_LEN, pos % SLC_LEN, np.ones(len(pos)))):
        for u in range(3):
            cols[:, HEAD_DIM + 3 * t + u] = vals
    return cols


def _ones_row_tail(n):
    tail = np.zeros((VT_ROWS, n), np.float32)
    tail[HEAD_DIM] = 1.0
    return tail


def _band_table(tq):
    u = np.arange(2 * NSA_WINDOW + tq)[:, None]
    r = (np.arange(NSA_REP * tq) % tq)[None, :]
    dist = NSA_WINDOW + r - u
    return np.where((dist >= 0) & (dist <= NSA_WINDOW - 1), 0.0, NEG).astype(np.float32)


def _dil_dist_table(tq, width, max_back, dil):
    u = np.arange(width + max_back)[:, None]
    r = np.arange(tq)[None, :]
    dist = max_back + r - u
    return np.where((dist >= 0) & (dist <= max_back), dist * dil, -NEG).astype(np.float32)


def _cmp_mask_table(n_ck, tq):
    u = np.arange(2 * n_ck - 8)[:, None]
    r = (np.arange(NSA_REP * tq) % tq)[None, :]
    end = (u - (n_ck - 8)) * CMP_STRIDE + CMP_LEN - 1
    return np.where(end <= r, 0.0, NEG).astype(np.float32)


def _mixer(x2d, B, S, mix_pre, mix_post, w_in, pe_k, w_ck1, w_ck2, pe_v, w_cv1, w_cv2,
           w_nsa_o, w_dil_o, w_mix_out):
    G = NSA_GROUPS
    BG = B * G
    assert S % (16 * 128) == 0 and NSA_TQ // CMP_STRIDE == 8
    o_q, o_kv, o_g, o_d, o_m = np.cumsum([0, Q_NSA, 6 * KV_NSA, GATE_NSA, 3 * QKV_DIL]).tolist()
    wq = w_in[:, o_q:o_kv] * (SCALE * LOG2E)
    w_gate = jnp.pad(w_in[:, o_g:o_d], ((0, 0), (0, LANES - GATE_NSA)))
    wd = w_in[:, o_d:o_m]
    q_pad, kv_pad, gates, *qkv_d = _proj(x2d, mix_pre[None, :], wq.astype(BF16), w_in[:, o_kv:o_g].astype(BF16),
                                         wd.astype(BF16), w_gate.astype(BF16), B=B, S=S)

    n_ck = S // CMP_STRIDE
    half = CMP_STRIDE * HEAD_DIM
    def chunks(kind):
        per_g = [kv_pad[:, (kind * G + g) * HEAD_DIM:(kind * G + g + 1) * HEAD_DIM].reshape(B, 1, n_ck, half)
                 for g in range(G)]
        return jnp.concatenate(per_g, axis=1).reshape(BG, n_ck, half)
    w1 = jnp.stack([w_ck1, w_cv1])
    w1cat = jnp.concatenate([w1[:, :half], w1[:, half:]], axis=2).astype(BF16)
    pe = jnp.stack([pe_k.reshape(1, -1), pe_v.reshape(1, -1)]).astype(BF16)
    w2 = jnp.pad(jnp.stack([w_ck2, w_cv2]), ((0, 0), (0, 0), (0, LANES - HEAD_DIM))).astype(BF16)
    cmp_pos = np.arange(n_ck) * CMP_STRIDE + CMP_LEN - 1
    kc, vct = _cmp_mlp(chunks(0), chunks(1), pe, w1cat, w1.astype(BF16), w2,
                       jnp.asarray(_pos_columns(cmp_pos), BF16), jnp.asarray(_ones_row_tail(n_ck), BF16))

    band = jnp.asarray(_band_table(NSA_TQ))
    cmpmask = jnp.asarray(_cmp_mask_table(n_ck, NSA_TQ))
    ovt = jnp.asarray(_overlap_t(n_ck, S // SLC_LEN), BF16)
    o_nsa = _nsa(q_pad, gates, kc, vct, kv_pad, jnp.asarray(_pos_columns(np.arange(S)), BF16),
                 jnp.asarray(_ones_row_tail(S), BF16), band, cmpmask, ovt, B=B, S=S)

    od, ld = [], []
    for gi, (window, dil) in enumerate(DIL_CONFIGS):
        Ls = S // dil
        max_back = window // dil
        tq = min(DIL_TQ, Ls)
        width = tq + max_back if Ls > tq else tq
        o, lse = _dil_attn(qkv_d[gi].reshape(B * dil, Ls, DIL_QKV),
                           jnp.asarray(_dil_dist_table(tq, width, max_back, dil)),
                           max_back=max_back, slopes=DIL_SLOPES[gi * DIL_HPG:(gi + 1) * DIL_HPG], tq=tq,
                           nb=DIL_TQ // tq)
        od.append(o.reshape(B, dil, Ls, DIL_OUT))
        ld.append(lse.reshape(B, dil, Ls, DIL_OUT))

    return _final(x2d, mix_pre[None, :], mix_post[None, :], o_nsa, od, ld, w_in[:, o_m:].astype(BF16),
                  w_nsa_o.astype(BF16), w_dil_o.astype(BF16), w_mix_out.astype(BF16), S=S)


def kernel(x, ffn1_pre, ffn1_post, ffn1_w_gate, ffn1_w_up, ffn1_w_down, mix_pre, mix_post, w_in, nsa_pe_k, nsa_w_ck1, nsa_w_ck2, nsa_pe_v, nsa_w_cv1, nsa_w_cv2, w_nsa_o, w_dil_o, w_mix_out, ffn2_pre, ffn2_post, ffn2_w_gate, ffn2_w_up, ffn2_w_down):
    B, S, _ = x.shape
    x2d = x.reshape(B * S, D_MODEL)
    for l in range(ffn1_pre.shape[0]):
        x2d = _ffn(x2d, ffn1_pre[l][None, :], ffn1_post[l][None, :], ffn1_w_gate[l].astype(BF16),
                   ffn1_w_up[l].astype(BF16), ffn1_w_down[l].astype(BF16))
        x2d = _mixer(x2d, B, S, mix_pre[l], mix_post[l], w_in[l], nsa_pe_k[l], nsa_w_ck1[l], nsa_w_ck2[l],
                     nsa_pe_v[l], nsa_w_cv1[l], nsa_w_cv2[l], w_nsa_o[l], w_dil_o[l], w_mix_out[l])
        x2d = _ffn(x2d, ffn2_pre[l][None, :], ffn2_post[l][None, :], ffn2_w_gate[l].astype(BF16),
                   ffn2_w_up[l].astype(BF16), ffn2_w_down[l].astype(BF16))
    return x2d.reshape(B, S, D_MODEL)
```

```python
import functools
import math

import numpy as np
import jax
import jax.numpy as jnp
from jax import lax
from jax.experimental import pallas as pl
from jax.experimental.pallas import tpu as pltpu

F32 = jnp.float32
BF16 = jnp.bfloat16

D_MODEL = 1024
HEAD_DIM = 64
RMS_EPS = 1e-6
NSA_HEADS = 8
NSA_GROUPS = 2
NSA_REP = NSA_HEADS // NSA_GROUPS
CMP_LEN = 32
CMP_STRIDE = 16
CMP_HIDDEN = 256
SLC_LEN = 64
N_SEL = 16
NSA_WINDOW = 512
DIL_CONFIGS = ((128, 1), (512, 4), (2048, 16))
DIL_HPG = 4
DIL_HEADS = DIL_HPG * len(DIL_CONFIGS)
D_FF = 2816
Q_NSA = NSA_HEADS * HEAD_DIM
KV_NSA = NSA_GROUPS * HEAD_DIM
GATE_NSA = NSA_HEADS * 3
QKV_DIL = DIL_HEADS * HEAD_DIM
DIL_OUT = DIL_HPG * HEAD_DIM
DIL_QKV = 3 * DIL_OUT
DIL_TQ = 512
LANES = 128
NEG = -1e30
SCALE = HEAD_DIM ** -0.5
LOG2E = math.log2(math.e)
VMEM_LIMIT = 56 * 1024 * 1024
NSA_TQ = 128
NSA_TILES_PER_STEP = 4
VT_ROWS = 80


def _slopes(n):
    return [float(np.float32(2.0 ** (-8.0 * (i + 1) / n))) for i in range(n)]


NSA_SLOPES = _slopes(NSA_HEADS)
DIL_SLOPES = _slopes(DIL_HEADS)


def _params(sem):
    return pltpu.CompilerParams(dimension_semantics=sem, vmem_limit_bytes=VMEM_LIMIT)


def _rms(x, g):
    ms = jnp.mean(x * x, axis=-1, keepdims=True)
    return x * lax.rsqrt(ms + RMS_EPS) * g


def _sigmoid(x):
    return 1.0 / (1.0 + jnp.exp(-x))


def _dot(a, b):
    return jnp.dot(a, b, preferred_element_type=F32)


def _dot_nt(a, b):
    return lax.dot_general(a, b, (((1,), (1,)), ((), ())), preferred_element_type=F32)


def _split3(a):
    hi = a.astype(BF16)
    r1 = a - hi.astype(F32)
    mid = r1.astype(BF16)
    lo = (r1 - mid.astype(F32)).astype(BF16)
    return hi, mid, lo


def _ffn_kernel(x_ref, pre_ref, post_ref, wg_ref, wu_ref, wd_ref, o_ref):
    x = x_ref[...]
    h = _rms(x, pre_ref[...]).astype(BF16)
    g = _dot(h, wg_ref[...])
    u = _dot(h, wu_ref[...])
    a = (g * _sigmoid(g) * u).astype(BF16)
    o_ref[...] = x + 0.5 * _rms(_dot(a, wd_ref[...]), post_ref[...])


def _resident(a):
    return pl.BlockSpec(a.shape, lambda *_: (0,) * a.ndim, pipeline_mode=pl.Buffered(1))


def _ffn(x2d, pre, post, wg, wu, wd, *, tm=512):
    T = x2d.shape[0]
    return pl.pallas_call(
        _ffn_kernel,
        grid=(T // tm,),
        in_specs=[pl.BlockSpec((tm, D_MODEL), lambda i: (i, 0)), _resident(pre), _resident(post),
                  _resident(wg), _resident(wu), _resident(wd)],
        out_specs=pl.BlockSpec((tm, D_MODEL), lambda i: (i, 0)),
        out_shape=jax.ShapeDtypeStruct((T, D_MODEL), F32),
        compiler_params=_params(("parallel",)),
        name="ffn",
    )(x2d, pre, post, wg, wu, wd)


def _proj_kernel(x_ref, pre_ref, wq_ref, wkv_ref, wd_ref, wgate_ref, q_ref, kv_ref, g_ref, xk_ref, xv_ref,
                 *rest):
    d_refs, scr = rest[:-1], rest[-1]
    tm = x_ref.shape[0]
    h = _rms(x_ref[...], pre_ref[...]).astype(BF16)
    q_ref[...] = _dot(h, wq_ref[...]).astype(BF16)
    kv = _dot(h, wkv_ref[...])
    kv_ref[...] = kv.astype(BF16)
    g_ref[...] = _dot(h, wgate_ref[...])
    lane = lax.broadcasted_iota(jnp.int32, (1, LANES), 1)
    n_rows = tm // CMP_STRIDE
    for kind, out in ((0, xk_ref), (1, xv_ref)):
        scr[kind] = kv[:, kind * LANES:(kind + 1) * LANES]
        for j in range(CMP_STRIDE // 2):
            even = scr[kind, pl.ds(2 * j, n_rows, stride=CMP_STRIDE), :]
            odd = scr[kind, pl.ds(2 * j + 1, n_rows, stride=CMP_STRIDE), :]
            out[0, :, j * LANES:(j + 1) * LANES] = jnp.where(
                lane < HEAD_DIM, even, pltpu.roll(odd, HEAD_DIM, 1)).astype(BF16)
            out[1, :, j * LANES:(j + 1) * LANES] = jnp.where(
                lane < HEAD_DIM, pltpu.roll(even, HEAD_DIM, 1), odd).astype(BF16)
    n_blk = DIL_QKV // LANES
    for gi, (_, dil) in enumerate(DIL_CONFIGS):
        parts = [_dot(h, wd_ref[:, t * QKV_DIL + gi * DIL_OUT:t * QKV_DIL + (gi + 1) * DIL_OUT]) for t in range(3)]
        res = jnp.concatenate([parts[0] * (SCALE * LOG2E), parts[1], parts[2]], axis=1)
        if dil == 1:
            d_refs[gi][0] = res.astype(BF16)
            continue
        for c in range(n_blk):
            scr[c] = res[:, c * LANES:(c + 1) * LANES]
        for r in range(dil):
            for c in range(n_blk):
                rows = scr[c, pl.ds(r, tm // dil, stride=dil), :]
                d_refs[gi][r, :, c * LANES:(c + 1) * LANES] = rows.astype(BF16)


def _proj(x2d, pre, wq, wkv, wd, wgate, *, B, S, tm=512):
    T = x2d.shape[0]
    nt = S // tm
    tok = lambda w: pl.BlockSpec((tm, w), lambda i: (i, 0))
    full = lambda a: pl.BlockSpec(a.shape, lambda i: (0,) * a.ndim)
    d_specs = [pl.BlockSpec((None, dil, tm // dil, DIL_QKV), lambda i: (i // nt, 0, i % nt, 0))
               for _, dil in DIL_CONFIGS]
    d_shapes = [jax.ShapeDtypeStruct((B, dil, S // dil, DIL_QKV), BF16) for _, dil in DIL_CONFIGS]
    chunk_width = CMP_STRIDE * HEAD_DIM
    c_spec = pl.BlockSpec((None, NSA_GROUPS, tm // CMP_STRIDE, chunk_width), lambda i: (i // nt, 0, i % nt, 0))
    c_shape = jax.ShapeDtypeStruct((B, NSA_GROUPS, S // CMP_STRIDE, chunk_width), BF16)
    return pl.pallas_call(
        _proj_kernel,
        grid=(T // tm,),
        in_specs=[tok(D_MODEL), _resident(pre), _resident(wq), _resident(wkv), _resident(wd), _resident(wgate)],
        out_specs=[tok(wq.shape[1]), tok(wkv.shape[1]), tok(LANES), c_spec, c_spec] + d_specs,
        out_shape=[jax.ShapeDtypeStruct((T, wq.shape[1]), BF16), jax.ShapeDtypeStruct((T, wkv.shape[1]), BF16),
                   jax.ShapeDtypeStruct((T, LANES), F32), c_shape, c_shape] + d_shapes,
        scratch_shapes=[pltpu.VMEM((DIL_QKV // LANES, tm, LANES), F32)],
        compiler_params=_params(("parallel",)),
        name="proj",
    )(x2d, pre, wq, wkv, wd, wgate)


def _gelu_tanh(x):
    c = math.sqrt(2.0 / math.pi)
    return x * (0.5 * (1.0 + jnp.tanh(c * (x + 0.044715 * (x * x * x)))))


def _cmp_mlp_kernel(xk_ref, xv_ref, pe_ref, w1cat_ref, w1_ref, w2_ref, kcols_ref, vtail_ref, kc_ref, vct_ref):
    n_chunks = xk_ref.shape[0]

    def mlp(x_ref, t):
        ab = _dot(x_ref[...], w1cat_ref[t])
        first = ab[:, :CMP_HIDDEN]
        second = pltpu.roll(ab[:, CMP_HIDDEN:], n_chunks - 1, 0)
        pe8 = jnp.broadcast_to(pe_ref[t], (8, pe_ref.shape[2]))
        bias = _dot(pe8, w1_ref[t])[0:1]
        hid = _gelu_tanh(first + second + bias).astype(BF16)
        out = _dot(hid, w2_ref[t])
        row = lax.broadcasted_iota(jnp.int32, out.shape, 0)
        return jnp.where(row < n_chunks - 1, out, 0.0)

    kc_ref[...] = mlp(xk_ref, 0).astype(BF16) + kcols_ref[...]
    vct_ref[...] = mlp(xv_ref, 1).T[0:VT_ROWS].astype(BF16) + vtail_ref[...]


def _cmp_mlp(xk, xv, pe, w1cat, w1, w2, kcols, vtail):
    BG, n_chunks, width = xk.shape
    per_n = lambda shape: pl.BlockSpec((None,) + shape, lambda n: (n, 0, 0))
    full = lambda a: pl.BlockSpec(a.shape, lambda n: (0,) * a.ndim)
    return pl.pallas_call(
        _cmp_mlp_kernel,
        grid=(BG,),
        in_specs=[per_n((n_chunks, width)), per_n((n_chunks, width)), full(pe), full(w1cat), full(w1),
                  full(w2), full(kcols), full(vtail)],
        out_specs=[per_n((n_chunks, LANES)), per_n((VT_ROWS, n_chunks))],
        out_shape=[jax.ShapeDtypeStruct((BG, n_chunks, LANES), BF16),
                   jax.ShapeDtypeStruct((BG, VT_ROWS, n_chunks), BF16)],
        compiler_params=_params(("parallel",)),
        name="cmp_mlp",
    )(xk, xv, pe, w1cat, w1, w2, kcols, vtail)


def _exp2_bf16(x):
    return jnp.exp2(x.astype(BF16))


def _low_half(x2, half):
    lane = lax.broadcasted_iota(jnp.int32, (1, LANES), 1)
    xf = x2.astype(F32)
    if half == 1:
        xf = pltpu.roll(xf, HEAD_DIM, 1)
    return jnp.where(lane < HEAD_DIM, xf, 0.0).astype(BF16)


def _nsa_kernel(q_ref, gate_ref, kc_ref, vct_ref, kslc_ref, vslc_ref, kwin_ref, vwin_ref, kcols_ref,
                vtail_ref, band_ref, cmpmask_ref, ovt_ref, o_ref,
                selbias_scr, m_scr, acc_scr, ks_ref, vst_ref, kw_ref, vwt_ref, need_smem, list_smem,
                *, tq, tk, nt):
    g = pl.program_id(1)
    i = pl.program_id(2)
    n_ck = kc_ref.shape[0]
    n_slc = ovt_ref.shape[0]
    S = kslc_ref.shape[0]

    @pl.when(i == 0)
    def _():
        t_chunk = min(S, 512)
        for src, dst in ((kslc_ref, ks_ref), (kwin_ref, kw_ref)):
            for c in range(S // t_chunk):
                rows_c = slice(c * t_chunk, (c + 1) * t_chunk)
                k2 = src[rows_c, :]
                mine = jnp.where(g == 0, _low_half(k2, 0), _low_half(k2, 1))
                dst[rows_c, :] = mine + kcols_ref[rows_c, :]
        for src, dst in ((vslc_ref, vst_ref), (vwin_ref, vwt_ref)):
            dst[HEAD_DIM:VT_ROWS, :] = vtail_ref[HEAD_DIM:VT_ROWS, :]
            for c in range(S // t_chunk):
                v = src[c * t_chunk:(c + 1) * t_chunk, :].astype(F32)
                v_t = v.T
                mine = jnp.where(g == 0, v_t[0:HEAD_DIM], v_t[HEAD_DIM:2 * HEAD_DIM])
                dst[0:HEAD_DIM, c * t_chunk:(c + 1) * t_chunk] = mine.astype(BF16)

    tiles = range(nt)
    lane = lax.broadcasted_iota(jnp.int32, (1, LANES), 1)
    width = NSA_WINDOW + tq
    blocks_per_chunk = tk // SLC_LEN
    n_w = width // SLC_LEN
    jb = lax.broadcasted_iota(jnp.int32, (n_slc, 1), 0)
    jbf = jb.astype(F32)
    q0s = [(i * nt + t) * tq for t in tiles]
    backs = [jnp.minimum(q0, NSA_WINDOW) for q0 in q0s]
    wss = [pl.multiple_of(q0 - back, tq) for q0, back in zip(q0s, backs)]
    qss = [pl.multiple_of(q0, tq) for q0 in q0s]

    q_augs = []
    for t in tiles:
        slabs = []
        for r in range(NSA_REP):
            sl = jnp.where(g == 0, NSA_SLOPES[r], NSA_SLOPES[NSA_REP + r]) * LOG2E
            meta = jnp.zeros((1, LANES), F32)
            for c, coef in enumerate((SLC_LEN * sl, sl, -sl * q0s[t].astype(F32))):
                terms = _split3(jnp.full((1, LANES), coef, F32))
                for u in range(3):
                    meta = jnp.where(lane == HEAD_DIM + 3 * c + u, terms[u].astype(F32), meta)
            q2 = q_ref[t * tq:(t + 1) * tq, (r // 2) * LANES:(r // 2 + 1) * LANES]
            slabs.append(_low_half(q2, r % 2) + meta.astype(BF16))
        q_augs.append(jnp.concatenate(slabs, axis=0))

    s_alls = [_dot_nt(jnp.concatenate([kc_ref[...], kw_ref[pl.ds(wss[t], width), :],
                                       ks_ref[pl.ds(qss[t], tq), :]], axis=0), q_augs[t]) for t in tiles]

    s_cs = [s_alls[t][0:n_ck] + cmpmask_ref[pl.ds(pl.multiple_of(
        (n_ck - 8) - (tq // CMP_STRIDE) * (i * nt + t), 8), n_ck), :] for t in tiles]
    m_cs = [jnp.max(s, axis=0, keepdims=True) for s in s_cs]
    p_cs = [jnp.exp2(s - m) for s, m in zip(s_cs, m_cs)]
    den_cs = [jnp.sum(p, axis=0, keepdims=True) for p in p_cs]
    p_cs = [p * jnp.where(m > 0.5 * NEG, 1.0 / den, 0.0) for p, m, den in zip(p_cs, m_cs, den_cs)]
    o_cmps = [_dot(vct_ref[0:HEAD_DIM, :], p.astype(BF16)) for p in p_cs]

    ovt = ovt_ref[...]
    imps = []
    for p in p_cs:
        psum = p[:, 0:tq]
        for r in range(1, NSA_REP):
            psum = psum + p[:, r * tq:(r + 1) * tq]
        hi, mid, _ = _split3(psum)
        imps.append(_dot(ovt, hi) + _dot(ovt, mid))
    vals, sels = [], []
    for t in tiles:
        pos = q0s[t] + lax.broadcasted_iota(jnp.int32, (1, tq), 1)
        own = pos // SLC_LEN
        forced = (jb == 0) | (jb == own) | (jb == own - 1)
        valid = jb * SLC_LEN <= pos
        vals.append(jnp.where(valid & jnp.logical_not(forced), imps[t], -jnp.inf))
        sels.append(jnp.where(forced, 1.0, 0.0))

    s_ws = [s_alls[t][n_ck:n_ck + width]
            + band_ref[pl.ds(pl.multiple_of(NSA_WINDOW - backs[t], tq), width), :] for t in tiles]
    m_ws = [jnp.max(s, axis=0, keepdims=True) for s in s_ws]
    s_ds = [s_alls[t][n_ck + width:n_ck + width + tq] + band_ref[NSA_WINDOW:NSA_WINDOW + tq, :] for t in tiles]
    m_ds = [jnp.max(s, axis=0, keepdims=True) for s in s_ds]
    slices = [([(s_ws[t], m_ws[t], r0) for r0 in range(0, width, SLC_LEN)]
               + [(s_ds[t], m_ds[t], r0) for r0 in range(0, tq, SLC_LEN)]) for t in tiles]
    probs = [[] for _ in tiles]

    for it in range(max(N_SEL - 3, len(slices[0]))):
        for t in tiles:
            if it < N_SEL - 3:
                mx = jnp.max(vals[t], axis=0, keepdims=True)
                cand = jnp.where(vals[t] == mx, jbf, 1e9)
                idx = jnp.min(cand, axis=0, keepdims=True)
                pick = (jbf == idx) & (mx > -jnp.inf)
                sels[t] = jnp.where(pick, 1.0, sels[t])
                vals[t] = jnp.where(pick, -jnp.inf, vals[t])
            if it < len(slices[t]):
                src, m_src, r0 = slices[t][it]
                probs[t].append(_exp2_bf16(src[r0:r0 + SLC_LEN] - m_src))
    o_wins = []
    for t in tiles:
        acc_w = _dot(vwt_ref[:, pl.ds(wss[t], width)], jnp.concatenate(probs[t][:n_w], axis=0))
        o_wins.append(acc_w[0:HEAD_DIM] * (1.0 / acc_w[HEAD_DIM:HEAD_DIM + 1]))
        m_scr[t] = m_ds[t]
        acc_scr[t] = _dot(vst_ref[:, pl.ds(qss[t], tq)], jnp.concatenate(probs[t][n_w:], axis=0))

    n_needs = []
    for t in tiles:
        sbias = jnp.where((sels[t] > 0.5) & (jb < q0s[t] // SLC_LEN), 0.0, NEG)
        selbias_scr[t] = jnp.concatenate([sbias] * NSA_REP, axis=1)
        for c in range(n_slc // blocks_per_chunk):
            chunk_max = jnp.max(sbias[c * blocks_per_chunk:(c + 1) * blocks_per_chunk])
            need_smem[t, c] = (chunk_max > 0.5 * NEG).astype(jnp.int32)
    for t in tiles:
        n_need = jnp.int32(0)
        for k in range(n_slc // blocks_per_chunk + 1):
            list_smem[t, k] = jnp.int32(0)
        for c in range(n_slc // blocks_per_chunk):
            list_smem[t, n_need] = jnp.int32(c)
            n_need = n_need + need_smem[t, c]
        n_needs.append(n_need)

    def entry(t, k):
        live = k < n_needs[t]
        c = jnp.where(live, list_smem[t, k], 0)
        return c, jnp.where(live, 0.0, NEG)

    def masked(s_c, t, c, pad):
        parts = []
        for jj in range(blocks_per_chunk):
            brow = selbias_scr[t, pl.ds(c * blocks_per_chunk + jj, 1), :] + pad
            parts.append(s_c[jj * SLC_LEN:(jj + 1) * SLC_LEN] + brow)
        return jnp.concatenate(parts, axis=0)

    for ta in range(0, nt, 4):
        group = list(range(ta, min(ta + 4, nt)))

        def joint_body(j, carry, group=group):
            ents = [(entry(t, 2 * j), entry(t, 2 * j + 1)) for t in group]
            offs = [(pl.multiple_of(ea[0] * tk, tk), pl.multiple_of(eb[0] * tk, tk)) for ea, eb in ents]
            scores = [_dot_nt(jnp.concatenate([ks_ref[pl.ds(ka, tk), :], ks_ref[pl.ds(kb, tk), :]], axis=0),
                              q_augs[t]) for t, (ka, kb) in zip(group, offs)]
            scores = [jnp.concatenate([masked(s[0:tk], t, ea[0], ea[1]), masked(s[tk:2 * tk], t, eb[0], eb[1])],
                                      axis=0) for s, t, (ea, eb) in zip(scores, group, ents)]
            m_old = [m_scr[t] for t in group]
            m_new = [jnp.maximum(m, jnp.max(s, axis=0, keepdims=True)) for m, s in zip(m_old, scores)]
            probs = [_exp2_bf16(s - m) for s, m in zip(scores, m_new)]
            for t, (ka, kb), p, mo, mn in zip(group, offs, probs, m_old, m_new):
                v_ab = jnp.concatenate([vst_ref[:, pl.ds(ka, tk)], vst_ref[:, pl.ds(kb, tk)]], axis=1)
                acc_scr[t] = jnp.exp2(mo - mn) * acc_scr[t] + _dot(v_ab, p)
                m_scr[t] = mn
            return carry

        trips = functools.reduce(jnp.maximum, [(n_needs[t] + 1) // 2 for t in group])
        lax.fori_loop(0, trips, joint_body, 0)

    for t in tiles:
        acc = acc_scr[t]
        o_slc = acc[0:HEAD_DIM] * (1.0 / acc[HEAD_DIM:HEAD_DIM + 1])
        gate_t = _sigmoid(gate_ref[t * tq:(t + 1) * tq, :].T)
        outs = []
        for r in range(NSA_REP):
            cols = slice(r * tq, (r + 1) * tq)

            def gate_row(branch):
                lo_row = 3 * r + branch
                hi_row = 3 * (NSA_REP + r) + branch
                return jnp.where(g == 0, gate_t[lo_row:lo_row + 1], gate_t[hi_row:hi_row + 1])

            outs.append(gate_row(0) * o_cmps[t][:, cols] + gate_row(1) * o_slc[:, cols]
                        + gate_row(2) * o_wins[t][:, cols])
        o_ref[t * tq:(t + 1) * tq, :] = jnp.concatenate(outs, axis=0).T.astype(BF16)


def _nsa(q_pad, gates, kc, vct, kv_pad, kcols, vtail, band, cmpmask, ovt, *, B, S, tq=NSA_TQ, tk=512,
         nt=NSA_TILES_PER_STEP):
    G = NSA_GROUPS
    n_ck = kc.shape[1]
    n_slc = ovt.shape[0]
    rows = NSA_REP * tq
    nq = S // (tq * nt)
    n_chunks = n_slc * SLC_LEN // tk
    per_bg = lambda shape: pl.BlockSpec((None,) + shape, lambda b, g, i: (b * G + g, 0, 0))
    full = lambda a: pl.BlockSpec(a.shape, lambda b, g, i: (0,) * a.ndim)
    kv_kind = lambda kind: pl.BlockSpec((S, LANES), lambda b, g, i: (b, kind))
    return pl.pallas_call(
        functools.partial(_nsa_kernel, tq=tq, tk=tk, nt=nt),
        grid=(B, G, nq),
        in_specs=[
            pl.BlockSpec((nt * tq, NSA_REP * HEAD_DIM), lambda b, g, i: (b * nq + i, g)),
            pl.BlockSpec((nt * tq, LANES), lambda b, g, i: (b * nq + i, 0)),
            per_bg((n_ck, LANES)), per_bg((VT_ROWS, n_ck)),
            kv_kind(2), kv_kind(3), kv_kind(4), kv_kind(5),
            full(kcols), full(vtail), full(band), full(cmpmask), full(ovt),
        ],
        out_specs=pl.BlockSpec((nt * tq, NSA_REP * HEAD_DIM), lambda b, g, i: (b * nq + i, g)),
        out_shape=jax.ShapeDtypeStruct((B * S, Q_NSA), BF16),
        scratch_shapes=[pltpu.VMEM((nt, n_slc, rows), F32), pltpu.VMEM((nt, 1, rows), F32),
                        pltpu.VMEM((nt, VT_ROWS, rows), F32),
                        pltpu.VMEM((S, LANES), BF16), pltpu.VMEM((VT_ROWS, S), BF16),
                        pltpu.VMEM((S, LANES), BF16), pltpu.VMEM((VT_ROWS, S), BF16),
                        pltpu.SMEM((nt, n_chunks), jnp.int32), pltpu.SMEM((nt, n_chunks + 1), jnp.int32)],
        compiler_params=_params(("arbitrary", "arbitrary", "arbitrary")),
        name="nsa",
    )(q_pad, gates, kc, vct, kv_pad, kv_pad, kv_pad, kv_pad, kcols, vtail, band, cmpmask, ovt)


def _dil_attn_kernel(q_ref, k_ref, v_ref, dist_ref, o_ref, lse_ref, vt_scr, *, tq, width, max_back, slopes):
    i = pl.program_id(1)
    nb, L, _ = k_ref.shape
    t_chunk = min(L, 512)

    @pl.when(i == 0)
    def _():
        tail_row = lax.broadcasted_iota(jnp.int32, (VT_ROWS - HEAD_DIM, L), 0)
        tail = jnp.where(tail_row == 0, 1.0, 0.0).astype(BF16)
        for b in range(nb):
            for h in range(DIL_HPG):
                vt_scr[b, h * VT_ROWS + HEAD_DIM:(h + 1) * VT_ROWS, :] = tail
            for pair in range(DIL_OUT // LANES):
                for c in range(L // t_chunk):
                    cols = slice(c * t_chunk, (c + 1) * t_chunk)
                    v_t = v_ref[b, cols, pair * LANES:(pair + 1) * LANES].astype(F32).T.astype(BF16)
                    for half in range(2):
                        h = 2 * pair + half
                        vt_scr[b, h * VT_ROWS:h * VT_ROWS + HEAD_DIM, cols] = v_t[half * HEAD_DIM:(half + 1) * HEAD_DIM]

    back = jnp.minimum(i * tq, max_back)
    ks = pl.multiple_of(i * tq - back, max_back)
    dist = dist_ref[pl.ds(pl.multiple_of(max_back - back, max_back), width), :]
    lane = lax.broadcasted_iota(jnp.int32, (1, LANES), 1)
    chains = [(b, h) for b in range(nb) for h in range(DIL_HPG)]
    scores = []
    for b, h in chains:
        pair, half = divmod(h, 2)
        in_head = (lane >= half * HEAD_DIM) & (lane < (half + 1) * HEAD_DIM)
        q = q_ref[b, :, pair * LANES:(pair + 1) * LANES]
        q = jnp.where(in_head, q, jnp.zeros_like(q))
        k = k_ref[b, pl.ds(ks, width), pair * LANES:(pair + 1) * LANES]
        scores.append(_dot_nt(k, q) - (slopes[h] * LOG2E) * dist)
    maxes = [jnp.max(s_t, axis=0, keepdims=True) for s_t in scores]
    probs = [_exp2_bf16(s_t - m) for s_t, m in zip(scores, maxes)]
    accs = [_dot(vt_scr[b, h * VT_ROWS:(h + 1) * VT_ROWS, pl.ds(ks, width)], p) for (b, h), p in zip(chains, probs)]
    dens = [acc[HEAD_DIM:HEAD_DIM + 1] for acc in accs]
    outs = [acc[0:HEAD_DIM] * (1.0 / den) for acc, den in zip(accs, dens)]
    lses = [jnp.broadcast_to((m + jnp.log2(den)) * (1.0 / LOG2E), (HEAD_DIM, tq)) for m, den in zip(maxes, dens)]
    for b in range(nb):
        o_ref[b] = jnp.concatenate(outs[b * DIL_HPG:(b + 1) * DIL_HPG], axis=0).T
        lse_ref[b] = jnp.concatenate(lses[b * DIL_HPG:(b + 1) * DIL_HPG], axis=0).T


def _dil_attn(qkv, dist_tbl, *, max_back, slopes, tq, nb):
    N, L, _ = qkv.shape
    width = dist_tbl.shape[0] - max_back
    kern = functools.partial(_dil_attn_kernel, tq=tq, width=width, max_back=max_back, slopes=slopes)
    return pl.pallas_call(
        kern,
        grid=(N // nb, L // tq),
        in_specs=[
            pl.BlockSpec((nb, tq, DIL_OUT), lambda n, i: (n, i, 0)),
            pl.BlockSpec((nb, L, DIL_OUT), lambda n, i: (n, 0, 1)),
            pl.BlockSpec((nb, L, DIL_OUT), lambda n, i: (n, 0, 2)),
            pl.BlockSpec(dist_tbl.shape, lambda n, i: (0, 0)),
        ],
        out_specs=[
            pl.BlockSpec((nb, tq, DIL_OUT), lambda n, i: (n, i, 0)),
            pl.BlockSpec((nb, tq, DIL_OUT), lambda n, i: (n, i, 0)),
        ],
        out_shape=[
            jax.ShapeDtypeStruct((N, L, DIL_OUT), F32),
            jax.ShapeDtypeStruct((N, L, DIL_OUT), F32),
        ],
        scratch_shapes=[pltpu.VMEM((nb, DIL_HPG * VT_ROWS, L), BF16)],
        compiler_params=_params(("parallel", "arbitrary")),
        name="dil_attn",
    )(qkv, qkv, qkv, dist_tbl)


def _final_kernel(x_ref, pre_ref, post_ref, onsa_ref, od0_ref, od1_ref, od2_ref, l0_ref, l1_ref, l2_ref,
                  wmerge_ref, wnsa_ref, wdil_ref, wout_ref, o_ref, scr):
    tm = x_ref.shape[0]
    x = x_ref[...]
    h = _rms(x, pre_ref[...]).astype(BF16)
    merge = _dot(h, wmerge_ref[...])
    g_a = _sigmoid(merge[:, :D_MODEL])
    g_b = _sigmoid(merge[:, D_MODEL:])
    y_nsa = _dot(onsa_ref[...], wnsa_ref[...])

    n_blk = DIL_OUT // LANES
    slot = 0
    token_major = []
    for (_, dil), o_in, l_in in zip(DIL_CONFIGS, (od0_ref, od1_ref, od2_ref), (l0_ref, l1_ref, l2_ref)):
        pair = []
        for src in (o_in, l_in):
            if dil == 1:
                pair.append(src[0])
                continue
            for r in range(dil):
                for c in range(n_blk):
                    scr[slot + c, pl.ds(r, tm // dil, stride=dil), :] = src[r, :, c * LANES:(c + 1) * LANES]
            pair.append(jnp.concatenate([scr[slot + c] for c in range(n_blk)], axis=1))
            slot += n_blk
        token_major.append(pair)
    (o0, l0), (o1, l1), (o2, l2) = token_major
    mx = jnp.maximum(jnp.maximum(l0, l1), l2)
    e0, e1, e2 = jnp.exp(l0 - mx), jnp.exp(l1 - mx), jnp.exp(l2 - mx)
    den = e0 + e1 + e2
    o_dil = (e0 / den) * o0 + (e1 / den) * o1 + (e2 / den) * o2
    y_dil = _dot(o_dil.astype(BF16), wdil_ref[...])

    y = g_a * y_nsa + g_b * y_dil
    z = _dot(y.astype(BF16), wout_ref[...])
    o_ref[...] = x + _rms(z, post_ref[...])


def _final(x2d, pre, post, onsa, od, ld, wmerge, wnsa, wdil, wout, *, S, tm=512):
    T = x2d.shape[0]
    nt = S // tm
    tok = lambda w: pl.BlockSpec((tm, w), lambda i: (i, 0))
    full = lambda a: pl.BlockSpec(a.shape, lambda i: (0,) * a.ndim)
    strided = [pl.BlockSpec((None, dil, tm // dil, DIL_OUT), lambda i: (i // nt, 0, i % nt, 0))
               for _, dil in DIL_CONFIGS]
    n_strided = sum(1 for _, dil in DIL_CONFIGS if dil > 1)
    return pl.pallas_call(
        _final_kernel,
        grid=(T // tm,),
        in_specs=[tok(D_MODEL), _resident(pre), _resident(post), tok(Q_NSA)] + strided + strided
        + [_resident(wmerge), _resident(wnsa), _resident(wdil), _resident(wout)],
        out_specs=tok(D_MODEL),
        out_shape=jax.ShapeDtypeStruct((T, D_MODEL), F32),
        scratch_shapes=[pltpu.VMEM((2 * n_strided * (DIL_OUT // LANES), tm, LANES), F32)],
        compiler_params=_params(("parallel",)),
        name="final",
    )(x2d, pre, post, onsa, od[0], od[1], od[2], ld[0], ld[1], ld[2], wmerge, wnsa, wdil, wout)


def _overlap_t(n_ck, n_slc):
    c0 = np.arange(n_ck)[None, :] * CMP_STRIDE
    s0 = np.arange(n_slc)[:, None] * SLC_LEN
    ov = np.clip(np.minimum(c0 + CMP_LEN, s0 + SLC_LEN) - np.maximum(c0, s0), 0, None) / CMP_LEN
    ov[:, n_ck - 1] = 0.0
    return ov.astype(np.float32)


def _pos_columns(pos):
    cols = np.zeros((len(pos), LANES), np.float32)
    for t, vals in enumerate((pos // SLC_LEN, pos % SLC_LEN, np.ones(len(pos)))):
        for u in range(3):
            cols[:, HEAD_DIM + 3 * t + u] = vals
    return cols


def _ones_row_tail(n):
    tail = np.zeros((VT_ROWS, n), np.float32)
    tail[HEAD_DIM] = 1.0
    return tail


def _band_table(tq):
    u = np.arange(2 * NSA_WINDOW + tq)[:, None]
    r = (np.arange(NSA_REP * tq) % tq)[None, :]
    dist = NSA_WINDOW + r - u
    return np.where((dist >= 0) & (dist <= NSA_WINDOW - 1), 0.0, NEG).astype(np.float32)


def _dil_dist_table(tq, width, max_back, dil):
    u = np.arange(width + max_back)[:, None]
    r = np.arange(tq)[None, :]
    dist = max_back + r - u
    return np.where((dist >= 0) & (dist <= max_back), dist * dil, -NEG).astype(np.float32)


def _cmp_mask_table(n_ck, tq):
    u = np.arange(2 * n_ck - 8)[:, None]
    r = (np.arange(NSA_REP * tq) % tq)[None, :]
    end = (u - (n_ck - 8)) * CMP_STRIDE + CMP_LEN - 1
    return np.where(end <= r, 0.0, NEG).astype(np.float32)


def _mixer(x2d, B, S, mix_pre, mix_post, w_in, pe_k, w_ck1, w_ck2, pe_v, w_cv1, w_cv2,
           w_nsa_o, w_dil_o, w_mix_out):
    G = NSA_GROUPS
    BG = B * G
    assert S % (16 * 128) == 0 and NSA_TQ // CMP_STRIDE == 8
    o_q, o_kv, o_g, o_d, o_m = np.cumsum([0, Q_NSA, 6 * KV_NSA, GATE_NSA, 3 * QKV_DIL]).tolist()
    wq = w_in[:, o_q:o_kv] * (SCALE * LOG2E)
    w_gate = jnp.pad(w_in[:, o_g:o_d], ((0, 0), (0, LANES - GATE_NSA)))
    wd = w_in[:, o_d:o_m]
    q_pad, kv_pad, gates, xk, xv, *qkv_d = _proj(
        x2d, mix_pre[None, :], wq.astype(BF16), w_in[:, o_kv:o_g].astype(BF16), wd.astype(BF16),
        w_gate.astype(BF16), B=B, S=S)

    n_ck = S // CMP_STRIDE
    half = CMP_STRIDE * HEAD_DIM
    w1 = jnp.stack([w_ck1, w_cv1])
    w1cat = jnp.concatenate([w1[:, :half], w1[:, half:]], axis=2).astype(BF16)
    pe = jnp.stack([pe_k.reshape(1, -1), pe_v.reshape(1, -1)]).astype(BF16)
    w2 = jnp.pad(jnp.stack([w_ck2, w_cv2]), ((0, 0), (0, 0), (0, LANES - HEAD_DIM))).astype(BF16)
    cmp_pos = np.arange(n_ck) * CMP_STRIDE + CMP_LEN - 1
    kc, vct = _cmp_mlp(xk.reshape(BG, n_ck, half), xv.reshape(BG, n_ck, half), pe, w1cat, w1.astype(BF16), w2,
                       jnp.asarray(_pos_columns(cmp_pos), BF16), jnp.asarray(_ones_row_tail(n_ck), BF16))

    band = jnp.asarray(_band_table(NSA_TQ))
    cmpmask = jnp.asarray(_cmp_mask_table(n_ck, NSA_TQ))
    ovt = jnp.asarray(_overlap_t(n_ck, S // SLC_LEN), BF16)
    o_nsa = _nsa(q_pad, gates, kc, vct, kv_pad, jnp.asarray(_pos_columns(np.arange(S)), BF16),
                 jnp.asarray(_ones_row_tail(S), BF16), band, cmpmask, ovt, B=B, S=S)

    od, ld = [], []
    for gi, (window, dil) in enumerate(DIL_CONFIGS):
        Ls = S // dil
        max_back = window // dil
        tq = min(DIL_TQ, Ls)
        width = tq + max_back if Ls > tq else tq
        o, lse = _dil_attn(qkv_d[gi].reshape(B * dil, Ls, DIL_QKV),
                           jnp.asarray(_dil_dist_table(tq, width, max_back, dil)),
                           max_back=max_back, slopes=DIL_SLOPES[gi * DIL_HPG:(gi + 1) * DIL_HPG], tq=tq,
                           nb=DIL_TQ // tq)
        od.append(o.reshape(B, dil, Ls, DIL_OUT))
        ld.append(lse.reshape(B, dil, Ls, DIL_OUT))

    return _final(x2d, mix_pre[None, :], mix_post[None, :], o_nsa, od, ld, w_in[:, o_m:].astype(BF16),
                  w_nsa_o.astype(BF16), w_dil_o.astype(BF16), w_mix_out.astype(BF16), S=S)


def kernel(x, ffn1_pre, ffn1_post, ffn1_w_gate, ffn1_w_up, ffn1_w_down, mix_pre, mix_post, w_in, nsa_pe_k, nsa_w_ck1, nsa_w_ck2, nsa_pe_v, nsa_w_cv1, nsa_w_cv2, w_nsa_o, w_dil_o, w_mix_out, ffn2_pre, ffn2_post, ffn2_w_gate, ffn2_w_up, ffn2_w_down):
    B, S, _ = x.shape
    x2d = x.reshape(B * S, D_MODEL)
    for l in range(ffn1_pre.shape[0]):
        x2d = _ffn(x2d, ffn1_pre[l][None, :], ffn1_post[l][None, :], ffn1_w_gate[l].astype(BF16),
                   ffn1_w_up[l].astype(BF16), ffn1_w_down[l].astype(BF16))
        x2d = _mixer(x2d, B, S, mix_pre[l], mix_post[l], w_in[l], nsa_pe_k[l], nsa_w_ck1[l], nsa_w_ck2[l],
                     nsa_pe_v[l], nsa_w_cv1[l], nsa_w_cv2[l], w_nsa_o[l], w_dil_o[l], w_mix_out[l])
        x2d = _ffn(x2d, ffn2_pre[l][None, :], ffn2_post[l][None, :], ffn2_w_gate[l].astype(BF16),
                   ffn2_w_up[l].astype(BF16), ffn2_w_down[l].astype(BF16))
    return x2d.reshape(B, S, D_MODEL)
```

```python
import functools
import math

import numpy as np
import jax
import jax.numpy as jnp
from jax import lax
from jax.experimental import pallas as pl
from jax.experimental.pallas import tpu as pltpu

F32 = jnp.float32
BF16 = jnp.bfloat16

D_MODEL = 1024
HEAD_DIM = 64
RMS_EPS = 1e-6
NSA_HEADS = 8
NSA_GROUPS = 2
NSA_REP = NSA_HEADS // NSA_GROUPS
CMP_LEN = 32
CMP_STRIDE = 16
CMP_HIDDEN = 256
SLC_LEN = 64
N_SEL = 16
NSA_WINDOW = 512
DIL_CONFIGS = ((128, 1), (512, 4), (2048, 16))
DIL_HPG = 4
DIL_HEADS = DIL_HPG * len(DIL_CONFIGS)
D_FF = 2816
Q_NSA = NSA_HEADS * HEAD_DIM
KV_NSA = NSA_GROUPS * HEAD_DIM
GATE_NSA = NSA_HEADS * 3
QKV_DIL = DIL_HEADS * HEAD_DIM
DIL_OUT = DIL_HPG * HEAD_DIM
DIL_QKV = 3 * DIL_OUT
DIL_TQ = 512
DIL_TS = 256
LANES = 128
NEG = -1e30
SCALE = HEAD_DIM ** -0.5
LOG2E = math.log2(math.e)
VMEM_LIMIT = 56 * 1024 * 1024
NSA_TQ = 128
NSA_TILES_PER_STEP = 4
VT_ROWS = 80


def _slopes(n):
    return [float(np.float32(2.0 ** (-8.0 * (i + 1) / n))) for i in range(n)]


NSA_SLOPES = _slopes(NSA_HEADS)
DIL_SLOPES = _slopes(DIL_HEADS)


def _params(sem):
    return pltpu.CompilerParams(dimension_semantics=sem, vmem_limit_bytes=VMEM_LIMIT)


def _rms(x, g):
    ms = jnp.mean(x * x, axis=-1, keepdims=True)
    return x * lax.rsqrt(ms + RMS_EPS) * g


def _sigmoid(x):
    return 1.0 / (1.0 + jnp.exp(-x))


def _dot(a, b):
    return jnp.dot(a, b, preferred_element_type=F32)


def _dot_nt(a, b):
    return lax.dot_general(a, b, (((1,), (1,)), ((), ())), preferred_element_type=F32)


def _split3(a):
    hi = a.astype(BF16)
    r1 = a - hi.astype(F32)
    mid = r1.astype(BF16)
    lo = (r1 - mid.astype(F32)).astype(BF16)
    return hi, mid, lo


def _ffn_kernel(x_ref, pre_ref, post_ref, wg_ref, wu_ref, wd_ref, o_ref):
    x = x_ref[...]
    h = _rms(x, pre_ref[...]).astype(BF16)
    g = _dot(h, wg_ref[...])
    u = _dot(h, wu_ref[...])
    a = (g * _sigmoid(g) * u).astype(BF16)
    o_ref[...] = x + 0.5 * _rms(_dot(a, wd_ref[...]), post_ref[...])


def _resident(a):
    return pl.BlockSpec(a.shape, lambda *_: (0,) * a.ndim, pipeline_mode=pl.Buffered(1))


def _ffn(x2d, pre, post, wg, wu, wd, *, tm=512):
    T = x2d.shape[0]
    return pl.pallas_call(
        _ffn_kernel,
        grid=(T // tm,),
        in_specs=[pl.BlockSpec((tm, D_MODEL), lambda i: (i, 0)), _resident(pre), _resident(post),
                  _resident(wg), _resident(wu), _resident(wd)],
        out_specs=pl.BlockSpec((tm, D_MODEL), lambda i: (i, 0)),
        out_shape=jax.ShapeDtypeStruct((T, D_MODEL), F32),
        compiler_params=_params(("parallel",)),
        name="ffn",
    )(x2d, pre, post, wg, wu, wd)


def _proj_kernel(x_ref, pre_ref, wq_ref, wkv_ref, wd_ref, wgate_ref, q_ref, kv_ref, g_ref, xk_ref, xv_ref,
                 *rest):
    d_refs, scr = rest[:-1], rest[-1]
    tm = x_ref.shape[0]
    h = _rms(x_ref[...], pre_ref[...]).astype(BF16)
    q_ref[...] = _dot(h, wq_ref[...]).astype(BF16)
    kv = _dot(h, wkv_ref[...])
    kv_ref[...] = kv.astype(BF16)
    g_ref[...] = _dot(h, wgate_ref[...])
    lane = lax.broadcasted_iota(jnp.int32, (1, LANES), 1)
    n_rows = tm // CMP_STRIDE
    for kind, out in ((0, xk_ref), (1, xv_ref)):
        scr[kind] = kv[:, kind * LANES:(kind + 1) * LANES]
        for j in range(CMP_STRIDE // 2):
            even = scr[kind, pl.ds(2 * j, n_rows, stride=CMP_STRIDE), :]
            odd = scr[kind, pl.ds(2 * j + 1, n_rows, stride=CMP_STRIDE), :]
            out[0, :, j * LANES:(j + 1) * LANES] = jnp.where(
                lane < HEAD_DIM, even, pltpu.roll(odd, HEAD_DIM, 1)).astype(BF16)
            out[1, :, j * LANES:(j + 1) * LANES] = jnp.where(
                lane < HEAD_DIM, pltpu.roll(even, HEAD_DIM, 1), odd).astype(BF16)
    n_blk = DIL_QKV // LANES
    for gi, (_, dil) in enumerate(DIL_CONFIGS):
        parts = [_dot(h, wd_ref[:, t * QKV_DIL + gi * DIL_OUT:t * QKV_DIL + (gi + 1) * DIL_OUT]) for t in range(3)]
        res = jnp.concatenate([parts[0] * (SCALE * LOG2E), parts[1], parts[2]], axis=1)
        if dil == 1:
            d_refs[gi][0] = res.astype(BF16)
            continue
        for c in range(n_blk):
            scr[c] = res[:, c * LANES:(c + 1) * LANES]
        for r in range(dil):
            for c in range(n_blk):
                rows = scr[c, pl.ds(r, tm // dil, stride=dil), :]
                d_refs[gi][r, :, c * LANES:(c + 1) * LANES] = rows.astype(BF16)


def _proj(x2d, pre, wq, wkv, wd, wgate, *, B, S, tm=512):
    T = x2d.shape[0]
    nt = S // tm
    tok = lambda w: pl.BlockSpec((tm, w), lambda i: (i, 0))
    full = lambda a: pl.BlockSpec(a.shape, lambda i: (0,) * a.ndim)
    d_specs = [pl.BlockSpec((None, dil, tm // dil, DIL_QKV), lambda i: (i // nt, 0, i % nt, 0))
               for _, dil in DIL_CONFIGS]
    d_shapes = [jax.ShapeDtypeStruct((B, dil, S // dil, DIL_QKV), BF16) for _, dil in DIL_CONFIGS]
    chunk_width = CMP_STRIDE * HEAD_DIM
    c_spec = pl.BlockSpec((None, NSA_GROUPS, tm // CMP_STRIDE, chunk_width), lambda i: (i // nt, 0, i % nt, 0))
    c_shape = jax.ShapeDtypeStruct((B, NSA_GROUPS, S // CMP_STRIDE, chunk_width), BF16)
    return pl.pallas_call(
        _proj_kernel,
        grid=(T // tm,),
        in_specs=[tok(D_MODEL), _resident(pre), _resident(wq), _resident(wkv), _resident(wd), _resident(wgate)],
        out_specs=[tok(wq.shape[1]), tok(wkv.shape[1]), tok(LANES), c_spec, c_spec] + d_specs,
        out_shape=[jax.ShapeDtypeStruct((T, wq.shape[1]), BF16), jax.ShapeDtypeStruct((T, wkv.shape[1]), BF16),
                   jax.ShapeDtypeStruct((T, LANES), F32), c_shape, c_shape] + d_shapes,
        scratch_shapes=[pltpu.VMEM((DIL_QKV // LANES, tm, LANES), F32)],
        compiler_params=_params(("parallel",)),
        name="proj",
    )(x2d, pre, wq, wkv, wd, wgate)


def _gelu_tanh(x):
    c = math.sqrt(2.0 / math.pi)
    return x * (0.5 * (1.0 + jnp.tanh(c * (x + 0.044715 * (x * x * x)))))


def _cmp_mlp_kernel(xk_ref, xv_ref, pe_ref, w1cat_ref, w1_ref, w2_ref, kcols_ref, vtail_ref, kc_ref, vct_ref):
    n_chunks = xk_ref.shape[0]

    def mlp(x_ref, t):
        ab = _dot(x_ref[...], w1cat_ref[t])
        first = ab[:, :CMP_HIDDEN]
        second = pltpu.roll(ab[:, CMP_HIDDEN:], n_chunks - 1, 0)
        pe8 = jnp.broadcast_to(pe_ref[t], (8, pe_ref.shape[2]))
        bias = _dot(pe8, w1_ref[t])[0:1]
        hid = _gelu_tanh(first + second + bias).astype(BF16)
        out = _dot(hid, w2_ref[t])
        row = lax.broadcasted_iota(jnp.int32, out.shape, 0)
        return jnp.where(row < n_chunks - 1, out, 0.0)

    kc_ref[...] = mlp(xk_ref, 0).astype(BF16) + kcols_ref[...]
    vct_ref[...] = mlp(xv_ref, 1).T[0:VT_ROWS].astype(BF16) + vtail_ref[...]


def _cmp_mlp(xk, xv, pe, w1cat, w1, w2, kcols, vtail):
    BG, n_chunks, width = xk.shape
    per_n = lambda shape: pl.BlockSpec((None,) + shape, lambda n: (n, 0, 0))
    full = lambda a: pl.BlockSpec(a.shape, lambda n: (0,) * a.ndim)
    return pl.pallas_call(
        _cmp_mlp_kernel,
        grid=(BG,),
        in_specs=[per_n((n_chunks, width)), per_n((n_chunks, width)), full(pe), full(w1cat), full(w1),
                  full(w2), full(kcols), full(vtail)],
        out_specs=[per_n((n_chunks, LANES)), per_n((VT_ROWS, n_chunks))],
        out_shape=[jax.ShapeDtypeStruct((BG, n_chunks, LANES), BF16),
                   jax.ShapeDtypeStruct((BG, VT_ROWS, n_chunks), BF16)],
        compiler_params=_params(("parallel",)),
        name="cmp_mlp",
    )(xk, xv, pe, w1cat, w1, w2, kcols, vtail)


def _exp2_bf16(x):
    return jnp.exp2(x.astype(BF16))


def _low_half(x2, half):
    lane = lax.broadcasted_iota(jnp.int32, (1, LANES), 1)
    xf = x2.astype(F32)
    if half == 1:
        xf = pltpu.roll(xf, HEAD_DIM, 1)
    return jnp.where(lane < HEAD_DIM, xf, 0.0).astype(BF16)


def _nsa_kernel(q_ref, gate_ref, kc_ref, vct_ref, kslc_ref, vslc_ref, kwin_ref, vwin_ref, kcols_ref,
                vtail_ref, band_ref, cmpmask_ref, ovt_ref, o_ref,
                selbias_scr, m_scr, acc_scr, ks_ref, vst_ref, kw_ref, vwt_ref, need_smem, list_smem,
                *, tq, tk, nt):
    g = pl.program_id(1)
    i = pl.program_id(2)
    n_ck = kc_ref.shape[0]
    n_slc = ovt_ref.shape[0]
    S = kslc_ref.shape[0]

    @pl.when(i == 0)
    def _():
        t_chunk = min(S, 512)
        for src, dst in ((kslc_ref, ks_ref), (kwin_ref, kw_ref)):
            for c in range(S // t_chunk):
                rows_c = slice(c * t_chunk, (c + 1) * t_chunk)
                k2 = src[rows_c, :]
                mine = jnp.where(g == 0, _low_half(k2, 0), _low_half(k2, 1))
                dst[rows_c, :] = mine + kcols_ref[rows_c, :]
        for src, dst in ((vslc_ref, vst_ref), (vwin_ref, vwt_ref)):
            dst[HEAD_DIM:VT_ROWS, :] = vtail_ref[HEAD_DIM:VT_ROWS, :]
            for c in range(S // t_chunk):
                v = src[c * t_chunk:(c + 1) * t_chunk, :].astype(F32)
                v_t = v.T
                mine = jnp.where(g == 0, v_t[0:HEAD_DIM], v_t[HEAD_DIM:2 * HEAD_DIM])
                dst[0:HEAD_DIM, c * t_chunk:(c + 1) * t_chunk] = mine.astype(BF16)

    tiles = range(nt)
    lane = lax.broadcasted_iota(jnp.int32, (1, LANES), 1)
    width = NSA_WINDOW + tq
    blocks_per_chunk = tk // SLC_LEN
    n_w = width // SLC_LEN
    jb = lax.broadcasted_iota(jnp.int32, (n_slc, 1), 0)
    jbf = jb.astype(F32)
    q0s = [(i * nt + t) * tq for t in tiles]
    backs = [jnp.minimum(q0, NSA_WINDOW) for q0 in q0s]
    wss = [pl.multiple_of(q0 - back, tq) for q0, back in zip(q0s, backs)]
    qss = [pl.multiple_of(q0, tq) for q0 in q0s]

    q_augs = []
    for t in tiles:
        slabs = []
        for r in range(NSA_REP):
            sl = jnp.where(g == 0, NSA_SLOPES[r], NSA_SLOPES[NSA_REP + r]) * LOG2E
            meta = jnp.zeros((1, LANES), F32)
            for c, coef in enumerate((SLC_LEN * sl, sl, -sl * q0s[t].astype(F32))):
                terms = _split3(jnp.full((1, LANES), coef, F32))
                for u in range(3):
                    meta = jnp.where(lane == HEAD_DIM + 3 * c + u, terms[u].astype(F32), meta)
            q2 = q_ref[t * tq:(t + 1) * tq, (r // 2) * LANES:(r // 2 + 1) * LANES]
            slabs.append(_low_half(q2, r % 2) + meta.astype(BF16))
        q_augs.append(jnp.concatenate(slabs, axis=0))

    s_alls = [_dot_nt(jnp.concatenate([kc_ref[...], kw_ref[pl.ds(wss[t], width), :],
                                       ks_ref[pl.ds(qss[t], tq), :]], axis=0), q_augs[t]) for t in tiles]

    s_cs = [s_alls[t][0:n_ck] + cmpmask_ref[pl.ds(pl.multiple_of(
        (n_ck - 8) - (tq // CMP_STRIDE) * (i * nt + t), 8), n_ck), :] for t in tiles]
    m_cs = [jnp.max(s, axis=0, keepdims=True) for s in s_cs]
    p_cs = [jnp.exp2(s - m) for s, m in zip(s_cs, m_cs)]
    den_cs = [jnp.sum(p, axis=0, keepdims=True) for p in p_cs]
    p_cs = [p * jnp.where(m > 0.5 * NEG, 1.0 / den, 0.0) for p, m, den in zip(p_cs, m_cs, den_cs)]
    o_cmps = [_dot(vct_ref[0:HEAD_DIM, :], p.astype(BF16)) for p in p_cs]

    ovt = ovt_ref[...]
    imps = []
    for p in p_cs:
        psum = p[:, 0:tq]
        for r in range(1, NSA_REP):
            psum = psum + p[:, r * tq:(r + 1) * tq]
        hi, mid, _ = _split3(psum)
        imps.append(_dot(ovt, hi) + _dot(ovt, mid))
    vals, sels = [], []
    for t in tiles:
        pos = q0s[t] + lax.broadcasted_iota(jnp.int32, (1, tq), 1)
        own = pos // SLC_LEN
        forced = (jb == 0) | (jb == own) | (jb == own - 1)
        valid = jb * SLC_LEN <= pos
        vals.append(jnp.where(valid & jnp.logical_not(forced), imps[t], -jnp.inf))
        sels.append(jnp.where(forced, 1.0, 0.0))

    s_ws = [s_alls[t][n_ck:n_ck + width]
            + band_ref[pl.ds(pl.multiple_of(NSA_WINDOW - backs[t], tq), width), :] for t in tiles]
    m_ws = [jnp.max(s, axis=0, keepdims=True) for s in s_ws]
    s_ds = [s_alls[t][n_ck + width:n_ck + width + tq] + band_ref[NSA_WINDOW:NSA_WINDOW + tq, :] for t in tiles]
    m_ds = [jnp.max(s, axis=0, keepdims=True) for s in s_ds]
    slices = [([(s_ws[t], m_ws[t], r0) for r0 in range(0, width, SLC_LEN)]
               + [(s_ds[t], m_ds[t], r0) for r0 in range(0, tq, SLC_LEN)]) for t in tiles]
    probs = [[] for _ in tiles]

    for it in range(max(N_SEL - 3, len(slices[0]))):
        for t in tiles:
            if it < N_SEL - 3:
                mx = jnp.max(vals[t], axis=0, keepdims=True)
                cand = jnp.where(vals[t] == mx, jbf, 1e9)
                idx = jnp.min(cand, axis=0, keepdims=True)
                pick = (jbf == idx) & (mx > -jnp.inf)
                sels[t] = jnp.where(pick, 1.0, sels[t])
                vals[t] = jnp.where(pick, -jnp.inf, vals[t])
            if it < len(slices[t]):
                src, m_src, r0 = slices[t][it]
                probs[t].append(_exp2_bf16(src[r0:r0 + SLC_LEN] - m_src))
    o_wins = []
    for t in tiles:
        acc_w = _dot(vwt_ref[:, pl.ds(wss[t], width)], jnp.concatenate(probs[t][:n_w], axis=0))
        o_wins.append(acc_w[0:HEAD_DIM] * (1.0 / acc_w[HEAD_DIM:HEAD_DIM + 1]))
        m_scr[t] = m_ds[t]
        acc_scr[t] = _dot(vst_ref[:, pl.ds(qss[t], tq)], jnp.concatenate(probs[t][n_w:], axis=0))

    n_needs = []
    for t in tiles:
        sbias = jnp.where((sels[t] > 0.5) & (jb < q0s[t] // SLC_LEN), 0.0, NEG)
        selbias_scr[t] = jnp.concatenate([sbias] * NSA_REP, axis=1)
        for c in range(n_slc // blocks_per_chunk):
            chunk_max = jnp.max(sbias[c * blocks_per_chunk:(c + 1) * blocks_per_chunk])
            need_smem[t, c] = (chunk_max > 0.5 * NEG).astype(jnp.int32)
    for t in tiles:
        n_need = jnp.int32(0)
        for k in range(n_slc // blocks_per_chunk + 1):
            list_smem[t, k] = jnp.int32(0)
        for c in range(n_slc // blocks_per_chunk):
            list_smem[t, n_need] = jnp.int32(c)
            n_need = n_need + need_smem[t, c]
        n_needs.append(n_need)

    def entry(t, k):
        live = k < n_needs[t]
        c = jnp.where(live, list_smem[t, k], 0)
        return c, jnp.where(live, 0.0, NEG)

    def masked(s_c, t, c, pad):
        parts = []
        for jj in range(blocks_per_chunk):
            brow = selbias_scr[t, pl.ds(c * blocks_per_chunk + jj, 1), :] + pad
            parts.append(s_c[jj * SLC_LEN:(jj + 1) * SLC_LEN] + brow)
        return jnp.concatenate(parts, axis=0)

    for ta in range(0, nt, 4):
        group = list(range(ta, min(ta + 4, nt)))

        def joint_body(j, carry, group=group):
            ents = [(entry(t, 2 * j), entry(t, 2 * j + 1)) for t in group]
            offs = [(pl.multiple_of(ea[0] * tk, tk), pl.multiple_of(eb[0] * tk, tk)) for ea, eb in ents]
            scores = [_dot_nt(jnp.concatenate([ks_ref[pl.ds(ka, tk), :], ks_ref[pl.ds(kb, tk), :]], axis=0),
                              q_augs[t]) for t, (ka, kb) in zip(group, offs)]
            scores = [jnp.concatenate([masked(s[0:tk], t, ea[0], ea[1]), masked(s[tk:2 * tk], t, eb[0], eb[1])],
                                      axis=0) for s, t, (ea, eb) in zip(scores, group, ents)]
            m_old = [m_scr[t] for t in group]
            m_new = [jnp.maximum(m, jnp.max(s, axis=0, keepdims=True)) for m, s in zip(m_old, scores)]
            probs = [_exp2_bf16(s - m) for s, m in zip(scores, m_new)]
            for t, (ka, kb), p, mo, mn in zip(group, offs, probs, m_old, m_new):
                v_ab = jnp.concatenate([vst_ref[:, pl.ds(ka, tk)], vst_ref[:, pl.ds(kb, tk)]], axis=1)
                acc_scr[t] = jnp.exp2(mo - mn) * acc_scr[t] + _dot(v_ab, p)
                m_scr[t] = mn
            return carry

        trips = functools.reduce(jnp.maximum, [(n_needs[t] + 1) // 2 for t in group])
        lax.fori_loop(0, trips, joint_body, 0)

    for t in tiles:
        acc = acc_scr[t]
        o_slc = acc[0:HEAD_DIM] * (1.0 / acc[HEAD_DIM:HEAD_DIM + 1])
        gate_t = _sigmoid(gate_ref[t * tq:(t + 1) * tq, :].T)
        outs = []
        for r in range(NSA_REP):
            cols = slice(r * tq, (r + 1) * tq)

            def gate_row(branch):
                lo_row = 3 * r + branch
                hi_row = 3 * (NSA_REP + r) + branch
                return jnp.where(g == 0, gate_t[lo_row:lo_row + 1], gate_t[hi_row:hi_row + 1])

            outs.append(gate_row(0) * o_cmps[t][:, cols] + gate_row(1) * o_slc[:, cols]
                        + gate_row(2) * o_wins[t][:, cols])
        o_ref[t * tq:(t + 1) * tq, :] = jnp.concatenate(outs, axis=0).T.astype(BF16)


def _nsa(q_pad, gates, kc, vct, kv_pad, kcols, vtail, band, cmpmask, ovt, *, B, S, tq=NSA_TQ, tk=512,
         nt=NSA_TILES_PER_STEP):
    G = NSA_GROUPS
    n_ck = kc.shape[1]
    n_slc = ovt.shape[0]
    rows = NSA_REP * tq
    nq = S // (tq * nt)
    n_chunks = n_slc * SLC_LEN // tk
    per_bg = lambda shape: pl.BlockSpec((None,) + shape, lambda b, g, i: (b * G + g, 0, 0))
    full = lambda a: pl.BlockSpec(a.shape, lambda b, g, i: (0,) * a.ndim)
    kv_kind = lambda kind: pl.BlockSpec((S, LANES), lambda b, g, i: (b, kind))
    return pl.pallas_call(
        functools.partial(_nsa_kernel, tq=tq, tk=tk, nt=nt),
        grid=(B, G, nq),
        in_specs=[
            pl.BlockSpec((nt * tq, NSA_REP * HEAD_DIM), lambda b, g, i: (b * nq + i, g)),
            pl.BlockSpec((nt * tq, LANES), lambda b, g, i: (b * nq + i, 0)),
            per_bg((n_ck, LANES)), per_bg((VT_ROWS, n_ck)),
            kv_kind(2), kv_kind(3), kv_kind(4), kv_kind(5),
            full(kcols), full(vtail), full(band), full(cmpmask), full(ovt),
        ],
        out_specs=pl.BlockSpec((nt * tq, NSA_REP * HEAD_DIM), lambda b, g, i: (b * nq + i, g)),
        out_shape=jax.ShapeDtypeStruct((B * S, Q_NSA), BF16),
        scratch_shapes=[pltpu.VMEM((nt, n_slc, rows), F32), pltpu.VMEM((nt, 1, rows), F32),
                        pltpu.VMEM((nt, VT_ROWS, rows), F32),
                        pltpu.VMEM((S, LANES), BF16), pltpu.VMEM((VT_ROWS, S), BF16),
                        pltpu.VMEM((S, LANES), BF16), pltpu.VMEM((VT_ROWS, S), BF16),
                        pltpu.SMEM((nt, n_chunks), jnp.int32), pltpu.SMEM((nt, n_chunks + 1), jnp.int32)],
        compiler_params=_params(("arbitrary", "arbitrary", "arbitrary")),
        name="nsa",
    )(q_pad, gates, kc, vct, kv_pad, kv_pad, kv_pad, kv_pad, kcols, vtail, band, cmpmask, ovt)


def _dil_attn_kernel(q_ref, k_ref, v_ref, dist_ref, o_ref, lse_ref, vt_scr, *, tq, ts, width, max_back, slopes):
    i = pl.program_id(1)
    nb, L, _ = k_ref.shape
    t_chunk = min(L, 512)

    @pl.when(i == 0)
    def _():
        tail_row = lax.broadcasted_iota(jnp.int32, (VT_ROWS - HEAD_DIM, L), 0)
        tail = jnp.where(tail_row == 0, 1.0, 0.0).astype(BF16)
        for b in range(nb):
            for h in range(DIL_HPG):
                vt_scr[b, h * VT_ROWS + HEAD_DIM:(h + 1) * VT_ROWS, :] = tail
            for pair in range(DIL_OUT // LANES):
                for c in range(L // t_chunk):
                    cols = slice(c * t_chunk, (c + 1) * t_chunk)
                    v_t = v_ref[b, cols, pair * LANES:(pair + 1) * LANES].astype(F32).T.astype(BF16)
                    for half in range(2):
                        h = 2 * pair + half
                        vt_scr[b, h * VT_ROWS:h * VT_ROWS + HEAD_DIM, cols] = v_t[half * HEAD_DIM:(half + 1) * HEAD_DIM]

    n_sub = tq // ts
    backs = [jnp.minimum(i * tq + u * ts, max_back) for u in range(n_sub)]
    kss = [pl.multiple_of(i * tq + u * ts - backs[u], max_back) for u in range(n_sub)]
    dists = [dist_ref[pl.ds(pl.multiple_of(max_back - backs[u], max_back), width), :] for u in range(n_sub)]
    lane = lax.broadcasted_iota(jnp.int32, (1, LANES), 1)
    chains = [(b, u, h) for b in range(nb) for u in range(n_sub) for h in range(DIL_HPG)]
    scores = []
    for b, u, h in chains:
        pair, half = divmod(h, 2)
        in_head = (lane >= half * HEAD_DIM) & (lane < (half + 1) * HEAD_DIM)
        q = q_ref[b, u * ts:(u + 1) * ts, pair * LANES:(pair + 1) * LANES]
        q = jnp.where(in_head, q, jnp.zeros_like(q))
        k = k_ref[b, pl.ds(kss[u], width), pair * LANES:(pair + 1) * LANES]
        scores.append(_dot_nt(k, q) - (slopes[h] * LOG2E) * dists[u])
    maxes = [jnp.max(s_t, axis=0, keepdims=True) for s_t in scores]
    probs = [_exp2_bf16(s_t - m) for s_t, m in zip(scores, maxes)]
    accs = [_dot(vt_scr[b, h * VT_ROWS:(h + 1) * VT_ROWS, pl.ds(kss[u], width)], p)
            for (b, u, h), p in zip(chains, probs)]
    dens = [acc[HEAD_DIM:HEAD_DIM + 1] for acc in accs]
    outs = [acc[0:HEAD_DIM] * (1.0 / den) for acc, den in zip(accs, dens)]
    lses = [jnp.broadcast_to((m + jnp.log2(den)) * (1.0 / LOG2E), (HEAD_DIM, ts)) for m, den in zip(maxes, dens)]
    for n, (b, u) in enumerate((b, u) for b in range(nb) for u in range(n_sub)):
        rows = slice(u * ts, (u + 1) * ts)
        o_ref[b, rows, :] = jnp.concatenate(outs[n * DIL_HPG:(n + 1) * DIL_HPG], axis=0).T
        lse_ref[b, rows, :] = jnp.concatenate(lses[n * DIL_HPG:(n + 1) * DIL_HPG], axis=0).T


def _dil_attn(qkv, dist_tbl, *, max_back, slopes, tq, ts, nb):
    N, L, _ = qkv.shape
    width = dist_tbl.shape[0] - max_back
    kern = functools.partial(_dil_attn_kernel, tq=tq, ts=ts, width=width, max_back=max_back, slopes=slopes)
    return pl.pallas_call(
        kern,
        grid=(N // nb, L // tq),
        in_specs=[
            pl.BlockSpec((nb, tq, DIL_OUT), lambda n, i: (n, i, 0)),
            pl.BlockSpec((nb, L, DIL_OUT), lambda n, i: (n, 0, 1)),
            pl.BlockSpec((nb, L, DIL_OUT), lambda n, i: (n, 0, 2)),
            pl.BlockSpec(dist_tbl.shape, lambda n, i: (0, 0)),
        ],
        out_specs=[
            pl.BlockSpec((nb, tq, DIL_OUT), lambda n, i: (n, i, 0)),
            pl.BlockSpec((nb, tq, DIL_OUT), lambda n, i: (n, i, 0)),
        ],
        out_shape=[
            jax.ShapeDtypeStruct((N, L, DIL_OUT), F32),
            jax.ShapeDtypeStruct((N, L, DIL_OUT), F32),
        ],
        scratch_shapes=[pltpu.VMEM((nb, DIL_HPG * VT_ROWS, L), BF16)],
        compiler_params=_params(("parallel", "arbitrary")),
        name="dil_attn",
    )(qkv, qkv, qkv, dist_tbl)


def _final_kernel(x_ref, pre_ref, post_ref, onsa_ref, od0_ref, od1_ref, od2_ref, l0_ref, l1_ref, l2_ref,
                  wmerge_ref, wnsa_ref, wdil_ref, wout_ref, o_ref, scr):
    tm = x_ref.shape[0]
    x = x_ref[...]
    h = _rms(x, pre_ref[...]).astype(BF16)
    merge = _dot(h, wmerge_ref[...])
    g_a = _sigmoid(merge[:, :D_MODEL])
    g_b = _sigmoid(merge[:, D_MODEL:])
    y_nsa = _dot(onsa_ref[...], wnsa_ref[...])

    n_blk = DIL_OUT // LANES
    slot = 0
    token_major = []
    for (_, dil), o_in, l_in in zip(DIL_CONFIGS, (od0_ref, od1_ref, od2_ref), (l0_ref, l1_ref, l2_ref)):
        pair = []
        for src in (o_in, l_in):
            if dil == 1:
                pair.append(src[0])
                continue
            for r in range(dil):
                for c in range(n_blk):
                    scr[slot + c, pl.ds(r, tm // dil, stride=dil), :] = src[r, :, c * LANES:(c + 1) * LANES]
            pair.append(jnp.concatenate([scr[slot + c] for c in range(n_blk)], axis=1))
            slot += n_blk
        token_major.append(pair)
    (o0, l0), (o1, l1), (o2, l2) = token_major
    mx = jnp.maximum(jnp.maximum(l0, l1), l2)
    e0, e1, e2 = jnp.exp(l0 - mx), jnp.exp(l1 - mx), jnp.exp(l2 - mx)
    den = e0 + e1 + e2
    o_dil = (e0 / den) * o0 + (e1 / den) * o1 + (e2 / den) * o2
    y_dil = _dot(o_dil.astype(BF16), wdil_ref[...])

    y = g_a * y_nsa + g_b * y_dil
    z = _dot(y.astype(BF16), wout_ref[...])
    o_ref[...] = x + _rms(z, post_ref[...])


def _final(x2d, pre, post, onsa, od, ld, wmerge, wnsa, wdil, wout, *, S, tm=512):
    T = x2d.shape[0]
    nt = S // tm
    tok = lambda w: pl.BlockSpec((tm, w), lambda i: (i, 0))
    full = lambda a: pl.BlockSpec(a.shape, lambda i: (0,) * a.ndim)
    strided = [pl.BlockSpec((None, dil, tm // dil, DIL_OUT), lambda i: (i // nt, 0, i % nt, 0))
               for _, dil in DIL_CONFIGS]
    n_strided = sum(1 for _, dil in DIL_CONFIGS if dil > 1)
    return pl.pallas_call(
        _final_kernel,
        grid=(T // tm,),
        in_specs=[tok(D_MODEL), _resident(pre), _resident(post), tok(Q_NSA)] + strided + strided
        + [_resident(wmerge), _resident(wnsa), _resident(wdil), _resident(wout)],
        out_specs=tok(D_MODEL),
        out_shape=jax.ShapeDtypeStruct((T, D_MODEL), F32),
        scratch_shapes=[pltpu.VMEM((2 * n_strided * (DIL_OUT // LANES), tm, LANES), F32)],
        compiler_params=_params(("parallel",)),
        name="final",
    )(x2d, pre, post, onsa, od[0], od[1], od[2], ld[0], ld[1], ld[2], wmerge, wnsa, wdil, wout)


def _overlap_t(n_ck, n_slc):
    c0 = np.arange(n_ck)[None, :] * CMP_STRIDE
    s0 = np.arange(n_slc)[:, None] * SLC_LEN
    ov = np.clip(np.minimum(c0 + CMP_LEN, s0 + SLC_LEN) - np.maximum(c0, s0), 0, None) / CMP_LEN
    ov[:, n_ck - 1] = 0.0
    return ov.astype(np.float32)


def _pos_columns(pos):
    cols = np.zeros((len(pos), LANES), np.float32)
    for t, vals in enumerate((pos // SLC_LEN, pos % SLC_LEN, np.ones(len(pos)))):
        for u in range(3):
            cols[:, HEAD_DIM + 3 * t + u] = vals
    return cols


def _ones_row_tail(n):
    tail = np.zeros((VT_ROWS, n), np.float32)
    tail[HEAD_DIM] = 1.0
    return tail


def _band_table(tq):
    u = np.arange(2 * NSA_WINDOW + tq)[:, None]
    r = (np.arange(NSA_REP * tq) % tq)[None, :]
    dist = NSA_WINDOW + r - u
    return np.where((dist >= 0) & (dist <= NSA_WINDOW - 1), 0.0, NEG).astype(np.float32)


def _dil_dist_table(tq, width, max_back, dil):
    u = np.arange(width + max_back)[:, None]
    r = np.arange(tq)[None, :]
    dist = max_back + r - u
    return np.where((dist >= 0) & (dist <= max_back), dist * dil, -NEG).astype(np.float32)


def _cmp_mask_table(n_ck, tq):
    u = np.arange(2 * n_ck - 8)[:, None]
    r = (np.arange(NSA_REP * tq) % tq)[None, :]
    end = (u - (n_ck - 8)) * CMP_STRIDE + CMP_LEN - 1
    return np.where(end <= r, 0.0, NEG).astype(np.float32)


def _mixer(x2d, B, S, mix_pre, mix_post, w_in, pe_k, w_ck1, w_ck2, pe_v, w_cv1, w_cv2,
           w_nsa_o, w_dil_o, w_mix_out):
    G = NSA_GROUPS
    BG = B * G
    assert S % (16 * 128) == 0 and NSA_TQ // CMP_STRIDE == 8
    o_q, o_kv, o_g, o_d, o_m = np.cumsum([0, Q_NSA, 6 * KV_NSA, GATE_NSA, 3 * QKV_DIL]).tolist()
    wq = w_in[:, o_q:o_kv] * (SCALE * LOG2E)
    w_gate = jnp.pad(w_in[:, o_g:o_d], ((0, 0), (0, LANES - GATE_NSA)))
    wd = w_in[:, o_d:o_m]
    q_pad, kv_pad, gates, xk, xv, *qkv_d = _proj(
        x2d, mix_pre[None, :], wq.astype(BF16), w_in[:, o_kv:o_g].astype(BF16), wd.astype(BF16),
        w_gate.astype(BF16), B=B, S=S)

    n_ck = S // CMP_STRIDE
    half = CMP_STRIDE * HEAD_DIM
    w1 = jnp.stack([w_ck1, w_cv1])
    w1cat = jnp.concatenate([w1[:, :half], w1[:, half:]], axis=2).astype(BF16)
    pe = jnp.stack([pe_k.reshape(1, -1), pe_v.reshape(1, -1)]).astype(BF16)
    w2 = jnp.pad(jnp.stack([w_ck2, w_cv2]), ((0, 0), (0, 0), (0, LANES - HEAD_DIM))).astype(BF16)
    cmp_pos = np.arange(n_ck) * CMP_STRIDE + CMP_LEN - 1
    kc, vct = _cmp_mlp(xk.reshape(BG, n_ck, half), xv.reshape(BG, n_ck, half), pe, w1cat, w1.astype(BF16), w2,
                       jnp.asarray(_pos_columns(cmp_pos), BF16), jnp.asarray(_ones_row_tail(n_ck), BF16))

    band = jnp.asarray(_band_table(NSA_TQ))
    cmpmask = jnp.asarray(_cmp_mask_table(n_ck, NSA_TQ))
    ovt = jnp.asarray(_overlap_t(n_ck, S // SLC_LEN), BF16)
    o_nsa = _nsa(q_pad, gates, kc, vct, kv_pad, jnp.asarray(_pos_columns(np.arange(S)), BF16),
                 jnp.asarray(_ones_row_tail(S), BF16), band, cmpmask, ovt, B=B, S=S)

    od, ld = [], []
    for gi, (window, dil) in enumerate(DIL_CONFIGS):
        Ls = S // dil
        max_back = window // dil
        tq = min(DIL_TQ, Ls)
        ts = min(DIL_TS, Ls)
        width = ts + max_back if Ls > ts else ts
        o, lse = _dil_attn(qkv_d[gi].reshape(B * dil, Ls, DIL_QKV),
                           jnp.asarray(_dil_dist_table(ts, width, max_back, dil)),
                           max_back=max_back, slopes=DIL_SLOPES[gi * DIL_HPG:(gi + 1) * DIL_HPG], tq=tq, ts=ts,
                           nb=DIL_TQ // tq)
        od.append(o.reshape(B, dil, Ls, DIL_OUT))
        ld.append(lse.reshape(B, dil, Ls, DIL_OUT))

    return _final(x2d, mix_pre[None, :], mix_post[None, :], o_nsa, od, ld, w_in[:, o_m:].astype(BF16),
                  w_nsa_o.astype(BF16), w_dil_o.astype(BF16), w_mix_out.astype(BF16), S=S)


def kernel(x, ffn1_pre, ffn1_post, ffn1_w_gate, ffn1_w_up, ffn1_w_down, mix_pre, mix_post, w_in, nsa_pe_k, nsa_w_ck1, nsa_w_ck2, nsa_pe_v, nsa_w_cv1, nsa_w_cv2, w_nsa_o, w_dil_o, w_mix_out, ffn2_pre, ffn2_post, ffn2_w_gate, ffn2_w_up, ffn2_w_down):
    B, S, _ = x.shape
    x2d = x.reshape(B * S, D_MODEL)
    for l in range(ffn1_pre.shape[0]):
        x2d = _ffn(x2d, ffn1_pre[l][None, :], ffn1_post[l][None, :], ffn1_w_gate[l].astype(BF16),
                   ffn1_w_up[l].astype(BF16), ffn1_w_down[l].astype(BF16))
        x2d = _mixer(x2d, B, S, mix_pre[l], mix_post[l], w_in[l], nsa_pe_k[l], nsa_w_ck1[l], nsa_w_ck2[l],
                     nsa_pe_v[l], nsa_w_cv1[l], nsa_w_cv2[l], w_nsa_o[l], w_dil_o[l], w_mix_out[l])
        x2d = _ffn(x2d, ffn2_pre[l][None, :], ffn2_post[l][None, :], ffn2_w_gate[l].astype(BF16),
                   ffn2_w_up[l].astype(BF16), ffn2_w_down[l].astype(BF16))
    return x2d.reshape(B, S, D_MODEL)
```

```python
import functools
import math

import numpy as np
import jax
import jax.numpy as jnp
from jax import lax
from jax.experimental import pallas as pl
from jax.experimental.pallas import tpu as pltpu

F32 = jnp.float32
BF16 = jnp.bfloat16

D_MODEL = 1024
HEAD_DIM = 64
RMS_EPS = 1e-6
NSA_HEADS = 8
NSA_GROUPS = 2
NSA_REP = NSA_HEADS // NSA_GROUPS
CMP_LEN = 32
CMP_STRIDE = 16
CMP_HIDDEN = 256
SLC_LEN = 64
N_SEL = 16
NSA_WINDOW = 512
DIL_CONFIGS = ((128, 1), (512, 4), (2048, 16))
DIL_HPG = 4
DIL_HEADS = DIL_HPG * len(DIL_CONFIGS)
D_FF = 2816
Q_NSA = NSA_HEADS * HEAD_DIM
KV_NSA = NSA_GROUPS * HEAD_DIM
GATE_NSA = NSA_HEADS * 3
QKV_DIL = DIL_HEADS * HEAD_DIM
DIL_OUT = DIL_HPG * HEAD_DIM
DIL_QKV = 3 * DIL_OUT
DIL_TQ = 512
DIL_TS = 128
LANES = 128
NEG = -1e30
SCALE = HEAD_DIM ** -0.5
LOG2E = math.log2(math.e)
VMEM_LIMIT = 56 * 1024 * 1024
NSA_TQ = 128
NSA_TILES_PER_STEP = 4
VT_ROWS = 80


def _slopes(n):
    return [float(np.float32(2.0 ** (-8.0 * (i + 1) / n))) for i in range(n)]


NSA_SLOPES = _slopes(NSA_HEADS)
DIL_SLOPES = _slopes(DIL_HEADS)


def _params(sem):
    return pltpu.CompilerParams(dimension_semantics=sem, vmem_limit_bytes=VMEM_LIMIT)


def _rms(x, g):
    ms = jnp.mean(x * x, axis=-1, keepdims=True)
    return x * lax.rsqrt(ms + RMS_EPS) * g


def _sigmoid(x):
    return 1.0 / (1.0 + jnp.exp(-x))


def _dot(a, b):
    return jnp.dot(a, b, preferred_element_type=F32)


def _dot_nt(a, b):
    return lax.dot_general(a, b, (((1,), (1,)), ((), ())), preferred_element_type=F32)


def _split3(a):
    hi = a.astype(BF16)
    r1 = a - hi.astype(F32)
    mid = r1.astype(BF16)
    lo = (r1 - mid.astype(F32)).astype(BF16)
    return hi, mid, lo


def _ffn_kernel(x_ref, pre_ref, post_ref, wg_ref, wu_ref, wd_ref, o_ref):
    x = x_ref[...]
    h = _rms(x, pre_ref[...]).astype(BF16)
    g = _dot(h, wg_ref[...])
    u = _dot(h, wu_ref[...])
    a = (g * _sigmoid(g) * u).astype(BF16)
    o_ref[...] = x + 0.5 * _rms(_dot(a, wd_ref[...]), post_ref[...])


def _resident(a):
    return pl.BlockSpec(a.shape, lambda *_: (0,) * a.ndim, pipeline_mode=pl.Buffered(1))


def _ffn(x2d, pre, post, wg, wu, wd, *, tm=512):
    T = x2d.shape[0]
    return pl.pallas_call(
        _ffn_kernel,
        grid=(T // tm,),
        in_specs=[pl.BlockSpec((tm, D_MODEL), lambda i: (i, 0)), _resident(pre), _resident(post),
                  _resident(wg), _resident(wu), _resident(wd)],
        out_specs=pl.BlockSpec((tm, D_MODEL), lambda i: (i, 0)),
        out_shape=jax.ShapeDtypeStruct((T, D_MODEL), F32),
        compiler_params=_params(("parallel",)),
        name="ffn",
    )(x2d, pre, post, wg, wu, wd)


def _proj_kernel(x_ref, pre_ref, wq_ref, wkv_ref, wd_ref, wgate_ref, q_ref, kv_ref, g_ref, xk_ref, xv_ref,
                 *rest):
    d_refs, scr = rest[:-1], rest[-1]
    tm = x_ref.shape[0]
    h = _rms(x_ref[...], pre_ref[...]).astype(BF16)
    q_ref[...] = _dot(h, wq_ref[...]).astype(BF16)
    kv = _dot(h, wkv_ref[...])
    kv_ref[...] = kv.astype(BF16)
    g_ref[...] = _dot(h, wgate_ref[...])
    lane = lax.broadcasted_iota(jnp.int32, (1, LANES), 1)
    n_rows = tm // CMP_STRIDE
    for kind, out in ((0, xk_ref), (1, xv_ref)):
        scr[kind] = kv[:, kind * LANES:(kind + 1) * LANES]
        for j in range(CMP_STRIDE // 2):
            even = scr[kind, pl.ds(2 * j, n_rows, stride=CMP_STRIDE), :]
            odd = scr[kind, pl.ds(2 * j + 1, n_rows, stride=CMP_STRIDE), :]
            out[0, :, j * LANES:(j + 1) * LANES] = jnp.where(
                lane < HEAD_DIM, even, pltpu.roll(odd, HEAD_DIM, 1)).astype(BF16)
            out[1, :, j * LANES:(j + 1) * LANES] = jnp.where(
                lane < HEAD_DIM, pltpu.roll(even, HEAD_DIM, 1), odd).astype(BF16)
    n_blk = DIL_QKV // LANES
    for gi, (_, dil) in enumerate(DIL_CONFIGS):
        parts = [_dot(h, wd_ref[:, t * QKV_DIL + gi * DIL_OUT:t * QKV_DIL + (gi + 1) * DIL_OUT]) for t in range(3)]
        res = jnp.concatenate([parts[0] * (SCALE * LOG2E), parts[1], parts[2]], axis=1)
        if dil == 1:
            d_refs[gi][0] = res.astype(BF16)
            continue
        for c in range(n_blk):
            scr[c] = res[:, c * LANES:(c + 1) * LANES]
        for r in range(dil):
            for c in range(n_blk):
                rows = scr[c, pl.ds(r, tm // dil, stride=dil), :]
                d_refs[gi][r, :, c * LANES:(c + 1) * LANES] = rows.astype(BF16)


def _proj(x2d, pre, wq, wkv, wd, wgate, *, B, S, tm=512):
    T = x2d.shape[0]
    nt = S // tm
    tok = lambda w: pl.BlockSpec((tm, w), lambda i: (i, 0))
    full = lambda a: pl.BlockSpec(a.shape, lambda i: (0,) * a.ndim)
    d_specs = [pl.BlockSpec((None, dil, tm // dil, DIL_QKV), lambda i: (i // nt, 0, i % nt, 0))
               for _, dil in DIL_CONFIGS]
    d_shapes = [jax.ShapeDtypeStruct((B, dil, S // dil, DIL_QKV), BF16) for _, dil in DIL_CONFIGS]
    chunk_width = CMP_STRIDE * HEAD_DIM
    c_spec = pl.BlockSpec((None, NSA_GROUPS, tm // CMP_STRIDE, chunk_width), lambda i: (i // nt, 0, i % nt, 0))
    c_shape = jax.ShapeDtypeStruct((B, NSA_GROUPS, S // CMP_STRIDE, chunk_width), BF16)
    return pl.pallas_call(
        _proj_kernel,
        grid=(T // tm,),
        in_specs=[tok(D_MODEL), _resident(pre), _resident(wq), _resident(wkv), _resident(wd), _resident(wgate)],
        out_specs=[tok(wq.shape[1]), tok(wkv.shape[1]), tok(LANES), c_spec, c_spec] + d_specs,
        out_shape=[jax.ShapeDtypeStruct((T, wq.shape[1]), BF16), jax.ShapeDtypeStruct((T, wkv.shape[1]), BF16),
                   jax.ShapeDtypeStruct((T, LANES), F32), c_shape, c_shape] + d_shapes,
        scratch_shapes=[pltpu.VMEM((DIL_QKV // LANES, tm, LANES), F32)],
        compiler_params=_params(("parallel",)),
        name="proj",
    )(x2d, pre, wq, wkv, wd, wgate)


def _gelu_tanh(x):
    c = math.sqrt(2.0 / math.pi)
    return x * (0.5 * (1.0 + jnp.tanh(c * (x + 0.044715 * (x * x * x)))))


def _cmp_mlp_kernel(xk_ref, xv_ref, pe_ref, w1cat_ref, w1_ref, w2_ref, kcols_ref, vtail_ref, kc_ref, vct_ref):
    n_chunks = xk_ref.shape[0]

    def mlp(x_ref, t):
        ab = _dot(x_ref[...], w1cat_ref[t])
        first = ab[:, :CMP_HIDDEN]
        second = pltpu.roll(ab[:, CMP_HIDDEN:], n_chunks - 1, 0)
        pe8 = jnp.broadcast_to(pe_ref[t], (8, pe_ref.shape[2]))
        bias = _dot(pe8, w1_ref[t])[0:1]
        hid = _gelu_tanh(first + second + bias).astype(BF16)
        out = _dot(hid, w2_ref[t])
        row = lax.broadcasted_iota(jnp.int32, out.shape, 0)
        return jnp.where(row < n_chunks - 1, out, 0.0)

    kc_ref[...] = mlp(xk_ref, 0).astype(BF16) + kcols_ref[...]
    vct_ref[...] = mlp(xv_ref, 1).T[0:VT_ROWS].astype(BF16) + vtail_ref[...]


def _cmp_mlp(xk, xv, pe, w1cat, w1, w2, kcols, vtail):
    BG, n_chunks, width = xk.shape
    per_n = lambda shape: pl.BlockSpec((None,) + shape, lambda n: (n, 0, 0))
    full = lambda a: pl.BlockSpec(a.shape, lambda n: (0,) * a.ndim)
    return pl.pallas_call(
        _cmp_mlp_kernel,
        grid=(BG,),
        in_specs=[per_n((n_chunks, width)), per_n((n_chunks, width)), full(pe), full(w1cat), full(w1),
                  full(w2), full(kcols), full(vtail)],
        out_specs=[per_n((n_chunks, LANES)), per_n((VT_ROWS, n_chunks))],
        out_shape=[jax.ShapeDtypeStruct((BG, n_chunks, LANES), BF16),
                   jax.ShapeDtypeStruct((BG, VT_ROWS, n_chunks), BF16)],
        compiler_params=_params(("parallel",)),
        name="cmp_mlp",
    )(xk, xv, pe, w1cat, w1, w2, kcols, vtail)


def _exp2_bf16(x):
    return jnp.exp2(x.astype(BF16))


def _low_half(x2, half):
    lane = lax.broadcasted_iota(jnp.int32, (1, LANES), 1)
    xf = x2.astype(F32)
    if half == 1:
        xf = pltpu.roll(xf, HEAD_DIM, 1)
    return jnp.where(lane < HEAD_DIM, xf, 0.0).astype(BF16)


def _nsa_kernel(q_ref, gate_ref, kc_ref, vct_ref, kslc_ref, vslc_ref, kwin_ref, vwin_ref, kcols_ref,
                vtail_ref, band_ref, cmpmask_ref, ovt_ref, o_ref,
                selbias_scr, m_scr, acc_scr, ks_ref, vst_ref, kw_ref, vwt_ref, need_smem, list_smem,
                *, tq, tk, nt):
    g = pl.program_id(1)
    i = pl.program_id(2)
    n_ck = kc_ref.shape[0]
    n_slc = ovt_ref.shape[0]
    S = kslc_ref.shape[0]

    @pl.when(i == 0)
    def _():
        t_chunk = min(S, 512)
        for src, dst in ((kslc_ref, ks_ref), (kwin_ref, kw_ref)):
            for c in range(S // t_chunk):
                rows_c = slice(c * t_chunk, (c + 1) * t_chunk)
                k2 = src[rows_c, :]
                mine = jnp.where(g == 0, _low_half(k2, 0), _low_half(k2, 1))
                dst[rows_c, :] = mine + kcols_ref[rows_c, :]
        for src, dst in ((vslc_ref, vst_ref), (vwin_ref, vwt_ref)):
            dst[HEAD_DIM:VT_ROWS, :] = vtail_ref[HEAD_DIM:VT_ROWS, :]
            for c in range(S // t_chunk):
                v = src[c * t_chunk:(c + 1) * t_chunk, :].astype(F32)
                v_t = v.T
                mine = jnp.where(g == 0, v_t[0:HEAD_DIM], v_t[HEAD_DIM:2 * HEAD_DIM])
                dst[0:HEAD_DIM, c * t_chunk:(c + 1) * t_chunk] = mine.astype(BF16)

    tiles = range(nt)
    lane = lax.broadcasted_iota(jnp.int32, (1, LANES), 1)
    width = NSA_WINDOW + tq
    blocks_per_chunk = tk // SLC_LEN
    n_w = width // SLC_LEN
    jb = lax.broadcasted_iota(jnp.int32, (n_slc, 1), 0)
    jbf = jb.astype(F32)
    q0s = [(i * nt + t) * tq for t in tiles]
    backs = [jnp.minimum(q0, NSA_WINDOW) for q0 in q0s]
    wss = [pl.multiple_of(q0 - back, tq) for q0, back in zip(q0s, backs)]
    qss = [pl.multiple_of(q0, tq) for q0 in q0s]

    q_augs = []
    for t in tiles:
        slabs = []
        for r in range(NSA_REP):
            sl = jnp.where(g == 0, NSA_SLOPES[r], NSA_SLOPES[NSA_REP + r]) * LOG2E
            meta = jnp.zeros((1, LANES), F32)
            for c, coef in enumerate((SLC_LEN * sl, sl, -sl * q0s[t].astype(F32))):
                terms = _split3(jnp.full((1, LANES), coef, F32))
                for u in range(3):
                    meta = jnp.where(lane == HEAD_DIM + 3 * c + u, terms[u].astype(F32), meta)
            q2 = q_ref[t * tq:(t + 1) * tq, (r // 2) * LANES:(r // 2 + 1) * LANES]
            slabs.append(_low_half(q2, r % 2) + meta.astype(BF16))
        q_augs.append(jnp.concatenate(slabs, axis=0))

    s_alls = [_dot_nt(jnp.concatenate([kc_ref[...], kw_ref[pl.ds(wss[t], width), :],
                                       ks_ref[pl.ds(qss[t], tq), :]], axis=0), q_augs[t]) for t in tiles]

    s_cs = [s_alls[t][0:n_ck] + cmpmask_ref[pl.ds(pl.multiple_of(
        (n_ck - 8) - (tq // CMP_STRIDE) * (i * nt + t), 8), n_ck), :] for t in tiles]
    m_cs = [jnp.max(s, axis=0, keepdims=True) for s in s_cs]
    p_cs = [jnp.exp2(s - m) for s, m in zip(s_cs, m_cs)]
    den_cs = [jnp.sum(p, axis=0, keepdims=True) for p in p_cs]
    p_cs = [p * jnp.where(m > 0.5 * NEG, 1.0 / den, 0.0) for p, m, den in zip(p_cs, m_cs, den_cs)]
    o_cmps = [_dot(vct_ref[0:HEAD_DIM, :], p.astype(BF16)) for p in p_cs]

    ovt = ovt_ref[...]
    imps = []
    for p in p_cs:
        psum = p[:, 0:tq]
        for r in range(1, NSA_REP):
            psum = psum + p[:, r * tq:(r + 1) * tq]
        hi, mid, _ = _split3(psum)
        imps.append(_dot(ovt, hi) + _dot(ovt, mid))
    vals, sels = [], []
    for t in tiles:
        pos = q0s[t] + lax.broadcasted_iota(jnp.int32, (1, tq), 1)
        own = pos // SLC_LEN
        forced = (jb == 0) | (jb == own) | (jb == own - 1)
        valid = jb * SLC_LEN <= pos
        vals.append(jnp.where(valid & jnp.logical_not(forced), imps[t], -jnp.inf))
        sels.append(jnp.where(forced, 1.0, 0.0))

    s_ws = [s_alls[t][n_ck:n_ck + width]
            + band_ref[pl.ds(pl.multiple_of(NSA_WINDOW - backs[t], tq), width), :] for t in tiles]
    m_ws = [jnp.max(s, axis=0, keepdims=True) for s in s_ws]
    s_ds = [s_alls[t][n_ck + width:n_ck + width + tq] + band_ref[NSA_WINDOW:NSA_WINDOW + tq, :] for t in tiles]
    m_ds = [jnp.max(s, axis=0, keepdims=True) for s in s_ds]
    slices = [([(s_ws[t], m_ws[t], r0) for r0 in range(0, width, SLC_LEN)]
               + [(s_ds[t], m_ds[t], r0) for r0 in range(0, tq, SLC_LEN)]) for t in tiles]
    probs = [[] for _ in tiles]

    for it in range(max(N_SEL - 3, len(slices[0]))):
        for t in tiles:
            if it < N_SEL - 3:
                mx = jnp.max(vals[t], axis=0, keepdims=True)
                cand = jnp.where(vals[t] == mx, jbf, 1e9)
                idx = jnp.min(cand, axis=0, keepdims=True)
                pick = (jbf == idx) & (mx > -jnp.inf)
                sels[t] = jnp.where(pick, 1.0, sels[t])
                vals[t] = jnp.where(pick, -jnp.inf, vals[t])
            if it < len(slices[t]):
                src, m_src, r0 = slices[t][it]
                probs[t].append(_exp2_bf16(src[r0:r0 + SLC_LEN] - m_src))
    o_wins = []
    for t in tiles:
        acc_w = _dot(vwt_ref[:, pl.ds(wss[t], width)], jnp.concatenate(probs[t][:n_w], axis=0))
        o_wins.append(acc_w[0:HEAD_DIM] * (1.0 / acc_w[HEAD_DIM:HEAD_DIM + 1]))
        m_scr[t] = m_ds[t]
        acc_scr[t] = _dot(vst_ref[:, pl.ds(qss[t], tq)], jnp.concatenate(probs[t][n_w:], axis=0))

    n_needs = []
    for t in tiles:
        sbias = jnp.where((sels[t] > 0.5) & (jb < q0s[t] // SLC_LEN), 0.0, NEG)
        selbias_scr[t] = jnp.concatenate([sbias] * NSA_REP, axis=1)
        for c in range(n_slc // blocks_per_chunk):
            chunk_max = jnp.max(sbias[c * blocks_per_chunk:(c + 1) * blocks_per_chunk])
            need_smem[t, c] = (chunk_max > 0.5 * NEG).astype(jnp.int32)
    for t in tiles:
        n_need = jnp.int32(0)
        for k in range(n_slc // blocks_per_chunk + 1):
            list_smem[t, k] = jnp.int32(0)
        for c in range(n_slc // blocks_per_chunk):
            list_smem[t, n_need] = jnp.int32(c)
            n_need = n_need + need_smem[t, c]
        n_needs.append(n_need)

    def entry(t, k):
        live = k < n_needs[t]
        c = jnp.where(live, list_smem[t, k], 0)
        return c, jnp.where(live, 0.0, NEG)

    def masked(s_c, t, c, pad):
        parts = []
        for jj in range(blocks_per_chunk):
            brow = selbias_scr[t, pl.ds(c * blocks_per_chunk + jj, 1), :] + pad
            parts.append(s_c[jj * SLC_LEN:(jj + 1) * SLC_LEN] + brow)
        return jnp.concatenate(parts, axis=0)

    for ta in range(0, nt, 4):
        group = list(range(ta, min(ta + 4, nt)))

        def joint_body(j, carry, group=group):
            ents = [(entry(t, 2 * j), entry(t, 2 * j + 1)) for t in group]
            offs = [(pl.multiple_of(ea[0] * tk, tk), pl.multiple_of(eb[0] * tk, tk)) for ea, eb in ents]
            scores = [_dot_nt(jnp.concatenate([ks_ref[pl.ds(ka, tk), :], ks_ref[pl.ds(kb, tk), :]], axis=0),
                              q_augs[t]) for t, (ka, kb) in zip(group, offs)]
            scores = [jnp.concatenate([masked(s[0:tk], t, ea[0], ea[1]), masked(s[tk:2 * tk], t, eb[0], eb[1])],
                                      axis=0) for s, t, (ea, eb) in zip(scores, group, ents)]
            m_old = [m_scr[t] for t in group]
            m_new = [jnp.maximum(m, jnp.max(s, axis=0, keepdims=True)) for m, s in zip(m_old, scores)]
            probs = [_exp2_bf16(s - m) for s, m in zip(scores, m_new)]
            for t, (ka, kb), p, mo, mn in zip(group, offs, probs, m_old, m_new):
                v_ab = jnp.concatenate([vst_ref[:, pl.ds(ka, tk)], vst_ref[:, pl.ds(kb, tk)]], axis=1)
                acc_scr[t] = jnp.exp2(mo - mn) * acc_scr[t] + _dot(v_ab, p)
                m_scr[t] = mn
            return carry

        trips = functools.reduce(jnp.maximum, [(n_needs[t] + 1) // 2 for t in group])
        lax.fori_loop(0, trips, joint_body, 0)

    for t in tiles:
        acc = acc_scr[t]
        o_slc = acc[0:HEAD_DIM] * (1.0 / acc[HEAD_DIM:HEAD_DIM + 1])
        gate_t = _sigmoid(gate_ref[t * tq:(t + 1) * tq, :].T)
        outs = []
        for r in range(NSA_REP):
            cols = slice(r * tq, (r + 1) * tq)

            def gate_row(branch):
                lo_row = 3 * r + branch
                hi_row = 3 * (NSA_REP + r) + branch
                return jnp.where(g == 0, gate_t[lo_row:lo_row + 1], gate_t[hi_row:hi_row + 1])

            outs.append(gate_row(0) * o_cmps[t][:, cols] + gate_row(1) * o_slc[:, cols]
                        + gate_row(2) * o_wins[t][:, cols])
        o_ref[t * tq:(t + 1) * tq, :] = jnp.concatenate(outs, axis=0).T.astype(BF16)


def _nsa(q_pad, gates, kc, vct, kv_pad, kcols, vtail, band, cmpmask, ovt, *, B, S, tq=NSA_TQ, tk=512,
         nt=NSA_TILES_PER_STEP):
    G = NSA_GROUPS
    n_ck = kc.shape[1]
    n_slc = ovt.shape[0]
    rows = NSA_REP * tq
    nq = S // (tq * nt)
    n_chunks = n_slc * SLC_LEN // tk
    per_bg = lambda shape: pl.BlockSpec((None,) + shape, lambda b, g, i: (b * G + g, 0, 0))
    full = lambda a: pl.BlockSpec(a.shape, lambda b, g, i: (0,) * a.ndim)
    kv_kind = lambda kind: pl.BlockSpec((S, LANES), lambda b, g, i: (b, kind))
    return pl.pallas_call(
        functools.partial(_nsa_kernel, tq=tq, tk=tk, nt=nt),
        grid=(B, G, nq),
        in_specs=[
            pl.BlockSpec((nt * tq, NSA_REP * HEAD_DIM), lambda b, g, i: (b * nq + i, g)),
            pl.BlockSpec((nt * tq, LANES), lambda b, g, i: (b * nq + i, 0)),
            per_bg((n_ck, LANES)), per_bg((VT_ROWS, n_ck)),
            kv_kind(2), kv_kind(3), kv_kind(4), kv_kind(5),
            full(kcols), full(vtail), full(band), full(cmpmask), full(ovt),
        ],
        out_specs=pl.BlockSpec((nt * tq, NSA_REP * HEAD_DIM), lambda b, g, i: (b * nq + i, g)),
        out_shape=jax.ShapeDtypeStruct((B * S, Q_NSA), BF16),
        scratch_shapes=[pltpu.VMEM((nt, n_slc, rows), F32), pltpu.VMEM((nt, 1, rows), F32),
                        pltpu.VMEM((nt, VT_ROWS, rows), F32),
                        pltpu.VMEM((S, LANES), BF16), pltpu.VMEM((VT_ROWS, S), BF16),
                        pltpu.VMEM((S, LANES), BF16), pltpu.VMEM((VT_ROWS, S), BF16),
                        pltpu.SMEM((nt, n_chunks), jnp.int32), pltpu.SMEM((nt, n_chunks + 1), jnp.int32)],
        compiler_params=_params(("arbitrary", "arbitrary", "arbitrary")),
        name="nsa",
    )(q_pad, gates, kc, vct, kv_pad, kv_pad, kv_pad, kv_pad, kcols, vtail, band, cmpmask, ovt)


def _dil_attn_kernel(q_ref, k_ref, v_ref, dist_ref, o_ref, lse_ref, vt_scr, *, tq, ts, width, max_back, slopes):
    i = pl.program_id(1)
    nb, L, _ = k_ref.shape
    t_chunk = min(L, 512)

    @pl.when(i == 0)
    def _():
        tail_row = lax.broadcasted_iota(jnp.int32, (VT_ROWS - HEAD_DIM, L), 0)
        tail = jnp.where(tail_row == 0, 1.0, 0.0).astype(BF16)
        for b in range(nb):
            for h in range(DIL_HPG):
                vt_scr[b, h * VT_ROWS + HEAD_DIM:(h + 1) * VT_ROWS, :] = tail
            for pair in range(DIL_OUT // LANES):
                for c in range(L // t_chunk):
                    cols = slice(c * t_chunk, (c + 1) * t_chunk)
                    v_t = v_ref[b, cols, pair * LANES:(pair + 1) * LANES].astype(F32).T.astype(BF16)
                    for half in range(2):
                        h = 2 * pair + half
                        vt_scr[b, h * VT_ROWS:h * VT_ROWS + HEAD_DIM, cols] = v_t[half * HEAD_DIM:(half + 1) * HEAD_DIM]

    n_sub = tq // ts
    backs = [jnp.minimum(i * tq + u * ts, max_back) for u in range(n_sub)]
    kss = [pl.multiple_of(i * tq + u * ts - backs[u], max_back) for u in range(n_sub)]
    dists = [dist_ref[pl.ds(pl.multiple_of(max_back - backs[u], max_back), width), :] for u in range(n_sub)]
    lane = lax.broadcasted_iota(jnp.int32, (1, LANES), 1)
    chains = [(b, u, h) for b in range(nb) for u in range(n_sub) for h in range(DIL_HPG)]
    scores = []
    for b, u, h in chains:
        pair, half = divmod(h, 2)
        in_head = (lane >= half * HEAD_DIM) & (lane < (half + 1) * HEAD_DIM)
        q = q_ref[b, u * ts:(u + 1) * ts, pair * LANES:(pair + 1) * LANES]
        q = jnp.where(in_head, q, jnp.zeros_like(q))
        k = k_ref[b, pl.ds(kss[u], width), pair * LANES:(pair + 1) * LANES]
        scores.append(_dot_nt(k, q) - (slopes[h] * LOG2E) * dists[u])
    maxes = [jnp.max(s_t, axis=0, keepdims=True) for s_t in scores]
    probs = [_exp2_bf16(s_t - m) for s_t, m in zip(scores, maxes)]
    accs = [_dot(vt_scr[b, h * VT_ROWS:(h + 1) * VT_ROWS, pl.ds(kss[u], width)], p)
            for (b, u, h), p in zip(chains, probs)]
    dens = [acc[HEAD_DIM:HEAD_DIM + 1] for acc in accs]
    outs = [acc[0:HEAD_DIM] * (1.0 / den) for acc, den in zip(accs, dens)]
    lses = [jnp.broadcast_to((m + jnp.log2(den)) * (1.0 / LOG2E), (HEAD_DIM, ts)) for m, den in zip(maxes, dens)]
    for n, (b, u) in enumerate((b, u) for b in range(nb) for u in range(n_sub)):
        rows = slice(u * ts, (u + 1) * ts)
        o_ref[b, rows, :] = jnp.concatenate(outs[n * DIL_HPG:(n + 1) * DIL_HPG], axis=0).T
        lse_ref[b, rows, :] = jnp.concatenate(lses[n * DIL_HPG:(n + 1) * DIL_HPG], axis=0).T


def _dil_attn(qkv, dist_tbl, *, max_back, slopes, tq, ts, nb):
    N, L, _ = qkv.shape
    width = dist_tbl.shape[0] - max_back
    kern = functools.partial(_dil_attn_kernel, tq=tq, ts=ts, width=width, max_back=max_back, slopes=slopes)
    return pl.pallas_call(
        kern,
        grid=(N // nb, L // tq),
        in_specs=[
            pl.BlockSpec((nb, tq, DIL_OUT), lambda n, i: (n, i, 0)),
            pl.BlockSpec((nb, L, DIL_OUT), lambda n, i: (n, 0, 1)),
            pl.BlockSpec((nb, L, DIL_OUT), lambda n, i: (n, 0, 2)),
            pl.BlockSpec(dist_tbl.shape, lambda n, i: (0, 0)),
        ],
        out_specs=[
            pl.BlockSpec((nb, tq, DIL_OUT), lambda n, i: (n, i, 0)),
            pl.BlockSpec((nb, tq, DIL_OUT), lambda n, i: (n, i, 0)),
        ],
        out_shape=[
            jax.ShapeDtypeStruct((N, L, DIL_OUT), F32),
            jax.ShapeDtypeStruct((N, L, DIL_OUT), F32),
        ],
        scratch_shapes=[pltpu.VMEM((nb, DIL_HPG * VT_ROWS, L), BF16)],
        compiler_params=_params(("parallel", "arbitrary")),
        name="dil_attn",
    )(qkv, qkv, qkv, dist_tbl)


def _final_kernel(x_ref, pre_ref, post_ref, onsa_ref, od0_ref, od1_ref, od2_ref, l0_ref, l1_ref, l2_ref,
                  wmerge_ref, wnsa_ref, wdil_ref, wout_ref, o_ref, scr):
    tm = x_ref.shape[0]
    x = x_ref[...]
    h = _rms(x, pre_ref[...]).astype(BF16)
    merge = _dot(h, wmerge_ref[...])
    g_a = _sigmoid(merge[:, :D_MODEL])
    g_b = _sigmoid(merge[:, D_MODEL:])
    y_nsa = _dot(onsa_ref[...], wnsa_ref[...])

    n_blk = DIL_OUT // LANES
    slot = 0
    token_major = []
    for (_, dil), o_in, l_in in zip(DIL_CONFIGS, (od0_ref, od1_ref, od2_ref), (l0_ref, l1_ref, l2_ref)):
        pair = []
        for src in (o_in, l_in):
            if dil == 1:
                pair.append(src[0])
                continue
            for r in range(dil):
                for c in range(n_blk):
                    scr[slot + c, pl.ds(r, tm // dil, stride=dil), :] = src[r, :, c * LANES:(c + 1) * LANES]
            pair.append(jnp.concatenate([scr[slot + c] for c in range(n_blk)], axis=1))
            slot += n_blk
        token_major.append(pair)
    (o0, l0), (o1, l1), (o2, l2) = token_major
    mx = jnp.maximum(jnp.maximum(l0, l1), l2)
    e0, e1, e2 = jnp.exp(l0 - mx), jnp.exp(l1 - mx), jnp.exp(l2 - mx)
    den = e0 + e1 + e2
    o_dil = (e0 / den) * o0 + (e1 / den) * o1 + (e2 / den) * o2
    y_dil = _dot(o_dil.astype(BF16), wdil_ref[...])

    y = g_a * y_nsa + g_b * y_dil
    z = _dot(y.astype(BF16), wout_ref[...])
    o_ref[...] = x + _rms(z, post_ref[...])


def _final(x2d, pre, post, onsa, od, ld, wmerge, wnsa, wdil, wout, *, S, tm=512):
    T = x2d.shape[0]
    nt = S // tm
    tok = lambda w: pl.BlockSpec((tm, w), lambda i: (i, 0))
    full = lambda a: pl.BlockSpec(a.shape, lambda i: (0,) * a.ndim)
    strided = [pl.BlockSpec((None, dil, tm // dil, DIL_OUT), lambda i: (i // nt, 0, i % nt, 0))
               for _, dil in DIL_CONFIGS]
    n_strided = sum(1 for _, dil in DIL_CONFIGS if dil > 1)
    return pl.pallas_call(
        _final_kernel,
        grid=(T // tm,),
        in_specs=[tok(D_MODEL), _resident(pre), _resident(post), tok(Q_NSA)] + strided + strided
        + [_resident(wmerge), _resident(wnsa), _resident(wdil), _resident(wout)],
        out_specs=tok(D_MODEL),
        out_shape=jax.ShapeDtypeStruct((T, D_MODEL), F32),
        scratch_shapes=[pltpu.VMEM((2 * n_strided * (DIL_OUT // LANES), tm, LANES), F32)],
        compiler_params=_params(("parallel",)),
        name="final",
    )(x2d, pre, post, onsa, od[0], od[1], od[2], ld[0], ld[1], ld[2], wmerge, wnsa, wdil, wout)


def _overlap_t(n_ck, n_slc):
    c0 = np.arange(n_ck)[None, :] * CMP_STRIDE
    s0 = np.arange(n_slc)[:, None] * SLC_LEN
    ov = np.clip(np.minimum(c0 + CMP_LEN, s0 + SLC_LEN) - np.maximum(c0, s0), 0, None) / CMP_LEN
    ov[:, n_ck - 1] = 0.0
    return ov.astype(np.float32)


def _pos_columns(pos):
    cols = np.zeros((len(pos), LANES), np.float32)
    for t, vals in enumerate((pos // SLC_LEN, pos % SLC_LEN, np.ones(len(pos)))):
        for u in range(3):
            cols[:, HEAD_DIM + 3 * t + u] = vals
    return cols


def _ones_row_tail(n):
    tail = np.zeros((VT_ROWS, n), np.float32)
    tail[HEAD_DIM] = 1.0
    return tail


def _band_table(tq):
    u = np.arange(2 * NSA_WINDOW + tq)[:, None]
    r = (np.arange(NSA_REP * tq) % tq)[None, :]
    dist = NSA_WINDOW + r - u
    return np.where((dist >= 0) & (dist <= NSA_WINDOW - 1), 0.0, NEG).astype(np.float32)


def _dil_dist_table(tq, width, max_back, dil):
    u = np.arange(width + max_back)[:, None]
    r = np.arange(tq)[None, :]
    dist = max_back + r - u
    return np.where((dist >= 0) & (dist <= max_back), dist * dil, -NEG).astype(np.float32)


def _cmp_mask_table(n_ck, tq):
    u = np.arange(2 * n_ck - 8)[:, None]
    r = (np.arange(NSA_REP * tq) % tq)[None, :]
    end = (u - (n_ck - 8)) * CMP_STRIDE + CMP_LEN - 1
    return np.where(end <= r, 0.0, NEG).astype(np.float32)


def _mixer(x2d, B, S, mix_pre, mix_post, w_in, pe_k, w_ck1, w_ck2, pe_v, w_cv1, w_cv2,
           w_nsa_o, w_dil_o, w_mix_out):
    G = NSA_GROUPS
    BG = B * G
    assert S % (16 * 128) == 0 and NSA_TQ // CMP_STRIDE == 8
    o_q, o_kv, o_g, o_d, o_m = np.cumsum([0, Q_NSA, 6 * KV_NSA, GATE_NSA, 3 * QKV_DIL]).tolist()
    wq = w_in[:, o_q:o_kv] * (SCALE * LOG2E)
    w_gate = jnp.pad(w_in[:, o_g:o_d], ((0, 0), (0, LANES - GATE_NSA)))
    wd = w_in[:, o_d:o_m]
    q_pad, kv_pad, gates, xk, xv, *qkv_d = _proj(
        x2d, mix_pre[None, :], wq.astype(BF16), w_in[:, o_kv:o_g].astype(BF16), wd.astype(BF16),
        w_gate.astype(BF16), B=B, S=S)

    n_ck = S // CMP_STRIDE
    half = CMP_STRIDE * HEAD_DIM
    w1 = jnp.stack([w_ck1, w_cv1])
    w1cat = jnp.concatenate([w1[:, :half], w1[:, half:]], axis=2).astype(BF16)
    pe = jnp.stack([pe_k.reshape(1, -1), pe_v.reshape(1, -1)]).astype(BF16)
    w2 = jnp.pad(jnp.stack([w_ck2, w_cv2]), ((0, 0), (0, 0), (0, LANES - HEAD_DIM))).astype(BF16)
    cmp_pos = np.arange(n_ck) * CMP_STRIDE + CMP_LEN - 1
    kc, vct = _cmp_mlp(xk.reshape(BG, n_ck, half), xv.reshape(BG, n_ck, half), pe, w1cat, w1.astype(BF16), w2,
                       jnp.asarray(_pos_columns(cmp_pos), BF16), jnp.asarray(_ones_row_tail(n_ck), BF16))

    band = jnp.asarray(_band_table(NSA_TQ))
    cmpmask = jnp.asarray(_cmp_mask_table(n_ck, NSA_TQ))
    ovt = jnp.asarray(_overlap_t(n_ck, S // SLC_LEN), BF16)
    o_nsa = _nsa(q_pad, gates, kc, vct, kv_pad, jnp.asarray(_pos_columns(np.arange(S)), BF16),
                 jnp.asarray(_ones_row_tail(S), BF16), band, cmpmask, ovt, B=B, S=S)

    od, ld = [], []
    for gi, (window, dil) in enumerate(DIL_CONFIGS):
        Ls = S // dil
        max_back = window // dil
        tq = min(DIL_TQ, Ls)
        ts = min(DIL_TS, Ls)
        width = ts + max_back if Ls > ts else ts
        o, lse = _dil_attn(qkv_d[gi].reshape(B * dil, Ls, DIL_QKV),
                           jnp.asarray(_dil_dist_table(ts, width, max_back, dil)),
                           max_back=max_back, slopes=DIL_SLOPES[gi * DIL_HPG:(gi + 1) * DIL_HPG], tq=tq, ts=ts,
                           nb=DIL_TQ // tq)
        od.append(o.reshape(B, dil, Ls, DIL_OUT))
        ld.append(lse.reshape(B, dil, Ls, DIL_OUT))

    return _final(x2d, mix_pre[None, :], mix_post[None, :], o_nsa, od, ld, w_in[:, o_m:].astype(BF16),
                  w_nsa_o.astype(BF16), w_dil_o.astype(BF16), w_mix_out.astype(BF16), S=S)


def kernel(x, ffn1_pre, ffn1_post, ffn1_w_gate, ffn1_w_up, ffn1_w_down, mix_pre, mix_post, w_in, nsa_pe_k, nsa_w_ck1, nsa_w_ck2, nsa_pe_v, nsa_w_cv1, nsa_w_cv2, w_nsa_o, w_dil_o, w_mix_out, ffn2_pre, ffn2_post, ffn2_w_gate, ffn2_w_up, ffn2_w_down):
    B, S, _ = x.shape
    x2d = x.reshape(B * S, D_MODEL)
    for l in range(ffn1_pre.shape[0]):
        x2d = _ffn(x2d, ffn1_pre[l][None, :], ffn1_post[l][None, :], ffn1_w_gate[l].astype(BF16),
                   ffn1_w_up[l].astype(BF16), ffn1_w_down[l].astype(BF16))
        x2d = _mixer(x2d, B, S, mix_pre[l], mix_post[l], w_in[l], nsa_pe_k[l], nsa_w_ck1[l], nsa_w_ck2[l],
                     nsa_pe_v[l], nsa_w_cv1[l], nsa_w_cv2[l], w_nsa_o[l], w_dil_o[l], w_mix_out[l])
        x2d = _ffn(x2d, ffn2_pre[l][None, :], ffn2_post[l][None, :], ffn2_w_gate[l].astype(BF16),
                   ffn2_w_up[l].astype(BF16), ffn2_w_down[l].astype(BF16))
    return x2d.reshape(B, S, D_MODEL)
```

```python
import functools
import math

import numpy as np
import jax
import jax.numpy as jnp
from jax import lax
from jax.experimental import pallas as pl
from jax.experimental.pallas import tpu as pltpu

F32 = jnp.float32
BF16 = jnp.bfloat16

D_MODEL = 1024
HEAD_DIM = 64
RMS_EPS = 1e-6
NSA_HEADS = 8
NSA_GROUPS = 2
NSA_REP = NSA_HEADS // NSA_GROUPS
CMP_LEN = 32
CMP_STRIDE = 16
CMP_HIDDEN = 256
SLC_LEN = 64
N_SEL = 16
NSA_WINDOW = 512
DIL_CONFIGS = ((128, 1), (512, 4), (2048, 16))
DIL_HPG = 4
DIL_HEADS = DIL_HPG * len(DIL_CONFIGS)
D_FF = 2816
Q_NSA = NSA_HEADS * HEAD_DIM
KV_NSA = NSA_GROUPS * HEAD_DIM
GATE_NSA = NSA_HEADS * 3
QKV_DIL = DIL_HEADS * HEAD_DIM
DIL_OUT = DIL_HPG * HEAD_DIM
DIL_QKV = 3 * DIL_OUT
DIL_TQ = 1024
DIL_TS = 128
LANES = 128
NEG = -1e30
SCALE = HEAD_DIM ** -0.5
LOG2E = math.log2(math.e)
VMEM_LIMIT = 56 * 1024 * 1024
NSA_TQ = 128
NSA_TILES_PER_STEP = 4
VT_ROWS = 80


def _slopes(n):
    return [float(np.float32(2.0 ** (-8.0 * (i + 1) / n))) for i in range(n)]


NSA_SLOPES = _slopes(NSA_HEADS)
DIL_SLOPES = _slopes(DIL_HEADS)


def _params(sem):
    return pltpu.CompilerParams(dimension_semantics=sem, vmem_limit_bytes=VMEM_LIMIT)


def _rms(x, g):
    ms = jnp.mean(x * x, axis=-1, keepdims=True)
    return x * lax.rsqrt(ms + RMS_EPS) * g


def _sigmoid(x):
    return 1.0 / (1.0 + jnp.exp(-x))


def _dot(a, b):
    return jnp.dot(a, b, preferred_element_type=F32)


def _dot_nt(a, b):
    return lax.dot_general(a, b, (((1,), (1,)), ((), ())), preferred_element_type=F32)


def _split3(a):
    hi = a.astype(BF16)
    r1 = a - hi.astype(F32)
    mid = r1.astype(BF16)
    lo = (r1 - mid.astype(F32)).astype(BF16)
    return hi, mid, lo


def _ffn_kernel(x_ref, pre_ref, post_ref, wg_ref, wu_ref, wd_ref, o_ref):
    x = x_ref[...]
    h = _rms(x, pre_ref[...]).astype(BF16)
    g = _dot(h, wg_ref[...])
    u = _dot(h, wu_ref[...])
    a = (g * _sigmoid(g) * u).astype(BF16)
    o_ref[...] = x + 0.5 * _rms(_dot(a, wd_ref[...]), post_ref[...])


def _resident(a):
    return pl.BlockSpec(a.shape, lambda *_: (0,) * a.ndim, pipeline_mode=pl.Buffered(1))


def _ffn(x2d, pre, post, wg, wu, wd, *, tm=512):
    T = x2d.shape[0]
    return pl.pallas_call(
        _ffn_kernel,
        grid=(T // tm,),
        in_specs=[pl.BlockSpec((tm, D_MODEL), lambda i: (i, 0)), _resident(pre), _resident(post),
                  _resident(wg), _resident(wu), _resident(wd)],
        out_specs=pl.BlockSpec((tm, D_MODEL), lambda i: (i, 0)),
        out_shape=jax.ShapeDtypeStruct((T, D_MODEL), F32),
        compiler_params=_params(("parallel",)),
        name="ffn",
    )(x2d, pre, post, wg, wu, wd)


def _proj_kernel(x_ref, pre_ref, wq_ref, wkv_ref, wd_ref, wgate_ref, q_ref, kv_ref, g_ref, xk_ref, xv_ref,
                 *rest):
    d_refs, scr = rest[:-1], rest[-1]
    tm = x_ref.shape[0]
    h = _rms(x_ref[...], pre_ref[...]).astype(BF16)
    q_ref[...] = _dot(h, wq_ref[...]).astype(BF16)
    kv = _dot(h, wkv_ref[...])
    kv_ref[...] = kv.astype(BF16)
    g_ref[...] = _dot(h, wgate_ref[...])
    lane = lax.broadcasted_iota(jnp.int32, (1, LANES), 1)
    n_rows = tm // CMP_STRIDE
    for kind, out in ((0, xk_ref), (1, xv_ref)):
        scr[kind] = kv[:, kind * LANES:(kind + 1) * LANES]
        for j in range(CMP_STRIDE // 2):
            even = scr[kind, pl.ds(2 * j, n_rows, stride=CMP_STRIDE), :]
            odd = scr[kind, pl.ds(2 * j + 1, n_rows, stride=CMP_STRIDE), :]
            out[0, :, j * LANES:(j + 1) * LANES] = jnp.where(
                lane < HEAD_DIM, even, pltpu.roll(odd, HEAD_DIM, 1)).astype(BF16)
            out[1, :, j * LANES:(j + 1) * LANES] = jnp.where(
                lane < HEAD_DIM, pltpu.roll(even, HEAD_DIM, 1), odd).astype(BF16)
    n_blk = DIL_QKV // LANES
    for gi, (_, dil) in enumerate(DIL_CONFIGS):
        parts = [_dot(h, wd_ref[:, t * QKV_DIL + gi * DIL_OUT:t * QKV_DIL + (gi + 1) * DIL_OUT]) for t in range(3)]
        res = jnp.concatenate([parts[0] * (SCALE * LOG2E), parts[1], parts[2]], axis=1)
        if dil == 1:
            d_refs[gi][0] = res.astype(BF16)
            continue
        for c in range(n_blk):
            scr[c] = res[:, c * LANES:(c + 1) * LANES]
        for r in range(dil):
            for c in range(n_blk):
                rows = scr[c, pl.ds(r, tm // dil, stride=dil), :]
                d_refs[gi][r, :, c * LANES:(c + 1) * LANES] = rows.astype(BF16)


def _proj(x2d, pre, wq, wkv, wd, wgate, *, B, S, tm=512):
    T = x2d.shape[0]
    nt = S // tm
    tok = lambda w: pl.BlockSpec((tm, w), lambda i: (i, 0))
    full = lambda a: pl.BlockSpec(a.shape, lambda i: (0,) * a.ndim)
    d_specs = [pl.BlockSpec((None, dil, tm // dil, DIL_QKV), lambda i: (i // nt, 0, i % nt, 0))
               for _, dil in DIL_CONFIGS]
    d_shapes = [jax.ShapeDtypeStruct((B, dil, S // dil, DIL_QKV), BF16) for _, dil in DIL_CONFIGS]
    chunk_width = CMP_STRIDE * HEAD_DIM
    c_spec = pl.BlockSpec((None, NSA_GROUPS, tm // CMP_STRIDE, chunk_width), lambda i: (i // nt, 0, i % nt, 0))
    c_shape = jax.ShapeDtypeStruct((B, NSA_GROUPS, S // CMP_STRIDE, chunk_width), BF16)
    return pl.pallas_call(
        _proj_kernel,
        grid=(T // tm,),
        in_specs=[tok(D_MODEL), _resident(pre), _resident(wq), _resident(wkv), _resident(wd), _resident(wgate)],
        out_specs=[tok(wq.shape[1]), tok(wkv.shape[1]), tok(LANES), c_spec, c_spec] + d_specs,
        out_shape=[jax.ShapeDtypeStruct((T, wq.shape[1]), BF16), jax.ShapeDtypeStruct((T, wkv.shape[1]), BF16),
                   jax.ShapeDtypeStruct((T, LANES), F32), c_shape, c_shape] + d_shapes,
        scratch_shapes=[pltpu.VMEM((DIL_QKV // LANES, tm, LANES), F32)],
        compiler_params=_params(("parallel",)),
        name="proj",
    )(x2d, pre, wq, wkv, wd, wgate)


def _gelu_tanh(x):
    c = math.sqrt(2.0 / math.pi)
    return x * (0.5 * (1.0 + jnp.tanh(c * (x + 0.044715 * (x * x * x)))))


def _cmp_mlp_kernel(xk_ref, xv_ref, pe_ref, w1cat_ref, w1_ref, w2_ref, kcols_ref, vtail_ref, kc_ref, vct_ref):
    n_chunks = xk_ref.shape[0]

    def mlp(x_ref, t):
        ab = _dot(x_ref[...], w1cat_ref[t])
        first = ab[:, :CMP_HIDDEN]
        second = pltpu.roll(ab[:, CMP_HIDDEN:], n_chunks - 1, 0)
        pe8 = jnp.broadcast_to(pe_ref[t], (8, pe_ref.shape[2]))
        bias = _dot(pe8, w1_ref[t])[0:1]
        hid = _gelu_tanh(first + second + bias).astype(BF16)
        out = _dot(hid, w2_ref[t])
        row = lax.broadcasted_iota(jnp.int32, out.shape, 0)
        return jnp.where(row < n_chunks - 1, out, 0.0)

    kc_ref[...] = mlp(xk_ref, 0).astype(BF16) + kcols_ref[...]
    vct_ref[...] = mlp(xv_ref, 1).T[0:VT_ROWS].astype(BF16) + vtail_ref[...]


def _cmp_mlp(xk, xv, pe, w1cat, w1, w2, kcols, vtail):
    BG, n_chunks, width = xk.shape
    per_n = lambda shape: pl.BlockSpec((None,) + shape, lambda n: (n, 0, 0))
    full = lambda a: pl.BlockSpec(a.shape, lambda n: (0,) * a.ndim)
    return pl.pallas_call(
        _cmp_mlp_kernel,
        grid=(BG,),
        in_specs=[per_n((n_chunks, width)), per_n((n_chunks, width)), full(pe), full(w1cat), full(w1),
                  full(w2), full(kcols), full(vtail)],
        out_specs=[per_n((n_chunks, LANES)), per_n((VT_ROWS, n_chunks))],
        out_shape=[jax.ShapeDtypeStruct((BG, n_chunks, LANES), BF16),
                   jax.ShapeDtypeStruct((BG, VT_ROWS, n_chunks), BF16)],
        compiler_params=_params(("parallel",)),
        name="cmp_mlp",
    )(xk, xv, pe, w1cat, w1, w2, kcols, vtail)


def _exp2_bf16(x):
    return jnp.exp2(x.astype(BF16))


def _low_half(x2, half):
    lane = lax.broadcasted_iota(jnp.int32, (1, LANES), 1)
    xf = x2.astype(F32)
    if half == 1:
        xf = pltpu.roll(xf, HEAD_DIM, 1)
    return jnp.where(lane < HEAD_DIM, xf, 0.0).astype(BF16)


def _nsa_kernel(q_ref, gate_ref, kc_ref, vct_ref, kslc_ref, vslc_ref, kwin_ref, vwin_ref, kcols_ref,
                vtail_ref, band_ref, cmpmask_ref, ovt_ref, o_ref,
                selbias_scr, m_scr, acc_scr, ks_ref, vst_ref, kw_ref, vwt_ref, need_smem, list_smem,
                *, tq, tk, nt):
    g = pl.program_id(1)
    i = pl.program_id(2)
    n_ck = kc_ref.shape[0]
    n_slc = ovt_ref.shape[0]
    S = kslc_ref.shape[0]

    @pl.when(i == 0)
    def _():
        t_chunk = min(S, 512)
        for src, dst in ((kslc_ref, ks_ref), (kwin_ref, kw_ref)):
            for c in range(S // t_chunk):
                rows_c = slice(c * t_chunk, (c + 1) * t_chunk)
                k2 = src[rows_c, :]
                mine = jnp.where(g == 0, _low_half(k2, 0), _low_half(k2, 1))
                dst[rows_c, :] = mine + kcols_ref[rows_c, :]
        for src, dst in ((vslc_ref, vst_ref), (vwin_ref, vwt_ref)):
            dst[HEAD_DIM:VT_ROWS, :] = vtail_ref[HEAD_DIM:VT_ROWS, :]
            for c in range(S // t_chunk):
                v = src[c * t_chunk:(c + 1) * t_chunk, :].astype(F32)
                v_t = v.T
                mine = jnp.where(g == 0, v_t[0:HEAD_DIM], v_t[HEAD_DIM:2 * HEAD_DIM])
                dst[0:HEAD_DIM, c * t_chunk:(c + 1) * t_chunk] = mine.astype(BF16)

    tiles = range(nt)
    lane = lax.broadcasted_iota(jnp.int32, (1, LANES), 1)
    width = NSA_WINDOW + tq
    blocks_per_chunk = tk // SLC_LEN
    n_w = width // SLC_LEN
    jb = lax.broadcasted_iota(jnp.int32, (n_slc, 1), 0)
    jbf = jb.astype(F32)
    q0s = [(i * nt + t) * tq for t in tiles]
    backs = [jnp.minimum(q0, NSA_WINDOW) for q0 in q0s]
    wss = [pl.multiple_of(q0 - back, tq) for q0, back in zip(q0s, backs)]
    qss = [pl.multiple_of(q0, tq) for q0 in q0s]

    q_augs = []
    for t in tiles:
        slabs = []
        for r in range(NSA_REP):
            sl = jnp.where(g == 0, NSA_SLOPES[r], NSA_SLOPES[NSA_REP + r]) * LOG2E
            meta = jnp.zeros((1, LANES), F32)
            for c, coef in enumerate((SLC_LEN * sl, sl, -sl * q0s[t].astype(F32))):
                terms = _split3(jnp.full((1, LANES), coef, F32))
                for u in range(3):
                    meta = jnp.where(lane == HEAD_DIM + 3 * c + u, terms[u].astype(F32), meta)
            q2 = q_ref[t * tq:(t + 1) * tq, (r // 2) * LANES:(r // 2 + 1) * LANES]
            slabs.append(_low_half(q2, r % 2) + meta.astype(BF16))
        q_augs.append(jnp.concatenate(slabs, axis=0))

    s_alls = [_dot_nt(jnp.concatenate([kc_ref[...], kw_ref[pl.ds(wss[t], width), :],
                                       ks_ref[pl.ds(qss[t], tq), :]], axis=0), q_augs[t]) for t in tiles]

    s_cs = [s_alls[t][0:n_ck] + cmpmask_ref[pl.ds(pl.multiple_of(
        (n_ck - 8) - (tq // CMP_STRIDE) * (i * nt + t), 8), n_ck), :] for t in tiles]
    m_cs = [jnp.max(s, axis=0, keepdims=True) for s in s_cs]
    p_cs = [jnp.exp2(s - m) for s, m in zip(s_cs, m_cs)]
    den_cs = [jnp.sum(p, axis=0, keepdims=True) for p in p_cs]
    p_cs = [p * jnp.where(m > 0.5 * NEG, 1.0 / den, 0.0) for p, m, den in zip(p_cs, m_cs, den_cs)]
    o_cmps = [_dot(vct_ref[0:HEAD_DIM, :], p.astype(BF16)) for p in p_cs]

    ovt = ovt_ref[...]
    imps = []
    for p in p_cs:
        psum = p[:, 0:tq]
        for r in range(1, NSA_REP):
            psum = psum + p[:, r * tq:(r + 1) * tq]
        hi, mid, _ = _split3(psum)
        imps.append(_dot(ovt, hi) + _dot(ovt, mid))
    vals, sels = [], []
    for t in tiles:
        pos = q0s[t] + lax.broadcasted_iota(jnp.int32, (1, tq), 1)
        own = pos // SLC_LEN
        forced = (jb == 0) | (jb == own) | (jb == own - 1)
        valid = jb * SLC_LEN <= pos
        vals.append(jnp.where(valid & jnp.logical_not(forced), imps[t], -jnp.inf))
        sels.append(jnp.where(forced, 1.0, 0.0))

    s_ws = [s_alls[t][n_ck:n_ck + width]
            + band_ref[pl.ds(pl.multiple_of(NSA_WINDOW - backs[t], tq), width), :] for t in tiles]
    m_ws = [jnp.max(s, axis=0, keepdims=True) for s in s_ws]
    s_ds = [s_alls[t][n_ck + width:n_ck + width + tq] + band_ref[NSA_WINDOW:NSA_WINDOW + tq, :] for t in tiles]
    m_ds = [jnp.max(s, axis=0, keepdims=True) for s in s_ds]
    slices = [([(s_ws[t], m_ws[t], r0) for r0 in range(0, width, SLC_LEN)]
               + [(s_ds[t], m_ds[t], r0) for r0 in range(0, tq, SLC_LEN)]) for t in tiles]
    probs = [[] for _ in tiles]

    for it in range(max(N_SEL - 3, len(slices[0]))):
        for t in tiles:
            if it < N_SEL - 3:
                mx = jnp.max(vals[t], axis=0, keepdims=True)
                cand = jnp.where(vals[t] == mx, jbf, 1e9)
                idx = jnp.min(cand, axis=0, keepdims=True)
                pick = (jbf == idx) & (mx > -jnp.inf)
                sels[t] = jnp.where(pick, 1.0, sels[t])
                vals[t] = jnp.where(pick, -jnp.inf, vals[t])
            if it < len(slices[t]):
                src, m_src, r0 = slices[t][it]
                probs[t].append(_exp2_bf16(src[r0:r0 + SLC_LEN] - m_src))
    o_wins = []
    for t in tiles:
        acc_w = _dot(vwt_ref[:, pl.ds(wss[t], width)], jnp.concatenate(probs[t][:n_w], axis=0))
        o_wins.append(acc_w[0:HEAD_DIM] * (1.0 / acc_w[HEAD_DIM:HEAD_DIM + 1]))
        m_scr[t] = m_ds[t]
        acc_scr[t] = _dot(vst_ref[:, pl.ds(qss[t], tq)], jnp.concatenate(probs[t][n_w:], axis=0))

    n_needs = []
    for t in tiles:
        sbias = jnp.where((sels[t] > 0.5) & (jb < q0s[t] // SLC_LEN), 0.0, NEG)
        selbias_scr[t] = jnp.concatenate([sbias] * NSA_REP, axis=1)
        for c in range(n_slc // blocks_per_chunk):
            chunk_max = jnp.max(sbias[c * blocks_per_chunk:(c + 1) * blocks_per_chunk])
            need_smem[t, c] = (chunk_max > 0.5 * NEG).astype(jnp.int32)
    for t in tiles:
        n_need = jnp.int32(0)
        for k in range(n_slc // blocks_per_chunk + 1):
            list_smem[t, k] = jnp.int32(0)
        for c in range(n_slc // blocks_per_chunk):
            list_smem[t, n_need] = jnp.int32(c)
            n_need = n_need + need_smem[t, c]
        n_needs.append(n_need)

    def entry(t, k):
        live = k < n_needs[t]
        c = jnp.where(live, list_smem[t, k], 0)
        return c, jnp.where(live, 0.0, NEG)

    def masked(s_c, t, c, pad):
        parts = []
        for jj in range(blocks_per_chunk):
            brow = selbias_scr[t, pl.ds(c * blocks_per_chunk + jj, 1), :] + pad
            parts.append(s_c[jj * SLC_LEN:(jj + 1) * SLC_LEN] + brow)
        return jnp.concatenate(parts, axis=0)

    for ta in range(0, nt, 4):
        group = list(range(ta, min(ta + 4, nt)))

        def joint_body(j, carry, group=group):
            ents = [(entry(t, 2 * j), entry(t, 2 * j + 1)) for t in group]
            offs = [(pl.multiple_of(ea[0] * tk, tk), pl.multiple_of(eb[0] * tk, tk)) for ea, eb in ents]
            scores = [_dot_nt(jnp.concatenate([ks_ref[pl.ds(ka, tk), :], ks_ref[pl.ds(kb, tk), :]], axis=0),
                              q_augs[t]) for t, (ka, kb) in zip(group, offs)]
            scores = [jnp.concatenate([masked(s[0:tk], t, ea[0], ea[1]), masked(s[tk:2 * tk], t, eb[0], eb[1])],
                                      axis=0) for s, t, (ea, eb) in zip(scores, group, ents)]
            m_old = [m_scr[t] for t in group]
            m_new = [jnp.maximum(m, jnp.max(s, axis=0, keepdims=True)) for m, s in zip(m_old, scores)]
            probs = [_exp2_bf16(s - m) for s, m in zip(scores, m_new)]
            for t, (ka, kb), p, mo, mn in zip(group, offs, probs, m_old, m_new):
                v_ab = jnp.concatenate([vst_ref[:, pl.ds(ka, tk)], vst_ref[:, pl.ds(kb, tk)]], axis=1)
                acc_scr[t] = jnp.exp2(mo - mn) * acc_scr[t] + _dot(v_ab, p)
                m_scr[t] = mn
            return carry

        trips = functools.reduce(jnp.maximum, [(n_needs[t] + 1) // 2 for t in group])
        lax.fori_loop(0, trips, joint_body, 0)

    for t in tiles:
        acc = acc_scr[t]
        o_slc = acc[0:HEAD_DIM] * (1.0 / acc[HEAD_DIM:HEAD_DIM + 1])
        gate_t = _sigmoid(gate_ref[t * tq:(t + 1) * tq, :].T)
        outs = []
        for r in range(NSA_REP):
            cols = slice(r * tq, (r + 1) * tq)

            def gate_row(branch):
                lo_row = 3 * r + branch
                hi_row = 3 * (NSA_REP + r) + branch
                return jnp.where(g == 0, gate_t[lo_row:lo_row + 1], gate_t[hi_row:hi_row + 1])

            outs.append(gate_row(0) * o_cmps[t][:, cols] + gate_row(1) * o_slc[:, cols]
                        + gate_row(2) * o_wins[t][:, cols])
        o_ref[t * tq:(t + 1) * tq, :] = jnp.concatenate(outs, axis=0).T.astype(BF16)


def _nsa(q_pad, gates, kc, vct, kv_pad, kcols, vtail, band, cmpmask, ovt, *, B, S, tq=NSA_TQ, tk=512,
         nt=NSA_TILES_PER_STEP):
    G = NSA_GROUPS
    n_ck = kc.shape[1]
    n_slc = ovt.shape[0]
    rows = NSA_REP * tq
    nq = S // (tq * nt)
    n_chunks = n_slc * SLC_LEN // tk
    per_bg = lambda shape: pl.BlockSpec((None,) + shape, lambda b, g, i: (b * G + g, 0, 0))
    full = lambda a: pl.BlockSpec(a.shape, lambda b, g, i: (0,) * a.ndim)
    kv_kind = lambda kind: pl.BlockSpec((S, LANES), lambda b, g, i: (b, kind))
    return pl.pallas_call(
        functools.partial(_nsa_kernel, tq=tq, tk=tk, nt=nt),
        grid=(B, G, nq),
        in_specs=[
            pl.BlockSpec((nt * tq, NSA_REP * HEAD_DIM), lambda b, g, i: (b * nq + i, g)),
            pl.BlockSpec((nt * tq, LANES), lambda b, g, i: (b * nq + i, 0)),
            per_bg((n_ck, LANES)), per_bg((VT_ROWS, n_ck)),
            kv_kind(2), kv_kind(3), kv_kind(4), kv_kind(5),
            full(kcols), full(vtail), full(band), full(cmpmask), full(ovt),
        ],
        out_specs=pl.BlockSpec((nt * tq, NSA_REP * HEAD_DIM), lambda b, g, i: (b * nq + i, g)),
        out_shape=jax.ShapeDtypeStruct((B * S, Q_NSA), BF16),
        scratch_shapes=[pltpu.VMEM((nt, n_slc, rows), F32), pltpu.VMEM((nt, 1, rows), F32),
                        pltpu.VMEM((nt, VT_ROWS, rows), F32),
                        pltpu.VMEM((S, LANES), BF16), pltpu.VMEM((VT_ROWS, S), BF16),
                        pltpu.VMEM((S, LANES), BF16), pltpu.VMEM((VT_ROWS, S), BF16),
                        pltpu.SMEM((nt, n_chunks), jnp.int32), pltpu.SMEM((nt, n_chunks + 1), jnp.int32)],
        compiler_params=_params(("arbitrary", "arbitrary", "arbitrary")),
        name="nsa",
    )(q_pad, gates, kc, vct, kv_pad, kv_pad, kv_pad, kv_pad, kcols, vtail, band, cmpmask, ovt)


def _dil_attn_kernel(q_ref, k_ref, v_ref, dist_ref, o_ref, lse_ref, vt_scr, *, tq, ts, width, max_back, slopes):
    i = pl.program_id(1)
    nb, L, _ = k_ref.shape
    t_chunk = min(L, 512)

    @pl.when(i == 0)
    def _():
        tail_row = lax.broadcasted_iota(jnp.int32, (VT_ROWS - HEAD_DIM, L), 0)
        tail = jnp.where(tail_row == 0, 1.0, 0.0).astype(BF16)
        for b in range(nb):
            for h in range(DIL_HPG):
                vt_scr[b, h * VT_ROWS + HEAD_DIM:(h + 1) * VT_ROWS, :] = tail
            for pair in range(DIL_OUT // LANES):
                for c in range(L // t_chunk):
                    cols = slice(c * t_chunk, (c + 1) * t_chunk)
                    v_t = v_ref[b, cols, pair * LANES:(pair + 1) * LANES].astype(F32).T.astype(BF16)
                    for half in range(2):
                        h = 2 * pair + half
                        vt_scr[b, h * VT_ROWS:h * VT_ROWS + HEAD_DIM, cols] = v_t[half * HEAD_DIM:(half + 1) * HEAD_DIM]

    n_sub = tq // ts
    backs = [jnp.minimum(i * tq + u * ts, max_back) for u in range(n_sub)]
    kss = [pl.multiple_of(i * tq + u * ts - backs[u], max_back) for u in range(n_sub)]
    dists = [dist_ref[pl.ds(pl.multiple_of(max_back - backs[u], max_back), width), :] for u in range(n_sub)]
    lane = lax.broadcasted_iota(jnp.int32, (1, LANES), 1)
    chains = [(b, u, h) for b in range(nb) for u in range(n_sub) for h in range(DIL_HPG)]
    scores = []
    for b, u, h in chains:
        pair, half = divmod(h, 2)
        in_head = (lane >= half * HEAD_DIM) & (lane < (half + 1) * HEAD_DIM)
        q = q_ref[b, u * ts:(u + 1) * ts, pair * LANES:(pair + 1) * LANES]
        q = jnp.where(in_head, q, jnp.zeros_like(q))
        k = k_ref[b, pl.ds(kss[u], width), pair * LANES:(pair + 1) * LANES]
        scores.append(_dot_nt(k, q) - (slopes[h] * LOG2E) * dists[u])
    maxes = [jnp.max(s_t, axis=0, keepdims=True) for s_t in scores]
    probs = [_exp2_bf16(s_t - m) for s_t, m in zip(scores, maxes)]
    accs = [_dot(vt_scr[b, h * VT_ROWS:(h + 1) * VT_ROWS, pl.ds(kss[u], width)], p)
            for (b, u, h), p in zip(chains, probs)]
    dens = [acc[HEAD_DIM:HEAD_DIM + 1] for acc in accs]
    outs = [acc[0:HEAD_DIM] * (1.0 / den) for acc, den in zip(accs, dens)]
    lses = [jnp.broadcast_to((m + jnp.log2(den)) * (1.0 / LOG2E), (HEAD_DIM, ts)) for m, den in zip(maxes, dens)]
    for n, (b, u) in enumerate((b, u) for b in range(nb) for u in range(n_sub)):
        rows = slice(u * ts, (u + 1) * ts)
        o_ref[b, rows, :] = jnp.concatenate(outs[n * DIL_HPG:(n + 1) * DIL_HPG], axis=0).T
        lse_ref[b, rows, :] = jnp.concatenate(lses[n * DIL_HPG:(n + 1) * DIL_HPG], axis=0).T


def _dil_attn(qkv, dist_tbl, *, max_back, slopes, tq, ts, nb):
    N, L, _ = qkv.shape
    width = dist_tbl.shape[0] - max_back
    kern = functools.partial(_dil_attn_kernel, tq=tq, ts=ts, width=width, max_back=max_back, slopes=slopes)
    return pl.pallas_call(
        kern,
        grid=(N // nb, L // tq),
        in_specs=[
            pl.BlockSpec((nb, tq, DIL_OUT), lambda n, i: (n, i, 0)),
            pl.BlockSpec((nb, L, DIL_OUT), lambda n, i: (n, 0, 1)),
            pl.BlockSpec((nb, L, DIL_OUT), lambda n, i: (n, 0, 2)),
            pl.BlockSpec(dist_tbl.shape, lambda n, i: (0, 0)),
        ],
        out_specs=[
            pl.BlockSpec((nb, tq, DIL_OUT), lambda n, i: (n, i, 0)),
            pl.BlockSpec((nb, tq, DIL_OUT), lambda n, i: (n, i, 0)),
        ],
        out_shape=[
            jax.ShapeDtypeStruct((N, L, DIL_OUT), F32),
            jax.ShapeDtypeStruct((N, L, DIL_OUT), F32),
        ],
        scratch_shapes=[pltpu.VMEM((nb, DIL_HPG * VT_ROWS, L), BF16)],
        compiler_params=_params(("parallel", "arbitrary")),
        name="dil_attn",
    )(qkv, qkv, qkv, dist_tbl)


def _final_kernel(x_ref, pre_ref, post_ref, onsa_ref, od0_ref, od1_ref, od2_ref, l0_ref, l1_ref, l2_ref,
                  wmerge_ref, wnsa_ref, wdil_ref, wout_ref, o_ref, scr):
    tm = x_ref.shape[0]
    x = x_ref[...]
    h = _rms(x, pre_ref[...]).astype(BF16)
    merge = _dot(h, wmerge_ref[...])
    g_a = _sigmoid(merge[:, :D_MODEL])
    g_b = _sigmoid(merge[:, D_MODEL:])
    y_nsa = _dot(onsa_ref[...], wnsa_ref[...])

    n_blk = DIL_OUT // LANES
    slot = 0
    token_major = []
    for (_, dil), o_in, l_in in zip(DIL_CONFIGS, (od0_ref, od1_ref, od2_ref), (l0_ref, l1_ref, l2_ref)):
        pair = []
        for src in (o_in, l_in):
            if dil == 1:
                pair.append(src[0])
                continue
            for r in range(dil):
                for c in range(n_blk):
                    scr[slot + c, pl.ds(r, tm // dil, stride=dil), :] = src[r, :, c * LANES:(c + 1) * LANES]
            pair.append(jnp.concatenate([scr[slot + c] for c in range(n_blk)], axis=1))
            slot += n_blk
        token_major.append(pair)
    (o0, l0), (o1, l1), (o2, l2) = token_major
    mx = jnp.maximum(jnp.maximum(l0, l1), l2)
    e0, e1, e2 = jnp.exp(l0 - mx), jnp.exp(l1 - mx), jnp.exp(l2 - mx)
    den = e0 + e1 + e2
    o_dil = (e0 / den) * o0 + (e1 / den) * o1 + (e2 / den) * o2
    y_dil = _dot(o_dil.astype(BF16), wdil_ref[...])

    y = g_a * y_nsa + g_b * y_dil
    z = _dot(y.astype(BF16), wout_ref[...])
    o_ref[...] = x + _rms(z, post_ref[...])


def _final(x2d, pre, post, onsa, od, ld, wmerge, wnsa, wdil, wout, *, S, tm=512):
    T = x2d.shape[0]
    nt = S // tm
    tok = lambda w: pl.BlockSpec((tm, w), lambda i: (i, 0))
    full = lambda a: pl.BlockSpec(a.shape, lambda i: (0,) * a.ndim)
    strided = [pl.BlockSpec((None, dil, tm // dil, DIL_OUT), lambda i: (i // nt, 0, i % nt, 0))
               for _, dil in DIL_CONFIGS]
    n_strided = sum(1 for _, dil in DIL_CONFIGS if dil > 1)
    return pl.pallas_call(
        _final_kernel,
        grid=(T // tm,),
        in_specs=[tok(D_MODEL), _resident(pre), _resident(post), tok(Q_NSA)] + strided + strided
        + [_resident(wmerge), _resident(wnsa), _resident(wdil), _resident(wout)],
        out_specs=tok(D_MODEL),
        out_shape=jax.ShapeDtypeStruct((T, D_MODEL), F32),
        scratch_shapes=[pltpu.VMEM((2 * n_strided * (DIL_OUT // LANES), tm, LANES), F32)],
        compiler_params=_params(("parallel",)),
        name="final",
    )(x2d, pre, post, onsa, od[0], od[1], od[2], ld[0], ld[1], ld[2], wmerge, wnsa, wdil, wout)


def _overlap_t(n_ck, n_slc):
    c0 = np.arange(n_ck)[None, :] * CMP_STRIDE
    s0 = np.arange(n_slc)[:, None] * SLC_LEN
    ov = np.clip(np.minimum(c0 + CMP_LEN, s0 + SLC_LEN) - np.maximum(c0, s0), 0, None) / CMP_LEN
    ov[:, n_ck - 1] = 0.0
    return ov.astype(np.float32)


def _pos_columns(pos):
    cols = np.zeros((len(pos), LANES), np.float32)
    for t, vals in enumerate((pos // SLC_LEN, pos % SLC_LEN, np.ones(len(pos)))):
        for u in range(3):
            cols[:, HEAD_DIM + 3 * t + u] = vals
    return cols


def _ones_row_tail(n):
    tail = np.zeros((VT_ROWS, n), np.float32)
    tail[HEAD_DIM] = 1.0
    return tail


def _band_table(tq):
    u = np.arange(2 * NSA_WINDOW + tq)[:, None]
    r = (np.arange(NSA_REP * tq) % tq)[None, :]
    dist = NSA_WINDOW + r - u
    return np.where((dist >= 0) & (dist <= NSA_WINDOW - 1), 0.0, NEG).astype(np.float32)


def _dil_dist_table(tq, width, max_back, dil):
    u = np.arange(width + max_back)[:, None]
    r = np.arange(tq)[None, :]
    dist = max_back + r - u
    return np.where((dist >= 0) & (dist <= max_back), dist * dil, -NEG).astype(np.float32)


def _cmp_mask_table(n_ck, tq):
    u = np.arange(2 * n_ck - 8)[:, None]
    r = (np.arange(NSA_REP * tq) % tq)[None, :]
    end = (u - (n_ck - 8)) * CMP_STRIDE + CMP_LEN - 1
    return np.where(end <= r, 0.0, NEG).astype(np.float32)


def _mixer(x2d, B, S, mix_pre, mix_post, w_in, pe_k, w_ck1, w_ck2, pe_v, w_cv1, w_cv2,
           w_nsa_o, w_dil_o, w_mix_out):
    G = NSA_GROUPS
    BG = B * G
    assert S % (16 * 128) == 0 and NSA_TQ // CMP_STRIDE == 8
    o_q, o_kv, o_g, o_d, o_m = np.cumsum([0, Q_NSA, 6 * KV_NSA, GATE_NSA, 3 * QKV_DIL]).tolist()
    wq = w_in[:, o_q:o_kv] * (SCALE * LOG2E)
    w_gate = jnp.pad(w_in[:, o_g:o_d], ((0, 0), (0, LANES - GATE_NSA)))
    wd = w_in[:, o_d:o_m]
    q_pad, kv_pad, gates, xk, xv, *qkv_d = _proj(
        x2d, mix_pre[None, :], wq.astype(BF16), w_in[:, o_kv:o_g].astype(BF16), wd.astype(BF16),
        w_gate.astype(BF16), B=B, S=S)

    n_ck = S // CMP_STRIDE
    half = CMP_STRIDE * HEAD_DIM
    w1 = jnp.stack([w_ck1, w_cv1])
    w1cat = jnp.concatenate([w1[:, :half], w1[:, half:]], axis=2).astype(BF16)
    pe = jnp.stack([pe_k.reshape(1, -1), pe_v.reshape(1, -1)]).astype(BF16)
    w2 = jnp.pad(jnp.stack([w_ck2, w_cv2]), ((0, 0), (0, 0), (0, LANES - HEAD_DIM))).astype(BF16)
    cmp_pos = np.arange(n_ck) * CMP_STRIDE + CMP_LEN - 1
    kc, vct = _cmp_mlp(xk.reshape(BG, n_ck, half), xv.reshape(BG, n_ck, half), pe, w1cat, w1.astype(BF16), w2,
                       jnp.asarray(_pos_columns(cmp_pos), BF16), jnp.asarray(_ones_row_tail(n_ck), BF16))

    band = jnp.asarray(_band_table(NSA_TQ))
    cmpmask = jnp.asarray(_cmp_mask_table(n_ck, NSA_TQ))
    ovt = jnp.asarray(_overlap_t(n_ck, S // SLC_LEN), BF16)
    o_nsa = _nsa(q_pad, gates, kc, vct, kv_pad, jnp.asarray(_pos_columns(np.arange(S)), BF16),
                 jnp.asarray(_ones_row_tail(S), BF16), band, cmpmask, ovt, B=B, S=S)

    od, ld = [], []
    for gi, (window, dil) in enumerate(DIL_CONFIGS):
        Ls = S // dil
        max_back = window // dil
        tq = min(DIL_TQ, Ls)
        ts = min(DIL_TS, Ls)
        width = ts + max_back if Ls > ts else ts
        o, lse = _dil_attn(qkv_d[gi].reshape(B * dil, Ls, DIL_QKV),
                           jnp.asarray(_dil_dist_table(ts, width, max_back, dil)),
                           max_back=max_back, slopes=DIL_SLOPES[gi * DIL_HPG:(gi + 1) * DIL_HPG], tq=tq, ts=ts,
                           nb=DIL_TQ // tq)
        od.append(o.reshape(B, dil, Ls, DIL_OUT))
        ld.append(lse.reshape(B, dil, Ls, DIL_OUT))

    return _final(x2d, mix_pre[None, :], mix_post[None, :], o_nsa, od, ld, w_in[:, o_m:].astype(BF16),
                  w_nsa_o.astype(BF16), w_dil_o.astype(BF16), w_mix_out.astype(BF16), S=S)


def kernel(x, ffn1_pre, ffn1_post, ffn1_w_gate, ffn1_w_up, ffn1_w_down, mix_pre, mix_post, w_in, nsa_pe_k, nsa_w_ck1, nsa_w_ck2, nsa_pe_v, nsa_w_cv1, nsa_w_cv2, w_nsa_o, w_dil_o, w_mix_out, ffn2_pre, ffn2_post, ffn2_w_gate, ffn2_w_up, ffn2_w_down):
    B, S, _ = x.shape
    x2d = x.reshape(B * S, D_MODEL)
    for l in range(ffn1_pre.shape[0]):
        x2d = _ffn(x2d, ffn1_pre[l][None, :], ffn1_post[l][None, :], ffn1_w_gate[l].astype(BF16),
                   ffn1_w_up[l].astype(BF16), ffn1_w_down[l].astype(BF16))
        x2d = _mixer(x2d, B, S, mix_pre[l], mix_post[l], w_in[l], nsa_pe_k[l], nsa_w_ck1[l], nsa_w_ck2[l],
                     nsa_pe_v[l], nsa_w_cv1[l], nsa_w_cv2[l], w_nsa_o[l], w_dil_o[l], w_mix_out[l])
        x2d = _ffn(x2d, ffn2_pre[l][None, :], ffn2_post[l][None, :], ffn2_w_gate[l].astype(BF16),
                   ffn2_w_up[l].astype(BF16), ffn2_w_down[l].astype(BF16))
    return x2d.reshape(B, S, D_MODEL)
```

```python
import functools
import math

import numpy as np
import jax
import jax.numpy as jnp
from jax import lax
from jax.experimental import pallas as pl
from jax.experimental.pallas import tpu as pltpu

F32 = jnp.float32
BF16 = jnp.bfloat16

D_MODEL = 1024
HEAD_DIM = 64
RMS_EPS = 1e-6
NSA_HEADS = 8
NSA_GROUPS = 2
NSA_REP = NSA_HEADS // NSA_GROUPS
CMP_LEN = 32
CMP_STRIDE = 16
CMP_HIDDEN = 256
SLC_LEN = 64
N_SEL = 16
NSA_WINDOW = 512
DIL_CONFIGS = ((128, 1), (512, 4), (2048, 16))
DIL_HPG = 4
DIL_HEADS = DIL_HPG * len(DIL_CONFIGS)
Q_NSA = NSA_HEADS * HEAD_DIM
KV_NSA = NSA_GROUPS * HEAD_DIM
GATE_NSA = NSA_HEADS * 3
QKV_DIL = DIL_HEADS * HEAD_DIM
DIL_OUT = DIL_HPG * HEAD_DIM
DIL_QKV = 3 * DIL_OUT
DIL_TQ = 2048
DIL_TS = 128
LANES = 128
NEG = -1e30
SCALE = HEAD_DIM ** -0.5
LOG2E = math.log2(math.e)
VMEM_LIMIT = 56 * 1024 * 1024
NSA_TQ = 128
NSA_TILES_PER_STEP = 4
VT_ROWS = 80


def _slopes(n):
    return [float(np.float32(2.0 ** (-8.0 * (i + 1) / n))) for i in range(n)]


NSA_SLOPES = _slopes(NSA_HEADS)
DIL_SLOPES = _slopes(DIL_HEADS)


def _params(sem):
    return pltpu.CompilerParams(dimension_semantics=sem, vmem_limit_bytes=VMEM_LIMIT)


def _rms(x, g):
    ms = jnp.mean(x * x, axis=-1, keepdims=True)
    return x * lax.rsqrt(ms + RMS_EPS) * g


def _sigmoid(x):
    return 1.0 / (1.0 + jnp.exp(-x))


def _dot(a, b):
    return jnp.dot(a, b, preferred_element_type=F32)


def _dot_nt(a, b):
    return lax.dot_general(a, b, (((1,), (1,)), ((), ())), preferred_element_type=F32)


def _split3(a):
    hi = a.astype(BF16)
    r1 = a - hi.astype(F32)
    mid = r1.astype(BF16)
    lo = (r1 - mid.astype(F32)).astype(BF16)
    return hi, mid, lo


def _ffn_kernel(x_ref, pre_ref, post_ref, wg_ref, wu_ref, wd_ref, o_ref):
    x = x_ref[...]
    h = _rms(x, pre_ref[...]).astype(BF16)
    g = _dot(h, wg_ref[...])
    u = _dot(h, wu_ref[...])
    a = (g * _sigmoid(g) * u).astype(BF16)
    o_ref[...] = x + 0.5 * _rms(_dot(a, wd_ref[...]), post_ref[...])


def _resident(a):
    return pl.BlockSpec(a.shape, lambda *_: (0,) * a.ndim, pipeline_mode=pl.Buffered(1))


def _ffn(x2d, pre, post, wg, wu, wd, *, tm=512):
    T = x2d.shape[0]
    return pl.pallas_call(
        _ffn_kernel,
        grid=(T // tm,),
        in_specs=[pl.BlockSpec((tm, D_MODEL), lambda i: (i, 0)), _resident(pre), _resident(post),
                  _resident(wg), _resident(wu), _resident(wd)],
        out_specs=pl.BlockSpec((tm, D_MODEL), lambda i: (i, 0)),
        out_shape=jax.ShapeDtypeStruct((T, D_MODEL), F32),
        compiler_params=_params(("parallel",)),
        name="ffn",
    )(x2d, pre, post, wg, wu, wd)


def _proj_kernel(x_ref, pre_ref, wq_ref, wkv_ref, wd_ref, wgate_ref, q_ref, kv_ref, g_ref, xk_ref, xv_ref,
                 *rest):
    d_refs, scr = rest[:-1], rest[-1]
    tm = x_ref.shape[0]
    h = _rms(x_ref[...], pre_ref[...]).astype(BF16)
    q_ref[...] = _dot(h, wq_ref[...]).astype(BF16)
    kv = _dot(h, wkv_ref[...])
    kv_ref[...] = kv.astype(BF16)
    g_ref[...] = _dot(h, wgate_ref[...])
    lane = lax.broadcasted_iota(jnp.int32, (1, LANES), 1)
    n_rows = tm // CMP_STRIDE
    for kind, out in ((0, xk_ref), (1, xv_ref)):
        scr[kind] = kv[:, kind * LANES:(kind + 1) * LANES]
        for j in range(CMP_STRIDE // 2):
            even = scr[kind, pl.ds(2 * j, n_rows, stride=CMP_STRIDE), :]
            odd = scr[kind, pl.ds(2 * j + 1, n_rows, stride=CMP_STRIDE), :]
            out[0, :, j * LANES:(j + 1) * LANES] = jnp.where(
                lane < HEAD_DIM, even, pltpu.roll(odd, HEAD_DIM, 1)).astype(BF16)
            out[1, :, j * LANES:(j + 1) * LANES] = jnp.where(
                lane < HEAD_DIM, pltpu.roll(even, HEAD_DIM, 1), odd).astype(BF16)
    n_blk = DIL_QKV // LANES
    for gi, (_, dil) in enumerate(DIL_CONFIGS):
        parts = [_dot(h, wd_ref[:, t * QKV_DIL + gi * DIL_OUT:t * QKV_DIL + (gi + 1) * DIL_OUT]) for t in range(3)]
        res = jnp.concatenate([parts[0] * (SCALE * LOG2E), parts[1], parts[2]], axis=1)
        if dil == 1:
            d_refs[gi][0] = res.astype(BF16)
            continue
        for c in range(n_blk):
            scr[c] = res[:, c * LANES:(c + 1) * LANES]
        for r in range(dil):
            for c in range(n_blk):
                rows = scr[c, pl.ds(r, tm // dil, stride=dil), :]
                d_refs[gi][r, :, c * LANES:(c + 1) * LANES] = rows.astype(BF16)


def _proj(x2d, pre, wq, wkv, wd, wgate, *, B, S, tm=512):
    T = x2d.shape[0]
    nt = S // tm
    tok = lambda w: pl.BlockSpec((tm, w), lambda i: (i, 0))
    full = lambda a: pl.BlockSpec(a.shape, lambda i: (0,) * a.ndim)
    d_specs = [pl.BlockSpec((None, dil, tm // dil, DIL_QKV), lambda i: (i // nt, 0, i % nt, 0))
               for _, dil in DIL_CONFIGS]
    d_shapes = [jax.ShapeDtypeStruct((B, dil, S // dil, DIL_QKV), BF16) for _, dil in DIL_CONFIGS]
    chunk_width = CMP_STRIDE * HEAD_DIM
    c_spec = pl.BlockSpec((None, NSA_GROUPS, tm // CMP_STRIDE, chunk_width), lambda i: (i // nt, 0, i % nt, 0))
    c_shape = jax.ShapeDtypeStruct((B, NSA_GROUPS, S // CMP_STRIDE, chunk_width), BF16)
    return pl.pallas_call(
        _proj_kernel,
        grid=(T // tm,),
        in_specs=[tok(D_MODEL), _resident(pre), _resident(wq), _resident(wkv), _resident(wd), _resident(wgate)],
        out_specs=[tok(wq.shape[1]), tok(wkv.shape[1]), tok(LANES), c_spec, c_spec] + d_specs,
        out_shape=[jax.ShapeDtypeStruct((T, wq.shape[1]), BF16), jax.ShapeDtypeStruct((T, wkv.shape[1]), BF16),
                   jax.ShapeDtypeStruct((T, LANES), F32), c_shape, c_shape] + d_shapes,
        scratch_shapes=[pltpu.VMEM((DIL_QKV // LANES, tm, LANES), F32)],
        compiler_params=_params(("parallel",)),
        name="proj",
    )(x2d, pre, wq, wkv, wd, wgate)


def _gelu_tanh(x):
    c = math.sqrt(2.0 / math.pi)
    return x * (0.5 * (1.0 + jnp.tanh(c * (x + 0.044715 * (x * x * x)))))


def _cmp_mlp_kernel(xk_ref, xv_ref, pe_ref, w1cat_ref, w1_ref, w2_ref, kcols_ref, vtail_ref, kc_ref, vct_ref):
    n_chunks = xk_ref.shape[0]

    def mlp(x_ref, t):
        ab = _dot(x_ref[...], w1cat_ref[t])
        first = ab[:, :CMP_HIDDEN]
        second = pltpu.roll(ab[:, CMP_HIDDEN:], n_chunks - 1, 0)
        pe8 = jnp.broadcast_to(pe_ref[t], (8, pe_ref.shape[2]))
        bias = _dot(pe8, w1_ref[t])[0:1]
        hid = _gelu_tanh(first + second + bias).astype(BF16)
        out = _dot(hid, w2_ref[t])
        row = lax.broadcasted_iota(jnp.int32, out.shape, 0)
        return jnp.where(row < n_chunks - 1, out, 0.0)

    kc_ref[...] = mlp(xk_ref, 0).astype(BF16) + kcols_ref[...]
    vct_ref[...] = mlp(xv_ref, 1).T[0:VT_ROWS].astype(BF16) + vtail_ref[...]


def _cmp_mlp(xk, xv, pe, w1cat, w1, w2, kcols, vtail):
    BG, n_chunks, width = xk.shape
    per_n = lambda shape: pl.BlockSpec((None,) + shape, lambda n: (n, 0, 0))
    full = lambda a: pl.BlockSpec(a.shape, lambda n: (0,) * a.ndim)
    return pl.pallas_call(
        _cmp_mlp_kernel,
        grid=(BG,),
        in_specs=[per_n((n_chunks, width)), per_n((n_chunks, width)), full(pe), full(w1cat), full(w1),
                  full(w2), full(kcols), full(vtail)],
        out_specs=[per_n((n_chunks, LANES)), per_n((VT_ROWS, n_chunks))],
        out_shape=[jax.ShapeDtypeStruct((BG, n_chunks, LANES), BF16),
                   jax.ShapeDtypeStruct((BG, VT_ROWS, n_chunks), BF16)],
        compiler_params=_params(("parallel",)),
        name="cmp_mlp",
    )(xk, xv, pe, w1cat, w1, w2, kcols, vtail)


def _exp2_bf16(x):
    return jnp.exp2(x.astype(BF16))


def _low_half(x2, half):
    lane = lax.broadcasted_iota(jnp.int32, (1, LANES), 1)
    xf = x2.astype(F32)
    if half == 1:
        xf = pltpu.roll(xf, HEAD_DIM, 1)
    return jnp.where(lane < HEAD_DIM, xf, 0.0).astype(BF16)


def _nsa_kernel(q_ref, gate_ref, kc_ref, vct_ref, kslc_ref, vslc_ref, kwin_ref, vwin_ref, kcols_ref,
                vtail_ref, band_ref, cmpmask_ref, ovt_ref, o_ref,
                selbias_scr, m_scr, acc_scr, ks_ref, vst_ref, kw_ref, vwt_ref, need_smem, list_smem,
                *, tq, tk, nt):
    g = pl.program_id(1)
    i = pl.program_id(2)
    n_ck = kc_ref.shape[0]
    n_slc = ovt_ref.shape[0]
    S = kslc_ref.shape[0]

    @pl.when(i == 0)
    def _():
        t_chunk = min(S, 512)
        for src, dst in ((kslc_ref, ks_ref), (kwin_ref, kw_ref)):
            for c in range(S // t_chunk):
                rows_c = slice(c * t_chunk, (c + 1) * t_chunk)
                k2 = src[rows_c, :]
                mine = jnp.where(g == 0, _low_half(k2, 0), _low_half(k2, 1))
                dst[rows_c, :] = mine + kcols_ref[rows_c, :]
        for src, dst in ((vslc_ref, vst_ref), (vwin_ref, vwt_ref)):
            dst[HEAD_DIM:VT_ROWS, :] = vtail_ref[HEAD_DIM:VT_ROWS, :]
            for c in range(S // t_chunk):
                v = src[c * t_chunk:(c + 1) * t_chunk, :].astype(F32)
                v_t = v.T
                mine = jnp.where(g == 0, v_t[0:HEAD_DIM], v_t[HEAD_DIM:2 * HEAD_DIM])
                dst[0:HEAD_DIM, c * t_chunk:(c + 1) * t_chunk] = mine.astype(BF16)

    tiles = range(nt)
    lane = lax.broadcasted_iota(jnp.int32, (1, LANES), 1)
    width = NSA_WINDOW + tq
    blocks_per_chunk = tk // SLC_LEN
    n_w = width // SLC_LEN
    jb = lax.broadcasted_iota(jnp.int32, (n_slc, 1), 0)
    jbf = jb.astype(F32)
    q0s = [(i * nt + t) * tq for t in tiles]
    backs = [jnp.minimum(q0, NSA_WINDOW) for q0 in q0s]
    wss = [pl.multiple_of(q0 - back, tq) for q0, back in zip(q0s, backs)]
    qss = [pl.multiple_of(q0, tq) for q0 in q0s]

    q_augs = []
    for t in tiles:
        slabs = []
        for r in range(NSA_REP):
            sl = jnp.where(g == 0, NSA_SLOPES[r], NSA_SLOPES[NSA_REP + r]) * LOG2E
            meta = jnp.zeros((1, LANES), F32)
            for c, coef in enumerate((SLC_LEN * sl, sl, -sl * q0s[t].astype(F32))):
                terms = _split3(jnp.full((1, LANES), coef, F32))
                for u in range(3):
                    meta = jnp.where(lane == HEAD_DIM + 3 * c + u, terms[u].astype(F32), meta)
            q2 = q_ref[t * tq:(t + 1) * tq, (r // 2) * LANES:(r // 2 + 1) * LANES]
            slabs.append(_low_half(q2, r % 2) + meta.astype(BF16))
        q_augs.append(jnp.concatenate(slabs, axis=0))

    s_alls = [_dot_nt(jnp.concatenate([kc_ref[...], kw_ref[pl.ds(wss[t], width), :],
                                       ks_ref[pl.ds(qss[t], tq), :]], axis=0), q_augs[t]) for t in tiles]

    s_cs = [s_alls[t][0:n_ck] + cmpmask_ref[pl.ds(pl.multiple_of(
        (n_ck - 8) - (tq // CMP_STRIDE) * (i * nt + t), 8), n_ck), :] for t in tiles]
    m_cs = [jnp.max(s, axis=0, keepdims=True) for s in s_cs]
    p_cs = [jnp.exp2(s - m) for s, m in zip(s_cs, m_cs)]
    den_cs = [jnp.sum(p, axis=0, keepdims=True) for p in p_cs]
    p_cs = [p * jnp.where(m > 0.5 * NEG, 1.0 / den, 0.0) for p, m, den in zip(p_cs, m_cs, den_cs)]
    o_cmps = [_dot(vct_ref[0:HEAD_DIM, :], p.astype(BF16)) for p in p_cs]

    ovt = ovt_ref[...]
    imps = []
    for p in p_cs:
        psum = p[:, 0:tq]
        for r in range(1, NSA_REP):
            psum = psum + p[:, r * tq:(r + 1) * tq]
        hi, mid, _ = _split3(psum)
        imps.append(_dot(ovt, hi) + _dot(ovt, mid))
    vals, sels = [], []
    for t in tiles:
        pos = q0s[t] + lax.broadcasted_iota(jnp.int32, (1, tq), 1)
        own = pos // SLC_LEN
        forced = (jb == 0) | (jb == own) | (jb == own - 1)
        valid = jb * SLC_LEN <= pos
        vals.append(jnp.where(valid & jnp.logical_not(forced), imps[t], -jnp.inf))
        sels.append(jnp.where(forced, 1.0, 0.0))

    s_ws = [s_alls[t][n_ck:n_ck + width]
            + band_ref[pl.ds(pl.multiple_of(NSA_WINDOW - backs[t], tq), width), :] for t in tiles]
    m_ws = [jnp.max(s, axis=0, keepdims=True) for s in s_ws]
    s_ds = [s_alls[t][n_ck + width:n_ck + width + tq] + band_ref[NSA_WINDOW:NSA_WINDOW + tq, :] for t in tiles]
    m_ds = [jnp.max(s, axis=0, keepdims=True) for s in s_ds]
    slices = [([(s_ws[t], m_ws[t], r0) for r0 in range(0, width, SLC_LEN)]
               + [(s_ds[t], m_ds[t], r0) for r0 in range(0, tq, SLC_LEN)]) for t in tiles]
    probs = [[] for _ in tiles]

    for it in range(max(N_SEL - 3, len(slices[0]))):
        for t in tiles:
            if it < N_SEL - 3:
                mx = jnp.max(vals[t], axis=0, keepdims=True)
                cand = jnp.where(vals[t] == mx, jbf, 1e9)
                idx = jnp.min(cand, axis=0, keepdims=True)
                pick = (jbf == idx) & (mx > -jnp.inf)
                sels[t] = jnp.where(pick, 1.0, sels[t])
                vals[t] = jnp.where(pick, -jnp.inf, vals[t])
            if it < len(slices[t]):
                src, m_src, r0 = slices[t][it]
                probs[t].append(_exp2_bf16(src[r0:r0 + SLC_LEN] - m_src))
    o_wins = []
    for t in tiles:
        acc_w = _dot(vwt_ref[:, pl.ds(wss[t], width)], jnp.concatenate(probs[t][:n_w], axis=0))
        o_wins.append(acc_w[0:HEAD_DIM] * (1.0 / acc_w[HEAD_DIM:HEAD_DIM + 1]))
        m_scr[t] = m_ds[t]
        acc_scr[t] = _dot(vst_ref[:, pl.ds(qss[t], tq)], jnp.concatenate(probs[t][n_w:], axis=0))

    n_needs = []
    for t in tiles:
        sbias = jnp.where((sels[t] > 0.5) & (jb < q0s[t] // SLC_LEN), 0.0, NEG)
        selbias_scr[t] = jnp.concatenate([sbias] * NSA_REP, axis=1)
        for c in range(n_slc // blocks_per_chunk):
            chunk_max = jnp.max(sbias[c * blocks_per_chunk:(c + 1) * blocks_per_chunk])
            need_smem[t, c] = (chunk_max > 0.5 * NEG).astype(jnp.int32)
    for t in tiles:
        n_need = jnp.int32(0)
        for k in range(n_slc // blocks_per_chunk + 1):
            list_smem[t, k] = jnp.int32(0)
        for c in range(n_slc // blocks_per_chunk):
            list_smem[t, n_need] = jnp.int32(c)
            n_need = n_need + need_smem[t, c]
        n_needs.append(n_need)

    def entry(t, k):
        live = k < n_needs[t]
        c = jnp.where(live, list_smem[t, k], 0)
        return c, jnp.where(live, 0.0, NEG)

    def masked(s_c, t, c, pad):
        parts = []
        for jj in range(blocks_per_chunk):
            brow = selbias_scr[t, pl.ds(c * blocks_per_chunk + jj, 1), :] + pad
            parts.append(s_c[jj * SLC_LEN:(jj + 1) * SLC_LEN] + brow)
        return jnp.concatenate(parts, axis=0)

    for ta in range(0, nt, 4):
        group = list(range(ta, min(ta + 4, nt)))

        def joint_body(j, carry, group=group):
            ents = [(entry(t, 2 * j), entry(t, 2 * j + 1)) for t in group]
            offs = [(pl.multiple_of(ea[0] * tk, tk), pl.multiple_of(eb[0] * tk, tk)) for ea, eb in ents]
            scores = [_dot_nt(jnp.concatenate([ks_ref[pl.ds(ka, tk), :], ks_ref[pl.ds(kb, tk), :]], axis=0),
                              q_augs[t]) for t, (ka, kb) in zip(group, offs)]
            scores = [jnp.concatenate([masked(s[0:tk], t, ea[0], ea[1]), masked(s[tk:2 * tk], t, eb[0], eb[1])],
                                      axis=0) for s, t, (ea, eb) in zip(scores, group, ents)]
            m_old = [m_scr[t] for t in group]
            m_new = [jnp.maximum(m, jnp.max(s, axis=0, keepdims=True)) for m, s in zip(m_old, scores)]
            probs = [_exp2_bf16(s - m) for s, m in zip(scores, m_new)]
            for t, (ka, kb), p, mo, mn in zip(group, offs, probs, m_old, m_new):
                v_ab = jnp.concatenate([vst_ref[:, pl.ds(ka, tk)], vst_ref[:, pl.ds(kb, tk)]], axis=1)
                acc_scr[t] = jnp.exp2(mo - mn) * acc_scr[t] + _dot(v_ab, p)
                m_scr[t] = mn
            return carry

        trips = functools.reduce(jnp.maximum, [(n_needs[t] + 1) // 2 for t in group])
        lax.fori_loop(0, trips, joint_body, 0)

    for t in tiles:
        acc = acc_scr[t]
        o_slc = acc[0:HEAD_DIM] * (1.0 / acc[HEAD_DIM:HEAD_DIM + 1])
        gate_t = _sigmoid(gate_ref[t * tq:(t + 1) * tq, :].T)
        outs = []
        for r in range(NSA_REP):
            cols = slice(r * tq, (r + 1) * tq)

            def gate_row(branch):
                lo_row = 3 * r + branch
                hi_row = 3 * (NSA_REP + r) + branch
                return jnp.where(g == 0, gate_t[lo_row:lo_row + 1], gate_t[hi_row:hi_row + 1])

            outs.append(gate_row(0) * o_cmps[t][:, cols] + gate_row(1) * o_slc[:, cols]
                        + gate_row(2) * o_wins[t][:, cols])
        o_ref[t * tq:(t + 1) * tq, :] = jnp.concatenate(outs, axis=0).T.astype(BF16)


def _nsa(q_pad, gates, kc, vct, kv_pad, kcols, vtail, band, cmpmask, ovt, *, B, S, tq=NSA_TQ, tk=512,
         nt=NSA_TILES_PER_STEP):
    G = NSA_GROUPS
    n_ck = kc.shape[1]
    n_slc = ovt.shape[0]
    rows = NSA_REP * tq
    nq = S // (tq * nt)
    n_chunks = n_slc * SLC_LEN // tk
    per_bg = lambda shape: pl.BlockSpec((None,) + shape, lambda b, g, i: (b * G + g, 0, 0))
    full = lambda a: pl.BlockSpec(a.shape, lambda b, g, i: (0,) * a.ndim)
    kv_kind = lambda kind: pl.BlockSpec((S, LANES), lambda b, g, i: (b, kind))
    return pl.pallas_call(
        functools.partial(_nsa_kernel, tq=tq, tk=tk, nt=nt),
        grid=(B, G, nq),
        in_specs=[
            pl.BlockSpec((nt * tq, NSA_REP * HEAD_DIM), lambda b, g, i: (b * nq + i, g)),
            pl.BlockSpec((nt * tq, LANES), lambda b, g, i: (b * nq + i, 0)),
            per_bg((n_ck, LANES)), per_bg((VT_ROWS, n_ck)),
            kv_kind(2), kv_kind(3), kv_kind(4), kv_kind(5),
            full(kcols), full(vtail), full(band), full(cmpmask), full(ovt),
        ],
        out_specs=pl.BlockSpec((nt * tq, NSA_REP * HEAD_DIM), lambda b, g, i: (b * nq + i, g)),
        out_shape=jax.ShapeDtypeStruct((B * S, Q_NSA), BF16),
        scratch_shapes=[pltpu.VMEM((nt, n_slc, rows), F32), pltpu.VMEM((nt, 1, rows), F32),
                        pltpu.VMEM((nt, VT_ROWS, rows), F32),
                        pltpu.VMEM((S, LANES), BF16), pltpu.VMEM((VT_ROWS, S), BF16),
                        pltpu.VMEM((S, LANES), BF16), pltpu.VMEM((VT_ROWS, S), BF16),
                        pltpu.SMEM((nt, n_chunks), jnp.int32), pltpu.SMEM((nt, n_chunks + 1), jnp.int32)],
        compiler_params=_params(("arbitrary", "arbitrary", "arbitrary")),
        name="nsa",
    )(q_pad, gates, kc, vct, kv_pad, kv_pad, kv_pad, kv_pad, kcols, vtail, band, cmpmask, ovt)


def _dil_attn_kernel(q_ref, k_ref, v_ref, dist_ref, o_ref, lse_ref, vt_scr, *, tq, ts, width, max_back, slopes):
    i = pl.program_id(1)
    nb, L, _ = k_ref.shape
    t_chunk = min(L, 512)

    @pl.when(i == 0)
    def _():
        tail_row = lax.broadcasted_iota(jnp.int32, (VT_ROWS - HEAD_DIM, L), 0)
        tail = jnp.where(tail_row == 0, 1.0, 0.0).astype(BF16)
        for b in range(nb):
            for h in range(DIL_HPG):
                vt_scr[b, h * VT_ROWS + HEAD_DIM:(h + 1) * VT_ROWS, :] = tail
            for pair in range(DIL_OUT // LANES):
                for c in range(L // t_chunk):
                    cols = slice(c * t_chunk, (c + 1) * t_chunk)
                    v_t = v_ref[b, cols, pair * LANES:(pair + 1) * LANES].astype(F32).T.astype(BF16)
                    for half in range(2):
                        h = 2 * pair + half
                        vt_scr[b, h * VT_ROWS:h * VT_ROWS + HEAD_DIM, cols] = v_t[half * HEAD_DIM:(half + 1) * HEAD_DIM]

    n_sub = tq // ts
    backs = [jnp.minimum(i * tq + u * ts, max_back) for u in range(n_sub)]
    kss = [pl.multiple_of(i * tq + u * ts - backs[u], max_back) for u in range(n_sub)]
    dists = [dist_ref[pl.ds(pl.multiple_of(max_back - backs[u], max_back), width), :] for u in range(n_sub)]
    lane = lax.broadcasted_iota(jnp.int32, (1, LANES), 1)
    chains = [(b, u, h) for b in range(nb) for u in range(n_sub) for h in range(DIL_HPG)]
    scores = []
    for b, u, h in chains:
        pair, half = divmod(h, 2)
        in_head = (lane >= half * HEAD_DIM) & (lane < (half + 1) * HEAD_DIM)
        q = q_ref[b, u * ts:(u + 1) * ts, pair * LANES:(pair + 1) * LANES]
        q = jnp.where(in_head, q, jnp.zeros_like(q))
        k = k_ref[b, pl.ds(kss[u], width), pair * LANES:(pair + 1) * LANES]
        scores.append(_dot_nt(k, q) - (slopes[h] * LOG2E) * dists[u])
    maxes = [jnp.max(s_t, axis=0, keepdims=True) for s_t in scores]
    probs = [_exp2_bf16(s_t - m) for s_t, m in zip(scores, maxes)]
    accs = [_dot(vt_scr[b, h * VT_ROWS:(h + 1) * VT_ROWS, pl.ds(kss[u], width)], p)
            for (b, u, h), p in zip(chains, probs)]
    dens = [acc[HEAD_DIM:HEAD_DIM + 1] for acc in accs]
    outs = [acc[0:HEAD_DIM] * (1.0 / den) for acc, den in zip(accs, dens)]
    lses = [jnp.broadcast_to((m + jnp.log2(den)) * (1.0 / LOG2E), (HEAD_DIM, ts)) for m, den in zip(maxes, dens)]
    for n, (b, u) in enumerate((b, u) for b in range(nb) for u in range(n_sub)):
        rows = slice(u * ts, (u + 1) * ts)
        o_ref[b, rows, :] = jnp.concatenate(outs[n * DIL_HPG:(n + 1) * DIL_HPG], axis=0).T
        lse_ref[b, rows, :] = jnp.concatenate(lses[n * DIL_HPG:(n + 1) * DIL_HPG], axis=0).T


def _dil_attn(qkv, dist_tbl, *, max_back, slopes, tq, ts, nb):
    N, L, _ = qkv.shape
    width = dist_tbl.shape[0] - max_back
    kern = functools.partial(_dil_attn_kernel, tq=tq, ts=ts, width=width, max_back=max_back, slopes=slopes)
    return pl.pallas_call(
        kern,
        grid=(N // nb, L // tq),
        in_specs=[
            pl.BlockSpec((nb, tq, DIL_OUT), lambda n, i: (n, i, 0)),
            pl.BlockSpec((nb, L, DIL_OUT), lambda n, i: (n, 0, 1)),
            pl.BlockSpec((nb, L, DIL_OUT), lambda n, i: (n, 0, 2)),
            pl.BlockSpec(dist_tbl.shape, lambda n, i: (0, 0)),
        ],
        out_specs=[
            pl.BlockSpec((nb, tq, DIL_OUT), lambda n, i: (n, i, 0)),
            pl.BlockSpec((nb, tq, DIL_OUT), lambda n, i: (n, i, 0)),
        ],
        out_shape=[
            jax.ShapeDtypeStruct((N, L, DIL_OUT), F32),
            jax.ShapeDtypeStruct((N, L, DIL_OUT), F32),
        ],
        scratch_shapes=[pltpu.VMEM((nb, DIL_HPG * VT_ROWS, L), BF16)],
        compiler_params=_params(("parallel", "arbitrary")),
        name="dil_attn",
    )(qkv, qkv, qkv, dist_tbl)


def _final_kernel(x_ref, pre_ref, post_ref, onsa_ref, od0_ref, od1_ref, od2_ref, l0_ref, l1_ref, l2_ref,
                  wmerge_ref, wnsa_ref, wdil_ref, wout_ref, o_ref, scr):
    tm = x_ref.shape[0]
    x = x_ref[...]
    h = _rms(x, pre_ref[...]).astype(BF16)
    merge = _dot(h, wmerge_ref[...])
    g_a = _sigmoid(merge[:, :D_MODEL])
    g_b = _sigmoid(merge[:, D_MODEL:])
    y_nsa = _dot(onsa_ref[...], wnsa_ref[...])

    n_blk = DIL_OUT // LANES
    slot = 0
    token_major = []
    for (_, dil), o_in, l_in in zip(DIL_CONFIGS, (od0_ref, od1_ref, od2_ref), (l0_ref, l1_ref, l2_ref)):
        pair = []
        for src in (o_in, l_in):
            if dil == 1:
                pair.append(src[0])
                continue
            for r in range(dil):
                for c in range(n_blk):
                    scr[slot + c, pl.ds(r, tm // dil, stride=dil), :] = src[r, :, c * LANES:(c + 1) * LANES]
            pair.append(jnp.concatenate([scr[slot + c] for c in range(n_blk)], axis=1))
            slot += n_blk
        token_major.append(pair)
    (o0, l0), (o1, l1), (o2, l2) = token_major
    mx = jnp.maximum(jnp.maximum(l0, l1), l2)
    e0, e1, e2 = jnp.exp(l0 - mx), jnp.exp(l1 - mx), jnp.exp(l2 - mx)
    den = e0 + e1 + e2
    o_dil = (e0 / den) * o0 + (e1 / den) * o1 + (e2 / den) * o2
    y_dil = _dot(o_dil.astype(BF16), wdil_ref[...])

    y = g_a * y_nsa + g_b * y_dil
    z = _dot(y.astype(BF16), wout_ref[...])
    o_ref[...] = x + _rms(z, post_ref[...])


def _final(x2d, pre, post, onsa, od, ld, wmerge, wnsa, wdil, wout, *, S, tm=512):
    T = x2d.shape[0]
    nt = S // tm
    tok = lambda w: pl.BlockSpec((tm, w), lambda i: (i, 0))
    full = lambda a: pl.BlockSpec(a.shape, lambda i: (0,) * a.ndim)
    strided = [pl.BlockSpec((None, dil, tm // dil, DIL_OUT), lambda i: (i // nt, 0, i % nt, 0))
               for _, dil in DIL_CONFIGS]
    n_strided = sum(1 for _, dil in DIL_CONFIGS if dil > 1)
    return pl.pallas_call(
        _final_kernel,
        grid=(T // tm,),
        in_specs=[tok(D_MODEL), _resident(pre), _resident(post), tok(Q_NSA)] + strided + strided
        + [_resident(wmerge), _resident(wnsa), _resident(wdil), _resident(wout)],
        out_specs=tok(D_MODEL),
        out_shape=jax.ShapeDtypeStruct((T, D_MODEL), F32),
        scratch_shapes=[pltpu.VMEM((2 * n_strided * (DIL_OUT // LANES), tm, LANES), F32)],
        compiler_params=_params(("parallel",)),
        name="final",
    )(x2d, pre, post, onsa, od[0], od[1], od[2], ld[0], ld[1], ld[2], wmerge, wnsa, wdil, wout)


def _overlap_t(n_ck, n_slc):
    c0 = np.arange(n_ck)[None, :] * CMP_STRIDE
    s0 = np.arange(n_slc)[:, None] * SLC_LEN
    ov = np.clip(np.minimum(c0 + CMP_LEN, s0 + SLC_LEN) - np.maximum(c0, s0), 0, None) / CMP_LEN
    ov[:, n_ck - 1] = 0.0
    return ov.astype(np.float32)


def _pos_columns(pos):
    cols = np.zeros((len(pos), LANES), np.float32)
    for t, vals in enumerate((pos // SLC_LEN, pos % SLC_LEN, np.ones(len(pos)))):
        for u in range(3):
            cols[:, HEAD_DIM + 3 * t + u] = vals
    return cols


def _ones_row_tail(n):
    tail = np.zeros((VT_ROWS, n), np.float32)
    tail[HEAD_DIM] = 1.0
    return tail


def _band_table(tq):
    u = np.arange(2 * NSA_WINDOW + tq)[:, None]
    r = (np.arange(NSA_REP * tq) % tq)[None, :]
    dist = NSA_WINDOW + r - u
    return np.where((dist >= 0) & (dist <= NSA_WINDOW - 1), 0.0, NEG).astype(np.float32)


def _dil_dist_table(tq, width, max_back, dil):
    u = np.arange(width + max_back)[:, None]
    r = np.arange(tq)[None, :]
    dist = max_back + r - u
    return np.where((dist >= 0) & (dist <= max_back), dist * dil, -NEG).astype(np.float32)


def _cmp_mask_table(n_ck, tq):
    u = np.arange(2 * n_ck - 8)[:, None]
    r = (np.arange(NSA_REP * tq) % tq)[None, :]
    end = (u - (n_ck - 8)) * CMP_STRIDE + CMP_LEN - 1
    return np.where(end <= r, 0.0, NEG).astype(np.float32)


def _mixer(x2d, B, S, mix_pre, mix_post, w_in, pe_k, w_ck1, w_ck2, pe_v, w_cv1, w_cv2,
           w_nsa_o, w_dil_o, w_mix_out):
    G = NSA_GROUPS
    BG = B * G
    assert S % (16 * 128) == 0 and NSA_TQ // CMP_STRIDE == 8
    o_q, o_kv, o_g, o_d, o_m = np.cumsum([0, Q_NSA, 6 * KV_NSA, GATE_NSA, 3 * QKV_DIL]).tolist()
    wq = w_in[:, o_q:o_kv] * (SCALE * LOG2E)
    w_gate = jnp.pad(w_in[:, o_g:o_d], ((0, 0), (0, LANES - GATE_NSA)))
    wd = w_in[:, o_d:o_m]
    q_pad, kv_pad, gates, xk, xv, *qkv_d = _proj(
        x2d, mix_pre[None, :], wq.astype(BF16), w_in[:, o_kv:o_g].astype(BF16), wd.astype(BF16),
        w_gate.astype(BF16), B=B, S=S)

    n_ck = S // CMP_STRIDE
    half = CMP_STRIDE * HEAD_DIM
    w1 = jnp.stack([w_ck1, w_cv1])
    w1cat = jnp.concatenate([w1[:, :half], w1[:, half:]], axis=2).astype(BF16)
    pe = jnp.stack([pe_k.reshape(1, -1), pe_v.reshape(1, -1)]).astype(BF16)
    w2 = jnp.pad(jnp.stack([w_ck2, w_cv2]), ((0, 0), (0, 0), (0, LANES - HEAD_DIM))).astype(BF16)
    cmp_pos = np.arange(n_ck) * CMP_STRIDE + CMP_LEN - 1
    kc, vct = _cmp_mlp(xk.reshape(BG, n_ck, half), xv.reshape(BG, n_ck, half), pe, w1cat, w1.astype(BF16), w2,
                       jnp.asarray(_pos_columns(cmp_pos), BF16), jnp.asarray(_ones_row_tail(n_ck), BF16))

    band = jnp.asarray(_band_table(NSA_TQ))
    cmpmask = jnp.asarray(_cmp_mask_table(n_ck, NSA_TQ))
    ovt = jnp.asarray(_overlap_t(n_ck, S // SLC_LEN), BF16)
    o_nsa = _nsa(q_pad, gates, kc, vct, kv_pad, jnp.asarray(_pos_columns(np.arange(S)), BF16),
                 jnp.asarray(_ones_row_tail(S), BF16), band, cmpmask, ovt, B=B, S=S)

    od, ld = [], []
    for gi, (window, dil) in enumerate(DIL_CONFIGS):
        Ls = S // dil
        max_back = window // dil
        tq = min(DIL_TQ, Ls)
        ts = min(DIL_TS, Ls)
        width = ts + max_back if Ls > ts else ts
        o, lse = _dil_attn(qkv_d[gi].reshape(B * dil, Ls, DIL_QKV),
                           jnp.asarray(_dil_dist_table(ts, width, max_back, dil)),
                           max_back=max_back, slopes=DIL_SLOPES[gi * DIL_HPG:(gi + 1) * DIL_HPG], tq=tq, ts=ts,
                           nb=DIL_TQ // tq)
        od.append(o.reshape(B, dil, Ls, DIL_OUT))
        ld.append(lse.reshape(B, dil, Ls, DIL_OUT))

    return _final(x2d, mix_pre[None, :], mix_post[None, :], o_nsa, od, ld, w_in[:, o_m:].astype(BF16),
                  w_nsa_o.astype(BF16), w_dil_o.astype(BF16), w_mix_out.astype(BF16), S=S)


def kernel(x, ffn1_pre, ffn1_post, ffn1_w_gate, ffn1_w_up, ffn1_w_down, mix_pre, mix_post, w_in, nsa_pe_k, nsa_w_ck1, nsa_w_ck2, nsa_pe_v, nsa_w_cv1, nsa_w_cv2, w_nsa_o, w_dil_o, w_mix_out, ffn2_pre, ffn2_post, ffn2_w_gate, ffn2_w_up, ffn2_w_down):
    B, S, _ = x.shape
    x2d = x.reshape(B * S, D_MODEL)
    for l in range(ffn1_pre.shape[0]):
        x2d = _ffn(x2d, ffn1_pre[l][None, :], ffn1_post[l][None, :], ffn1_w_gate[l].astype(BF16),
                   ffn1_w_up[l].astype(BF16), ffn1_w_down[l].astype(BF16))
        x2d = _mixer(x2d, B, S, mix_pre[l], mix_post[l], w_in[l], nsa_pe_k[l], nsa_w_ck1[l], nsa_w_ck2[l],
                     nsa_pe_v[l], nsa_w_cv1[l], nsa_w_cv2[l], w_nsa_o[l], w_dil_o[l], w_mix_out[l])
        x2d = _ffn(x2d, ffn2_pre[l][None, :], ffn2_post[l][None, :], ffn2_w_gate[l].astype(BF16),
                   ffn2_w_up[l].astype(BF16), ffn2_w_down[l].astype(BF16))
    return x2d.reshape(B, S, D_MODEL)
```

```python
import functools
import math

import numpy as np
import jax
import jax.numpy as jnp
from jax import lax
from jax.experimental import pallas as pl
from jax.experimental.pallas import tpu as pltpu

F32 = jnp.float32
BF16 = jnp.bfloat16

D_MODEL = 1024
HEAD_DIM = 64
RMS_EPS = 1e-6
NSA_HEADS = 8
NSA_GROUPS = 2
NSA_REP = NSA_HEADS // NSA_GROUPS
CMP_LEN = 32
CMP_STRIDE = 16
CMP_HIDDEN = 256
SLC_LEN = 64
N_SEL = 16
NSA_WINDOW = 512
DIL_CONFIGS = ((128, 1), (512, 4), (2048, 16))
DIL_HPG = 4
DIL_HEADS = DIL_HPG * len(DIL_CONFIGS)
D_FF = 2816
Q_NSA = NSA_HEADS * HEAD_DIM
KV_NSA = NSA_GROUPS * HEAD_DIM
GATE_NSA = NSA_HEADS * 3
QKV_DIL = DIL_HEADS * HEAD_DIM
DIL_OUT = DIL_HPG * HEAD_DIM
DIL_QKV = 3 * DIL_OUT
DIL_TQ = 1024
DIL_TS = 128
LANES = 128
NEG = -1e30
SCALE = HEAD_DIM ** -0.5
LOG2E = math.log2(math.e)
VMEM_LIMIT = 56 * 1024 * 1024
NSA_TQ = 128
NSA_TILES_PER_STEP = 4
GATHER_BLOCKS = 2
VT_ROWS = 80


def _slopes(n):
    return [float(np.float32(2.0 ** (-8.0 * (i + 1) / n))) for i in range(n)]


NSA_SLOPES = _slopes(NSA_HEADS)
DIL_SLOPES = _slopes(DIL_HEADS)


def _params(sem):
    return pltpu.CompilerParams(dimension_semantics=sem, vmem_limit_bytes=VMEM_LIMIT)


def _rms(x, g):
    ms = jnp.mean(x * x, axis=-1, keepdims=True)
    return x * lax.rsqrt(ms + RMS_EPS) * g


def _sigmoid(x):
    return 1.0 / (1.0 + jnp.exp(-x))


def _dot(a, b):
    return jnp.dot(a, b, preferred_element_type=F32)


def _dot_nt(a, b):
    return lax.dot_general(a, b, (((1,), (1,)), ((), ())), preferred_element_type=F32)


def _split3(a):
    hi = a.astype(BF16)
    r1 = a - hi.astype(F32)
    mid = r1.astype(BF16)
    lo = (r1 - mid.astype(F32)).astype(BF16)
    return hi, mid, lo


def _ffn_kernel(x_ref, pre_ref, post_ref, wg_ref, wu_ref, wd_ref, o_ref):
    x = x_ref[...]
    h = _rms(x, pre_ref[...]).astype(BF16)
    g = _dot(h, wg_ref[...])
    u = _dot(h, wu_ref[...])
    a = (g * _sigmoid(g) * u).astype(BF16)
    o_ref[...] = x + 0.5 * _rms(_dot(a, wd_ref[...]), post_ref[...])


def _resident(a):
    return pl.BlockSpec(a.shape, lambda *_: (0,) * a.ndim, pipeline_mode=pl.Buffered(1))


def _ffn(x2d, pre, post, wg, wu, wd, *, tm=512):
    T = x2d.shape[0]
    return pl.pallas_call(
        _ffn_kernel,
        grid=(T // tm,),
        in_specs=[pl.BlockSpec((tm, D_MODEL), lambda i: (i, 0)), _resident(pre), _resident(post),
                  _resident(wg), _resident(wu), _resident(wd)],
        out_specs=pl.BlockSpec((tm, D_MODEL), lambda i: (i, 0)),
        out_shape=jax.ShapeDtypeStruct((T, D_MODEL), F32),
        compiler_params=_params(("parallel",)),
        name="ffn",
    )(x2d, pre, post, wg, wu, wd)


def _proj_kernel(x_ref, pre_ref, wq_ref, wkv_ref, wd_ref, wgate_ref, q_ref, kv_ref, g_ref, xk_ref, xv_ref,
                 *rest):
    d_refs, scr = rest[:-1], rest[-1]
    tm = x_ref.shape[0]
    h = _rms(x_ref[...], pre_ref[...]).astype(BF16)
    q_ref[...] = _dot(h, wq_ref[...]).astype(BF16)
    kv = _dot(h, wkv_ref[...])
    kv_ref[...] = kv.astype(BF16)
    g_ref[...] = _dot(h, wgate_ref[...])
    lane = lax.broadcasted_iota(jnp.int32, (1, LANES), 1)
    n_rows = tm // CMP_STRIDE
    for kind, out in ((0, xk_ref), (1, xv_ref)):
        scr[kind] = kv[:, kind * LANES:(kind + 1) * LANES]
        for j in range(CMP_STRIDE // 2):
            even = scr[kind, pl.ds(2 * j, n_rows, stride=CMP_STRIDE), :]
            odd = scr[kind, pl.ds(2 * j + 1, n_rows, stride=CMP_STRIDE), :]
            out[0, :, j * LANES:(j + 1) * LANES] = jnp.where(
                lane < HEAD_DIM, even, pltpu.roll(odd, HEAD_DIM, 1)).astype(BF16)
            out[1, :, j * LANES:(j + 1) * LANES] = jnp.where(
                lane < HEAD_DIM, pltpu.roll(even, HEAD_DIM, 1), odd).astype(BF16)
    n_blk = DIL_QKV // LANES
    for gi, (_, dil) in enumerate(DIL_CONFIGS):
        parts = [_dot(h, wd_ref[:, t * QKV_DIL + gi * DIL_OUT:t * QKV_DIL + (gi + 1) * DIL_OUT]) for t in range(3)]
        res = jnp.concatenate([parts[0] * (SCALE * LOG2E), parts[1], parts[2]], axis=1)
        if dil == 1:
            d_refs[gi][0] = res.astype(BF16)
            continue
        for c in range(n_blk):
            scr[c] = res[:, c * LANES:(c + 1) * LANES]
        for r in range(dil):
            for c in range(n_blk):
                rows = scr[c, pl.ds(r, tm // dil, stride=dil), :]
                d_refs[gi][r, :, c * LANES:(c + 1) * LANES] = rows.astype(BF16)


def _proj(x2d, pre, wq, wkv, wd, wgate, *, B, S, tm=512):
    T = x2d.shape[0]
    nt = S // tm
    tok = lambda w: pl.BlockSpec((tm, w), lambda i: (i, 0))
    full = lambda a: pl.BlockSpec(a.shape, lambda i: (0,) * a.ndim)
    d_specs = [pl.BlockSpec((None, dil, tm // dil, DIL_QKV), lambda i: (i // nt, 0, i % nt, 0))
               for _, dil in DIL_CONFIGS]
    d_shapes = [jax.ShapeDtypeStruct((B, dil, S // dil, DIL_QKV), BF16) for _, dil in DIL_CONFIGS]
    chunk_width = CMP_STRIDE * HEAD_DIM
    c_spec = pl.BlockSpec((None, NSA_GROUPS, tm // CMP_STRIDE, chunk_width), lambda i: (i // nt, 0, i % nt, 0))
    c_shape = jax.ShapeDtypeStruct((B, NSA_GROUPS, S // CMP_STRIDE, chunk_width), BF16)
    return pl.pallas_call(
        _proj_kernel,
        grid=(T // tm,),
        in_specs=[tok(D_MODEL), _resident(pre), _resident(wq), _resident(wkv), _resident(wd), _resident(wgate)],
        out_specs=[tok(wq.shape[1]), tok(wkv.shape[1]), tok(LANES), c_spec, c_spec] + d_specs,
        out_shape=[jax.ShapeDtypeStruct((T, wq.shape[1]), BF16), jax.ShapeDtypeStruct((T, wkv.shape[1]), BF16),
                   jax.ShapeDtypeStruct((T, LANES), F32), c_shape, c_shape] + d_shapes,
        scratch_shapes=[pltpu.VMEM((DIL_QKV // LANES, tm, LANES), F32)],
        compiler_params=_params(("parallel",)),
        name="proj",
    )(x2d, pre, wq, wkv, wd, wgate)


def _gelu_tanh(x):
    c = math.sqrt(2.0 / math.pi)
    return x * (0.5 * (1.0 + jnp.tanh(c * (x + 0.044715 * (x * x * x)))))


def _cmp_mlp_kernel(xk_ref, xv_ref, pe_ref, w1cat_ref, w1_ref, w2_ref, kcols_ref, vtail_ref, kc_ref, vct_ref):
    n_chunks = xk_ref.shape[0]

    def mlp(x_ref, t):
        ab = _dot(x_ref[...], w1cat_ref[t])
        first = ab[:, :CMP_HIDDEN]
        second = pltpu.roll(ab[:, CMP_HIDDEN:], n_chunks - 1, 0)
        pe8 = jnp.broadcast_to(pe_ref[t], (8, pe_ref.shape[2]))
        bias = _dot(pe8, w1_ref[t])[0:1]
        hid = _gelu_tanh(first + second + bias).astype(BF16)
        out = _dot(hid, w2_ref[t])
        row = lax.broadcasted_iota(jnp.int32, out.shape, 0)
        return jnp.where(row < n_chunks - 1, out, 0.0)

    kc_ref[...] = mlp(xk_ref, 0).astype(BF16) + kcols_ref[...]
    vct_ref[...] = mlp(xv_ref, 1).T[0:VT_ROWS].astype(BF16) + vtail_ref[...]


def _cmp_mlp(xk, xv, pe, w1cat, w1, w2, kcols, vtail):
    BG, n_chunks, width = xk.shape
    per_n = lambda shape: pl.BlockSpec((None,) + shape, lambda n: (n, 0, 0))
    full = lambda a: pl.BlockSpec(a.shape, lambda n: (0,) * a.ndim)
    return pl.pallas_call(
        _cmp_mlp_kernel,
        grid=(BG,),
        in_specs=[per_n((n_chunks, width)), per_n((n_chunks, width)), full(pe), full(w1cat), full(w1),
                  full(w2), full(kcols), full(vtail)],
        out_specs=[per_n((n_chunks, LANES)), per_n((VT_ROWS, n_chunks))],
        out_shape=[jax.ShapeDtypeStruct((BG, n_chunks, LANES), BF16),
                   jax.ShapeDtypeStruct((BG, VT_ROWS, n_chunks), BF16)],
        compiler_params=_params(("parallel",)),
        name="cmp_mlp",
    )(xk, xv, pe, w1cat, w1, w2, kcols, vtail)


def _exp2_bf16(x):
    return jnp.exp2(x.astype(BF16))


def _low_half(x2, half):
    lane = lax.broadcasted_iota(jnp.int32, (1, LANES), 1)
    xf = x2.astype(F32)
    if half == 1:
        xf = pltpu.roll(xf, HEAD_DIM, 1)
    return jnp.where(lane < HEAD_DIM, xf, 0.0).astype(BF16)


def _nsa_kernel(q_ref, gate_ref, kc_ref, vct_ref, kslc_ref, vslc_ref, kwin_ref, vwin_ref, kcols_ref,
                vtail_ref, band_ref, cmpmask_ref, ovt_ref, o_ref,
                selbias_scr, m_scr, acc_scr, ks_ref, vst_ref, kw_ref, vwt_ref, list_smem,
                *, tq, tk, nt):
    g = pl.program_id(1)
    i = pl.program_id(2)
    n_ck = kc_ref.shape[0]
    n_slc = ovt_ref.shape[0]
    S = kslc_ref.shape[0]

    @pl.when(i == 0)
    def _():
        t_chunk = min(S, 512)
        for src, dst in ((kslc_ref, ks_ref), (kwin_ref, kw_ref)):
            for c in range(S // t_chunk):
                rows_c = slice(c * t_chunk, (c + 1) * t_chunk)
                k2 = src[rows_c, :]
                mine = jnp.where(g == 0, _low_half(k2, 0), _low_half(k2, 1))
                dst[rows_c, :] = mine + kcols_ref[rows_c, :]
        for src, dst in ((vslc_ref, vst_ref), (vwin_ref, vwt_ref)):
            dst[HEAD_DIM:VT_ROWS, :] = vtail_ref[HEAD_DIM:VT_ROWS, :]
            for c in range(S // t_chunk):
                v = src[c * t_chunk:(c + 1) * t_chunk, :].astype(F32)
                v_t = v.T
                mine = jnp.where(g == 0, v_t[0:HEAD_DIM], v_t[HEAD_DIM:2 * HEAD_DIM])
                dst[0:HEAD_DIM, c * t_chunk:(c + 1) * t_chunk] = mine.astype(BF16)

    tiles = range(nt)
    lane = lax.broadcasted_iota(jnp.int32, (1, LANES), 1)
    width = NSA_WINDOW + tq
    blocks_per_chunk = tk // SLC_LEN
    n_w = width // SLC_LEN
    jb = lax.broadcasted_iota(jnp.int32, (n_slc, 1), 0)
    jbf = jb.astype(F32)
    q0s = [(i * nt + t) * tq for t in tiles]
    backs = [jnp.minimum(q0, NSA_WINDOW) for q0 in q0s]
    wss = [pl.multiple_of(q0 - back, tq) for q0, back in zip(q0s, backs)]
    qss = [pl.multiple_of(q0, tq) for q0 in q0s]

    q_augs = []
    for t in tiles:
        slabs = []
        for r in range(NSA_REP):
            sl = jnp.where(g == 0, NSA_SLOPES[r], NSA_SLOPES[NSA_REP + r]) * LOG2E
            meta = jnp.zeros((1, LANES), F32)
            for c, coef in enumerate((SLC_LEN * sl, sl, -sl * q0s[t].astype(F32))):
                terms = _split3(jnp.full((1, LANES), coef, F32))
                for u in range(3):
                    meta = jnp.where(lane == HEAD_DIM + 3 * c + u, terms[u].astype(F32), meta)
            q2 = q_ref[t * tq:(t + 1) * tq, (r // 2) * LANES:(r // 2 + 1) * LANES]
            slabs.append(_low_half(q2, r % 2) + meta.astype(BF16))
        q_augs.append(jnp.concatenate(slabs, axis=0))

    s_alls = [_dot_nt(jnp.concatenate([kc_ref[...], kw_ref[pl.ds(wss[t], width), :],
                                       ks_ref[pl.ds(qss[t], tq), :]], axis=0), q_augs[t]) for t in tiles]

    s_cs = [s_alls[t][0:n_ck] + cmpmask_ref[pl.ds(pl.multiple_of(
        (n_ck - 8) - (tq // CMP_STRIDE) * (i * nt + t), 8), n_ck), :] for t in tiles]
    m_cs = [jnp.max(s, axis=0, keepdims=True) for s in s_cs]
    p_cs = [jnp.exp2(s - m) for s, m in zip(s_cs, m_cs)]
    den_cs = [jnp.sum(p, axis=0, keepdims=True) for p in p_cs]
    p_cs = [p * jnp.where(m > 0.5 * NEG, 1.0 / den, 0.0) for p, m, den in zip(p_cs, m_cs, den_cs)]
    o_cmps = [_dot(vct_ref[0:HEAD_DIM, :], p.astype(BF16)) for p in p_cs]

    ovt = ovt_ref[...]
    imps = []
    for p in p_cs:
        psum = p[:, 0:tq]
        for r in range(1, NSA_REP):
            psum = psum + p[:, r * tq:(r + 1) * tq]
        hi, mid, _ = _split3(psum)
        imps.append(_dot(ovt, hi) + _dot(ovt, mid))
    vals, sels = [], []
    for t in tiles:
        pos = q0s[t] + lax.broadcasted_iota(jnp.int32, (1, tq), 1)
        own = pos // SLC_LEN
        forced = (jb == 0) | (jb == own) | (jb == own - 1)
        valid = jb * SLC_LEN <= pos
        vals.append(jnp.where(valid & jnp.logical_not(forced), imps[t], -jnp.inf))
        sels.append(jnp.where(forced, 1.0, 0.0))

    s_ws = [s_alls[t][n_ck:n_ck + width]
            + band_ref[pl.ds(pl.multiple_of(NSA_WINDOW - backs[t], tq), width), :] for t in tiles]
    m_ws = [jnp.max(s, axis=0, keepdims=True) for s in s_ws]
    s_ds = [s_alls[t][n_ck + width:n_ck + width + tq] + band_ref[NSA_WINDOW:NSA_WINDOW + tq, :] for t in tiles]
    m_ds = [jnp.max(s, axis=0, keepdims=True) for s in s_ds]
    slices = [([(s_ws[t], m_ws[t], r0) for r0 in range(0, width, SLC_LEN)]
               + [(s_ds[t], m_ds[t], r0) for r0 in range(0, tq, SLC_LEN)]) for t in tiles]
    probs = [[] for _ in tiles]

    for it in range(max(N_SEL - 3, len(slices[0]))):
        for t in tiles:
            if it < N_SEL - 3:
                mx = jnp.max(vals[t], axis=0, keepdims=True)
                cand = jnp.where(vals[t] == mx, jbf, 1e9)
                idx = jnp.min(cand, axis=0, keepdims=True)
                pick = (jbf == idx) & (mx > -jnp.inf)
                sels[t] = jnp.where(pick, 1.0, sels[t])
                vals[t] = jnp.where(pick, -jnp.inf, vals[t])
            if it < len(slices[t]):
                src, m_src, r0 = slices[t][it]
                probs[t].append(_exp2_bf16(src[r0:r0 + SLC_LEN] - m_src))
    o_wins = []
    for t in tiles:
        acc_w = _dot(vwt_ref[:, pl.ds(wss[t], width)], jnp.concatenate(probs[t][:n_w], axis=0))
        o_wins.append(acc_w[0:HEAD_DIM] * (1.0 / acc_w[HEAD_DIM:HEAD_DIM + 1]))
        m_scr[t] = m_ds[t]
        acc_scr[t] = _dot(vst_ref[:, pl.ds(qss[t], tq)], jnp.concatenate(probs[t][n_w:], axis=0))

    n_needs = []
    bit = jnp.exp2((jb % 16).astype(F32))
    masks = []
    for t in tiles:
        chosen = (sels[t] > 0.5) & (jb < q0s[t] // SLC_LEN)
        sbias = jnp.where(chosen, 0.0, NEG)
        selbias_scr[t] = jnp.concatenate([sbias] * NSA_REP, axis=1)
        any_q = jnp.max(jnp.where(chosen, 1.0, 0.0), axis=1, keepdims=True) * bit
        masks.append([jnp.sum(any_q[16 * w:16 * (w + 1)]).astype(jnp.int32) for w in range(n_slc // 16)])
    for t in tiles:
        n_need = jnp.int32(0)
        for k in range(list_smem.shape[1]):
            list_smem[t, k] = jnp.int32(0)
        for c in range(n_slc // GATHER_BLOCKS):
            word, shift = divmod(GATHER_BLOCKS * c, 16)
            pair_bits = lax.shift_right_logical(masks[t][word], jnp.int32(shift)) & (2 ** GATHER_BLOCKS - 1)
            list_smem[t, n_need] = jnp.int32(c)
            n_need = n_need + jnp.minimum(pair_bits, 1)
        n_needs.append(n_need)

    gk = GATHER_BLOCKS * SLC_LEN
    slots = 2 * tk // gk

    def entry(t, k):
        live = k < n_needs[t]
        c = jnp.where(live, list_smem[t, k], 0)
        return c, jnp.where(live, 0.0, NEG)

    def joint_body(j, carry):
        ents = [[entry(t, slots * j + u) for u in range(slots)] for t in tiles]
        offs = [[pl.multiple_of(c * gk, gk) for c, _ in ent] for ent in ents]
        scores = [_dot_nt(jnp.concatenate([ks_ref[pl.ds(o, gk), :] for o in off], axis=0), q_augs[t])
                  for t, off in zip(tiles, offs)]
        masked = []
        for t, s, ent in zip(tiles, scores, ents):
            parts = []
            for u, (c, pad) in enumerate(ent):
                for jj in range(GATHER_BLOCKS):
                    brow = selbias_scr[t, pl.ds(c * GATHER_BLOCKS + jj, 1), :] + pad
                    r0 = u * gk + jj * SLC_LEN
                    parts.append(s[r0:r0 + SLC_LEN] + brow)
            masked.append(jnp.concatenate(parts, axis=0))
        m_old = [m_scr[t] for t in tiles]
        m_new = [jnp.maximum(m, jnp.max(s, axis=0, keepdims=True)) for m, s in zip(m_old, masked)]
        probs = [_exp2_bf16(s - m) for s, m in zip(masked, m_new)]
        for t, off, p, mo, mn in zip(tiles, offs, probs, m_old, m_new):
            v_g = jnp.concatenate([vst_ref[:, pl.ds(o, gk)] for o in off], axis=1)
            acc_scr[t] = jnp.exp2(mo - mn) * acc_scr[t] + _dot(v_g, p)
            m_scr[t] = mn
        return carry

    trips = functools.reduce(jnp.maximum, [(n_needs[t] + slots - 1) // slots for t in tiles])
    lax.fori_loop(0, trips, joint_body, 0)

    for t in tiles:
        acc = acc_scr[t]
        o_slc = acc[0:HEAD_DIM] * (1.0 / acc[HEAD_DIM:HEAD_DIM + 1])
        gate_t = _sigmoid(gate_ref[t * tq:(t + 1) * tq, :].T)
        outs = []
        for r in range(NSA_REP):
            cols = slice(r * tq, (r + 1) * tq)

            def gate_row(branch):
                lo_row = 3 * r + branch
                hi_row = 3 * (NSA_REP + r) + branch
                return jnp.where(g == 0, gate_t[lo_row:lo_row + 1], gate_t[hi_row:hi_row + 1])

            outs.append(gate_row(0) * o_cmps[t][:, cols] + gate_row(1) * o_slc[:, cols]
                        + gate_row(2) * o_wins[t][:, cols])
        o_ref[t * tq:(t + 1) * tq, :] = jnp.concatenate(outs, axis=0).T.astype(BF16)


def _nsa(q_pad, gates, kc, vct, kv_pad, kcols, vtail, band, cmpmask, ovt, *, B, S, tq=NSA_TQ, tk=512,
         nt=NSA_TILES_PER_STEP):
    G = NSA_GROUPS
    n_ck = kc.shape[1]
    n_slc = ovt.shape[0]
    rows = NSA_REP * tq
    nq = S // (tq * nt)
    n_list = n_slc // GATHER_BLOCKS + 2 * tk // (GATHER_BLOCKS * SLC_LEN)
    per_bg = lambda shape: pl.BlockSpec((None,) + shape, lambda b, g, i: (b * G + g, 0, 0))
    full = lambda a: pl.BlockSpec(a.shape, lambda b, g, i: (0,) * a.ndim)
    kv_kind = lambda kind: pl.BlockSpec((S, LANES), lambda b, g, i: (b, kind))
    return pl.pallas_call(
        functools.partial(_nsa_kernel, tq=tq, tk=tk, nt=nt),
        grid=(B, G, nq),
        in_specs=[
            pl.BlockSpec((nt * tq, NSA_REP * HEAD_DIM), lambda b, g, i: (b * nq + i, g)),
            pl.BlockSpec((nt * tq, LANES), lambda b, g, i: (b * nq + i, 0)),
            per_bg((n_ck, LANES)), per_bg((VT_ROWS, n_ck)),
            kv_kind(2), kv_kind(3), kv_kind(4), kv_kind(5),
            full(kcols), full(vtail), full(band), full(cmpmask), full(ovt),
        ],
        out_specs=pl.BlockSpec((nt * tq, NSA_REP * HEAD_DIM), lambda b, g, i: (b * nq + i, g)),
        out_shape=jax.ShapeDtypeStruct((B * S, Q_NSA), BF16),
        scratch_shapes=[pltpu.VMEM((nt, n_slc, rows), F32), pltpu.VMEM((nt, 1, rows), F32),
                        pltpu.VMEM((nt, VT_ROWS, rows), F32),
                        pltpu.VMEM((S, LANES), BF16), pltpu.VMEM((VT_ROWS, S), BF16),
                        pltpu.VMEM((S, LANES), BF16), pltpu.VMEM((VT_ROWS, S), BF16),
                        pltpu.SMEM((nt, n_list), jnp.int32)],
        compiler_params=_params(("arbitrary", "arbitrary", "arbitrary")),
        name="nsa",
    )(q_pad, gates, kc, vct, kv_pad, kv_pad, kv_pad, kv_pad, kcols, vtail, band, cmpmask, ovt)


def _dil_attn_kernel(q_ref, k_ref, v_ref, dist_ref, o_ref, lse_ref, vt_scr, *, tq, ts, width, max_back, slopes):
    i = pl.program_id(1)
    nb, L, _ = k_ref.shape
    t_chunk = min(L, 512)

    @pl.when(i == 0)
    def _():
        tail_row = lax.broadcasted_iota(jnp.int32, (VT_ROWS - HEAD_DIM, L), 0)
        tail = jnp.where(tail_row == 0, 1.0, 0.0).astype(BF16)
        for b in range(nb):
            for h in range(DIL_HPG):
                vt_scr[b, h * VT_ROWS + HEAD_DIM:(h + 1) * VT_ROWS, :] = tail
            for pair in range(DIL_OUT // LANES):
                for c in range(L // t_chunk):
                    cols = slice(c * t_chunk, (c + 1) * t_chunk)
                    v_t = v_ref[b, cols, pair * LANES:(pair + 1) * LANES].astype(F32).T.astype(BF16)
                    for half in range(2):
                        h = 2 * pair + half
                        vt_scr[b, h * VT_ROWS:h * VT_ROWS + HEAD_DIM, cols] = v_t[half * HEAD_DIM:(half + 1) * HEAD_DIM]

    n_sub = tq // ts
    backs = [jnp.minimum(i * tq + u * ts, max_back) for u in range(n_sub)]
    kss = [pl.multiple_of(i * tq + u * ts - backs[u], max_back) for u in range(n_sub)]
    dists = [dist_ref[pl.ds(pl.multiple_of(max_back - backs[u], max_back), width), :] for u in range(n_sub)]
    lane = lax.broadcasted_iota(jnp.int32, (1, LANES), 1)
    chains = [(b, u, h) for b in range(nb) for u in range(n_sub) for h in range(DIL_HPG)]
    scores = []
    for b, u, h in chains:
        pair, half = divmod(h, 2)
        in_head = (lane >= half * HEAD_DIM) & (lane < (half + 1) * HEAD_DIM)
        q = q_ref[b, u * ts:(u + 1) * ts, pair * LANES:(pair + 1) * LANES]
        q = jnp.where(in_head, q, jnp.zeros_like(q))
        k = k_ref[b, pl.ds(kss[u], width), pair * LANES:(pair + 1) * LANES]
        scores.append(_dot_nt(k, q) - (slopes[h] * LOG2E) * dists[u])
    maxes = [jnp.max(s_t, axis=0, keepdims=True) for s_t in scores]
    probs = [_exp2_bf16(s_t - m) for s_t, m in zip(scores, maxes)]
    accs = [_dot(vt_scr[b, h * VT_ROWS:(h + 1) * VT_ROWS, pl.ds(kss[u], width)], p)
            for (b, u, h), p in zip(chains, probs)]
    dens = [acc[HEAD_DIM:HEAD_DIM + 1] for acc in accs]
    outs = [acc[0:HEAD_DIM] * (1.0 / den) for acc, den in zip(accs, dens)]
    lses = [jnp.broadcast_to((m + jnp.log2(den)) * (1.0 / LOG2E), (HEAD_DIM, ts)) for m, den in zip(maxes, dens)]
    for n, (b, u) in enumerate((b, u) for b in range(nb) for u in range(n_sub)):
        rows = slice(u * ts, (u + 1) * ts)
        o_ref[b, rows, :] = jnp.concatenate(outs[n * DIL_HPG:(n + 1) * DIL_HPG], axis=0).T
        lse_ref[b, rows, :] = jnp.concatenate(lses[n * DIL_HPG:(n + 1) * DIL_HPG], axis=0).T


def _dil_attn(qkv, dist_tbl, *, max_back, slopes, tq, ts, nb):
    N, L, _ = qkv.shape
    width = dist_tbl.shape[0] - max_back
    kern = functools.partial(_dil_attn_kernel, tq=tq, ts=ts, width=width, max_back=max_back, slopes=slopes)
    return pl.pallas_call(
        kern,
        grid=(N // nb, L // tq),
        in_specs=[
            pl.BlockSpec((nb, tq, DIL_OUT), lambda n, i: (n, i, 0)),
            pl.BlockSpec((nb, L, DIL_OUT), lambda n, i: (n, 0, 1)),
            pl.BlockSpec((nb, L, DIL_OUT), lambda n, i: (n, 0, 2)),
            pl.BlockSpec(dist_tbl.shape, lambda n, i: (0, 0)),
        ],
        out_specs=[
            pl.BlockSpec((nb, tq, DIL_OUT), lambda n, i: (n, i, 0)),
            pl.BlockSpec((nb, tq, DIL_OUT), lambda n, i: (n, i, 0)),
        ],
        out_shape=[
            jax.ShapeDtypeStruct((N, L, DIL_OUT), F32),
            jax.ShapeDtypeStruct((N, L, DIL_OUT), F32),
        ],
        scratch_shapes=[pltpu.VMEM((nb, DIL_HPG * VT_ROWS, L), BF16)],
        compiler_params=_params(("parallel", "arbitrary")),
        name="dil_attn",
    )(qkv, qkv, qkv, dist_tbl)


def _final_kernel(x_ref, pre_ref, post_ref, onsa_ref, od0_ref, od1_ref, od2_ref, l0_ref, l1_ref, l2_ref,
                  wmerge_ref, wnsa_ref, wdil_ref, wout_ref, o_ref, scr):
    tm = x_ref.shape[0]
    x = x_ref[...]
    h = _rms(x, pre_ref[...]).astype(BF16)
    merge = _dot(h, wmerge_ref[...])
    g_a = _sigmoid(merge[:, :D_MODEL])
    g_b = _sigmoid(merge[:, D_MODEL:])
    y_nsa = _dot(onsa_ref[...], wnsa_ref[...])

    n_blk = DIL_OUT // LANES
    slot = 0
    token_major = []
    for (_, dil), o_in, l_in in zip(DIL_CONFIGS, (od0_ref, od1_ref, od2_ref), (l0_ref, l1_ref, l2_ref)):
        pair = []
        for src in (o_in, l_in):
            if dil == 1:
                pair.append(src[0])
                continue
            for r in range(dil):
                for c in range(n_blk):
                    scr[slot + c, pl.ds(r, tm // dil, stride=dil), :] = src[r, :, c * LANES:(c + 1) * LANES]
            pair.append(jnp.concatenate([scr[slot + c] for c in range(n_blk)], axis=1))
            slot += n_blk
        token_major.append(pair)
    (o0, l0), (o1, l1), (o2, l2) = token_major
    mx = jnp.maximum(jnp.maximum(l0, l1), l2)
    e0, e1, e2 = jnp.exp(l0 - mx), jnp.exp(l1 - mx), jnp.exp(l2 - mx)
    den = e0 + e1 + e2
    o_dil = (e0 / den) * o0 + (e1 / den) * o1 + (e2 / den) * o2
    y_dil = _dot(o_dil.astype(BF16), wdil_ref[...])

    y = g_a * y_nsa + g_b * y_dil
    z = _dot(y.astype(BF16), wout_ref[...])
    o_ref[...] = x + _rms(z, post_ref[...])


def _final(x2d, pre, post, onsa, od, ld, wmerge, wnsa, wdil, wout, *, S, tm=512):
    T = x2d.shape[0]
    nt = S // tm
    tok = lambda w: pl.BlockSpec((tm, w), lambda i: (i, 0))
    full = lambda a: pl.BlockSpec(a.shape, lambda i: (0,) * a.ndim)
    strided = [pl.BlockSpec((None, dil, tm // dil, DIL_OUT), lambda i: (i // nt, 0, i % nt, 0))
               for _, dil in DIL_CONFIGS]
    n_strided = sum(1 for _, dil in DIL_CONFIGS if dil > 1)
    return pl.pallas_call(
        _final_kernel,
        grid=(T // tm,),
        in_specs=[tok(D_MODEL), _resident(pre), _resident(post), tok(Q_NSA)] + strided + strided
        + [_resident(wmerge), _resident(wnsa), _resident(wdil), _resident(wout)],
        out_specs=tok(D_MODEL),
        out_shape=jax.ShapeDtypeStruct((T, D_MODEL), F32),
        scratch_shapes=[pltpu.VMEM((2 * n_strided * (DIL_OUT // LANES), tm, LANES), F32)],
        compiler_params=_params(("parallel",)),
        name="final",
    )(x2d, pre, post, onsa, od[0], od[1], od[2], ld[0], ld[1], ld[2], wmerge, wnsa, wdil, wout)


def _overlap_t(n_ck, n_slc):
    c0 = np.arange(n_ck)[None, :] * CMP_STRIDE
    s0 = np.arange(n_slc)[:, None] * SLC_LEN
    ov = np.clip(np.minimum(c0 + CMP_LEN, s0 + SLC_LEN) - np.maximum(c0, s0), 0, None) / CMP_LEN
    ov[:, n_ck - 1] = 0.0
    return ov.astype(np.float32)


def _pos_columns(pos):
    cols = np.zeros((len(pos), LANES), np.float32)
    for t, vals in enumerate((pos // SLC_LEN, pos % SLC_LEN, np.ones(len(pos)))):
        for u in range(3):
            cols[:, HEAD_DIM + 3 * t + u] = vals
    return cols


def _ones_row_tail(n):
    tail = np.zeros((VT_ROWS, n), np.float32)
    tail[HEAD_DIM] = 1.0
    return tail


def _band_table(tq):
    u = np.arange(2 * NSA_WINDOW + tq)[:, None]
    r = (np.arange(NSA_REP * tq) % tq)[None, :]
    dist = NSA_WINDOW + r - u
    return np.where((dist >= 0) & (dist <= NSA_WINDOW - 1), 0.0, NEG).astype(np.float32)


def _dil_dist_table(tq, width, max_back, dil):
    u = np.arange(width + max_back)[:, None]
    r = np.arange(tq)[None, :]
    dist = max_back + r - u
    return np.where((dist >= 0) & (dist <= max_back), dist * dil, -NEG).astype(np.float32)


def _cmp_mask_table(n_ck, tq):
    u = np.arange(2 * n_ck - 8)[:, None]
    r = (np.arange(NSA_REP * tq) % tq)[None, :]
    end = (u - (n_ck - 8)) * CMP_STRIDE + CMP_LEN - 1
    return np.where(end <= r, 0.0, NEG).astype(np.float32)


def _mixer(x2d, B, S, mix_pre, mix_post, w_in, pe_k, w_ck1, w_ck2, pe_v, w_cv1, w_cv2,
           w_nsa_o, w_dil_o, w_mix_out):
    G = NSA_GROUPS
    BG = B * G
    assert S % (16 * 128) == 0 and NSA_TQ // CMP_STRIDE == 8
    o_q, o_kv, o_g, o_d, o_m = np.cumsum([0, Q_NSA, 6 * KV_NSA, GATE_NSA, 3 * QKV_DIL]).tolist()
    wq = w_in[:, o_q:o_kv] * (SCALE * LOG2E)
    w_gate = jnp.pad(w_in[:, o_g:o_d], ((0, 0), (0, LANES - GATE_NSA)))
    wd = w_in[:, o_d:o_m]
    q_pad, kv_pad, gates, xk, xv, *qkv_d = _proj(
        x2d, mix_pre[None, :], wq.astype(BF16), w_in[:, o_kv:o_g].astype(BF16), wd.astype(BF16),
        w_gate.astype(BF16), B=B, S=S)

    n_ck = S // CMP_STRIDE
    half = CMP_STRIDE * HEAD_DIM
    w1 = jnp.stack([w_ck1, w_cv1])
    w1cat = jnp.concatenate([w1[:, :half], w1[:, half:]], axis=2).astype(BF16)
    pe = jnp.stack([pe_k.reshape(1, -1), pe_v.reshape(1, -1)]).astype(BF16)
    w2 = jnp.pad(jnp.stack([w_ck2, w_cv2]), ((0, 0), (0, 0), (0, LANES - HEAD_DIM))).astype(BF16)
    cmp_pos = np.arange(n_ck) * CMP_STRIDE + CMP_LEN - 1
    kc, vct = _cmp_mlp(xk.reshape(BG, n_ck, half), xv.reshape(BG, n_ck, half), pe, w1cat, w1.astype(BF16), w2,
                       jnp.asarray(_pos_columns(cmp_pos), BF16), jnp.asarray(_ones_row_tail(n_ck), BF16))

    band = jnp.asarray(_band_table(NSA_TQ))
    cmpmask = jnp.asarray(_cmp_mask_table(n_ck, NSA_TQ))
    ovt = jnp.asarray(_overlap_t(n_ck, S // SLC_LEN), BF16)
    o_nsa = _nsa(q_pad, gates, kc, vct, kv_pad, jnp.asarray(_pos_columns(np.arange(S)), BF16),
                 jnp.asarray(_ones_row_tail(S), BF16), band, cmpmask, ovt, B=B, S=S)

    od, ld = [], []
    for gi, (window, dil) in enumerate(DIL_CONFIGS):
        Ls = S // dil
        max_back = window // dil
        tq = min(DIL_TQ, Ls)
        ts = min(DIL_TS, Ls)
        width = ts + max_back if Ls > ts else ts
        o, lse = _dil_attn(qkv_d[gi].reshape(B * dil, Ls, DIL_QKV),
                           jnp.asarray(_dil_dist_table(ts, width, max_back, dil)),
                           max_back=max_back, slopes=DIL_SLOPES[gi * DIL_HPG:(gi + 1) * DIL_HPG], tq=tq, ts=ts,
                           nb=DIL_TQ // tq)
        od.append(o.reshape(B, dil, Ls, DIL_OUT))
        ld.append(lse.reshape(B, dil, Ls, DIL_OUT))

    return _final(x2d, mix_pre[None, :], mix_post[None, :], o_nsa, od, ld, w_in[:, o_m:].astype(BF16),
                  w_nsa_o.astype(BF16), w_dil_o.astype(BF16), w_mix_out.astype(BF16), S=S)


def kernel(x, ffn1_pre, ffn1_post, ffn1_w_gate, ffn1_w_up, ffn1_w_down, mix_pre, mix_post, w_in, nsa_pe_k, nsa_w_ck1, nsa_w_ck2, nsa_pe_v, nsa_w_cv1, nsa_w_cv2, w_nsa_o, w_dil_o, w_mix_out, ffn2_pre, ffn2_post, ffn2_w_gate, ffn2_w_up, ffn2_w_down):
    B, S, _ = x.shape
    x2d = x.reshape(B * S, D_MODEL)
    for l in range(ffn1_pre.shape[0]):
        x2d = _ffn(x2d, ffn1_pre[l][None, :], ffn1_post[l][None, :], ffn1_w_gate[l].astype(BF16),
                   ffn1_w_up[l].astype(BF16), ffn1_w_down[l].astype(BF16))
        x2d = _mixer(x2d, B, S, mix_pre[l], mix_post[l], w_in[l], nsa_pe_k[l], nsa_w_ck1[l], nsa_w_ck2[l],
                     nsa_pe_v[l], nsa_w_cv1[l], nsa_w_cv2[l], w_nsa_o[l], w_dil_o[l], w_mix_out[l])
        x2d = _ffn(x2d, ffn2_pre[l][None, :], ffn2_post[l][None, :], ffn2_w_gate[l].astype(BF16),
                   ffn2_w_up[l].astype(BF16), ffn2_w_down[l].astype(BF16))
    return x2d.reshape(B, S, D_MODEL)
```

```python
import functools
import math

import numpy as np
import jax
import jax.numpy as jnp
from jax import lax
from jax.experimental import pallas as pl
from jax.experimental.pallas import tpu as pltpu

F32 = jnp.float32
BF16 = jnp.bfloat16

D_MODEL = 1024
HEAD_DIM = 64
RMS_EPS = 1e-6
NSA_HEADS = 8
NSA_GROUPS = 2
NSA_REP = NSA_HEADS // NSA_GROUPS
CMP_LEN = 32
CMP_STRIDE = 16
CMP_HIDDEN = 256
SLC_LEN = 64
N_SEL = 16
NSA_WINDOW = 512
DIL_CONFIGS = ((128, 1), (512, 4), (2048, 16))
DIL_HPG = 4
DIL_HEADS = DIL_HPG * len(DIL_CONFIGS)
D_FF = 2816
Q_NSA = NSA_HEADS * HEAD_DIM
KV_NSA = NSA_GROUPS * HEAD_DIM
GATE_NSA = NSA_HEADS * 3
QKV_DIL = DIL_HEADS * HEAD_DIM
DIL_OUT = DIL_HPG * HEAD_DIM
DIL_QKV = 3 * DIL_OUT
DIL_TQ = 1024
DIL_TS = 128
LANES = 128
NEG = -1e30
SCALE = HEAD_DIM ** -0.5
LOG2E = math.log2(math.e)
VMEM_LIMIT = 56 * 1024 * 1024
NSA_TQ = 128
NSA_TILES_PER_STEP = 4
GATHER_BLOCKS = 2
VT_ROWS = 80


def _slopes(n):
    return [float(np.float32(2.0 ** (-8.0 * (i + 1) / n))) for i in range(n)]


NSA_SLOPES = _slopes(NSA_HEADS)
DIL_SLOPES = _slopes(DIL_HEADS)


def _params(sem):
    return pltpu.CompilerParams(dimension_semantics=sem, vmem_limit_bytes=VMEM_LIMIT)


def _rms(x, g):
    ms = jnp.mean(x * x, axis=-1, keepdims=True)
    return x * lax.rsqrt(ms + RMS_EPS) * g


def _sigmoid(x):
    return 1.0 / (1.0 + jnp.exp(-x))


def _dot(a, b):
    return jnp.dot(a, b, preferred_element_type=F32)


def _dot_nt(a, b):
    return lax.dot_general(a, b, (((1,), (1,)), ((), ())), preferred_element_type=F32)


def _split3(a):
    hi = a.astype(BF16)
    r1 = a - hi.astype(F32)
    mid = r1.astype(BF16)
    lo = (r1 - mid.astype(F32)).astype(BF16)
    return hi, mid, lo


def _ffn_kernel(x_ref, pre_ref, post_ref, wg_ref, wu_ref, wd_ref, o_ref):
    x = x_ref[...]
    h = _rms(x, pre_ref[...]).astype(BF16)
    g = _dot(h, wg_ref[...])
    u = _dot(h, wu_ref[...])
    a = (g * _sigmoid(g) * u).astype(BF16)
    o_ref[...] = x + 0.5 * _rms(_dot(a, wd_ref[...]), post_ref[...])


def _resident(a):
    return pl.BlockSpec(a.shape, lambda *_: (0,) * a.ndim, pipeline_mode=pl.Buffered(1))


def _ffn(x2d, pre, post, wg, wu, wd, *, tm=512):
    T = x2d.shape[0]
    return pl.pallas_call(
        _ffn_kernel,
        grid=(T // tm,),
        in_specs=[pl.BlockSpec((tm, D_MODEL), lambda i: (i, 0)), _resident(pre), _resident(post),
                  _resident(wg), _resident(wu), _resident(wd)],
        out_specs=pl.BlockSpec((tm, D_MODEL), lambda i: (i, 0)),
        out_shape=jax.ShapeDtypeStruct((T, D_MODEL), F32),
        compiler_params=_params(("parallel",)),
        name="ffn",
    )(x2d, pre, post, wg, wu, wd)


def _proj_kernel(x_ref, pre_ref, wq_ref, wkv_ref, wd_ref, wgate_ref, q_ref, kv_ref, g_ref, xk_ref, xv_ref,
                 *rest):
    d_refs, scr = rest[:-1], rest[-1]
    tm = x_ref.shape[0]
    h = _rms(x_ref[...], pre_ref[...]).astype(BF16)
    q_ref[...] = _dot(h, wq_ref[...]).astype(BF16)
    kv = _dot(h, wkv_ref[...])
    kv_ref[...] = kv.astype(BF16)
    g_ref[...] = _dot(h, wgate_ref[...])
    lane = lax.broadcasted_iota(jnp.int32, (1, LANES), 1)
    n_rows = tm // CMP_STRIDE
    for kind, out in ((0, xk_ref), (1, xv_ref)):
        scr[kind] = kv[:, kind * LANES:(kind + 1) * LANES]
        for j in range(CMP_STRIDE // 2):
            even = scr[kind, pl.ds(2 * j, n_rows, stride=CMP_STRIDE), :]
            odd = scr[kind, pl.ds(2 * j + 1, n_rows, stride=CMP_STRIDE), :]
            out[0, :, j * LANES:(j + 1) * LANES] = jnp.where(
                lane < HEAD_DIM, even, pltpu.roll(odd, HEAD_DIM, 1)).astype(BF16)
            out[1, :, j * LANES:(j + 1) * LANES] = jnp.where(
                lane < HEAD_DIM, pltpu.roll(even, HEAD_DIM, 1), odd).astype(BF16)
    n_blk = DIL_QKV // LANES
    for gi, (_, dil) in enumerate(DIL_CONFIGS):
        parts = [_dot(h, wd_ref[:, t * QKV_DIL + gi * DIL_OUT:t * QKV_DIL + (gi + 1) * DIL_OUT]) for t in range(3)]
        res = jnp.concatenate([parts[0] * (SCALE * LOG2E), parts[1], parts[2]], axis=1)
        if dil == 1:
            d_refs[gi][0] = res.astype(BF16)
            continue
        for c in range(n_blk):
            scr[c] = res[:, c * LANES:(c + 1) * LANES]
        for r in range(dil):
            for c in range(n_blk):
                rows = scr[c, pl.ds(r, tm // dil, stride=dil), :]
                d_refs[gi][r, :, c * LANES:(c + 1) * LANES] = rows.astype(BF16)


def _proj(x2d, pre, wq, wkv, wd, wgate, *, B, S, tm=512):
    T = x2d.shape[0]
    nt = S // tm
    tok = lambda w: pl.BlockSpec((tm, w), lambda i: (i, 0))
    full = lambda a: pl.BlockSpec(a.shape, lambda i: (0,) * a.ndim)
    d_specs = [pl.BlockSpec((None, dil, tm // dil, DIL_QKV), lambda i: (i // nt, 0, i % nt, 0))
               for _, dil in DIL_CONFIGS]
    d_shapes = [jax.ShapeDtypeStruct((B, dil, S // dil, DIL_QKV), BF16) for _, dil in DIL_CONFIGS]
    chunk_width = CMP_STRIDE * HEAD_DIM
    c_spec = pl.BlockSpec((None, NSA_GROUPS, tm // CMP_STRIDE, chunk_width), lambda i: (i // nt, 0, i % nt, 0))
    c_shape = jax.ShapeDtypeStruct((B, NSA_GROUPS, S // CMP_STRIDE, chunk_width), BF16)
    return pl.pallas_call(
        _proj_kernel,
        grid=(T // tm,),
        in_specs=[tok(D_MODEL), _resident(pre), _resident(wq), _resident(wkv), _resident(wd), _resident(wgate)],
        out_specs=[tok(wq.shape[1]), tok(wkv.shape[1]), tok(LANES), c_spec, c_spec] + d_specs,
        out_shape=[jax.ShapeDtypeStruct((T, wq.shape[1]), BF16), jax.ShapeDtypeStruct((T, wkv.shape[1]), BF16),
                   jax.ShapeDtypeStruct((T, LANES), F32), c_shape, c_shape] + d_shapes,
        scratch_shapes=[pltpu.VMEM((DIL_QKV // LANES, tm, LANES), F32)],
        compiler_params=_params(("parallel",)),
        name="proj",
    )(x2d, pre, wq, wkv, wd, wgate)


def _gelu_tanh(x):
    c = math.sqrt(2.0 / math.pi)
    return x * (0.5 * (1.0 + jnp.tanh(c * (x + 0.044715 * (x * x * x)))))


def _cmp_mlp_kernel(xk_ref, xv_ref, pe_ref, w1cat_ref, w1_ref, w2_ref, kcols_ref, vtail_ref, kc_ref, vct_ref):
    n_chunks = xk_ref.shape[0]

    def mlp(x_ref, t):
        ab = _dot(x_ref[...], w1cat_ref[t])
        first = ab[:, :CMP_HIDDEN]
        second = pltpu.roll(ab[:, CMP_HIDDEN:], n_chunks - 1, 0)
        pe8 = jnp.broadcast_to(pe_ref[t], (8, pe_ref.shape[2]))
        bias = _dot(pe8, w1_ref[t])[0:1]
        hid = _gelu_tanh(first + second + bias).astype(BF16)
        out = _dot(hid, w2_ref[t])
        row = lax.broadcasted_iota(jnp.int32, out.shape, 0)
        return jnp.where(row < n_chunks - 1, out, 0.0)

    kc_ref[...] = mlp(xk_ref, 0).astype(BF16) + kcols_ref[...]
    vct_ref[...] = mlp(xv_ref, 1).T[0:VT_ROWS].astype(BF16) + vtail_ref[...]


def _cmp_mlp(xk, xv, pe, w1cat, w1, w2, kcols, vtail):
    BG, n_chunks, width = xk.shape
    per_n = lambda shape: pl.BlockSpec((None,) + shape, lambda n: (n, 0, 0))
    full = lambda a: pl.BlockSpec(a.shape, lambda n: (0,) * a.ndim)
    return pl.pallas_call(
        _cmp_mlp_kernel,
        grid=(BG,),
        in_specs=[per_n((n_chunks, width)), per_n((n_chunks, width)), full(pe), full(w1cat), full(w1),
                  full(w2), full(kcols), full(vtail)],
        out_specs=[per_n((n_chunks, LANES)), per_n((VT_ROWS, n_chunks))],
        out_shape=[jax.ShapeDtypeStruct((BG, n_chunks, LANES), BF16),
                   jax.ShapeDtypeStruct((BG, VT_ROWS, n_chunks), BF16)],
        compiler_params=_params(("parallel",)),
        name="cmp_mlp",
    )(xk, xv, pe, w1cat, w1, w2, kcols, vtail)


def _exp2_bf16(x):
    return jnp.exp2(x.astype(BF16))


def _low_half(x2, half):
    lane = lax.broadcasted_iota(jnp.int32, (1, LANES), 1)
    xf = x2.astype(F32)
    if half == 1:
        xf = pltpu.roll(xf, HEAD_DIM, 1)
    return jnp.where(lane < HEAD_DIM, xf, 0.0).astype(BF16)


def _nsa_kernel(q_ref, gate_ref, kc_ref, vct_ref, kslc_ref, vslc_ref, kwin_ref, vwin_ref, kcols_ref,
                vtail_ref, band_ref, cmpmask_ref, ovt_ref, o_ref,
                selbias_scr, m_scr, acc_scr, ks_ref, vst_ref, kw_ref, vwt_ref, list_smem,
                *, tq, tk, nt):
    g = pl.program_id(1)
    i = pl.program_id(2)
    n_ck = kc_ref.shape[0]
    n_slc = ovt_ref.shape[0]
    S = kslc_ref.shape[0]

    @pl.when(i == 0)
    def _():
        t_chunk = min(S, 512)
        for src, dst in ((kslc_ref, ks_ref), (kwin_ref, kw_ref)):
            for c in range(S // t_chunk):
                rows_c = slice(c * t_chunk, (c + 1) * t_chunk)
                k2 = src[rows_c, :]
                mine = jnp.where(g == 0, _low_half(k2, 0), _low_half(k2, 1))
                dst[rows_c, :] = mine + kcols_ref[rows_c, :]
        for src, dst in ((vslc_ref, vst_ref), (vwin_ref, vwt_ref)):
            dst[HEAD_DIM:VT_ROWS, :] = vtail_ref[HEAD_DIM:VT_ROWS, :]
            for c in range(S // t_chunk):
                v = src[c * t_chunk:(c + 1) * t_chunk, :].astype(F32)
                v_t = v.T
                mine = jnp.where(g == 0, v_t[0:HEAD_DIM], v_t[HEAD_DIM:2 * HEAD_DIM])
                dst[0:HEAD_DIM, c * t_chunk:(c + 1) * t_chunk] = mine.astype(BF16)

    tiles = range(nt)
    lane = lax.broadcasted_iota(jnp.int32, (1, LANES), 1)
    width = NSA_WINDOW + tq
    blocks_per_chunk = tk // SLC_LEN
    n_w = width // SLC_LEN
    jb = lax.broadcasted_iota(jnp.int32, (n_slc, 1), 0)
    jbf = jb.astype(F32)
    q0s = [(i * nt + t) * tq for t in tiles]
    backs = [jnp.minimum(q0, NSA_WINDOW) for q0 in q0s]
    wss = [pl.multiple_of(q0 - back, tq) for q0, back in zip(q0s, backs)]
    qss = [pl.multiple_of(q0, tq) for q0 in q0s]

    q_augs = []
    for t in tiles:
        slabs = []
        for r in range(NSA_REP):
            sl = jnp.where(g == 0, NSA_SLOPES[r], NSA_SLOPES[NSA_REP + r]) * LOG2E
            meta = jnp.zeros((1, LANES), F32)
            for c, coef in enumerate((SLC_LEN * sl, sl, -sl * q0s[t].astype(F32))):
                terms = _split3(jnp.full((1, LANES), coef, F32))
                for u in range(3):
                    meta = jnp.where(lane == HEAD_DIM + 3 * c + u, terms[u].astype(F32), meta)
            q2 = q_ref[t * tq:(t + 1) * tq, (r // 2) * LANES:(r // 2 + 1) * LANES]
            slabs.append(_low_half(q2, r % 2) + meta.astype(BF16))
        q_augs.append(jnp.concatenate(slabs, axis=0))

    s_alls = [_dot_nt(jnp.concatenate([kc_ref[...], kw_ref[pl.ds(wss[t], width), :],
                                       ks_ref[pl.ds(qss[t], tq), :]], axis=0), q_augs[t]) for t in tiles]

    s_cs = [s_alls[t][0:n_ck] + cmpmask_ref[pl.ds(pl.multiple_of(
        (n_ck - 8) - (tq // CMP_STRIDE) * (i * nt + t), 8), n_ck), :] for t in tiles]
    m_cs = [jnp.max(s, axis=0, keepdims=True) for s in s_cs]
    p_cs = [jnp.exp2(s - m) for s, m in zip(s_cs, m_cs)]
    den_cs = [jnp.sum(p, axis=0, keepdims=True) for p in p_cs]
    p_cs = [p * jnp.where(m > 0.5 * NEG, 1.0 / den, 0.0) for p, m, den in zip(p_cs, m_cs, den_cs)]
    o_cmps = [_dot(vct_ref[0:HEAD_DIM, :], p.astype(BF16)) for p in p_cs]

    ovt = ovt_ref[...]
    imps = []
    for p in p_cs:
        psum = p[:, 0:tq]
        for r in range(1, NSA_REP):
            psum = psum + p[:, r * tq:(r + 1) * tq]
        hi, mid, _ = _split3(psum)
        imps.append(_dot(ovt, hi) + _dot(ovt, mid))
    vals, sels = [], []
    for t in tiles:
        pos = q0s[t] + lax.broadcasted_iota(jnp.int32, (1, tq), 1)
        own = pos // SLC_LEN
        forced = (jb == 0) | (jb == own) | (jb == own - 1)
        valid = jb * SLC_LEN <= pos
        vals.append(jnp.where(valid & jnp.logical_not(forced), imps[t], -jnp.inf))
        sels.append(jnp.where(forced, 1.0, 0.0))

    s_ws = [s_alls[t][n_ck:n_ck + width]
            + band_ref[pl.ds(pl.multiple_of(NSA_WINDOW - backs[t], tq), width), :] for t in tiles]
    m_ws = [jnp.max(s, axis=0, keepdims=True) for s in s_ws]
    s_ds = [s_alls[t][n_ck + width:n_ck + width + tq] + band_ref[NSA_WINDOW:NSA_WINDOW + tq, :] for t in tiles]
    m_ds = [jnp.max(s, axis=0, keepdims=True) for s in s_ds]
    slices = [([(s_ws[t], m_ws[t], r0) for r0 in range(0, width, SLC_LEN)]
               + [(s_ds[t], m_ds[t], r0) for r0 in range(0, tq, SLC_LEN)]) for t in tiles]
    probs = [[] for _ in tiles]

    for it in range(max(N_SEL - 3, len(slices[0]))):
        for t in tiles:
            if it < N_SEL - 3:
                mx = jnp.max(vals[t], axis=0, keepdims=True)
                cand = jnp.where(vals[t] == mx, jbf, 1e9)
                idx = jnp.min(cand, axis=0, keepdims=True)
                pick = (jbf == idx) & (mx > -jnp.inf)
                sels[t] = jnp.where(pick, 1.0, sels[t])
                vals[t] = jnp.where(pick, -jnp.inf, vals[t])
            if it < len(slices[t]):
                src, m_src, r0 = slices[t][it]
                probs[t].append(_exp2_bf16(src[r0:r0 + SLC_LEN] - m_src))
    o_wins = []
    for t in tiles:
        acc_w = _dot(vwt_ref[:, pl.ds(wss[t], width)], jnp.concatenate(probs[t][:n_w], axis=0))
        o_wins.append(acc_w[0:HEAD_DIM] * (1.0 / acc_w[HEAD_DIM:HEAD_DIM + 1]))
        m_scr[t] = m_ds[t]
        acc_scr[t] = _dot(vst_ref[:, pl.ds(qss[t], tq)], jnp.concatenate(probs[t][n_w:], axis=0))

    n_needs = []
    bit = jnp.exp2((jb % 16).astype(F32))
    masks = []
    for t in tiles:
        chosen = (sels[t] > 0.5) & (jb < q0s[t] // SLC_LEN)
        sbias = jnp.where(chosen, 0.0, NEG)
        selbias_scr[t] = jnp.concatenate([sbias] * NSA_REP, axis=1)
        any_q = jnp.max(jnp.where(chosen, 1.0, 0.0), axis=1, keepdims=True) * bit
        masks.append([jnp.sum(any_q[16 * w:16 * (w + 1)]).astype(jnp.int32) for w in range(n_slc // 16)])
    for t in tiles:
        n_need = jnp.int32(0)
        for k in range(list_smem.shape[1]):
            list_smem[t, k] = jnp.int32(0)
        for c in range(n_slc // GATHER_BLOCKS):
            word, shift = divmod(GATHER_BLOCKS * c, 16)
            pair_bits = lax.shift_right_logical(masks[t][word], jnp.int32(shift)) & (2 ** GATHER_BLOCKS - 1)
            list_smem[t, n_need] = jnp.int32(c)
            n_need = n_need + jnp.minimum(pair_bits, 1)
        n_needs.append(n_need)

    gk = GATHER_BLOCKS * SLC_LEN
    slots = tk // gk

    def entry(t, k):
        live = k < n_needs[t]
        c = jnp.where(live, list_smem[t, k], 0)
        return c, jnp.where(live, 0.0, NEG)

    def joint_body(j, carry):
        ents = [[entry(t, slots * j + u) for u in range(slots)] for t in tiles]
        offs = [[pl.multiple_of(c * gk, gk) for c, _ in ent] for ent in ents]
        scores = [_dot_nt(jnp.concatenate([ks_ref[pl.ds(o, gk), :] for o in off], axis=0), q_augs[t])
                  for t, off in zip(tiles, offs)]
        masked = []
        for t, s, ent in zip(tiles, scores, ents):
            parts = []
            for u, (c, pad) in enumerate(ent):
                for jj in range(GATHER_BLOCKS):
                    brow = selbias_scr[t, pl.ds(c * GATHER_BLOCKS + jj, 1), :] + pad
                    r0 = u * gk + jj * SLC_LEN
                    parts.append(s[r0:r0 + SLC_LEN] + brow)
            masked.append(jnp.concatenate(parts, axis=0))
        m_old = [m_scr[t] for t in tiles]
        m_new = [jnp.maximum(m, jnp.max(s, axis=0, keepdims=True)) for m, s in zip(m_old, masked)]
        probs = [_exp2_bf16(s - m) for s, m in zip(masked, m_new)]
        for t, off, p, mo, mn in zip(tiles, offs, probs, m_old, m_new):
            v_g = jnp.concatenate([vst_ref[:, pl.ds(o, gk)] for o in off], axis=1)
            acc_scr[t] = jnp.exp2(mo - mn) * acc_scr[t] + _dot(v_g, p)
            m_scr[t] = mn
        return carry

    trips = functools.reduce(jnp.maximum, [(n_needs[t] + slots - 1) // slots for t in tiles])
    lax.fori_loop(0, trips, joint_body, 0)

    for t in tiles:
        acc = acc_scr[t]
        o_slc = acc[0:HEAD_DIM] * (1.0 / acc[HEAD_DIM:HEAD_DIM + 1])
        gate_t = _sigmoid(gate_ref[t * tq:(t + 1) * tq, :].T)
        outs = []
        for r in range(NSA_REP):
            cols = slice(r * tq, (r + 1) * tq)

            def gate_row(branch):
                lo_row = 3 * r + branch
                hi_row = 3 * (NSA_REP + r) + branch
                return jnp.where(g == 0, gate_t[lo_row:lo_row + 1], gate_t[hi_row:hi_row + 1])

            outs.append(gate_row(0) * o_cmps[t][:, cols] + gate_row(1) * o_slc[:, cols]
                        + gate_row(2) * o_wins[t][:, cols])
        o_ref[t * tq:(t + 1) * tq, :] = jnp.concatenate(outs, axis=0).T.astype(BF16)


def _nsa(q_pad, gates, kc, vct, kv_pad, kcols, vtail, band, cmpmask, ovt, *, B, S, tq=NSA_TQ, tk=512,
         nt=NSA_TILES_PER_STEP):
    G = NSA_GROUPS
    n_ck = kc.shape[1]
    n_slc = ovt.shape[0]
    rows = NSA_REP * tq
    nq = S // (tq * nt)
    n_list = n_slc // GATHER_BLOCKS + tk // (GATHER_BLOCKS * SLC_LEN)
    per_bg = lambda shape: pl.BlockSpec((None,) + shape, lambda b, g, i: (b * G + g, 0, 0))
    full = lambda a: pl.BlockSpec(a.shape, lambda b, g, i: (0,) * a.ndim)
    kv_kind = lambda kind: pl.BlockSpec((S, LANES), lambda b, g, i: (b, kind))
    return pl.pallas_call(
        functools.partial(_nsa_kernel, tq=tq, tk=tk, nt=nt),
        grid=(B, G, nq),
        in_specs=[
            pl.BlockSpec((nt * tq, NSA_REP * HEAD_DIM), lambda b, g, i: (b * nq + i, g)),
            pl.BlockSpec((nt * tq, LANES), lambda b, g, i: (b * nq + i, 0)),
            per_bg((n_ck, LANES)), per_bg((VT_ROWS, n_ck)),
            kv_kind(2), kv_kind(3), kv_kind(4), kv_kind(5),
            full(kcols), full(vtail), full(band), full(cmpmask), full(ovt),
        ],
        out_specs=pl.BlockSpec((nt * tq, NSA_REP * HEAD_DIM), lambda b, g, i: (b * nq + i, g)),
        out_shape=jax.ShapeDtypeStruct((B * S, Q_NSA), BF16),
        scratch_shapes=[pltpu.VMEM((nt, n_slc, rows), F32), pltpu.VMEM((nt, 1, rows), F32),
                        pltpu.VMEM((nt, VT_ROWS, rows), F32),
                        pltpu.VMEM((S, LANES), BF16), pltpu.VMEM((VT_ROWS, S), BF16),
                        pltpu.VMEM((S, LANES), BF16), pltpu.VMEM((VT_ROWS, S), BF16),
                        pltpu.SMEM((nt, n_list), jnp.int32)],
        compiler_params=_params(("arbitrary", "arbitrary", "arbitrary")),
        name="nsa",
    )(q_pad, gates, kc, vct, kv_pad, kv_pad, kv_pad, kv_pad, kcols, vtail, band, cmpmask, ovt)


def _dil_attn_kernel(q_ref, k_ref, v_ref, dist_ref, o_ref, lse_ref, vt_scr, *, tq, ts, width, max_back, slopes):
    i = pl.program_id(1)
    nb, L, _ = k_ref.shape
    t_chunk = min(L, 512)

    @pl.when(i == 0)
    def _():
        tail_row = lax.broadcasted_iota(jnp.int32, (VT_ROWS - HEAD_DIM, L), 0)
        tail = jnp.where(tail_row == 0, 1.0, 0.0).astype(BF16)
        for b in range(nb):
            for h in range(DIL_HPG):
                vt_scr[b, h * VT_ROWS + HEAD_DIM:(h + 1) * VT_ROWS, :] = tail
            for pair in range(DIL_OUT // LANES):
                for c in range(L // t_chunk):
                    cols = slice(c * t_chunk, (c + 1) * t_chunk)
                    v_t = v_ref[b, cols, pair * LANES:(pair + 1) * LANES].astype(F32).T.astype(BF16)
                    for half in range(2):
                        h = 2 * pair + half
                        vt_scr[b, h * VT_ROWS:h * VT_ROWS + HEAD_DIM, cols] = v_t[half * HEAD_DIM:(half + 1) * HEAD_DIM]

    n_sub = tq // ts
    backs = [jnp.minimum(i * tq + u * ts, max_back) for u in range(n_sub)]
    kss = [pl.multiple_of(i * tq + u * ts - backs[u], max_back) for u in range(n_sub)]
    dists = [dist_ref[pl.ds(pl.multiple_of(max_back - backs[u], max_back), width), :] for u in range(n_sub)]
    lane = lax.broadcasted_iota(jnp.int32, (1, LANES), 1)
    chains = [(b, u, h) for b in range(nb) for u in range(n_sub) for h in range(DIL_HPG)]
    scores = []
    for b, u, h in chains:
        pair, half = divmod(h, 2)
        in_head = (lane >= half * HEAD_DIM) & (lane < (half + 1) * HEAD_DIM)
        q = q_ref[b, u * ts:(u + 1) * ts, pair * LANES:(pair + 1) * LANES]
        q = jnp.where(in_head, q, jnp.zeros_like(q))
        k = k_ref[b, pl.ds(kss[u], width), pair * LANES:(pair + 1) * LANES]
        scores.append(_dot_nt(k, q) - (slopes[h] * LOG2E) * dists[u])
    maxes = [jnp.max(s_t, axis=0, keepdims=True) for s_t in scores]
    probs = [_exp2_bf16(s_t - m) for s_t, m in zip(scores, maxes)]
    accs = [_dot(vt_scr[b, h * VT_ROWS:(h + 1) * VT_ROWS, pl.ds(kss[u], width)], p)
            for (b, u, h), p in zip(chains, probs)]
    dens = [acc[HEAD_DIM:HEAD_DIM + 1] for acc in accs]
    outs = [acc[0:HEAD_DIM] * (1.0 / den) for acc, den in zip(accs, dens)]
    lses = [jnp.broadcast_to((m + jnp.log2(den)) * (1.0 / LOG2E), (HEAD_DIM, ts)) for m, den in zip(maxes, dens)]
    for n, (b, u) in enumerate((b, u) for b in range(nb) for u in range(n_sub)):
        rows = slice(u * ts, (u + 1) * ts)
        o_ref[b, rows, :] = jnp.concatenate(outs[n * DIL_HPG:(n + 1) * DIL_HPG], axis=0).T
        lse_ref[b, rows, :] = jnp.concatenate(lses[n * DIL_HPG:(n + 1) * DIL_HPG], axis=0).T


def _dil_attn(qkv, dist_tbl, *, max_back, slopes, tq, ts, nb):
    N, L, _ = qkv.shape
    width = dist_tbl.shape[0] - max_back
    kern = functools.partial(_dil_attn_kernel, tq=tq, ts=ts, width=width, max_back=max_back, slopes=slopes)
    return pl.pallas_call(
        kern,
        grid=(N // nb, L // tq),
        in_specs=[
            pl.BlockSpec((nb, tq, DIL_OUT), lambda n, i: (n, i, 0)),
            pl.BlockSpec((nb, L, DIL_OUT), lambda n, i: (n, 0, 1)),
            pl.BlockSpec((nb, L, DIL_OUT), lambda n, i: (n, 0, 2)),
            pl.BlockSpec(dist_tbl.shape, lambda n, i: (0, 0)),
        ],
        out_specs=[
            pl.BlockSpec((nb, tq, DIL_OUT), lambda n, i: (n, i, 0)),
            pl.BlockSpec((nb, tq, DIL_OUT), lambda n, i: (n, i, 0)),
        ],
        out_shape=[
            jax.ShapeDtypeStruct((N, L, DIL_OUT), F32),
            jax.ShapeDtypeStruct((N, L, DIL_OUT), F32),
        ],
        scratch_shapes=[pltpu.VMEM((nb, DIL_HPG * VT_ROWS, L), BF16)],
        compiler_params=_params(("parallel", "arbitrary")),
        name="dil_attn",
    )(qkv, qkv, qkv, dist_tbl)


def _final_kernel(x_ref, pre_ref, post_ref, onsa_ref, od0_ref, od1_ref, od2_ref, l0_ref, l1_ref, l2_ref,
                  wmerge_ref, wnsa_ref, wdil_ref, wout_ref, o_ref, scr):
    tm = x_ref.shape[0]
    x = x_ref[...]
    h = _rms(x, pre_ref[...]).astype(BF16)
    merge = _dot(h, wmerge_ref[...])
    g_a = _sigmoid(merge[:, :D_MODEL])
    g_b = _sigmoid(merge[:, D_MODEL:])
    y_nsa = _dot(onsa_ref[...], wnsa_ref[...])

    n_blk = DIL_OUT // LANES
    slot = 0
    token_major = []
    for (_, dil), o_in, l_in in zip(DIL_CONFIGS, (od0_ref, od1_ref, od2_ref), (l0_ref, l1_ref, l2_ref)):
        pair = []
        for src in (o_in, l_in):
            if dil == 1:
                pair.append(src[0])
                continue
            for r in range(dil):
                for c in range(n_blk):
                    scr[slot + c, pl.ds(r, tm // dil, stride=dil), :] = src[r, :, c * LANES:(c + 1) * LANES]
            pair.append(jnp.concatenate([scr[slot + c] for c in range(n_blk)], axis=1))
            slot += n_blk
        token_major.append(pair)
    (o0, l0), (o1, l1), (o2, l2) = token_major
    mx = jnp.maximum(jnp.maximum(l0, l1), l2)
    e0, e1, e2 = jnp.exp(l0 - mx), jnp.exp(l1 - mx), jnp.exp(l2 - mx)
    den = e0 + e1 + e2
    o_dil = (e0 / den) * o0 + (e1 / den) * o1 + (e2 / den) * o2
    y_dil = _dot(o_dil.astype(BF16), wdil_ref[...])

    y = g_a * y_nsa + g_b * y_dil
    z = _dot(y.astype(BF16), wout_ref[...])
    o_ref[...] = x + _rms(z, post_ref[...])


def _final(x2d, pre, post, onsa, od, ld, wmerge, wnsa, wdil, wout, *, S, tm=512):
    T = x2d.shape[0]
    nt = S // tm
    tok = lambda w: pl.BlockSpec((tm, w), lambda i: (i, 0))
    full = lambda a: pl.BlockSpec(a.shape, lambda i: (0,) * a.ndim)
    strided = [pl.BlockSpec((None, dil, tm // dil, DIL_OUT), lambda i: (i // nt, 0, i % nt, 0))
               for _, dil in DIL_CONFIGS]
    n_strided = sum(1 for _, dil in DIL_CONFIGS if dil > 1)
    return pl.pallas_call(
        _final_kernel,
        grid=(T // tm,),
        in_specs=[tok(D_MODEL), _resident(pre), _resident(post), tok(Q_NSA)] + strided + strided
        + [_resident(wmerge), _resident(wnsa), _resident(wdil), _resident(wout)],
        out_specs=tok(D_MODEL),
        out_shape=jax.ShapeDtypeStruct((T, D_MODEL), F32),
        scratch_shapes=[pltpu.VMEM((2 * n_strided * (DIL_OUT // LANES), tm, LANES), F32)],
        compiler_params=_params(("parallel",)),
        name="final",
    )(x2d, pre, post, onsa, od[0], od[1], od[2], ld[0], ld[1], ld[2], wmerge, wnsa, wdil, wout)


def _overlap_t(n_ck, n_slc):
    c0 = np.arange(n_ck)[None, :] * CMP_STRIDE
    s0 = np.arange(n_slc)[:, None] * SLC_LEN
    ov = np.clip(np.minimum(c0 + CMP_LEN, s0 + SLC_LEN) - np.maximum(c0, s0), 0, None) / CMP_LEN
    ov[:, n_ck - 1] = 0.0
    return ov.astype(np.float32)


def _pos_columns(pos):
    cols = np.zeros((len(pos), LANES), np.float32)
    for t, vals in enumerate((pos // SLC_LEN, pos % SLC_LEN, np.ones(len(pos)))):
        for u in range(3):
            cols[:, HEAD_DIM + 3 * t + u] = vals
    return cols


def _ones_row_tail(n):
    tail = np.zeros((VT_ROWS, n), np.float32)
    tail[HEAD_DIM] = 1.0
    return tail


def _band_table(tq):
    u = np.arange(2 * NSA_WINDOW + tq)[:, None]
    r = (np.arange(NSA_REP * tq) % tq)[None, :]
    dist = NSA_WINDOW + r - u
    return np.where((dist >= 0) & (dist <= NSA_WINDOW - 1), 0.0, NEG).astype(np.float32)


def _dil_dist_table(tq, width, max_back, dil):
    u = np.arange(width + max_back)[:, None]
    r = np.arange(tq)[None, :]
    dist = max_back + r - u
    return np.where((dist >= 0) & (dist <= max_back), dist * dil, -NEG).astype(np.float32)


def _cmp_mask_table(n_ck, tq):
    u = np.arange(2 * n_ck - 8)[:, None]
    r = (np.arange(NSA_REP * tq) % tq)[None, :]
    end = (u - (n_ck - 8)) * CMP_STRIDE + CMP_LEN - 1
    return np.where(end <= r, 0.0, NEG).astype(np.float32)


def _mixer(x2d, B, S, mix_pre, mix_post, w_in, pe_k, w_ck1, w_ck2, pe_v, w_cv1, w_cv2,
           w_nsa_o, w_dil_o, w_mix_out):
    G = NSA_GROUPS
    BG = B * G
    assert S % (16 * 128) == 0 and NSA_TQ // CMP_STRIDE == 8
    o_q, o_kv, o_g, o_d, o_m = np.cumsum([0, Q_NSA, 6 * KV_NSA, GATE_NSA, 3 * QKV_DIL]).tolist()
    wq = w_in[:, o_q:o_kv] * (SCALE * LOG2E)
    w_gate = jnp.pad(w_in[:, o_g:o_d], ((0, 0), (0, LANES - GATE_NSA)))
    wd = w_in[:, o_d:o_m]
    q_pad, kv_pad, gates, xk, xv, *qkv_d = _proj(
        x2d, mix_pre[None, :], wq.astype(BF16), w_in[:, o_kv:o_g].astype(BF16), wd.astype(BF16),
        w_gate.astype(BF16), B=B, S=S)

    n_ck = S // CMP_STRIDE
    half = CMP_STRIDE * HEAD_DIM
    w1 = jnp.stack([w_ck1, w_cv1])
    w1cat = jnp.concatenate([w1[:, :half], w1[:, half:]], axis=2).astype(BF16)
    pe = jnp.stack([pe_k.reshape(1, -1), pe_v.reshape(1, -1)]).astype(BF16)
    w2 = jnp.pad(jnp.stack([w_ck2, w_cv2]), ((0, 0), (0, 0), (0, LANES - HEAD_DIM))).astype(BF16)
    cmp_pos = np.arange(n_ck) * CMP_STRIDE + CMP_LEN - 1
    kc, vct = _cmp_mlp(xk.reshape(BG, n_ck, half), xv.reshape(BG, n_ck, half), pe, w1cat, w1.astype(BF16), w2,
                       jnp.asarray(_pos_columns(cmp_pos), BF16), jnp.asarray(_ones_row_tail(n_ck), BF16))

    band = jnp.asarray(_band_table(NSA_TQ))
    cmpmask = jnp.asarray(_cmp_mask_table(n_ck, NSA_TQ))
    ovt = jnp.asarray(_overlap_t(n_ck, S // SLC_LEN), BF16)
    o_nsa = _nsa(q_pad, gates, kc, vct, kv_pad, jnp.asarray(_pos_columns(np.arange(S)), BF16),
                 jnp.asarray(_ones_row_tail(S), BF16), band, cmpmask, ovt, B=B, S=S)

    od, ld = [], []
    for gi, (window, dil) in enumerate(DIL_CONFIGS):
        Ls = S // dil
        max_back = window // dil
        tq = min(DIL_TQ, Ls)
        ts = min(DIL_TS, Ls)
        width = ts + max_back if Ls > ts else ts
        o, lse = _dil_attn(qkv_d[gi].reshape(B * dil, Ls, DIL_QKV),
                           jnp.asarray(_dil_dist_table(ts, width, max_back, dil)),
                           max_back=max_back, slopes=DIL_SLOPES[gi * DIL_HPG:(gi + 1) * DIL_HPG], tq=tq, ts=ts,
                           nb=DIL_TQ // tq)
        od.append(o.reshape(B, dil, Ls, DIL_OUT))
        ld.append(lse.reshape(B, dil, Ls, DIL_OUT))

    return _final(x2d, mix_pre[None, :], mix_post[None, :], o_nsa, od, ld, w_in[:, o_m:].astype(BF16),
                  w_nsa_o.astype(BF16), w_dil_o.astype(BF16), w_mix_out.astype(BF16), S=S)


def kernel(x, ffn1_pre, ffn1_post, ffn1_w_gate, ffn1_w_up, ffn1_w_down, mix_pre, mix_post, w_in, nsa_pe_k, nsa_w_ck1, nsa_w_ck2, nsa_pe_v, nsa_w_cv1, nsa_w_cv2, w_nsa_o, w_dil_o, w_mix_out, ffn2_pre, ffn2_post, ffn2_w_gate, ffn2_w_up, ffn2_w_down):
    B, S, _ = x.shape
    x2d = x.reshape(B * S, D_MODEL)
    for l in range(ffn1_pre.shape[0]):
        x2d = _ffn(x2d, ffn1_pre[l][None, :], ffn1_post[l][None, :], ffn1_w_gate[l].astype(BF16),
                   ffn1_w_up[l].astype(BF16), ffn1_w_down[l].astype(BF16))
        x2d = _mixer(x2d, B, S, mix_pre[l], mix_post[l], w_in[l], nsa_pe_k[l], nsa_w_ck1[l], nsa_w_ck2[l],
                     nsa_pe_v[l], nsa_w_cv1[l], nsa_w_cv2[l], w_nsa_o[l], w_dil_o[l], w_mix_out[l])
        x2d = _ffn(x2d, ffn2_pre[l][None, :], ffn2_post[l][None, :], ffn2_w_gate[l].astype(BF16),
                   ffn2_w_up[l].astype(BF16), ffn2_w_down[l].astype(BF16))
    return x2d.reshape(B, S, D_MODEL)
```
